```python
import math
import jax, jax.numpy as jnp
from jax import lax
import numpy as np

D_MODEL = 2048
BATCH = 4
SEQ = 4096
DEPTH = 1

CHUNK = 64
Q_BLOCK = 128
D_MIX = D_MODEL
A_HEADS = 8
A_DK = 64
A_DV = 2 * A_DK
A_WIDTH = A_HEADS * A_DV
B_HEADS = 4
B_WIDTH = D_MIX - A_WIDTH
B_DV = B_WIDTH // B_HEADS
B_DK = B_DV // 2
GATE_RANK = 16
GATE_TAU = 16.0
N_BUCKETS = 32
MAX_DISTANCE = 256
N_GROUPS = 4
EXPERTS_PER_GROUP = 8
N_EXPERTS = N_GROUPS * EXPERTS_PER_GROUP
TOP_K = 2
D_EXPERT = D_MODEL // 4
MOE_BLOCK = 128
EPS = 1e-6
NEG_INF = -1e30
SZ_QA = A_HEADS * 2 * A_DK
SZ_KA = A_HEADS * 2 * A_DK
SZ_VA = A_WIDTH
SZ_QB = B_HEADS * B_DK
SZ_KB = B_HEADS * B_DK
SZ_VB = B_WIDTH
SZ_RB = B_WIDTH
SZ_ZB = GATE_RANK
D_IN = SZ_QA + SZ_KA + SZ_VA + SZ_QB + SZ_KB + SZ_VB + SZ_RB + SZ_ZB
IN_SPLITS = (SZ_QA,
             SZ_QA + SZ_KA,
             SZ_QA + SZ_KA + SZ_VA,
             SZ_QA + SZ_KA + SZ_VA + SZ_QB,
             SZ_QA + SZ_KA + SZ_VA + SZ_QB + SZ_KB,
             SZ_QA + SZ_KA + SZ_VA + SZ_QB + SZ_KB + SZ_VB,
             SZ_QA + SZ_KA + SZ_VA + SZ_QB + SZ_KB + SZ_VB + SZ_RB)

kernel_name = "hybrid_diffattn_gla_hmoe_block"


def rms_norm(x, g):
    xf = x.astype(jnp.float32)
    y = xf * lax.rsqrt(jnp.mean(xf * xf, axis=-1, keepdims=True) + EPS)
    return (y * g.astype(jnp.float32)).astype(x.dtype)


def t5_bucket(rel):
    nb = N_BUCKETS // 2
    max_exact = nb // 2
    base = jnp.where(rel > 0, nb, 0)
    n = jnp.abs(rel)
    nf = jnp.maximum(n, 1).astype(jnp.float32)
    large = max_exact + (jnp.log(nf / max_exact) / math.log(MAX_DISTANCE / max_exact)
                         * (nb - max_exact)).astype(jnp.int32)
    large = jnp.minimum(large, nb - 1)
    return base + jnp.where(n < max_exact, n, large)


def diff_attention(q, k, v, rel_bias, lam, sub_gain, lam_init):
    Bn, S = q.shape[0], q.shape[1]
    nblk = S // Q_BLOCK
    qb = (q * (A_DK ** -0.5)).reshape(Bn, nblk, Q_BLOCK, A_HEADS, 2, A_DK)
    qb = qb.transpose(1, 0, 3, 4, 2, 5)
    kt = k.transpose(0, 2, 3, 1, 4)
    vt = v.transpose(0, 2, 1, 3)
    kpos = jnp.arange(S)

    def block(args):
        blk, qblk = args
        qpos = blk * Q_BLOCK + jnp.arange(Q_BLOCK)
        rel = kpos[None, :] - qpos[:, None]
        bias = rel_bias[t5_bucket(rel)].transpose(2, 0, 1)
        allowed = (kpos[None, :] // CHUNK) <= (qpos[:, None] // CHUNK)
        s = jnp.einsum('bhmqd,bhmkd->bhmqk', qblk, kt).astype(jnp.float32)
        s = s + bias[None, :, None].astype(jnp.float32)
        s = jnp.where(allowed, s, NEG_INF)
        p = jax.nn.softmax(s, axis=-1)
        pd = p[:, :, 0] - lam * p[:, :, 1]
        return jnp.einsum('bhqk,bhkd->bhqd', pd.astype(v.dtype), vt)

    o = lax.map(block, (jnp.arange(nblk), qb))
    o = o.transpose(1, 0, 3, 2, 4).reshape(Bn, S, A_HEADS, A_DV)
    o = rms_norm(o, sub_gain) * (1.0 - lam_init)
    return o.reshape(Bn, S, A_WIDTH)


def gla(q, k, v, log_a):
    Bn, S = q.shape[0], q.shape[1]
    nc = S // CHUNK

    def to_chunks(t):
        return t.reshape(Bn, nc, CHUNK, B_HEADS, t.shape[-1]).transpose(1, 0, 3, 2, 4).astype(jnp.float32)

    qc = to_chunks(q * (B_DK ** -0.5))
    kc = to_chunks(k)
    vc = to_chunks(v)
    ac = to_chunks(log_a)

    def step(state, inp):
        qn, kn, vn, an = inp
        cum = jnp.cumsum(an, axis=2)
        total = cum[:, :, -1:]
        k_dec = kn * jnp.exp(total - cum)
        state = jnp.exp(total[:, :, 0])[..., None] * state + jnp.einsum('bhcd,bhce->bhde', k_dec, vn)
        return state, jnp.einsum('bhcd,bhde->bhce', qn, state)

    s0 = jnp.zeros((Bn, B_HEADS, B_DK, B_DV), jnp.float32)
    _, o = lax.scan(step, s0, (qc, kc, vc, ac))
    return o.transpose(1, 0, 3, 2, 4).reshape(Bn, S, B_HEADS, B_DV).astype(v.dtype)


def token_mixer(h, rel_bias, w_in, w_alpha, b_alpha, lam_q1, lam_k1, lam_q2, lam_k2,
                g_sub_a, g_norm_b, w_out, lam_init):
    Bn, S = h.shape[0], h.shape[1]
    proj = h @ w_in
    qa, ka, va, qb, kb, vb, rb, zb = jnp.split(proj, IN_SPLITS, axis=-1)
    lam = (jnp.exp(jnp.dot(lam_q1.astype(jnp.float32), lam_k1.astype(jnp.float32)))
           - jnp.exp(jnp.dot(lam_q2.astype(jnp.float32), lam_k2.astype(jnp.float32))) + lam_init)
    oa = diff_attention(qa.reshape(Bn, S, A_HEADS, 2, A_DK),
                        ka.reshape(Bn, S, A_HEADS, 2, A_DK),
                        va.reshape(Bn, S, A_HEADS, A_DV),
                        rel_bias, lam, g_sub_a, lam_init)
    log_a = jax.nn.log_sigmoid((zb @ w_alpha + b_alpha).astype(jnp.float32)) / GATE_TAU
    ob = gla(qb.reshape(Bn, S, B_HEADS, B_DK),
             kb.reshape(Bn, S, B_HEADS, B_DK),
             vb.reshape(Bn, S, B_HEADS, B_DV),
             log_a.reshape(Bn, S, B_HEADS, B_DK))
    ob = rms_norm(ob, g_norm_b).reshape(Bn, S, B_WIDTH) * jax.nn.silu(rb)
    return jnp.concatenate([oa, ob], axis=-1) @ w_out


def hier_moe(h, w_rg, b_rg, w_re, b_re, w1, w3, w2):
    Bn, S, D = h.shape
    g_prob = jax.nn.softmax((h @ w_rg + b_rg).astype(jnp.float32), axis=-1)
    g_val, g_idx = lax.top_k(g_prob, 1)
    e_logit = (h @ w_re + b_re).astype(jnp.float32).reshape(Bn, S, N_GROUPS, EXPERTS_PER_GROUP)
    sel = jax.nn.one_hot(g_idx[..., 0], N_GROUPS, dtype=jnp.float32)
    e_logit = jnp.einsum('bsg,bsge->bse', sel, e_logit)
    e_prob = jax.nn.softmax(e_logit, axis=-1)
    e_val, e_idx = lax.top_k(e_prob, TOP_K)
    e_val = e_val / jnp.sum(e_val, axis=-1, keepdims=True)
    weights = g_val * e_val
    expert_id = g_idx * EXPERTS_PER_GROUP + e_idx
    combine = jnp.sum(jax.nn.one_hot(expert_id, N_EXPERTS, dtype=jnp.float32) * weights[..., None], axis=2)

    nblk = S // MOE_BLOCK
    tb = h.reshape(Bn, nblk, MOE_BLOCK, D).transpose(1, 0, 2, 3).reshape(nblk, Bn * MOE_BLOCK, D)
    cb = combine.reshape(Bn, nblk, MOE_BLOCK, N_EXPERTS).transpose(1, 0, 2, 3).reshape(nblk, Bn * MOE_BLOCK, N_EXPERTS)

    def block(args):
        t, cw = args
        a = jnp.einsum('td,edf->tef', t, w1)
        b = jnp.einsum('td,edf->tef', t, w3)
        hid = jax.nn.silu(a) * b * cw[:, :, None].astype(t.dtype)
        return jnp.einsum('tef,efd->td', hid, w2)

    y = lax.map(block, (tb, cb))
    return y.reshape(nblk, Bn, MOE_BLOCK, D).transpose(1, 0, 2, 3).reshape(Bn, S, D)


def setup_inputs(seed: int = 0) -> dict:
    key = jax.random.key(seed)
    ks = jax.random.split(key, 26)
    f32 = jnp.float32
    nrm = lambda k, shape, s: jax.random.normal(k, shape, f32) * s
    gain = lambda k, shape: 1.0 + 0.05 * jax.random.normal(k, shape, f32)
    D, L = D_MODEL, DEPTH
    return {
        "x": nrm(ks[0], (BATCH, SEQ, D), 1.0),
        "c": nrm(ks[1], (BATCH, D), 1.0),
        "rel_bias": nrm(ks[2], (N_BUCKETS, A_HEADS), 0.5),
        "w_ada": nrm(ks[3], (L, D, 6 * D), 0.5 * D ** -0.5),
        "b_ada": nrm(ks[4], (L, 6 * D), 0.02),
        "g_pre_mix": gain(ks[5], (L, D)),
        "g_post_mix": gain(ks[6], (L, D)),
        "w_in": nrm(ks[7], (L, D, D_IN), D ** -0.5),
        "w_alpha": nrm(ks[8], (L, GATE_RANK, B_HEADS * B_DK), GATE_RANK ** -0.5),
        "b_alpha": nrm(ks[9], (L, B_HEADS * B_DK), 0.1),
        "lam_q1": nrm(ks[10], (L, A_DK), 0.1),
        "lam_k1": nrm(ks[11], (L, A_DK), 0.1),
        "lam_q2": nrm(ks[12], (L, A_DK), 0.1),
        "lam_k2": nrm(ks[13], (L, A_DK), 0.1),
        "g_sub_a": gain(ks[14], (L, A_DV)),
        "g_norm_b": gain(ks[15], (L, B_DV)),
        "w_out": nrm(ks[16], (L, D_MIX, D), D_MIX ** -0.5),
        "g_pre_ffn": gain(ks[17], (L, D)),
        "g_post_ffn": gain(ks[18], (L, D)),
        "w_router_g": nrm(ks[19], (L, D, N_GROUPS), D ** -0.5),
        "b_router_g": nrm(ks[20], (L, N_GROUPS), 0.01),
        "w_router_e": nrm(ks[21], (L, D, N_EXPERTS), D ** -0.5),
        "b_router_e": nrm(ks[22], (L, N_EXPERTS), 0.01),
        "w1": nrm(ks[23], (L, N_EXPERTS, D, D_EXPERT), D ** -0.5),
        "w3": nrm(ks[24], (L, N_EXPERTS, D, D_EXPERT), D ** -0.5),
        "w2": nrm(ks[25], (L, N_EXPERTS, D_EXPERT, D), D_EXPERT ** -0.5),
    }


def reference(x, c, rel_bias, w_ada, b_ada, g_pre_mix, g_post_mix, w_in, w_alpha, b_alpha,
              lam_q1, lam_k1, lam_q2, lam_k2, g_sub_a, g_norm_b, w_out, g_pre_ffn, g_post_ffn,
              w_router_g, b_router_g, w_router_e, b_router_e, w1, w3, w2):
    for i in range(DEPTH):
        lam_init = 0.8 - 0.6 * math.exp(-0.3 * i)
        ada = jax.nn.silu(c) @ w_ada[i] + b_ada[i]
        sh_m, sc_m, gt_m, sh_f, sc_f, gt_f = [a[:, None, :] for a in jnp.split(ada, 6, axis=-1)]
        h = rms_norm(x, g_pre_mix[i]) * (1 + sc_m) + sh_m
        y = token_mixer(h, rel_bias, w_in[i], w_alpha[i], b_alpha[i], lam_q1[i], lam_k1[i],
                        lam_q2[i], lam_k2[i], g_sub_a[i], g_norm_b[i], w_out[i], lam_init)
        x = x + gt_m * rms_norm(y, g_post_mix[i])
        h = rms_norm(x, g_pre_ffn[i]) * (1 + sc_f) + sh_f
        y = hier_moe(h, w_router_g[i], b_router_g[i], w_router_e[i], b_router_e[i], w1[i], w3[i], w2[i])
        x = x + gt_f * rms_norm(y, g_post_ffn[i])
    return x
```

```python
import functools
import math

import jax
import jax.numpy as jnp
from jax import lax
from jax.experimental import pallas as pl
from jax.experimental.pallas import tpu as pltpu

F32 = jnp.float32
BF16 = jnp.bfloat16

D_MODEL = 2048
CHUNK = 64
A_HEADS = 8
A_DK = 64
A_DV = 2 * A_DK
A_WIDTH = A_HEADS * A_DV
B_HEADS = 4
B_WIDTH = D_MODEL - A_WIDTH
B_DV = B_WIDTH // B_HEADS
B_DK = B_DV // 2
GATE_RANK = 16
GATE_TAU = 16.0
N_BUCKETS = 32
MAX_DISTANCE = 256
N_GROUPS = 4
EXPERTS_PER_GROUP = 8
N_EXPERTS = N_GROUPS * EXPERTS_PER_GROUP
D_EXPERT = D_MODEL // 4
EPS = 1e-6
NEG_INF = -1e30

LANES = 128
ROW_SLABS = D_MODEL // LANES
D_MAIN = 3 * A_WIDTH + 2 * B_HEADS * B_DK + 2 * B_WIDTH
COL_QA, COL_KA, COL_VA = 0, A_HEADS, 2 * A_HEADS
COL_QB = 3 * A_HEADS
COL_KB = COL_QB + B_HEADS
COL_VB256 = (3 * A_WIDTH + 2 * B_HEADS * B_DK) // B_DV
COL_RB256 = COL_VB256 + B_HEADS
ROUTE_E1, ROUTE_E2, ROUTE_W1, ROUTE_W2, ROUTE_R1, ROUTE_R2 = 0, 1, 2, 3, 4, 5
ROUTER_EXPERT_LANE0 = N_GROUPS

VMEM_LIMIT = 56 * 1024 * 1024


def _params(*sem):
    return pltpu.CompilerParams(dimension_semantics=sem, vmem_limit_bytes=VMEM_LIMIT)


def _rms(v):
    return v * lax.rsqrt(jnp.mean(v * v, axis=-1, keepdims=True) + EPS)


def _silu(v):
    return v * jax.nn.sigmoid(v)


def _ada_kernel(c_ref, w_ref, b_ref, o_ref):
    s = _silu(c_ref[...])
    o_ref[...] = jnp.dot(s.astype(BF16), w_ref[...].astype(BF16), preferred_element_type=F32) + b_ref[...]


def _ada(c_pad, w, b, tn=1024):
    m, d = c_pad.shape
    n = w.shape[1]
    return pl.pallas_call(
        _ada_kernel,
        out_shape=jax.ShapeDtypeStruct((m, n), F32),
        grid=(n // tn,),
        in_specs=[pl.BlockSpec((m, d), lambda j: (0, 0)),
                  pl.BlockSpec((d, tn), lambda j: (0, j)),
                  pl.BlockSpec((1, tn), lambda j: (0, j))],
        out_specs=pl.BlockSpec((m, tn), lambda j: (0, j)),
        compiler_params=_params("arbitrary"),
        name="ada_proj",
    )(c_pad, w, b)


def _inproj_kernel(x_ref, g_ref, sc_ref, sh_ref, w_ref, wz_ref, o_ref, z_ref, h_scr):
    @pl.when(pl.program_id(1) == 0)
    def _():
        h = _rms(x_ref[...]) * g_ref[...]
        h = h * (1.0 + sc_ref[0]) + sh_ref[0]
        h_scr[...] = h.astype(BF16)
        z_ref[...] = jnp.dot(h_scr[...], wz_ref[...], preferred_element_type=F32)

    o_ref[...] = jnp.dot(h_scr[...], w_ref[...], preferred_element_type=F32).astype(BF16)


def _inproj(x2d, g, sc, sh, w_main, w_z, seq, tm=1024, tn=512):
    t, d = x2d.shape
    tm = min(tm, seq)
    n = w_main.shape[1]
    per_b = seq // tm
    return pl.pallas_call(
        _inproj_kernel,
        out_shape=(jax.ShapeDtypeStruct((t, n), BF16), jax.ShapeDtypeStruct((t, LANES), F32)),
        grid=(t // tm, n // tn),
        in_specs=[pl.BlockSpec((tm, d), lambda i, j: (i, 0)),
                  pl.BlockSpec((1, d), lambda i, j: (0, 0)),
                  pl.BlockSpec((1, 1, d), lambda i, j: (i // per_b, 0, 0)),
                  pl.BlockSpec((1, 1, d), lambda i, j: (i // per_b, 0, 0)),
                  pl.BlockSpec((d, tn), lambda i, j: (0, j)),
                  pl.BlockSpec((d, LANES), lambda i, j: (0, 0))],
        out_specs=(pl.BlockSpec((tm, tn), lambda i, j: (i, j)),
                   pl.BlockSpec((tm, LANES), lambda i, j: (i, 0))),
        scratch_shapes=[pltpu.VMEM((tm, d), BF16)],
        compiler_params=_params("arbitrary", "arbitrary"),
        name="in_proj",
    )(x2d, g, sc, sh, w_main, w_z)


def _t5_bucket(rel):
    nb = N_BUCKETS // 2
    max_exact = nb // 2
    base = jnp.where(rel > 0, nb, 0)
    n = jnp.abs(rel)
    nf = jnp.maximum(n, 1).astype(F32)
    large = max_exact + (jnp.log(nf / max_exact) / math.log(MAX_DISTANCE / max_exact)
                         * (nb - max_exact)).astype(jnp.int32)
    large = jnp.minimum(large, nb - 1)
    return base + jnp.where(n < max_exact, n, large)


def _bias_tiles(rel_bias, tq):
    qi = jnp.arange(tq)[:, None]
    kj = jnp.arange(tq)[None, :]
    far = rel_bias[N_BUCKETS // 2 - 1]
    tiles = []
    for off in (-1, 0):
        rel = kj - qi + off * tq
        b = rel_bias[_t5_bucket(rel)] - far
        if off == 0:
            allowed = (kj // CHUNK) <= (qi // CHUNK)
            b = jnp.where(allowed[:, :, None], b, NEG_INF)
        tiles.append(b.transpose(2, 0, 1))
    return jnp.stack(tiles, axis=1).astype(F32)


def _attn_kernel(q_ref, k_ref, v_ref, bias_ref, lq1_ref, lk1_ref, lq2_ref, lk2_ref, g_ref, o_ref,
                 m_scr, l_scr, acc_scr, *, tq, lam_init):
    i = pl.program_id(2)
    lane = lax.broadcasted_iota(jnp.int32, (1, A_DV), 1)
    q = q_ref[...] * (A_DK ** -0.5)
    zero = jnp.zeros_like(q)
    qmaps = (jnp.where(lane < A_DK, q, zero), jnp.where(lane >= A_DK, q, zero))

    m_scr[...] = jnp.full(m_scr.shape, NEG_INF, F32)
    l_scr[...] = jnp.zeros(l_scr.shape, F32)
    acc_scr[...] = jnp.zeros(acc_scr.shape, F32)

    def process(j, bias):
        start = pl.multiple_of(j * tq, tq)
        k = k_ref[pl.ds(start, tq), :]
        v = v_ref[pl.ds(start, tq), :]
        for mp in range(2):
            s = lax.dot_general(qmaps[mp], k, (((1,), (1,)), ((), ())), preferred_element_type=F32)
            if bias is not None:
                s = s + bias
            m_old = m_scr[mp]
            m_new = jnp.maximum(m_old, jnp.max(s, axis=-1, keepdims=True))
            alpha = jnp.exp(m_old - m_new)
            p = jnp.exp(s - m_new)
            l_scr[mp] = alpha * l_scr[mp] + jnp.sum(p, axis=-1, keepdims=True)
            acc_scr[mp] = alpha * acc_scr[mp] + jnp.dot(p.astype(BF16), v, preferred_element_type=F32)
            m_scr[mp] = m_new

    def far_body(j, carry):
        process(j, None)
        return carry

    lax.fori_loop(0, jnp.maximum(i - 1, 0), far_body, 0)

    @pl.when(i >= 1)
    def _():
        process(i - 1, bias_ref[0])

    process(i, bias_ref[1])

    lam = (jnp.exp(jnp.sum(lq1_ref[...] * lk1_ref[...], axis=-1, keepdims=True))
           - jnp.exp(jnp.sum(lq2_ref[...] * lk2_ref[...], axis=-1, keepdims=True)) + lam_init)
    o = acc_scr[0] / l_scr[0] - lam * (acc_scr[1] / l_scr[1])
    o_ref[...] = ((_rms(o) * g_ref[...]) * (1.0 - lam_init)).astype(BF16)


def _attention(proj, bias_tiles, lq1, lk1, lq2, lk2, g_sub, batch, seq, lam_init, tq):
    t = proj.shape[0]
    nq = seq // tq
    kern = functools.partial(_attn_kernel, tq=tq, lam_init=lam_init)
    vec = lambda n: pl.BlockSpec((1, n), lambda b, h, i: (0, 0))
    return pl.pallas_call(
        kern,
        out_shape=jax.ShapeDtypeStruct((t, A_WIDTH), BF16),
        grid=(batch, A_HEADS, nq),
        in_specs=[pl.BlockSpec((tq, A_DV), lambda b, h, i: (b * nq + i, COL_QA + h)),
                  pl.BlockSpec((seq, A_DV), lambda b, h, i: (b, COL_KA + h)),
                  pl.BlockSpec((seq, A_DV), lambda b, h, i: (b, COL_VA + h)),
                  pl.BlockSpec((None, 2, tq, tq), lambda b, h, i: (h, 0, 0, 0)),
                  vec(A_DK), vec(A_DK), vec(A_DK), vec(A_DK), vec(A_DV)],
        out_specs=pl.BlockSpec((tq, A_DV), lambda b, h, i: (b * nq + i, h)),
        scratch_shapes=[pltpu.VMEM((2, tq, 1), F32), pltpu.VMEM((2, tq, 1), F32),
                        pltpu.VMEM((2, tq, A_DV), F32)],
        compiler_params=_params("arbitrary", "arbitrary", "arbitrary"),
        name="diff_attention",
    )(proj, proj, proj, bias_tiles, lq1, lk1, lq2, lk2, g_sub)


def _gla_kernel(q_ref, k_ref, v_ref, r_ref, z_ref, wa_ref, ba_ref, g_ref, o_ref, state_scr, *, n_chunks):
    @pl.when(pl.program_id(2) == 0)
    def _():
        state_scr[...] = jnp.zeros(state_scr.shape, F32)

    pre = jnp.dot(z_ref[...], wa_ref[...], precision=lax.Precision.HIGHEST,
                  preferred_element_type=F32) + ba_ref[...]
    log_a = (jnp.minimum(pre, 0.0) - jnp.log1p(jnp.exp(-jnp.abs(pre)))) * (1.0 / GATE_TAU)
    row = lax.broadcasted_iota(jnp.int32, (CHUNK, CHUNK), 0)
    col = lax.broadcasted_iota(jnp.int32, (CHUNK, CHUNK), 1)
    tri = (row >= col).astype(F32)

    outs = []
    for c in range(n_chunks):
        rows = pl.ds(c * CHUNK, CHUNK)
        cum = jnp.dot(tri, log_a[c * CHUNK:(c + 1) * CHUNK], precision=lax.Precision.HIGHEST,
                      preferred_element_type=F32)
        total = cum[CHUNK - 1:CHUNK, :]
        k_dec = (k_ref[rows, :].astype(F32) * jnp.exp(total - cum)).astype(BF16)
        kv = lax.dot_general(v_ref[rows, :], k_dec, (((0,), (0,)), ((), ())),
                             preferred_element_type=F32)
        state = state_scr[...] * jnp.exp(total) + kv
        state_scr[...] = state
        outs.append(lax.dot_general(q_ref[rows, :], state.astype(BF16), (((1,), (1,)), ((), ())),
                                    preferred_element_type=F32))
    o = jnp.concatenate(outs, axis=0) * (B_DK ** -0.5)
    o_ref[...] = (_rms(o) * g_ref[...] * _silu(r_ref[...].astype(F32))).astype(BF16)


def _gla(proj, zb, w_alpha_pad, b_alpha, g_norm, batch, seq, lc=512):
    t = proj.shape[0]
    lc = min(lc, seq)
    nl = seq // lc
    kern = functools.partial(_gla_kernel, n_chunks=lc // CHUNK)
    return pl.pallas_call(
        kern,
        out_shape=jax.ShapeDtypeStruct((t, B_WIDTH), BF16),
        grid=(batch, B_HEADS, nl),
        in_specs=[pl.BlockSpec((lc, B_DK), lambda b, h, l: (b * nl + l, COL_QB + h)),
                  pl.BlockSpec((lc, B_DK), lambda b, h, l: (b * nl + l, COL_KB + h)),
                  pl.BlockSpec((lc, B_DV), lambda b, h, l: (b * nl + l, COL_VB256 + h)),
                  pl.BlockSpec((lc, B_DV), lambda b, h, l: (b * nl + l, COL_RB256 + h)),
                  pl.BlockSpec((lc, LANES), lambda b, h, l: (b * nl + l, 0)),
                  pl.BlockSpec((LANES, B_DK), lambda b, h, l: (0, h)),
                  pl.BlockSpec((1, B_DK), lambda b, h, l: (0, h)),
                  pl.BlockSpec((1, B_DV), lambda b, h, l: (0, 0))],
        out_specs=pl.BlockSpec((lc, B_DV), lambda b, h, l: (b * nl + l, h)),
        scratch_shapes=[pltpu.VMEM((B_DV, B_DK), F32)],
        compiler_params=_params("arbitrary", "arbitrary", "arbitrary"),
        name="gla",
    )(proj, proj, proj, proj, zb, w_alpha_pad, b_alpha, g_norm)


def _outproj_kernel(oa_ref, ob_ref, wo_ref, x_ref, gt_ref, gpost_ref, gpre_ref, sc_ref, sh_ref, wr_ref, br_ref,
                    x1_ref, h2_ref, lg_ref, *, tm):
    y = (jnp.dot(oa_ref[...], wo_ref[:A_WIDTH, :], preferred_element_type=F32)
         + jnp.dot(ob_ref[...], wo_ref[A_WIDTH:, :], preferred_element_type=F32))
    x1 = x_ref[...] + gt_ref[0] * (_rms(y) * gpost_ref[...])
    x1_ref[...] = x1
    h2 = (_rms(x1) * gpre_ref[...]) * (1.0 + sc_ref[0]) + sh_ref[0]
    lg_ref[...] = jnp.dot(h2, wr_ref[...], precision=lax.Precision.HIGHEST,
                          preferred_element_type=F32) + br_ref[...]
    for s in range(ROW_SLABS):
        h2_ref[pl.ds(s, tm, stride=ROW_SLABS), :] = h2[:, s * LANES:(s + 1) * LANES]


def _outproj(oa, ob, w_out, x2d, gt, g_post, g_pre, sc, sh, w_router, b_router, seq, tm=512):
    t, d = x2d.shape
    tm = min(tm, seq)
    per_b = seq // tm
    kern = functools.partial(_outproj_kernel, tm=tm)
    row = lambda: pl.BlockSpec((1, d), lambda i: (0, 0))
    per_batch = lambda: pl.BlockSpec((1, 1, d), lambda i: (i // per_b, 0, 0))
    return pl.pallas_call(
        kern,
        out_shape=(jax.ShapeDtypeStruct((t, d), F32),
                   jax.ShapeDtypeStruct((t * ROW_SLABS, LANES), F32),
                   jax.ShapeDtypeStruct((t, LANES), F32)),
        grid=(t // tm,),
        in_specs=[pl.BlockSpec((tm, A_WIDTH), lambda i: (i, 0)),
                  pl.BlockSpec((tm, B_WIDTH), lambda i: (i, 0)),
                  pl.BlockSpec((d, d), lambda i: (0, 0)),
                  pl.BlockSpec((tm, d), lambda i: (i, 0)),
                  per_batch(), row(), row(), per_batch(), per_batch(),
                  pl.BlockSpec((d, LANES), lambda i: (0, 0)),
                  pl.BlockSpec((1, LANES), lambda i: (0, 0))],
        out_specs=(pl.BlockSpec((tm, d), lambda i: (i, 0)),
                   pl.BlockSpec((tm * ROW_SLABS, LANES), lambda i: (i, 0)),
                   pl.BlockSpec((tm, LANES), lambda i: (i, 0))),
        compiler_params=_params("arbitrary"),
        name="out_proj",
    )(oa, ob, w_out, x2d, gt, g_post, g_pre, sc, sh, w_router, b_router)


def _route_kernel(lg_ref, rec_ref, cnt_ref, carry_scr, *, tr):
    @pl.when(pl.program_id(0) == 0)
    def _():
        carry_scr[...] = jnp.zeros(carry_scr.shape, F32)

    lg = lg_ref[...]
    lane = lax.broadcasted_iota(jnp.int32, lg.shape, 1)
    big = jnp.int32(LANES)

    def first_lane(mask):
        return jnp.min(jnp.where(mask, lane, big), axis=-1, keepdims=True)

    gmask = lane < N_GROUPS
    gmax = jnp.max(jnp.where(gmask, lg, -jnp.inf), axis=-1, keepdims=True)
    gexp = jnp.where(gmask, jnp.exp(lg - gmax), 0.0)
    gprob = gexp / jnp.sum(gexp, axis=-1, keepdims=True)
    g_val = jnp.max(gprob, axis=-1, keepdims=True)
    g_idx = first_lane(gmask & (gprob == g_val))

    lo = ROUTER_EXPERT_LANE0 + g_idx * EXPERTS_PER_GROUP
    emask = (lane >= lo) & (lane < lo + EXPERTS_PER_GROUP)
    emax = jnp.max(jnp.where(emask, lg, -jnp.inf), axis=-1, keepdims=True)
    eexp = jnp.where(emask, jnp.exp(lg - emax), 0.0)
    eprob = eexp / jnp.sum(eexp, axis=-1, keepdims=True)
    v1 = jnp.max(eprob, axis=-1, keepdims=True)
    i1 = first_lane(emask & (eprob == v1))
    rest = emask & (lane != i1)
    v2 = jnp.max(jnp.where(rest, eprob, -1.0), axis=-1, keepdims=True)
    i2 = first_lane(rest & (eprob == v2))
    w1 = g_val * (v1 / (v1 + v2))
    w2 = g_val * (v2 / (v1 + v2))

    hit1 = lane == i1
    hit2 = lane == i2
    onehot = (hit1 | hit2).astype(BF16)
    r = lax.broadcasted_iota(jnp.int32, (tr, tr), 0)
    c = lax.broadcasted_iota(jnp.int32, (tr, tr), 1)
    before = (c < r).astype(BF16)
    pos = carry_scr[...] + jnp.dot(before, onehot, preferred_element_type=F32)
    rank1 = jnp.sum(jnp.where(hit1, pos, 0.0), axis=-1, keepdims=True)
    rank2 = jnp.sum(jnp.where(hit2, pos, 0.0), axis=-1, keepdims=True)
    carry_scr[...] = carry_scr[...] + jnp.sum(onehot.astype(F32), axis=0, keepdims=True)
    cnt_ref[...] = carry_scr[...]

    e1 = (i1 - ROUTER_EXPERT_LANE0).astype(F32)
    e2 = (i2 - ROUTER_EXPERT_LANE0).astype(F32)
    rec = jnp.zeros(lg.shape, F32)
    for ln, val in ((ROUTE_E1, e1), (ROUTE_E2, e2), (ROUTE_W1, w1), (ROUTE_W2, w2),
                    (ROUTE_R1, rank1), (ROUTE_R2, rank2)):
        rec = jnp.where(lane == ln, val, rec)
    rec_ref[...] = rec


def _route(logits, tr=512):
    t = logits.shape[0]
    tr = min(tr, t)
    kern = functools.partial(_route_kernel, tr=tr)
    return pl.pallas_call(
        kern,
        out_shape=(jax.ShapeDtypeStruct((t, LANES), F32), jax.ShapeDtypeStruct((1, LANES), F32)),
        grid=(t // tr,),
        in_specs=[pl.BlockSpec((tr, LANES), lambda i: (i, 0))],
        out_specs=(pl.BlockSpec((tr, LANES), lambda i: (i, 0)),
                   pl.BlockSpec((1, LANES), lambda i: (0, 0))),
        scratch_shapes=[pltpu.VMEM((1, LANES), F32)],
        compiler_params=_params("arbitrary"),
        name="route",
    )(logits)


def _expert_kernel(texp_ref, tvalid_ref, src_ref, dst_ref,
                   h2_hbm, cw_ref, w1_ref, w3_ref, w2_ref, out_hbm,
                   xg_scr, x_scr, yo_scr, sems, *, tm):
    i = pl.program_id(0)
    nvalid = tvalid_ref[i]
    base = i * tm

    def row_copy_in(r):
        tok = src_ref[base + r]
        return pltpu.make_async_copy(
            h2_hbm.at[pl.ds(pl.multiple_of(tok * ROW_SLABS, ROW_SLABS), ROW_SLABS), :],
            xg_scr.at[pl.ds(pl.multiple_of(r * ROW_SLABS, ROW_SLABS), ROW_SLABS), :],
            sems.at[0])

    def row_copy_out(r):
        d = dst_ref[base + r]
        return pltpu.make_async_copy(
            yo_scr.at[pl.ds(pl.multiple_of(r * ROW_SLABS, ROW_SLABS), ROW_SLABS), :],
            out_hbm.at[pl.ds(pl.multiple_of(d * ROW_SLABS, ROW_SLABS), ROW_SLABS), :],
            sems.at[1])

    def each_row(n, fn):
        def body(r, carry):
            fn(r)
            return carry
        lax.fori_loop(0, n, body, 0)

    @pl.when(nvalid > 0)
    def _():
        each_row(tm, lambda r: row_copy_in(r).start())
        each_row(tm, lambda r: row_copy_in(r).wait())
        for s in range(ROW_SLABS):
            x_scr[:, s * LANES:(s + 1) * LANES] = xg_scr[pl.ds(s, tm, stride=ROW_SLABS), :].astype(BF16)
        x = x_scr[...]
        a = jnp.dot(x, w1_ref[...], preferred_element_type=F32)
        b = jnp.dot(x, w3_ref[...], preferred_element_type=F32)
        hid = (_silu(a) * b * cw_ref[...]).astype(BF16)
        y = jnp.dot(hid, w2_ref[...], preferred_element_type=F32)
        for s in range(ROW_SLABS):
            yo_scr[pl.ds(s, tm, stride=ROW_SLABS), :] = y[:, s * LANES:(s + 1) * LANES]
        each_row(nvalid, lambda r: row_copy_out(r).start())
        each_row(nvalid, lambda r: row_copy_out(r).wait())


def _experts(tile_expert, tile_valid, src, dst, h2_rows, cw_slot, w1, w3, w2, n_out_rows, tm):
    n_tiles = tile_expert.shape[0]
    d, f = w1.shape[1], w1.shape[2]
    kern = functools.partial(_expert_kernel, tm=tm)
    grid_spec = pltpu.PrefetchScalarGridSpec(
        num_scalar_prefetch=4,
        grid=(n_tiles,),
        in_specs=[pl.BlockSpec(memory_space=pl.ANY),
                  pl.BlockSpec((tm, 1), lambda i, te, tv, s, dd: (i, 0)),
                  pl.BlockSpec((None, d, f), lambda i, te, tv, s, dd: (te[i], 0, 0)),
                  pl.BlockSpec((None, d, f), lambda i, te, tv, s, dd: (te[i], 0, 0)),
                  pl.BlockSpec((None, f, d), lambda i, te, tv, s, dd: (te[i], 0, 0))],
        out_specs=pl.BlockSpec(memory_space=pl.ANY),
        scratch_shapes=[pltpu.VMEM((tm * ROW_SLABS, LANES), F32),
                        pltpu.VMEM((tm, d), BF16),
                        pltpu.VMEM((tm * ROW_SLABS, LANES), F32),
                        pltpu.SemaphoreType.DMA((2,))],
    )
    return pl.pallas_call(
        kern,
        out_shape=jax.ShapeDtypeStruct((n_out_rows * ROW_SLABS, LANES), F32),
        grid_spec=grid_spec,
        compiler_params=_params("arbitrary"),
        name="expert_mlp",
    )(tile_expert, tile_valid, src, dst, h2_rows, cw_slot, w1, w3, w2)


def _final_kernel(e0_ref, e1_ref, x1_ref, gt_ref, g_ref, o_ref, y_scr, *, tm):
    for s in range(ROW_SLABS):
        rows = pl.ds(s, tm, stride=ROW_SLABS)
        y_scr[:, s * LANES:(s + 1) * LANES] = e0_ref[rows, :] + e1_ref[rows, :]
    o_ref[...] = x1_ref[...] + gt_ref[0] * (_rms(y_scr[...]) * g_ref[...])


def _final(eo_rows, x1, gt, g_post, seq, tm=512):
    t, d = x1.shape
    tm = min(tm, seq)
    per_b = seq // tm
    nt = t // tm
    kern = functools.partial(_final_kernel, tm=tm)
    return pl.pallas_call(
        kern,
        out_shape=jax.ShapeDtypeStruct((t, d), F32),
        grid=(nt,),
        in_specs=[pl.BlockSpec((tm * ROW_SLABS, LANES), lambda i: (i, 0)),
                  pl.BlockSpec((tm * ROW_SLABS, LANES), lambda i: (nt + i, 0)),
                  pl.BlockSpec((tm, d), lambda i: (i, 0)),
                  pl.BlockSpec((1, 1, d), lambda i: (i // per_b, 0, 0)),
                  pl.BlockSpec((1, d), lambda i: (0, 0))],
        out_specs=pl.BlockSpec((tm, d), lambda i: (i, 0)),
        scratch_shapes=[pltpu.VMEM((tm, d), F32)],
        compiler_params=_params("arbitrary"),
        name="combine_final",
    )(eo_rows, eo_rows, x1, gt, g_post)


def _dispatch_tables(rec, counts, t, tm):
    e = rec[:, ROUTE_E1:ROUTE_E2 + 1].astype(jnp.int32)
    w = rec[:, ROUTE_W1:ROUTE_W2 + 1]
    rank = rec[:, ROUTE_R1:ROUTE_R2 + 1].astype(jnp.int32)
    cnt = counts[0, ROUTER_EXPERT_LANE0:ROUTER_EXPERT_LANE0 + N_EXPERTS].astype(jnp.int32)
    tiles_per = (cnt + tm - 1) // tm
    tile_end = jnp.cumsum(tiles_per)
    tile_start = tile_end - tiles_per
    n_tiles = (2 * t + N_EXPERTS * (tm - 1)) // tm
    n_slots = n_tiles * tm
    slot = tile_start[e] * tm + rank
    tok = jnp.broadcast_to(jnp.arange(t, dtype=jnp.int32)[:, None], (t, 2))
    dest = tok + jnp.arange(2, dtype=jnp.int32)[None, :] * t
    flat = slot.reshape(-1)
    src = jnp.zeros((n_slots,), jnp.int32).at[flat].set(tok.reshape(-1))
    dst = jnp.zeros((n_slots,), jnp.int32).at[flat].set(dest.reshape(-1))
    cw = jnp.zeros((n_slots,), F32).at[flat].set(w.reshape(-1))
    tile_id = jnp.arange(n_tiles, dtype=jnp.int32)
    used = tile_end[-1]
    texp = jnp.searchsorted(tile_end, tile_id, side="right").astype(jnp.int32)
    last = jnp.searchsorted(tile_end, used - 1, side="right").astype(jnp.int32)
    texp = jnp.where(tile_id < used, texp, last)
    tvalid = jnp.clip(cnt[texp] - (tile_id - tile_start[texp]) * tm, 0, tm)
    tvalid = jnp.where(tile_id < used, tvalid, 0).astype(jnp.int32)
    return texp, tvalid, src, dst, cw.reshape(n_slots, 1)


def kernel(x, c, rel_bias, w_ada, b_ada, g_pre_mix, g_post_mix, w_in, w_alpha, b_alpha, lam_q1, lam_k1, lam_q2,
           lam_k2, g_sub_a, g_norm_b, w_out, g_pre_ffn, g_post_ffn, w_router_g, b_router_g, w_router_e,
           b_router_e, w1, w3, w2):
    batch, seq, d = x.shape
    t = batch * seq
    depth = w_in.shape[0]
    tq = min(512, seq)
    tm_e = 256
    xf = x.reshape(t, d)
    for i in range(depth):
        lam_init = 0.8 - 0.6 * math.exp(-0.3 * i)
        c_pad = jnp.pad(c, ((0, 8 - batch % 8 if batch % 8 else 0), (0, 0)))
        ada = _ada(c_pad, w_ada[i], b_ada[i][None, :])[:batch]
        sh_m, sc_m, gt_m, sh_f, sc_f, gt_f = [a[:, None, :] for a in jnp.split(ada, 6, axis=-1)]

        w_in_b = w_in[i].astype(BF16)
        w_main = w_in_b[:, :D_MAIN]
        w_z = jnp.pad(w_in_b[:, D_MAIN:], ((0, 0), (0, LANES - GATE_RANK)))
        proj, zb = _inproj(xf, g_pre_mix[i][None, :], sc_m, sh_m, w_main, w_z, seq)

        oa = _attention(proj, _bias_tiles(rel_bias, tq), lam_q1[i][None, :], lam_k1[i][None, :],
                        lam_q2[i][None, :], lam_k2[i][None, :], g_sub_a[i][None, :], batch, seq, lam_init, tq)
        w_alpha_pad = jnp.pad(w_alpha[i], ((0, LANES - GATE_RANK), (0, 0)))
        ob = _gla(proj, zb, w_alpha_pad, b_alpha[i][None, :], g_norm_b[i][None, :], batch, seq)

        w_router = jnp.pad(jnp.concatenate([w_router_g[i], w_router_e[i]], axis=1),
                           ((0, 0), (0, LANES - N_GROUPS - N_EXPERTS)))
        b_router = jnp.pad(jnp.concatenate([b_router_g[i], b_router_e[i]]),
                           (0, LANES - N_GROUPS - N_EXPERTS))[None, :]
        x1, h2_rows, logits = _outproj(oa, ob, w_out[i].astype(BF16), xf, gt_m, g_post_mix[i][None, :],
                                       g_pre_ffn[i][None, :], sc_f, sh_f, w_router, b_router, seq)

        rec, counts = _route(logits)
        texp, tvalid, src, dst, cw = _dispatch_tables(rec, counts, t, tm_e)
        eo = _experts(texp, tvalid, src, dst, h2_rows, cw, w1[i].astype(BF16), w3[i].astype(BF16),
                      w2[i].astype(BF16), 2 * t, tm_e)
        xf = _final(eo, x1, gt_f, g_post_ffn[i][None, :], seq)
    return xf.reshape(batch, seq, d)
```

```python
import functools
import math

import jax
import jax.numpy as jnp
from jax import lax
from jax.experimental import pallas as pl
from jax.experimental.pallas import tpu as pltpu

F32 = jnp.float32
BF16 = jnp.bfloat16

D_MODEL = 2048
CHUNK = 64
A_HEADS = 8
A_DK = 64
A_DV = 2 * A_DK
A_WIDTH = A_HEADS * A_DV
B_HEADS = 4
B_WIDTH = D_MODEL - A_WIDTH
B_DV = B_WIDTH // B_HEADS
B_DK = B_DV // 2
GATE_RANK = 16
GATE_TAU = 16.0
N_BUCKETS = 32
MAX_DISTANCE = 256
N_GROUPS = 4
EXPERTS_PER_GROUP = 8
N_EXPERTS = N_GROUPS * EXPERTS_PER_GROUP
D_EXPERT = D_MODEL // 4
EPS = 1e-6
NEG_INF = -1e30
LOG2E = math.log2(math.e)

LANES = 128
ROW_SLABS = D_MODEL // LANES
D_MAIN = 3 * A_WIDTH + 2 * B_HEADS * B_DK + 2 * B_WIDTH
COL_QA, COL_KA, COL_VA = 0, A_HEADS, 2 * A_HEADS
COL_QB = 3 * A_HEADS
COL_KB = COL_QB + B_HEADS
COL_VB256 = (3 * A_WIDTH + 2 * B_HEADS * B_DK) // B_DV
COL_RB256 = COL_VB256 + B_HEADS
ROUTE_E1, ROUTE_E2, ROUTE_W1, ROUTE_W2, ROUTE_R1, ROUTE_R2 = 0, 1, 2, 3, 4, 5
ROUTER_EXPERT_LANE0 = N_GROUPS

VMEM_LIMIT = 56 * 1024 * 1024


def _params(*sem):
    return pltpu.CompilerParams(dimension_semantics=sem, vmem_limit_bytes=VMEM_LIMIT)


def _rms(v):
    return v * lax.rsqrt(jnp.mean(v * v, axis=-1, keepdims=True) + EPS)


def _silu(v):
    return v * jax.nn.sigmoid(v)


def _ada_kernel(c_ref, w_ref, b_ref, o_ref):
    s = _silu(c_ref[...])
    o_ref[...] = jnp.dot(s.astype(BF16), w_ref[...].astype(BF16), preferred_element_type=F32) + b_ref[...]


def _ada(c_pad, w, b, tn=1024):
    m, d = c_pad.shape
    n = w.shape[1]
    return pl.pallas_call(
        _ada_kernel,
        out_shape=jax.ShapeDtypeStruct((m, n), F32),
        grid=(n // tn,),
        in_specs=[pl.BlockSpec((m, d), lambda j: (0, 0)),
                  pl.BlockSpec((d, tn), lambda j: (0, j)),
                  pl.BlockSpec((1, tn), lambda j: (0, j))],
        out_specs=pl.BlockSpec((m, tn), lambda j: (0, j)),
        compiler_params=_params("arbitrary"),
        name="ada_proj",
    )(c_pad, w, b)


def _inproj_kernel(x_ref, g_ref, sc_ref, sh_ref, w_ref, wz_ref, o_ref, z_ref, h_scr):
    @pl.when(pl.program_id(1) == 0)
    def _():
        h = _rms(x_ref[...]) * g_ref[...]
        h = h * (1.0 + sc_ref[0]) + sh_ref[0]
        h_scr[...] = h.astype(BF16)
        z_ref[...] = jnp.dot(h_scr[...], wz_ref[...], preferred_element_type=F32)

    o_ref[...] = jnp.dot(h_scr[...], w_ref[...], preferred_element_type=F32).astype(BF16)


def _inproj(x2d, g, sc, sh, w_main, w_z, seq, tm=1024, tn=512):
    t, d = x2d.shape
    tm = min(tm, seq)
    n = w_main.shape[1]
    per_b = seq // tm
    return pl.pallas_call(
        _inproj_kernel,
        out_shape=(jax.ShapeDtypeStruct((t, n), BF16), jax.ShapeDtypeStruct((t, LANES), F32)),
        grid=(t // tm, n // tn),
        in_specs=[pl.BlockSpec((tm, d), lambda i, j: (i, 0)),
                  pl.BlockSpec((1, d), lambda i, j: (0, 0)),
                  pl.BlockSpec((1, 1, d), lambda i, j: (i // per_b, 0, 0)),
                  pl.BlockSpec((1, 1, d), lambda i, j: (i // per_b, 0, 0)),
                  pl.BlockSpec((d, tn), lambda i, j: (0, j)),
                  pl.BlockSpec((d, LANES), lambda i, j: (0, 0))],
        out_specs=(pl.BlockSpec((tm, tn), lambda i, j: (i, j)),
                   pl.BlockSpec((tm, LANES), lambda i, j: (i, 0))),
        scratch_shapes=[pltpu.VMEM((tm, d), BF16)],
        compiler_params=_params("arbitrary", "arbitrary"),
        name="in_proj",
    )(x2d, g, sc, sh, w_main, w_z)


def _t5_bucket(rel):
    nb = N_BUCKETS // 2
    max_exact = nb // 2
    base = jnp.where(rel > 0, nb, 0)
    n = jnp.abs(rel)
    nf = jnp.maximum(n, 1).astype(F32)
    large = max_exact + (jnp.log(nf / max_exact) / math.log(MAX_DISTANCE / max_exact)
                         * (nb - max_exact)).astype(jnp.int32)
    large = jnp.minimum(large, nb - 1)
    return base + jnp.where(n < max_exact, n, large)


def _bias_buckets(tq):
    kj = jnp.arange(tq, dtype=jnp.int32)[:, None]
    qi = jnp.arange(tq, dtype=jnp.int32)[None, :]
    near = _t5_bucket(kj - qi - tq)
    diag = jnp.where((kj // CHUNK) <= (qi // CHUNK), _t5_bucket(kj - qi), N_BUCKETS)
    return jnp.stack([near, diag]).astype(jnp.int32)


def _bias_kernel(rb_ref, bk_ref, o_ref):
    h = pl.program_id(0)
    far = rb_ref[N_BUCKETS // 2 - 1, h]
    bucket = bk_ref[...]
    acc = jnp.full(bucket.shape, NEG_INF, F32)
    for n in range(N_BUCKETS):
        acc = jnp.where(bucket == n, (rb_ref[n, h] - far) * LOG2E, acc)
    o_ref[...] = acc


def _bias_tiles(rel_bias, tq):
    return pl.pallas_call(
        _bias_kernel,
        out_shape=jax.ShapeDtypeStruct((A_HEADS, 2, tq, tq), F32),
        grid=(A_HEADS, 2),
        in_specs=[pl.BlockSpec(memory_space=pltpu.SMEM),
                  pl.BlockSpec((None, tq, tq), lambda h, d: (d, 0, 0))],
        out_specs=pl.BlockSpec((None, None, tq, tq), lambda h, d: (h, d, 0, 0)),
        compiler_params=_params("arbitrary", "arbitrary"),
        name="bias_tiles",
    )(rel_bias, _bias_buckets(tq))


def _attn_kernel(q_ref, k_ref, v_ref, bias_ref, lq1_ref, lk1_ref, lq2_ref, lk2_ref, g_ref, o_ref,
                 vt_scr, sa_scr, sb_scr, m_scr, l_scr, acc_scr, *, tq, tk, lam_init):
    i = pl.program_id(2)
    nsub = tq // tk
    bufs = (sa_scr, sb_scr)

    @pl.when(i == 0)
    def _():
        for c in range(vt_scr.shape[0]):
            vt_scr[c] = v_ref[c * tk:(c + 1) * tk, :].astype(F32).T.astype(BF16)

    lane = lax.broadcasted_iota(jnp.int32, (1, A_DV), 1)
    q = q_ref[...] * (A_DK ** -0.5 * LOG2E)
    zero = jnp.zeros_like(q)
    q2 = jnp.concatenate([jnp.where(lane < A_DK, q, zero), jnp.where(lane >= A_DK, q, zero)], axis=0)

    m_scr[...] = jnp.full(m_scr.shape, NEG_INF, F32)
    l_scr[...] = jnp.zeros(l_scr.shape, F32)
    acc_scr[...] = jnp.zeros(acc_scr.shape, F32)

    def scores(j):
        k = k_ref[pl.ds(pl.multiple_of(j * tk, tk), tk), :]
        return lax.dot_general(k, q2, (((1,), (1,)), ((), ())), preferred_element_type=F32)

    def softmax_pv(s_ref, j, bias):
        s = s_ref[...]
        if bias is not None:
            s = jnp.concatenate([s[:, :tq] + bias, s[:, tq:] + bias], axis=1)
        m_old = m_scr[...]
        m_new = jnp.maximum(m_old, jnp.max(s, axis=0, keepdims=True))
        alpha = jnp.exp2(m_old - m_new)
        p = jnp.exp2(s - m_new)
        l_scr[...] = alpha * l_scr[...] + jnp.sum(p, axis=0, keepdims=True)
        acc_scr[...] = alpha * acc_scr[...] + jnp.dot(vt_scr[j], p.astype(BF16), preferred_element_type=F32)
        m_scr[...] = m_new

    n_far = jnp.maximum(i - 1, 0) * nsub
    sa_scr[...] = scores(0)

    def far_pair(jj, carry):
        j = 2 * jj
        sb_scr[...] = scores(j + 1)
        softmax_pv(sa_scr, j, None)
        sa_scr[...] = scores(j + 2)
        softmax_pv(sb_scr, j + 1, None)
        return carry

    lax.fori_loop(0, n_far // 2, far_pair, 0)

    @pl.when(i >= 1)
    def _():
        for c in range(nsub):
            bufs[(c + 1) % 2][...] = scores(n_far + c + 1)
            softmax_pv(bufs[c % 2], n_far + c, bias_ref[0, c * tk:(c + 1) * tk, :])

    for c in range(nsub):
        if c + 1 < nsub:
            bufs[(c + 1) % 2][...] = scores(i * nsub + c + 1)
        softmax_pv(bufs[c % 2], i * nsub + c, bias_ref[1, c * tk:(c + 1) * tk, :])

    lam = (jnp.exp(jnp.sum(lq1_ref[...] * lk1_ref[...], axis=-1, keepdims=True))
           - jnp.exp(jnp.sum(lq2_ref[...] * lk2_ref[...], axis=-1, keepdims=True)) + lam_init)
    on = acc_scr[...] / l_scr[...]
    o = on[:, :tq] - lam * on[:, tq:]
    y = o * lax.rsqrt(jnp.mean(o * o, axis=0, keepdims=True) + EPS) * g_ref[...] * (1.0 - lam_init)
    o_ref[...] = y.T.astype(BF16)


def _attention(proj, bias_tiles, lq1, lk1, lq2, lk2, g_sub_col, batch, seq, lam_init, tq, tk=256):
    t = proj.shape[0]
    nq = seq // tq
    assert (tq // tk) % 2 == 0 and tq % tk == 0, "the score pipeline alternates two buffers per query tile"
    kern = functools.partial(_attn_kernel, tq=tq, tk=tk, lam_init=lam_init)
    vec = lambda n: pl.BlockSpec((1, n), lambda b, h, i: (0, 0))
    return pl.pallas_call(
        kern,
        out_shape=jax.ShapeDtypeStruct((t, A_WIDTH), BF16),
        grid=(batch, A_HEADS, nq),
        in_specs=[pl.BlockSpec((tq, A_DV), lambda b, h, i: (b * nq + i, COL_QA + h)),
                  pl.BlockSpec((seq, A_DV), lambda b, h, i: (b, COL_KA + h)),
                  pl.BlockSpec((seq, A_DV), lambda b, h, i: (b, COL_VA + h)),
                  pl.BlockSpec((None, 2, tq, tq), lambda b, h, i: (h, 0, 0, 0)),
                  vec(A_DK), vec(A_DK), vec(A_DK), vec(A_DK),
                  pl.BlockSpec((A_DV, 1), lambda b, h, i: (0, 0))],
        out_specs=pl.BlockSpec((tq, A_DV), lambda b, h, i: (b * nq + i, h)),
        scratch_shapes=[pltpu.VMEM((seq // tk, A_DV, tk), BF16),
                        pltpu.VMEM((tk, 2 * tq), F32), pltpu.VMEM((tk, 2 * tq), F32),
                        pltpu.VMEM((1, 2 * tq), F32), pltpu.VMEM((1, 2 * tq), F32),
                        pltpu.VMEM((A_DV, 2 * tq), F32)],
        compiler_params=_params("arbitrary", "arbitrary", "arbitrary"),
        name="diff_attention",
    )(proj, proj, proj, bias_tiles, lq1, lk1, lq2, lk2, g_sub_col)


def _gla_kernel(q_ref, k_ref, v_ref, r_ref, z_ref, wa_ref, ba_ref, g_ref, o_ref, state_scr, *, n_chunks):
    @pl.when(pl.program_id(2) == 0)
    def _():
        state_scr[...] = jnp.zeros(state_scr.shape, F32)

    pre = jnp.dot(z_ref[...], wa_ref[...], precision=lax.Precision.HIGHEST,
                  preferred_element_type=F32) + ba_ref[...]
    log_a = (jnp.minimum(pre, 0.0) - jnp.log1p(jnp.exp(-jnp.abs(pre)))) * (1.0 / GATE_TAU)
    row = lax.broadcasted_iota(jnp.int32, (CHUNK, CHUNK), 0)
    col = lax.broadcasted_iota(jnp.int32, (CHUNK, CHUNK), 1)
    tri = (row >= col).astype(F32)

    outs = []
    for c in range(n_chunks):
        rows = pl.ds(c * CHUNK, CHUNK)
        cum = jnp.dot(tri, log_a[c * CHUNK:(c + 1) * CHUNK], precision=lax.Precision.HIGHEST,
                      preferred_element_type=F32)
        total = cum[CHUNK - 1:CHUNK, :]
        k_dec = (k_ref[rows, :].astype(F32) * jnp.exp(total - cum)).astype(BF16)
        kv = lax.dot_general(v_ref[rows, :], k_dec, (((0,), (0,)), ((), ())),
                             preferred_element_type=F32)
        state = state_scr[...] * jnp.exp(total) + kv
        state_scr[...] = state
        outs.append(lax.dot_general(q_ref[rows, :], state.astype(BF16), (((1,), (1,)), ((), ())),
                                    preferred_element_type=F32))
    o = jnp.concatenate(outs, axis=0) * (B_DK ** -0.5)
    o_ref[...] = (_rms(o) * g_ref[...] * _silu(r_ref[...].astype(F32))).astype(BF16)


def _gla(proj, zb, w_alpha_pad, b_alpha, g_norm, batch, seq, lc=512):
    t = proj.shape[0]
    lc = min(lc, seq)
    nl = seq // lc
    kern = functools.partial(_gla_kernel, n_chunks=lc // CHUNK)
    return pl.pallas_call(
        kern,
        out_shape=jax.ShapeDtypeStruct((t, B_WIDTH), BF16),
        grid=(batch, B_HEADS, nl),
        in_specs=[pl.BlockSpec((lc, B_DK), lambda b, h, l: (b * nl + l, COL_QB + h)),
                  pl.BlockSpec((lc, B_DK), lambda b, h, l: (b * nl + l, COL_KB + h)),
                  pl.BlockSpec((lc, B_DV), lambda b, h, l: (b * nl + l, COL_VB256 + h)),
                  pl.BlockSpec((lc, B_DV), lambda b, h, l: (b * nl + l, COL_RB256 + h)),
                  pl.BlockSpec((lc, LANES), lambda b, h, l: (b * nl + l, 0)),
                  pl.BlockSpec((LANES, B_DK), lambda b, h, l: (0, h)),
                  pl.BlockSpec((1, B_DK), lambda b, h, l: (0, h)),
                  pl.BlockSpec((1, B_DV), lambda b, h, l: (0, 0))],
        out_specs=pl.BlockSpec((lc, B_DV), lambda b, h, l: (b * nl + l, h)),
        scratch_shapes=[pltpu.VMEM((B_DV, B_DK), F32)],
        compiler_params=_params("arbitrary", "arbitrary", "arbitrary"),
        name="gla",
    )(proj, proj, proj, proj, zb, w_alpha_pad, b_alpha, g_norm)


def _outproj_kernel(oa_ref, ob_ref, wo_ref, x_ref, gt_ref, gpost_ref, gpre_ref, sc_ref, sh_ref, wr_ref, br_ref,
                    x1_ref, h2_ref, lg_ref, *, tm):
    y = (jnp.dot(oa_ref[...], wo_ref[:A_WIDTH, :], preferred_element_type=F32)
         + jnp.dot(ob_ref[...], wo_ref[A_WIDTH:, :], preferred_element_type=F32))
    x1 = x_ref[...] + gt_ref[0] * (_rms(y) * gpost_ref[...])
    x1_ref[...] = x1
    h2 = (_rms(x1) * gpre_ref[...]) * (1.0 + sc_ref[0]) + sh_ref[0]
    lg_ref[...] = jnp.dot(h2, wr_ref[...], precision=lax.Precision.HIGHEST,
                          preferred_element_type=F32) + br_ref[...]
    for s in range(ROW_SLABS):
        h2_ref[pl.ds(s, tm, stride=ROW_SLABS), :] = h2[:, s * LANES:(s + 1) * LANES]


def _outproj(oa, ob, w_out, x2d, gt, g_post, g_pre, sc, sh, w_router, b_router, seq, tm=512):
    t, d = x2d.shape
    tm = min(tm, seq)
    per_b = seq // tm
    kern = functools.partial(_outproj_kernel, tm=tm)
    row = lambda: pl.BlockSpec((1, d), lambda i: (0, 0))
    per_batch = lambda: pl.BlockSpec((1, 1, d), lambda i: (i // per_b, 0, 0))
    return pl.pallas_call(
        kern,
        out_shape=(jax.ShapeDtypeStruct((t, d), F32),
                   jax.ShapeDtypeStruct((t * ROW_SLABS, LANES), F32),
                   jax.ShapeDtypeStruct((t, LANES), F32)),
        grid=(t // tm,),
        in_specs=[pl.BlockSpec((tm, A_WIDTH), lambda i: (i, 0)),
                  pl.BlockSpec((tm, B_WIDTH), lambda i: (i, 0)),
                  pl.BlockSpec((d, d), lambda i: (0, 0)),
                  pl.BlockSpec((tm, d), lambda i: (i, 0)),
                  per_batch(), row(), row(), per_batch(), per_batch(),
                  pl.BlockSpec((d, LANES), lambda i: (0, 0)),
                  pl.BlockSpec((1, LANES), lambda i: (0, 0))],
        out_specs=(pl.BlockSpec((tm, d), lambda i: (i, 0)),
                   pl.BlockSpec((tm * ROW_SLABS, LANES), lambda i: (i, 0)),
                   pl.BlockSpec((tm, LANES), lambda i: (i, 0))),
        compiler_params=_params("arbitrary"),
        name="out_proj",
    )(oa, ob, w_out, x2d, gt, g_post, g_pre, sc, sh, w_router, b_router)


def _route_kernel(lg_ref, rec_ref, cnt_ref, carry_scr, *, tr):
    @pl.when(pl.program_id(0) == 0)
    def _():
        carry_scr[...] = jnp.zeros(carry_scr.shape, F32)

    lg = lg_ref[...]
    lane = lax.broadcasted_iota(jnp.int32, lg.shape, 1)
    big = jnp.int32(LANES)

    def first_lane(mask):
        return jnp.min(jnp.where(mask, lane, big), axis=-1, keepdims=True)

    gmask = lane < N_GROUPS
    gmax = jnp.max(jnp.where(gmask, lg, -jnp.inf), axis=-1, keepdims=True)
    gexp = jnp.where(gmask, jnp.exp(lg - gmax), 0.0)
    gprob = gexp / jnp.sum(gexp, axis=-1, keepdims=True)
    g_val = jnp.max(gprob, axis=-1, keepdims=True)
    g_idx = first_lane(gmask & (gprob == g_val))

    lo = ROUTER_EXPERT_LANE0 + g_idx * EXPERTS_PER_GROUP
    emask = (lane >= lo) & (lane < lo + EXPERTS_PER_GROUP)
    emax = jnp.max(jnp.where(emask, lg, -jnp.inf), axis=-1, keepdims=True)
    eexp = jnp.where(emask, jnp.exp(lg - emax), 0.0)
    eprob = eexp / jnp.sum(eexp, axis=-1, keepdims=True)
    v1 = jnp.max(eprob, axis=-1, keepdims=True)
    i1 = first_lane(emask & (eprob == v1))
    rest = emask & (lane != i1)
    v2 = jnp.max(jnp.where(rest, eprob, -1.0), axis=-1, keepdims=True)
    i2 = first_lane(rest & (eprob == v2))
    w1 = g_val * (v1 / (v1 + v2))
    w2 = g_val * (v2 / (v1 + v2))

    hit1 = lane == i1
    hit2 = lane == i2
    onehot = (hit1 | hit2).astype(BF16)
    r = lax.broadcasted_iota(jnp.int32, (tr, tr), 0)
    c = lax.broadcasted_iota(jnp.int32, (tr, tr), 1)
    before = (c < r).astype(BF16)
    pos = carry_scr[...] + jnp.dot(before, onehot, preferred_element_type=F32)
    rank1 = jnp.sum(jnp.where(hit1, pos, 0.0), axis=-1, keepdims=True)
    rank2 = jnp.sum(jnp.where(hit2, pos, 0.0), axis=-1, keepdims=True)
    carry_scr[...] = carry_scr[...] + jnp.sum(onehot.astype(F32), axis=0, keepdims=True)
    cnt_ref[...] = carry_scr[...]

    e1 = (i1 - ROUTER_EXPERT_LANE0).astype(F32)
    e2 = (i2 - ROUTER_EXPERT_LANE0).astype(F32)
    rec = jnp.zeros(lg.shape, F32)
    for ln, val in ((ROUTE_E1, e1), (ROUTE_E2, e2), (ROUTE_W1, w1), (ROUTE_W2, w2),
                    (ROUTE_R1, rank1), (ROUTE_R2, rank2)):
        rec = jnp.where(lane == ln, val, rec)
    rec_ref[...] = rec


def _route(logits, tr=512):
    t = logits.shape[0]
    tr = min(tr, t)
    kern = functools.partial(_route_kernel, tr=tr)
    return pl.pallas_call(
        kern,
        out_shape=(jax.ShapeDtypeStruct((t, LANES), F32), jax.ShapeDtypeStruct((1, LANES), F32)),
        grid=(t // tr,),
        in_specs=[pl.BlockSpec((tr, LANES), lambda i: (i, 0))],
        out_specs=(pl.BlockSpec((tr, LANES), lambda i: (i, 0)),
                   pl.BlockSpec((1, LANES), lambda i: (0, 0))),
        scratch_shapes=[pltpu.VMEM((1, LANES), F32)],
        compiler_params=_params("arbitrary"),
        name="route",
    )(logits)


def _slab_rows(ref, row):
    return ref.at[pl.ds(pl.multiple_of(row * ROW_SLABS, ROW_SLABS), ROW_SLABS), :]


def _dispatch_kernel(slot_ref, pad_start_ref, pad_len_ref, used_ref, h2_hbm, xs_hbm, zero_scr, sems, *, td, tm):
    g = pl.program_id(0)
    tile_rows = tm * ROW_SLABS
    n_tiles = xs_hbm.shape[0] // tile_rows

    def zero_copy(slot):
        return pltpu.make_async_copy(zero_scr.at[pl.ds(0, ROW_SLABS), :], _slab_rows(xs_hbm, slot), sems.at[1])

    def zero_tile_copy(tile):
        rows = pl.ds(pl.multiple_of(tile * tile_rows, tile_rows), tile_rows)
        return pltpu.make_async_copy(zero_scr, xs_hbm.at[rows, :], sems.at[1])

    @pl.when(g == 0)
    def _():
        zero_scr[...] = jnp.zeros(zero_scr.shape, F32)

        def unused_tile(tile, carry):
            zero_tile_copy(tile).start()
            zero_tile_copy(tile).wait()
            return carry

        lax.fori_loop(used_ref[0], n_tiles, unused_tile, 0)

        def per_expert(e, carry):
            start = pad_start_ref[e]

            def issue(r, c):
                zero_copy(start + r).start()
                return c

            def drain(r, c):
                zero_copy(start).wait()
                return c

            lax.fori_loop(0, pad_len_ref[e], issue, 0)
            lax.fori_loop(0, pad_len_ref[e], drain, 0)
            return carry

        lax.fori_loop(0, N_EXPERTS, per_expert, 0)

    def row_copy(tok, slot):
        return pltpu.make_async_copy(_slab_rows(h2_hbm, tok), _slab_rows(xs_hbm, slot), sems.at[0])

    base = g * td

    def issue(r, c):
        tok = base + r
        row_copy(tok, slot_ref[2 * tok]).start()
        row_copy(tok, slot_ref[2 * tok + 1]).start()
        return c

    def drain(r, c):
        row_copy(0, 0).wait()
        row_copy(0, 0).wait()
        return c

    lax.fori_loop(0, td, issue, 0)
    lax.fori_loop(0, td, drain, 0)


def _dispatch(slot, pad_start, pad_len, used, h2_rows, n_slots, tm, td=2048):
    t = slot.shape[0] // 2
    td = min(td, t)
    kern = functools.partial(_dispatch_kernel, td=td, tm=tm)
    grid_spec = pltpu.PrefetchScalarGridSpec(
        num_scalar_prefetch=4,
        grid=(t // td,),
        in_specs=[pl.BlockSpec(memory_space=pl.ANY)],
        out_specs=pl.BlockSpec(memory_space=pl.ANY),
        scratch_shapes=[pltpu.VMEM((tm * ROW_SLABS, LANES), F32), pltpu.SemaphoreType.DMA((2,))],
    )
    return pl.pallas_call(
        kern,
        out_shape=jax.ShapeDtypeStruct((n_slots * ROW_SLABS, LANES), F32),
        grid_spec=grid_spec,
        compiler_params=_params("arbitrary"),
        name="dispatch",
    )(slot, pad_start, pad_len, used, h2_rows)


def _expert_kernel(texp_ref, tblk_ref, tused_ref, xs_ref, w1_ref, w3_ref, w2_ref, eo_ref, x_scr, *, tm):
    used = tused_ref[pl.program_id(0)] > 0

    @pl.when(jnp.logical_not(used))
    def _():
        eo_ref[...] = jnp.zeros(eo_ref.shape, F32)

    @pl.when(used)
    def _():
        for s in range(ROW_SLABS):
            x_scr[:, s * LANES:(s + 1) * LANES] = xs_ref[pl.ds(s, tm, stride=ROW_SLABS), :].astype(BF16)
        x = x_scr[...]
        a = jnp.dot(x, w1_ref[...], preferred_element_type=F32)
        b = jnp.dot(x, w3_ref[...], preferred_element_type=F32)
        hid = (_silu(a) * b).astype(BF16)
        y = jnp.dot(hid, w2_ref[...], preferred_element_type=F32)
        for s in range(ROW_SLABS):
            eo_ref[pl.ds(s, tm, stride=ROW_SLABS), :] = y[:, s * LANES:(s + 1) * LANES]


def _experts(tile_expert, tile_block, tile_used, xs_rows, w1, w3, w2, tm):
    n_tiles = tile_expert.shape[0]
    d, f = w1.shape[1], w1.shape[2]
    kern = functools.partial(_expert_kernel, tm=tm)
    grid_spec = pltpu.PrefetchScalarGridSpec(
        num_scalar_prefetch=3,
        grid=(n_tiles,),
        in_specs=[pl.BlockSpec((tm * ROW_SLABS, LANES), lambda i, te, tb, tu: (tb[i], 0)),
                  pl.BlockSpec((None, d, f), lambda i, te, tb, tu: (te[i], 0, 0)),
                  pl.BlockSpec((None, d, f), lambda i, te, tb, tu: (te[i], 0, 0)),
                  pl.BlockSpec((None, f, d), lambda i, te, tb, tu: (te[i], 0, 0))],
        out_specs=pl.BlockSpec((tm * ROW_SLABS, LANES), lambda i, te, tb, tu: (i, 0)),
        scratch_shapes=[pltpu.VMEM((tm, d), BF16)],
    )
    return pl.pallas_call(
        kern,
        out_shape=jax.ShapeDtypeStruct(xs_rows.shape, F32),
        grid_spec=grid_spec,
        compiler_params=_params("arbitrary"),
        name="expert_mlp",
    )(tile_expert, tile_block, tile_used, xs_rows, w1, w3, w2)


def _final_kernel(slot_ref, eo_hbm, rec_ref, x1_ref, gt_ref, g_ref, o_ref, e_scr, y_scr, sems, *, tf):
    i = pl.program_id(0)
    par = i % 2

    def start_all(step, buf):
        def body(r, c):
            tok = step * tf + r
            for k in range(2):
                pltpu.make_async_copy(_slab_rows(eo_hbm, slot_ref[2 * tok + k]),
                                      _slab_rows(e_scr.at[buf, k], r), sems.at[buf]).start()
            return c
        lax.fori_loop(0, tf, body, 0)

    def wait_all(buf):
        def body(r, c):
            for k in range(2):
                pltpu.make_async_copy(_slab_rows(eo_hbm, 0), _slab_rows(e_scr.at[buf, k], 0), sems.at[buf]).wait()
            return c
        lax.fori_loop(0, tf, body, 0)

    @pl.when(i == 0)
    def _():
        start_all(0, 0)

    @pl.when(i + 1 < pl.num_programs(0))
    def _():
        start_all(i + 1, 1 - par)

    wait_all(par)
    rec = rec_ref[...]
    w1 = rec[:, ROUTE_W1:ROUTE_W1 + 1]
    w2 = rec[:, ROUTE_W2:ROUTE_W2 + 1]
    for s in range(ROW_SLABS):
        rows = pl.ds(s, tf, stride=ROW_SLABS)
        y_scr[:, s * LANES:(s + 1) * LANES] = w1 * e_scr[par, 0, rows, :] + w2 * e_scr[par, 1, rows, :]
    o_ref[...] = x1_ref[...] + gt_ref[0] * (_rms(y_scr[...]) * g_ref[...])


def _final(slot, eo_rows, rec, x1, gt, g_post, seq, tf=256):
    t, d = x1.shape
    tf = min(tf, seq)
    per_b = seq // tf
    kern = functools.partial(_final_kernel, tf=tf)
    grid_spec = pltpu.PrefetchScalarGridSpec(
        num_scalar_prefetch=1,
        grid=(t // tf,),
        in_specs=[pl.BlockSpec(memory_space=pl.ANY),
                  pl.BlockSpec((tf, LANES), lambda i, sl: (i, 0)),
                  pl.BlockSpec((tf, d), lambda i, sl: (i, 0)),
                  pl.BlockSpec((1, 1, d), lambda i, sl: (i // per_b, 0, 0)),
                  pl.BlockSpec((1, d), lambda i, sl: (0, 0))],
        out_specs=pl.BlockSpec((tf, d), lambda i, sl: (i, 0)),
        scratch_shapes=[pltpu.VMEM((2, 2, tf * ROW_SLABS, LANES), F32),
                        pltpu.VMEM((tf, d), F32),
                        pltpu.SemaphoreType.DMA((2,))],
    )
    return pl.pallas_call(
        kern,
        out_shape=jax.ShapeDtypeStruct((t, d), F32),
        grid_spec=grid_spec,
        compiler_params=_params("arbitrary"),
        name="combine_final",
    )(slot, eo_rows, rec, x1, gt, g_post)


def _dispatch_tables(rec, counts, t, tm):
    e = rec[:, ROUTE_E1:ROUTE_E2 + 1].astype(jnp.int32)
    rank = rec[:, ROUTE_R1:ROUTE_R2 + 1].astype(jnp.int32)
    cnt = counts[0, ROUTER_EXPERT_LANE0:ROUTER_EXPERT_LANE0 + N_EXPERTS].astype(jnp.int32)
    tiles_per = (cnt + tm - 1) // tm
    tile_end = jnp.cumsum(tiles_per)
    tile_start = tile_end - tiles_per
    n_tiles = (2 * t + N_EXPERTS * (tm - 1)) // tm
    slot = (tile_start[e] * tm + rank).reshape(-1)
    pad_start = tile_start * tm + cnt
    pad_len = tiles_per * tm - cnt
    tile_id = jnp.arange(n_tiles, dtype=jnp.int32)
    used = tile_end[-1]
    tblk = jnp.minimum(tile_id, used - 1)
    texp = jnp.searchsorted(tile_end, tblk, side="right").astype(jnp.int32)
    tused = (tile_id < used).astype(jnp.int32)
    return slot, pad_start, pad_len, used.reshape(1), texp, tblk, tused, n_tiles * tm


def kernel(x, c, rel_bias, w_ada, b_ada, g_pre_mix, g_post_mix, w_in, w_alpha, b_alpha, lam_q1, lam_k1, lam_q2,
           lam_k2, g_sub_a, g_norm_b, w_out, g_pre_ffn, g_post_ffn, w_router_g, b_router_g, w_router_e,
           b_router_e, w1, w3, w2):
    batch, seq, d = x.shape
    t = batch * seq
    depth = w_in.shape[0]
    tq = min(512, seq)
    tm_e = 256
    xf = x.reshape(t, d)
    for i in range(depth):
        lam_init = 0.8 - 0.6 * math.exp(-0.3 * i)
        c_pad = jnp.pad(c, ((0, 8 - batch % 8 if batch % 8 else 0), (0, 0)))
        ada = _ada(c_pad, w_ada[i], b_ada[i][None, :])[:batch]
        sh_m, sc_m, gt_m, sh_f, sc_f, gt_f = [a[:, None, :] for a in jnp.split(ada, 6, axis=-1)]

        w_in_b = w_in[i].astype(BF16)
        w_main = w_in_b[:, :D_MAIN]
        w_z = jnp.pad(w_in_b[:, D_MAIN:], ((0, 0), (0, LANES - GATE_RANK)))
        proj, zb = _inproj(xf, g_pre_mix[i][None, :], sc_m, sh_m, w_main, w_z, seq)

        oa = _attention(proj, _bias_tiles(rel_bias, tq), lam_q1[i][None, :], lam_k1[i][None, :],
                        lam_q2[i][None, :], lam_k2[i][None, :], g_sub_a[i][:, None], batch, seq, lam_init, tq)
        w_alpha_pad = jnp.pad(w_alpha[i], ((0, LANES - GATE_RANK), (0, 0)))
        ob = _gla(proj, zb, w_alpha_pad, b_alpha[i][None, :], g_norm_b[i][None, :], batch, seq)

        w_router = jnp.pad(jnp.concatenate([w_router_g[i], w_router_e[i]], axis=1),
                           ((0, 0), (0, LANES - N_GROUPS - N_EXPERTS)))
        b_router = jnp.pad(jnp.concatenate([b_router_g[i], b_router_e[i]]),
                           (0, LANES - N_GROUPS - N_EXPERTS))[None, :]
        x1, h2_rows, logits = _outproj(oa, ob, w_out[i].astype(BF16), xf, gt_m, g_post_mix[i][None, :],
                                       g_pre_ffn[i][None, :], sc_f, sh_f, w_router, b_router, seq)

        rec, counts = _route(logits)
        slot, pad_start, pad_len, used, texp, tblk, tused, n_slots = _dispatch_tables(rec, counts, t, tm_e)
        xs = _dispatch(slot, pad_start, pad_len, used, h2_rows, n_slots, tm_e)
        eo = _experts(texp, tblk, tused, xs, w1[i].astype(BF16), w3[i].astype(BF16), w2[i].astype(BF16), tm_e)
        xf = _final(slot, eo, rec, x1, gt_f, g_post_ffn[i][None, :], seq)
    return xf.reshape(batch, seq, d)
```

```python
import functools
import math

import jax
import jax.numpy as jnp
from jax import lax
from jax.experimental import pallas as pl
from jax.experimental.pallas import tpu as pltpu

F32 = jnp.float32
BF16 = jnp.bfloat16

D_MODEL = 2048
CHUNK = 64
A_HEADS = 8
A_DK = 64
A_DV = 2 * A_DK
A_WIDTH = A_HEADS * A_DV
B_HEADS = 4
B_WIDTH = D_MODEL - A_WIDTH
B_DV = B_WIDTH // B_HEADS
B_DK = B_DV // 2
GATE_RANK = 16
GATE_TAU = 16.0
N_BUCKETS = 32
MAX_DISTANCE = 256
N_GROUPS = 4
EXPERTS_PER_GROUP = 8
N_EXPERTS = N_GROUPS * EXPERTS_PER_GROUP
D_EXPERT = D_MODEL // 4
EPS = 1e-6
NEG_INF = -1e30
LOG2E = math.log2(math.e)

LANES = 128
ROW_SLABS = D_MODEL // LANES
ROW_CHUNK = 16
CHUNK_UNROLL = 4
D_MAIN = 3 * A_WIDTH + 2 * B_HEADS * B_DK + 2 * B_WIDTH
COL_QA, COL_KA, COL_VA = 0, A_HEADS, 2 * A_HEADS
COL_QB = 3 * A_HEADS
COL_KB = COL_QB + B_HEADS
COL_VB256 = (3 * A_WIDTH + 2 * B_HEADS * B_DK) // B_DV
COL_RB256 = COL_VB256 + B_HEADS
ROUTE_E1, ROUTE_E2, ROUTE_W1, ROUTE_W2, ROUTE_R1, ROUTE_R2 = 0, 1, 2, 3, 4, 5
ROUTER_EXPERT_LANE0 = N_GROUPS

VMEM_LIMIT = 56 * 1024 * 1024


def _params(*sem):
    return pltpu.CompilerParams(dimension_semantics=sem, vmem_limit_bytes=VMEM_LIMIT)


def _rms(v):
    return v * lax.rsqrt(jnp.mean(v * v, axis=-1, keepdims=True) + EPS)


def _silu(v):
    return v * jax.nn.sigmoid(v)


def _ada_kernel(c_ref, w_ref, b_ref, o_ref):
    s = _silu(c_ref[...])
    o_ref[...] = jnp.dot(s.astype(BF16), w_ref[...].astype(BF16), preferred_element_type=F32) + b_ref[...]


def _ada(c_pad, w, b, tn=1024):
    m, d = c_pad.shape
    n = w.shape[1]
    return pl.pallas_call(
        _ada_kernel,
        out_shape=jax.ShapeDtypeStruct((m, n), F32),
        grid=(n // tn,),
        in_specs=[pl.BlockSpec((m, d), lambda j: (0, 0)),
                  pl.BlockSpec((d, tn), lambda j: (0, j)),
                  pl.BlockSpec((1, tn), lambda j: (0, j))],
        out_specs=pl.BlockSpec((m, tn), lambda j: (0, j)),
        compiler_params=_params("arbitrary"),
        name="ada_proj",
    )(c_pad, w, b)


def _inproj_kernel(x_ref, g_ref, sc_ref, sh_ref, w_ref, wz_ref, o_ref, z_ref, h_scr):
    @pl.when(pl.program_id(1) == 0)
    def _():
        h = _rms(x_ref[...]) * g_ref[...]
        h = h * (1.0 + sc_ref[0]) + sh_ref[0]
        h_scr[...] = h.astype(BF16)
        z_ref[...] = jnp.dot(h_scr[...], wz_ref[...], preferred_element_type=F32)

    o_ref[...] = jnp.dot(h_scr[...], w_ref[...], preferred_element_type=F32).astype(BF16)


def _inproj(x2d, g, sc, sh, w_main, w_z, seq, tm=1024, tn=512):
    t, d = x2d.shape
    tm = min(tm, seq)
    n = w_main.shape[1]
    per_b = seq // tm
    return pl.pallas_call(
        _inproj_kernel,
        out_shape=(jax.ShapeDtypeStruct((t, n), BF16), jax.ShapeDtypeStruct((t, LANES), F32)),
        grid=(t // tm, n // tn),
        in_specs=[pl.BlockSpec((tm, d), lambda i, j: (i, 0)),
                  pl.BlockSpec((1, d), lambda i, j: (0, 0)),
                  pl.BlockSpec((1, 1, d), lambda i, j: (i // per_b, 0, 0)),
                  pl.BlockSpec((1, 1, d), lambda i, j: (i // per_b, 0, 0)),
                  pl.BlockSpec((d, tn), lambda i, j: (0, j)),
                  pl.BlockSpec((d, LANES), lambda i, j: (0, 0))],
        out_specs=(pl.BlockSpec((tm, tn), lambda i, j: (i, j)),
                   pl.BlockSpec((tm, LANES), lambda i, j: (i, 0))),
        scratch_shapes=[pltpu.VMEM((tm, d), BF16)],
        compiler_params=_params("arbitrary", "arbitrary"),
        name="in_proj",
    )(x2d, g, sc, sh, w_main, w_z)


def _t5_bucket(rel):
    nb = N_BUCKETS // 2
    max_exact = nb // 2
    base = jnp.where(rel > 0, nb, 0)
    n = jnp.abs(rel)
    nf = jnp.maximum(n, 1).astype(F32)
    large = max_exact + (jnp.log(nf / max_exact) / math.log(MAX_DISTANCE / max_exact)
                         * (nb - max_exact)).astype(jnp.int32)
    large = jnp.minimum(large, nb - 1)
    return base + jnp.where(n < max_exact, n, large)


def _bias_buckets(tq):
    kj = jnp.arange(tq, dtype=jnp.int32)[:, None]
    qi = jnp.arange(tq, dtype=jnp.int32)[None, :]
    near = _t5_bucket(kj - qi - tq)
    diag = jnp.where((kj // CHUNK) <= (qi // CHUNK), _t5_bucket(kj - qi), N_BUCKETS)
    return jnp.stack([near, diag]).astype(jnp.int32)


def _bias_kernel(rb_ref, bk_ref, o_ref):
    h = pl.program_id(0)
    far = rb_ref[N_BUCKETS // 2 - 1, h]
    bucket = bk_ref[...]
    acc = jnp.full(bucket.shape, NEG_INF, F32)
    for n in range(N_BUCKETS):
        acc = jnp.where(bucket == n, (rb_ref[n, h] - far) * LOG2E, acc)
    o_ref[...] = acc


def _bias_tiles(rel_bias, tq):
    return pl.pallas_call(
        _bias_kernel,
        out_shape=jax.ShapeDtypeStruct((A_HEADS, 2, tq, tq), F32),
        grid=(A_HEADS, 2),
        in_specs=[pl.BlockSpec(memory_space=pltpu.SMEM),
                  pl.BlockSpec((None, tq, tq), lambda h, d: (d, 0, 0))],
        out_specs=pl.BlockSpec((None, None, tq, tq), lambda h, d: (h, d, 0, 0)),
        compiler_params=_params("arbitrary", "arbitrary"),
        name="bias_tiles",
    )(rel_bias, _bias_buckets(tq))


def _attn_kernel(q_ref, k_ref, v_ref, bias_ref, lq1_ref, lk1_ref, lq2_ref, lk2_ref, g_ref, o_ref,
                 vt_scr, sa_scr, sb_scr, m_scr, l_scr, acc_scr, *, tq, tk, lam_init):
    i = pl.program_id(2)
    nsub = tq // tk
    bufs = (sa_scr, sb_scr)

    @pl.when(i == 0)
    def _():
        for c in range(vt_scr.shape[0]):
            vt_scr[c] = v_ref[c * tk:(c + 1) * tk, :].astype(F32).T.astype(BF16)

    lane = lax.broadcasted_iota(jnp.int32, (1, A_DV), 1)
    q = q_ref[...] * (A_DK ** -0.5 * LOG2E)
    zero = jnp.zeros_like(q)
    q2 = jnp.concatenate([jnp.where(lane < A_DK, q, zero), jnp.where(lane >= A_DK, q, zero)], axis=0)

    m_scr[...] = jnp.full(m_scr.shape, NEG_INF, F32)
    l_scr[...] = jnp.zeros(l_scr.shape, F32)
    acc_scr[...] = jnp.zeros(acc_scr.shape, F32)

    def scores(j):
        k = k_ref[pl.ds(pl.multiple_of(j * tk, tk), tk), :]
        return lax.dot_general(k, q2, (((1,), (1,)), ((), ())), preferred_element_type=F32)

    def softmax_pv(s_ref, j, bias):
        s = s_ref[...]
        if bias is not None:
            s = jnp.concatenate([s[:, :tq] + bias, s[:, tq:] + bias], axis=1)
        m_old = m_scr[...]
        m_new = jnp.maximum(m_old, jnp.max(s, axis=0, keepdims=True))
        alpha = jnp.exp2(m_old - m_new)
        p = jnp.exp2(s - m_new)
        l_scr[...] = alpha * l_scr[...] + jnp.sum(p, axis=0, keepdims=True)
        acc_scr[...] = alpha * acc_scr[...] + jnp.dot(vt_scr[j], p.astype(BF16), preferred_element_type=F32)
        m_scr[...] = m_new

    n_far = jnp.maximum(i - 1, 0) * nsub
    sa_scr[...] = scores(0)

    def far_pair(jj, carry):
        j = 2 * jj
        sb_scr[...] = scores(j + 1)
        softmax_pv(sa_scr, j, None)
        sa_scr[...] = scores(j + 2)
        softmax_pv(sb_scr, j + 1, None)
        return carry

    lax.fori_loop(0, n_far // 2, far_pair, 0)

    @pl.when(i >= 1)
    def _():
        for c in range(nsub):
            bufs[(c + 1) % 2][...] = scores(n_far + c + 1)
            softmax_pv(bufs[c % 2], n_far + c, bias_ref[0, c * tk:(c + 1) * tk, :])

    for c in range(nsub):
        if c + 1 < nsub:
            bufs[(c + 1) % 2][...] = scores(i * nsub + c + 1)
        softmax_pv(bufs[c % 2], i * nsub + c, bias_ref[1, c * tk:(c + 1) * tk, :])

    lam = (jnp.exp(jnp.sum(lq1_ref[...] * lk1_ref[...], axis=-1, keepdims=True))
           - jnp.exp(jnp.sum(lq2_ref[...] * lk2_ref[...], axis=-1, keepdims=True)) + lam_init)
    on = acc_scr[...] / l_scr[...]
    o = on[:, :tq] - lam * on[:, tq:]
    y = o * lax.rsqrt(jnp.mean(o * o, axis=0, keepdims=True) + EPS) * g_ref[...] * (1.0 - lam_init)
    o_ref[...] = y.T.astype(BF16)


def _attention(proj, bias_tiles, lq1, lk1, lq2, lk2, g_sub_col, batch, seq, lam_init, tq, tk=256):
    t = proj.shape[0]
    nq = seq // tq
    assert (tq // tk) % 2 == 0 and tq % tk == 0, "the score pipeline alternates two buffers per query tile"
    kern = functools.partial(_attn_kernel, tq=tq, tk=tk, lam_init=lam_init)
    vec = lambda n: pl.BlockSpec((1, n), lambda b, h, i: (0, 0))
    return pl.pallas_call(
        kern,
        out_shape=jax.ShapeDtypeStruct((t, A_WIDTH), BF16),
        grid=(batch, A_HEADS, nq),
        in_specs=[pl.BlockSpec((tq, A_DV), lambda b, h, i: (b * nq + i, COL_QA + h)),
                  pl.BlockSpec((seq, A_DV), lambda b, h, i: (b, COL_KA + h)),
                  pl.BlockSpec((seq, A_DV), lambda b, h, i: (b, COL_VA + h)),
                  pl.BlockSpec((None, 2, tq, tq), lambda b, h, i: (h, 0, 0, 0)),
                  vec(A_DK), vec(A_DK), vec(A_DK), vec(A_DK),
                  pl.BlockSpec((A_DV, 1), lambda b, h, i: (0, 0))],
        out_specs=pl.BlockSpec((tq, A_DV), lambda b, h, i: (b * nq + i, h)),
        scratch_shapes=[pltpu.VMEM((seq // tk, A_DV, tk), BF16),
                        pltpu.VMEM((tk, 2 * tq), F32), pltpu.VMEM((tk, 2 * tq), F32),
                        pltpu.VMEM((1, 2 * tq), F32), pltpu.VMEM((1, 2 * tq), F32),
                        pltpu.VMEM((A_DV, 2 * tq), F32)],
        compiler_params=_params("arbitrary", "arbitrary", "arbitrary"),
        name="diff_attention",
    )(proj, proj, proj, bias_tiles, lq1, lk1, lq2, lk2, g_sub_col)


def _gla_kernel(q_ref, k_ref, v_ref, r_ref, z_ref, wa_ref, ba_ref, g_ref, o_ref, state_scr, *, n_chunks):
    @pl.when(pl.program_id(2) == 0)
    def _():
        state_scr[...] = jnp.zeros(state_scr.shape, F32)

    pre = jnp.dot(z_ref[...], wa_ref[...], precision=lax.Precision.HIGHEST,
                  preferred_element_type=F32) + ba_ref[...]
    log_a = (jnp.minimum(pre, 0.0) - jnp.log1p(jnp.exp(-jnp.abs(pre)))) * (1.0 / GATE_TAU)
    row = lax.broadcasted_iota(jnp.int32, (CHUNK, CHUNK), 0)
    col = lax.broadcasted_iota(jnp.int32, (CHUNK, CHUNK), 1)
    tri = (row >= col).astype(F32)

    outs = []
    for c in range(n_chunks):
        rows = pl.ds(c * CHUNK, CHUNK)
        cum = jnp.dot(tri, log_a[c * CHUNK:(c + 1) * CHUNK], precision=lax.Precision.HIGHEST,
                      preferred_element_type=F32)
        total = cum[CHUNK - 1:CHUNK, :]
        k_dec = (k_ref[rows, :].astype(F32) * jnp.exp(total - cum)).astype(BF16)
        kv = lax.dot_general(v_ref[rows, :], k_dec, (((0,), (0,)), ((), ())),
                             preferred_element_type=F32)
        state = state_scr[...] * jnp.exp(total) + kv
        state_scr[...] = state
        outs.append(lax.dot_general(q_ref[rows, :], state.astype(BF16), (((1,), (1,)), ((), ())),
                                    preferred_element_type=F32))
    o = jnp.concatenate(outs, axis=0) * (B_DK ** -0.5)
    o_ref[...] = (_rms(o) * g_ref[...] * _silu(r_ref[...].astype(F32))).astype(BF16)


def _gla(proj, zb, w_alpha_pad, b_alpha, g_norm, batch, seq, lc=512):
    t = proj.shape[0]
    lc = min(lc, seq)
    nl = seq // lc
    kern = functools.partial(_gla_kernel, n_chunks=lc // CHUNK)
    return pl.pallas_call(
        kern,
        out_shape=jax.ShapeDtypeStruct((t, B_WIDTH), BF16),
        grid=(batch, B_HEADS, nl),
        in_specs=[pl.BlockSpec((lc, B_DK), lambda b, h, l: (b * nl + l, COL_QB + h)),
                  pl.BlockSpec((lc, B_DK), lambda b, h, l: (b * nl + l, COL_KB + h)),
                  pl.BlockSpec((lc, B_DV), lambda b, h, l: (b * nl + l, COL_VB256 + h)),
                  pl.BlockSpec((lc, B_DV), lambda b, h, l: (b * nl + l, COL_RB256 + h)),
                  pl.BlockSpec((lc, LANES), lambda b, h, l: (b * nl + l, 0)),
                  pl.BlockSpec((LANES, B_DK), lambda b, h, l: (0, h)),
                  pl.BlockSpec((1, B_DK), lambda b, h, l: (0, h)),
                  pl.BlockSpec((1, B_DV), lambda b, h, l: (0, 0))],
        out_specs=pl.BlockSpec((lc, B_DV), lambda b, h, l: (b * nl + l, h)),
        scratch_shapes=[pltpu.VMEM((B_DV, B_DK), F32)],
        compiler_params=_params("arbitrary", "arbitrary", "arbitrary"),
        name="gla",
    )(proj, proj, proj, proj, zb, w_alpha_pad, b_alpha, g_norm)


def _outproj_kernel(oa_ref, ob_ref, wo_ref, x_ref, gt_ref, gpost_ref, gpre_ref, sc_ref, sh_ref, wrh_ref, wrl_ref,
                    br_ref, x1_ref, h2_ref, lg_ref, y_scr, hh_scr, hl_scr, *, tm):
    y_scr[...] = (jnp.dot(oa_ref[...], wo_ref[:A_WIDTH, :], preferred_element_type=F32)
                  + jnp.dot(ob_ref[...], wo_ref[A_WIDTH:, :], preferred_element_type=F32))

    def chunk(c, carry):
        r0 = pl.multiple_of(c * ROW_CHUNK, ROW_CHUNK)
        rows = pl.ds(r0, ROW_CHUNK)
        x1 = x_ref[rows, :] + gt_ref[0] * (_rms(y_scr[rows, :]) * gpost_ref[...])
        x1_ref[rows, :] = x1
        h2 = (_rms(x1) * gpre_ref[...]) * (1.0 + sc_ref[0]) + sh_ref[0]
        hi = h2.astype(BF16)
        hh_scr[rows, :] = hi
        hl_scr[rows, :] = (h2 - hi.astype(F32)).astype(BF16)
        for s in range(ROW_SLABS):
            h2_ref[pl.ds(r0 * ROW_SLABS + s, ROW_CHUNK, stride=ROW_SLABS), :] = h2[:, s * LANES:(s + 1) * LANES]
        return carry

    lax.fori_loop(0, tm // ROW_CHUNK, chunk, 0, unroll=CHUNK_UNROLL)
    hi, lo = hh_scr[...], hl_scr[...]
    lg_ref[...] = (jnp.dot(hi, wrh_ref[...], preferred_element_type=F32)
                   + jnp.dot(lo, wrh_ref[...], preferred_element_type=F32)
                   + jnp.dot(hi, wrl_ref[...], preferred_element_type=F32) + br_ref[...])


def _outproj(oa, ob, w_out, x2d, gt, g_post, g_pre, sc, sh, w_router, b_router, seq, tm=512):
    t, d = x2d.shape
    tm = min(tm, seq)
    per_b = seq // tm
    kern = functools.partial(_outproj_kernel, tm=tm)
    wr_hi = w_router.astype(BF16)
    wr_lo = (w_router - wr_hi.astype(F32)).astype(BF16)
    row = lambda: pl.BlockSpec((1, d), lambda i: (0, 0))
    per_batch = lambda: pl.BlockSpec((1, 1, d), lambda i: (i // per_b, 0, 0))
    return pl.pallas_call(
        kern,
        out_shape=(jax.ShapeDtypeStruct((t, d), F32),
                   jax.ShapeDtypeStruct((t * ROW_SLABS, LANES), F32),
                   jax.ShapeDtypeStruct((t, LANES), F32)),
        grid=(t // tm,),
        in_specs=[pl.BlockSpec((tm, A_WIDTH), lambda i: (i, 0)),
                  pl.BlockSpec((tm, B_WIDTH), lambda i: (i, 0)),
                  pl.BlockSpec((d, d), lambda i: (0, 0)),
                  pl.BlockSpec((tm, d), lambda i: (i, 0)),
                  per_batch(), row(), row(), per_batch(), per_batch(),
                  pl.BlockSpec((d, LANES), lambda i: (0, 0)),
                  pl.BlockSpec((d, LANES), lambda i: (0, 0)),
                  pl.BlockSpec((1, LANES), lambda i: (0, 0))],
        out_specs=(pl.BlockSpec((tm, d), lambda i: (i, 0)),
                   pl.BlockSpec((tm * ROW_SLABS, LANES), lambda i: (i, 0)),
                   pl.BlockSpec((tm, LANES), lambda i: (i, 0))),
        scratch_shapes=[pltpu.VMEM((tm, d), F32), pltpu.VMEM((tm, d), BF16), pltpu.VMEM((tm, d), BF16)],
        compiler_params=_params("arbitrary"),
        name="out_proj",
    )(oa, ob, w_out, x2d, gt, g_post, g_pre, sc, sh, wr_hi, wr_lo, b_router)


def _route_kernel(lg_ref, rec_ref, cnt_ref, carry_scr, *, tr):
    @pl.when(pl.program_id(0) == 0)
    def _():
        carry_scr[...] = jnp.zeros(carry_scr.shape, F32)

    lg = lg_ref[...]
    lane = lax.broadcasted_iota(jnp.int32, lg.shape, 1)
    big = jnp.int32(LANES)

    def first_lane(mask):
        return jnp.min(jnp.where(mask, lane, big), axis=-1, keepdims=True)

    gmask = lane < N_GROUPS
    gmax = jnp.max(jnp.where(gmask, lg, -jnp.inf), axis=-1, keepdims=True)
    gexp = jnp.where(gmask, jnp.exp(lg - gmax), 0.0)
    gprob = gexp / jnp.sum(gexp, axis=-1, keepdims=True)
    g_val = jnp.max(gprob, axis=-1, keepdims=True)
    g_idx = first_lane(gmask & (gprob == g_val))

    lo = ROUTER_EXPERT_LANE0 + g_idx * EXPERTS_PER_GROUP
    emask = (lane >= lo) & (lane < lo + EXPERTS_PER_GROUP)
    emax = jnp.max(jnp.where(emask, lg, -jnp.inf), axis=-1, keepdims=True)
    eexp = jnp.where(emask, jnp.exp(lg - emax), 0.0)
    eprob = eexp / jnp.sum(eexp, axis=-1, keepdims=True)
    v1 = jnp.max(eprob, axis=-1, keepdims=True)
    i1 = first_lane(emask & (eprob == v1))
    rest = emask & (lane != i1)
    v2 = jnp.max(jnp.where(rest, eprob, -1.0), axis=-1, keepdims=True)
    i2 = first_lane(rest & (eprob == v2))
    w1 = g_val * (v1 / (v1 + v2))
    w2 = g_val * (v2 / (v1 + v2))

    hit1 = lane == i1
    hit2 = lane == i2
    onehot = (hit1 | hit2).astype(BF16)
    r = lax.broadcasted_iota(jnp.int32, (tr, tr), 0)
    c = lax.broadcasted_iota(jnp.int32, (tr, tr), 1)
    before = (c < r).astype(BF16)
    pos = carry_scr[...] + jnp.dot(before, onehot, preferred_element_type=F32)
    rank1 = jnp.sum(jnp.where(hit1, pos, 0.0), axis=-1, keepdims=True)
    rank2 = jnp.sum(jnp.where(hit2, pos, 0.0), axis=-1, keepdims=True)
    carry_scr[...] = carry_scr[...] + jnp.sum(onehot.astype(F32), axis=0, keepdims=True)
    cnt_ref[...] = carry_scr[...]

    e1 = (i1 - ROUTER_EXPERT_LANE0).astype(F32)
    e2 = (i2 - ROUTER_EXPERT_LANE0).astype(F32)
    rec = jnp.zeros(lg.shape, F32)
    for ln, val in ((ROUTE_E1, e1), (ROUTE_E2, e2), (ROUTE_W1, w1), (ROUTE_W2, w2),
                    (ROUTE_R1, rank1), (ROUTE_R2, rank2)):
        rec = jnp.where(lane == ln, val, rec)
    rec_ref[...] = rec


def _route(logits, tr=512):
    t = logits.shape[0]
    tr = min(tr, t)
    kern = functools.partial(_route_kernel, tr=tr)
    return pl.pallas_call(
        kern,
        out_shape=(jax.ShapeDtypeStruct((t, LANES), F32), jax.ShapeDtypeStruct((1, LANES), F32)),
        grid=(t // tr,),
        in_specs=[pl.BlockSpec((tr, LANES), lambda i: (i, 0))],
        out_specs=(pl.BlockSpec((tr, LANES), lambda i: (i, 0)),
                   pl.BlockSpec((1, LANES), lambda i: (0, 0))),
        scratch_shapes=[pltpu.VMEM((1, LANES), F32)],
        compiler_params=_params("arbitrary"),
        name="route",
    )(logits)


def _slab_rows(ref, row):
    return ref.at[pl.ds(pl.multiple_of(row * ROW_SLABS, ROW_SLABS), ROW_SLABS), :]


def _dispatch_kernel(slot_ref, pad_start_ref, pad_len_ref, used_ref, h2_ref, xs_hbm, zero_scr, sems, *, td, tm):
    g = pl.program_id(0)
    tile_rows = tm * ROW_SLABS
    n_tiles = xs_hbm.shape[0] // tile_rows

    def zero_copy(slot):
        return pltpu.make_async_copy(zero_scr.at[pl.ds(0, ROW_SLABS), :], _slab_rows(xs_hbm, slot), sems.at[1])

    def zero_tile_copy(tile):
        rows = pl.ds(pl.multiple_of(tile * tile_rows, tile_rows), tile_rows)
        return pltpu.make_async_copy(zero_scr, xs_hbm.at[rows, :], sems.at[1])

    @pl.when(g == 0)
    def _():
        zero_scr[...] = jnp.zeros(zero_scr.shape, F32)

        def unused_tile(tile, carry):
            zero_tile_copy(tile).start()
            zero_tile_copy(tile).wait()
            return carry

        lax.fori_loop(used_ref[0], n_tiles, unused_tile, 0)

        def per_expert(e, carry):
            start = pad_start_ref[e]

            def issue(r, c):
                zero_copy(start + r).start()
                return c

            def drain(r, c):
                zero_copy(start).wait()
                return c

            lax.fori_loop(0, pad_len_ref[e], issue, 0)
            lax.fori_loop(0, pad_len_ref[e], drain, 0)
            return carry

        lax.fori_loop(0, N_EXPERTS, per_expert, 0)

    def row_copy(r, slot):
        return pltpu.make_async_copy(_slab_rows(h2_ref, r), _slab_rows(xs_hbm, slot), sems.at[0])

    base = g * td

    def issue(r, c):
        tok = base + r
        row_copy(r, slot_ref[2 * tok]).start()
        row_copy(r, slot_ref[2 * tok + 1]).start()
        return c

    def drain(r, c):
        row_copy(0, 0).wait()
        row_copy(0, 0).wait()
        return c

    lax.fori_loop(0, td, issue, 0)
    lax.fori_loop(0, td, drain, 0)


def _dispatch(slot, pad_start, pad_len, used, h2_rows, n_slots, tm, td=512):
    t = slot.shape[0] // 2
    td = min(td, t)
    kern = functools.partial(_dispatch_kernel, td=td, tm=tm)
    grid_spec = pltpu.PrefetchScalarGridSpec(
        num_scalar_prefetch=4,
        grid=(t // td,),
        in_specs=[pl.BlockSpec((td * ROW_SLABS, LANES), lambda g, sl, ps, pn, us: (g, 0))],
        out_specs=pl.BlockSpec(memory_space=pl.ANY),
        scratch_shapes=[pltpu.VMEM((tm * ROW_SLABS, LANES), F32), pltpu.SemaphoreType.DMA((2,))],
    )
    return pl.pallas_call(
        kern,
        out_shape=jax.ShapeDtypeStruct((n_slots * ROW_SLABS, LANES), F32),
        grid_spec=grid_spec,
        compiler_params=_params("arbitrary"),
        name="dispatch",
    )(slot, pad_start, pad_len, used, h2_rows)


def _expert_kernel(texp_ref, tblk_ref, tused_ref, xs_ref, w1_ref, w3_ref, w2_ref, eo_ref, x_scr, *, tm):
    used = tused_ref[pl.program_id(0)] > 0

    @pl.when(jnp.logical_not(used))
    def _():
        eo_ref[...] = jnp.zeros(eo_ref.shape, F32)

    @pl.when(used)
    def _():
        for s in range(ROW_SLABS):
            x_scr[:, s * LANES:(s + 1) * LANES] = xs_ref[pl.ds(s, tm, stride=ROW_SLABS), :].astype(BF16)
        x = x_scr[...]
        a = jnp.dot(x, w1_ref[...], preferred_element_type=F32)
        b = jnp.dot(x, w3_ref[...], preferred_element_type=F32)
        hid = (_silu(a) * b).astype(BF16)
        y = jnp.dot(hid, w2_ref[...], preferred_element_type=F32)
        for s in range(ROW_SLABS):
            eo_ref[pl.ds(s, tm, stride=ROW_SLABS), :] = y[:, s * LANES:(s + 1) * LANES]


def _experts(tile_expert, tile_block, tile_used, xs_rows, w1, w3, w2, tm):
    n_tiles = tile_expert.shape[0]
    d, f = w1.shape[1], w1.shape[2]
    kern = functools.partial(_expert_kernel, tm=tm)
    grid_spec = pltpu.PrefetchScalarGridSpec(
        num_scalar_prefetch=3,
        grid=(n_tiles,),
        in_specs=[pl.BlockSpec((tm * ROW_SLABS, LANES), lambda i, te, tb, tu: (tb[i], 0)),
                  pl.BlockSpec((None, d, f), lambda i, te, tb, tu: (te[i], 0, 0)),
                  pl.BlockSpec((None, d, f), lambda i, te, tb, tu: (te[i], 0, 0)),
                  pl.BlockSpec((None, f, d), lambda i, te, tb, tu: (te[i], 0, 0))],
        out_specs=pl.BlockSpec((tm * ROW_SLABS, LANES), lambda i, te, tb, tu: (i, 0)),
        scratch_shapes=[pltpu.VMEM((tm, d), BF16)],
    )
    return pl.pallas_call(
        kern,
        out_shape=jax.ShapeDtypeStruct(xs_rows.shape, F32),
        grid_spec=grid_spec,
        compiler_params=_params("arbitrary"),
        name="expert_mlp",
    )(tile_expert, tile_block, tile_used, xs_rows, w1, w3, w2)


def _final_kernel(slot_ref, eo_hbm, rec_ref, x1_ref, gt_ref, g_ref, o_ref, e_scr, sems, *, tf):
    i = pl.program_id(0)
    par = i % 2

    def start_all(step, buf):
        def body(r, c):
            tok = step * tf + r
            for k in range(2):
                pltpu.make_async_copy(_slab_rows(eo_hbm, slot_ref[2 * tok + k]),
                                      _slab_rows(e_scr.at[buf, k], r), sems.at[buf]).start()
            return c
        lax.fori_loop(0, tf, body, 0)

    def wait_all(buf):
        def body(r, c):
            for k in range(2):
                pltpu.make_async_copy(_slab_rows(eo_hbm, 0), _slab_rows(e_scr.at[buf, k], 0), sems.at[buf]).wait()
            return c
        lax.fori_loop(0, tf, body, 0)

    @pl.when(i == 0)
    def _():
        start_all(0, 0)

    @pl.when(i + 1 < pl.num_programs(0))
    def _():
        start_all(i + 1, 1 - par)

    wait_all(par)

    def chunk(c, carry):
        r0 = pl.multiple_of(c * ROW_CHUNK, ROW_CHUNK)
        rows = pl.ds(r0, ROW_CHUNK)
        rec = rec_ref[rows, :]
        w1 = rec[:, ROUTE_W1:ROUTE_W1 + 1]
        w2 = rec[:, ROUTE_W2:ROUTE_W2 + 1]
        slabs = []
        for s in range(ROW_SLABS):
            srows = pl.ds(r0 * ROW_SLABS + s, ROW_CHUNK, stride=ROW_SLABS)
            slabs.append(w1 * e_scr[par, 0, srows, :] + w2 * e_scr[par, 1, srows, :])
        y = jnp.concatenate(slabs, axis=1)
        o_ref[rows, :] = x1_ref[rows, :] + gt_ref[0] * (_rms(y) * g_ref[...])
        return carry

    lax.fori_loop(0, tf // ROW_CHUNK, chunk, 0, unroll=CHUNK_UNROLL)


def _final(slot, eo_rows, rec, x1, gt, g_post, seq, tf=256):
    t, d = x1.shape
    tf = min(tf, seq)
    per_b = seq // tf
    kern = functools.partial(_final_kernel, tf=tf)
    grid_spec = pltpu.PrefetchScalarGridSpec(
        num_scalar_prefetch=1,
        grid=(t // tf,),
        in_specs=[pl.BlockSpec(memory_space=pl.ANY),
                  pl.BlockSpec((tf, LANES), lambda i, sl: (i, 0)),
                  pl.BlockSpec((tf, d), lambda i, sl: (i, 0)),
                  pl.BlockSpec((1, 1, d), lambda i, sl: (i // per_b, 0, 0)),
                  pl.BlockSpec((1, d), lambda i, sl: (0, 0))],
        out_specs=pl.BlockSpec((tf, d), lambda i, sl: (i, 0)),
        scratch_shapes=[pltpu.VMEM((2, 2, tf * ROW_SLABS, LANES), F32),
                        pltpu.SemaphoreType.DMA((2,))],
    )
    return pl.pallas_call(
        kern,
        out_shape=jax.ShapeDtypeStruct((t, d), F32),
        grid_spec=grid_spec,
        compiler_params=_params("arbitrary"),
        name="combine_final",
    )(slot, eo_rows, rec, x1, gt, g_post)


def _dispatch_tables(rec, counts, t, tm):
    e = rec[:, ROUTE_E1:ROUTE_E2 + 1].astype(jnp.int32)
    rank = rec[:, ROUTE_R1:ROUTE_R2 + 1].astype(jnp.int32)
    cnt = counts[0, ROUTER_EXPERT_LANE0:ROUTER_EXPERT_LANE0 + N_EXPERTS].astype(jnp.int32)
    tiles_per = (cnt + tm - 1) // tm
    tile_end = jnp.cumsum(tiles_per)
    tile_start = tile_end - tiles_per
    n_tiles = (2 * t + N_EXPERTS * (tm - 1)) // tm
    slot = (tile_start[e] * tm + rank).reshape(-1)
    pad_start = tile_start * tm + cnt
    pad_len = tiles_per * tm - cnt
    tile_id = jnp.arange(n_tiles, dtype=jnp.int32)
    used = tile_end[-1]
    tblk = jnp.minimum(tile_id, used - 1)
    texp = jnp.searchsorted(tile_end, tblk, side="right").astype(jnp.int32)
    tused = (tile_id < used).astype(jnp.int32)
    return slot, pad_start, pad_len, used.reshape(1), texp, tblk, tused, n_tiles * tm


def kernel(x, c, rel_bias, w_ada, b_ada, g_pre_mix, g_post_mix, w_in, w_alpha, b_alpha, lam_q1, lam_k1, lam_q2,
           lam_k2, g_sub_a, g_norm_b, w_out, g_pre_ffn, g_post_ffn, w_router_g, b_router_g, w_router_e,
           b_router_e, w1, w3, w2):
    batch, seq, d = x.shape
    t = batch * seq
    depth = w_in.shape[0]
    tq = min(512, seq)
    tm_e = 256
    xf = x.reshape(t, d)
    for i in range(depth):
        lam_init = 0.8 - 0.6 * math.exp(-0.3 * i)
        c_pad = jnp.pad(c, ((0, 8 - batch % 8 if batch % 8 else 0), (0, 0)))
        ada = _ada(c_pad, w_ada[i], b_ada[i][None, :])[:batch]
        sh_m, sc_m, gt_m, sh_f, sc_f, gt_f = [a[:, None, :] for a in jnp.split(ada, 6, axis=-1)]

        w_in_b = w_in[i].astype(BF16)
        w_main = w_in_b[:, :D_MAIN]
        w_z = jnp.pad(w_in_b[:, D_MAIN:], ((0, 0), (0, LANES - GATE_RANK)))
        proj, zb = _inproj(xf, g_pre_mix[i][None, :], sc_m, sh_m, w_main, w_z, seq)

        oa = _attention(proj, _bias_tiles(rel_bias, tq), lam_q1[i][None, :], lam_k1[i][None, :],
                        lam_q2[i][None, :], lam_k2[i][None, :], g_sub_a[i][:, None], batch, seq, lam_init, tq)
        w_alpha_pad = jnp.pad(w_alpha[i], ((0, LANES - GATE_RANK), (0, 0)))
        ob = _gla(proj, zb, w_alpha_pad, b_alpha[i][None, :], g_norm_b[i][None, :], batch, seq)

        w_router = jnp.pad(jnp.concatenate([w_router_g[i], w_router_e[i]], axis=1),
                           ((0, 0), (0, LANES - N_GROUPS - N_EXPERTS)))
        b_router = jnp.pad(jnp.concatenate([b_router_g[i], b_router_e[i]]),
                           (0, LANES - N_GROUPS - N_EXPERTS))[None, :]
        x1, h2_rows, logits = _outproj(oa, ob, w_out[i].astype(BF16), xf, gt_m, g_post_mix[i][None, :],
                                       g_pre_ffn[i][None, :], sc_f, sh_f, w_router, b_router, seq)

        rec, counts = _route(logits)
        slot, pad_start, pad_len, used, texp, tblk, tused, n_slots = _dispatch_tables(rec, counts, t, tm_e)
        xs = _dispatch(slot, pad_start, pad_len, used, h2_rows, n_slots, tm_e)
        eo = _experts(texp, tblk, tused, xs, w1[i].astype(BF16), w3[i].astype(BF16), w2[i].astype(BF16), tm_e)
        xf = _final(slot, eo, rec, x1, gt_f, g_post_ffn[i][None, :], seq)
    return xf.reshape(batch, seq, d)
```

```python
import functools
import math

import jax
import jax.numpy as jnp
from jax import lax
from jax.experimental import pallas as pl
from jax.experimental.pallas import tpu as pltpu

F32 = jnp.float32
BF16 = jnp.bfloat16

D_MODEL = 2048
CHUNK = 64
A_HEADS = 8
A_DK = 64
A_DV = 2 * A_DK
A_WIDTH = A_HEADS * A_DV
B_HEADS = 4
B_WIDTH = D_MODEL - A_WIDTH
B_DV = B_WIDTH // B_HEADS
B_DK = B_DV // 2
GATE_RANK = 16
GATE_TAU = 16.0
N_BUCKETS = 32
MAX_DISTANCE = 256
N_GROUPS = 4
EXPERTS_PER_GROUP = 8
N_EXPERTS = N_GROUPS * EXPERTS_PER_GROUP
D_EXPERT = D_MODEL // 4
EPS = 1e-6
NEG_INF = -1e30
LOG2E = math.log2(math.e)

LANES = 128
ROW_SLABS = D_MODEL // LANES
ROW_CHUNK = 16
CHUNK_UNROLL = 4
OUTPROJ_PIECE = 128
D_MAIN = 3 * A_WIDTH + 2 * B_HEADS * B_DK + 2 * B_WIDTH
COL_QA, COL_KA, COL_VA = 0, A_HEADS, 2 * A_HEADS
COL_QB = 3 * A_HEADS
COL_KB = COL_QB + B_HEADS
COL_VB256 = (3 * A_WIDTH + 2 * B_HEADS * B_DK) // B_DV
COL_RB256 = COL_VB256 + B_HEADS
ROUTE_E1, ROUTE_E2, ROUTE_W1, ROUTE_W2, ROUTE_R1, ROUTE_R2 = 0, 1, 2, 3, 4, 5
ROUTER_EXPERT_LANE0 = N_GROUPS

VMEM_LIMIT = 56 * 1024 * 1024


def _params(*sem):
    return pltpu.CompilerParams(dimension_semantics=sem, vmem_limit_bytes=VMEM_LIMIT)


def _rms(v):
    return v * lax.rsqrt(jnp.mean(v * v, axis=-1, keepdims=True) + EPS)


def _silu(v):
    return v * jax.nn.sigmoid(v)


def _ada_kernel(c_ref, w_ref, b_ref, o_ref):
    s = _silu(c_ref[...])
    o_ref[...] = jnp.dot(s.astype(BF16), w_ref[...].astype(BF16), preferred_element_type=F32) + b_ref[...]


def _ada(c_pad, w, b, tn=1024):
    m, d = c_pad.shape
    n = w.shape[1]
    return pl.pallas_call(
        _ada_kernel,
        out_shape=jax.ShapeDtypeStruct((m, n), F32),
        grid=(n // tn,),
        in_specs=[pl.BlockSpec((m, d), lambda j: (0, 0)),
                  pl.BlockSpec((d, tn), lambda j: (0, j)),
                  pl.BlockSpec((1, tn), lambda j: (0, j))],
        out_specs=pl.BlockSpec((m, tn), lambda j: (0, j)),
        compiler_params=_params("arbitrary"),
        name="ada_proj",
    )(c_pad, w, b)


def _inproj_kernel(x_ref, g_ref, sc_ref, sh_ref, w_ref, wz_ref, o_ref, z_ref, h_scr):
    @pl.when(pl.program_id(1) == 0)
    def _():
        h = _rms(x_ref[...]) * g_ref[...]
        h = h * (1.0 + sc_ref[0]) + sh_ref[0]
        h_scr[...] = h.astype(BF16)
        z_ref[...] = jnp.dot(h_scr[...], wz_ref[...], preferred_element_type=F32)

    o_ref[...] = jnp.dot(h_scr[...], w_ref[...], preferred_element_type=F32).astype(BF16)


def _inproj(x2d, g, sc, sh, w_main, w_z, seq, tm=1024, tn=512):
    t, d = x2d.shape
    tm = min(tm, seq)
    n = w_main.shape[1]
    per_b = seq // tm
    return pl.pallas_call(
        _inproj_kernel,
        out_shape=(jax.ShapeDtypeStruct((t, n), BF16), jax.ShapeDtypeStruct((t, LANES), F32)),
        grid=(t // tm, n // tn),
        in_specs=[pl.BlockSpec((tm, d), lambda i, j: (i, 0)),
                  pl.BlockSpec((1, d), lambda i, j: (0, 0)),
                  pl.BlockSpec((1, 1, d), lambda i, j: (i // per_b, 0, 0)),
                  pl.BlockSpec((1, 1, d), lambda i, j: (i // per_b, 0, 0)),
                  pl.BlockSpec((d, tn), lambda i, j: (0, j)),
                  pl.BlockSpec((d, LANES), lambda i, j: (0, 0))],
        out_specs=(pl.BlockSpec((tm, tn), lambda i, j: (i, j)),
                   pl.BlockSpec((tm, LANES), lambda i, j: (i, 0))),
        scratch_shapes=[pltpu.VMEM((tm, d), BF16)],
        compiler_params=_params("arbitrary", "arbitrary"),
        name="in_proj",
    )(x2d, g, sc, sh, w_main, w_z)


def _t5_bucket(rel):
    nb = N_BUCKETS // 2
    max_exact = nb // 2
    base = jnp.where(rel > 0, nb, 0)
    n = jnp.abs(rel)
    nf = jnp.maximum(n, 1).astype(F32)
    large = max_exact + (jnp.log(nf / max_exact) / math.log(MAX_DISTANCE / max_exact)
                         * (nb - max_exact)).astype(jnp.int32)
    large = jnp.minimum(large, nb - 1)
    return base + jnp.where(n < max_exact, n, large)


def _bias_buckets(tq):
    kj = jnp.arange(tq, dtype=jnp.int32)[:, None]
    qi = jnp.arange(tq, dtype=jnp.int32)[None, :]
    near = _t5_bucket(kj - qi - tq)
    diag = jnp.where((kj // CHUNK) <= (qi // CHUNK), _t5_bucket(kj - qi), N_BUCKETS)
    return jnp.stack([near, diag]).astype(jnp.int32)


def _bias_kernel(rb_ref, bk_ref, o_ref):
    h = pl.program_id(0)
    far = rb_ref[N_BUCKETS // 2 - 1, h]
    bucket = bk_ref[...]
    acc = jnp.full(bucket.shape, NEG_INF, F32)
    for n in range(N_BUCKETS):
        acc = jnp.where(bucket == n, (rb_ref[n, h] - far) * LOG2E, acc)
    o_ref[...] = acc


def _bias_tiles(rel_bias, tq):
    return pl.pallas_call(
        _bias_kernel,
        out_shape=jax.ShapeDtypeStruct((A_HEADS, 2, tq, tq), F32),
        grid=(A_HEADS, 2),
        in_specs=[pl.BlockSpec(memory_space=pltpu.SMEM),
                  pl.BlockSpec((None, tq, tq), lambda h, d: (d, 0, 0))],
        out_specs=pl.BlockSpec((None, None, tq, tq), lambda h, d: (h, d, 0, 0)),
        compiler_params=_params("arbitrary", "arbitrary"),
        name="bias_tiles",
    )(rel_bias, _bias_buckets(tq))


def _attn_kernel(q_ref, k_ref, v_ref, bias_ref, lq1_ref, lk1_ref, lq2_ref, lk2_ref, g_ref, o_ref,
                 vt_scr, sa_scr, sb_scr, m_scr, l_scr, acc_scr, *, tq, tk, lam_init):
    i = pl.program_id(2)
    nsub = tq // tk
    bufs = (sa_scr, sb_scr)

    @pl.when(i == 0)
    def _():
        for c in range(vt_scr.shape[0]):
            vt_scr[c] = v_ref[c * tk:(c + 1) * tk, :].astype(F32).T.astype(BF16)

    lane = lax.broadcasted_iota(jnp.int32, (1, A_DV), 1)
    q = q_ref[...] * (A_DK ** -0.5 * LOG2E)
    zero = jnp.zeros_like(q)
    q2 = jnp.concatenate([jnp.where(lane < A_DK, q, zero), jnp.where(lane >= A_DK, q, zero)], axis=0)

    m_scr[...] = jnp.full(m_scr.shape, NEG_INF, F32)
    l_scr[...] = jnp.zeros(l_scr.shape, F32)
    acc_scr[...] = jnp.zeros(acc_scr.shape, F32)

    def scores(j):
        k = k_ref[pl.ds(pl.multiple_of(j * tk, tk), tk), :]
        return lax.dot_general(k, q2, (((1,), (1,)), ((), ())), preferred_element_type=F32)

    def softmax_pv(s_ref, j, bias):
        s = s_ref[...]
        if bias is not None:
            s = jnp.concatenate([s[:, :tq] + bias, s[:, tq:] + bias], axis=1)
        m_old = m_scr[...]
        m_new = jnp.maximum(m_old, jnp.max(s, axis=0, keepdims=True))
        alpha = jnp.exp2(m_old - m_new)
        p = jnp.exp2(s - m_new)
        l_scr[...] = alpha * l_scr[...] + jnp.sum(p, axis=0, keepdims=True)
        acc_scr[...] = alpha * acc_scr[...] + jnp.dot(vt_scr[j], p.astype(BF16), preferred_element_type=F32)
        m_scr[...] = m_new

    n_far = jnp.maximum(i - 1, 0) * nsub
    sa_scr[...] = scores(0)

    def far_pair(jj, carry):
        j = 2 * jj
        sb_scr[...] = scores(j + 1)
        softmax_pv(sa_scr, j, None)
        sa_scr[...] = scores(j + 2)
        softmax_pv(sb_scr, j + 1, None)
        return carry

    lax.fori_loop(0, n_far // 2, far_pair, 0)

    @pl.when(i >= 1)
    def _():
        for c in range(nsub):
            bufs[(c + 1) % 2][...] = scores(n_far + c + 1)
            softmax_pv(bufs[c % 2], n_far + c, bias_ref[0, c * tk:(c + 1) * tk, :])

    for c in range(nsub):
        if c + 1 < nsub:
            bufs[(c + 1) % 2][...] = scores(i * nsub + c + 1)
        softmax_pv(bufs[c % 2], i * nsub + c, bias_ref[1, c * tk:(c + 1) * tk, :])

    lam = (jnp.exp(jnp.sum(lq1_ref[...] * lk1_ref[...], axis=-1, keepdims=True))
           - jnp.exp(jnp.sum(lq2_ref[...] * lk2_ref[...], axis=-1, keepdims=True)) + lam_init)
    on = acc_scr[...] / l_scr[...]
    o = on[:, :tq] - lam * on[:, tq:]
    y = o * lax.rsqrt(jnp.mean(o * o, axis=0, keepdims=True) + EPS) * g_ref[...] * (1.0 - lam_init)
    o_ref[...] = y.T.astype(BF16)


def _attention(proj, bias_tiles, lq1, lk1, lq2, lk2, g_sub_col, batch, seq, lam_init, tq, tk=256):
    t = proj.shape[0]
    nq = seq // tq
    assert (tq // tk) % 2 == 0 and tq % tk == 0, "the score pipeline alternates two buffers per query tile"
    kern = functools.partial(_attn_kernel, tq=tq, tk=tk, lam_init=lam_init)
    vec = lambda n: pl.BlockSpec((1, n), lambda b, h, i: (0, 0))
    return pl.pallas_call(
        kern,
        out_shape=jax.ShapeDtypeStruct((t, A_WIDTH), BF16),
        grid=(batch, A_HEADS, nq),
        in_specs=[pl.BlockSpec((tq, A_DV), lambda b, h, i: (b * nq + i, COL_QA + h)),
                  pl.BlockSpec((seq, A_DV), lambda b, h, i: (b, COL_KA + h)),
                  pl.BlockSpec((seq, A_DV), lambda b, h, i: (b, COL_VA + h)),
                  pl.BlockSpec((None, 2, tq, tq), lambda b, h, i: (h, 0, 0, 0)),
                  vec(A_DK), vec(A_DK), vec(A_DK), vec(A_DK),
                  pl.BlockSpec((A_DV, 1), lambda b, h, i: (0, 0))],
        out_specs=pl.BlockSpec((tq, A_DV), lambda b, h, i: (b * nq + i, h)),
        scratch_shapes=[pltpu.VMEM((seq // tk, A_DV, tk), BF16),
                        pltpu.VMEM((tk, 2 * tq), F32), pltpu.VMEM((tk, 2 * tq), F32),
                        pltpu.VMEM((1, 2 * tq), F32), pltpu.VMEM((1, 2 * tq), F32),
                        pltpu.VMEM((A_DV, 2 * tq), F32)],
        compiler_params=_params("arbitrary", "arbitrary", "arbitrary"),
        name="diff_attention",
    )(proj, proj, proj, bias_tiles, lq1, lk1, lq2, lk2, g_sub_col)


def _split3(a):
    a1 = a.astype(BF16)
    r1 = a - a1.astype(F32)
    a2 = r1.astype(BF16)
    return a1, a2, (r1 - a2.astype(F32)).astype(BF16)


def _sum3(x):
    return x[:, :B_DK] + x[:, B_DK:2 * B_DK] + x[:, 2 * B_DK:]


def _gla_kernel(q_ref, k_ref, v_ref, r_ref, z_ref, wa_ref, ba_ref, g_ref, o_ref,
                state_scr, mask_scr, kv_scr, st_scr, *, n_chunks):
    lc = n_chunks * CHUNK

    @pl.when(pl.program_id(2) == 0)
    def _():
        state_scr[...] = jnp.zeros(state_scr.shape, F32)
        row = lax.broadcasted_iota(jnp.int32, (lc, lc), 0)
        col = lax.broadcasted_iota(jnp.int32, (lc, lc), 1)
        same = (row // CHUNK) == (col // CHUNK)
        mask_scr[...] = (same & (row >= col)).astype(BF16)

    z = z_ref[...]
    zh = z.astype(BF16)
    zl = (z - zh.astype(F32)).astype(BF16)
    pre = jnp.dot(jnp.concatenate([zh, zl, zh], axis=1), wa_ref[...], preferred_element_type=F32) + ba_ref[...]
    log_a = (jnp.minimum(pre, 0.0) - jnp.log1p(jnp.exp(-jnp.abs(pre)))) * (1.0 / GATE_TAU)
    parts = jnp.concatenate(_split3(log_a), axis=1)
    cum = _sum3(jnp.dot(mask_scr[...], parts, preferred_element_type=F32))
    totals = [cum[(c + 1) * CHUNK - 1:(c + 1) * CHUNK, :] for c in range(n_chunks)]
    total = jnp.concatenate([jnp.broadcast_to(tc, (CHUNK, B_DK)) for tc in totals], axis=0)
    k_dec = (k_ref[...].astype(F32) * jnp.exp(total - cum)).astype(BF16)

    for c in range(n_chunks):
        rows = slice(c * CHUNK, (c + 1) * CHUNK)
        kv_scr[c] = lax.dot_general(v_ref[rows, :], k_dec[rows], (((0,), (0,)), ((), ())),
                                    preferred_element_type=F32)
    state = state_scr[...]
    for c in range(n_chunks):
        state = state * jnp.exp(totals[c]) + kv_scr[c]
        st_scr[c] = state.astype(BF16)
    state_scr[...] = state

    for c in range(n_chunks):
        rows = slice(c * CHUNK, (c + 1) * CHUNK)
        o = lax.dot_general(q_ref[rows, :], st_scr[c], (((1,), (1,)), ((), ())),
                            preferred_element_type=F32) * (B_DK ** -0.5)
        o_ref[rows, :] = (_rms(o) * g_ref[...] * _silu(r_ref[rows, :].astype(F32))).astype(BF16)


def _gla(proj, zb, w_alpha_pad, b_alpha, g_norm, batch, seq, lc=512):
    t = proj.shape[0]
    lc = min(lc, seq)
    nl = seq // lc
    n_chunks = lc // CHUNK
    kern = functools.partial(_gla_kernel, n_chunks=n_chunks)
    wa_hi = w_alpha_pad.astype(BF16)
    wa_lo = (w_alpha_pad - wa_hi.astype(F32)).astype(BF16)
    wa3 = jnp.concatenate([wa_hi, wa_hi, wa_lo], axis=0)
    return pl.pallas_call(
        kern,
        out_shape=jax.ShapeDtypeStruct((t, B_WIDTH), BF16),
        grid=(batch, B_HEADS, nl),
        in_specs=[pl.BlockSpec((lc, B_DK), lambda b, h, l: (b * nl + l, COL_QB + h)),
                  pl.BlockSpec((lc, B_DK), lambda b, h, l: (b * nl + l, COL_KB + h)),
                  pl.BlockSpec((lc, B_DV), lambda b, h, l: (b * nl + l, COL_VB256 + h)),
                  pl.BlockSpec((lc, B_DV), lambda b, h, l: (b * nl + l, COL_RB256 + h)),
                  pl.BlockSpec((lc, LANES), lambda b, h, l: (b * nl + l, 0)),
                  pl.BlockSpec((3 * LANES, B_DK), lambda b, h, l: (0, h)),
                  pl.BlockSpec((1, B_DK), lambda b, h, l: (0, h)),
                  pl.BlockSpec((1, B_DV), lambda b, h, l: (0, 0))],
        out_specs=pl.BlockSpec((lc, B_DV), lambda b, h, l: (b * nl + l, h)),
        scratch_shapes=[pltpu.VMEM((B_DV, B_DK), F32),
                        pltpu.VMEM((lc, lc), BF16),
                        pltpu.VMEM((n_chunks, B_DV, B_DK), F32),
                        pltpu.VMEM((n_chunks, B_DV, B_DK), BF16)],
        compiler_params=_params("arbitrary", "arbitrary", "arbitrary"),
        name="gla",
    )(proj, proj, proj, proj, zb, wa3, b_alpha, g_norm)


def _outproj_kernel(oa_ref, ob_ref, wo_ref, x_ref, gt_ref, gpost_ref, gpre_ref, sc_ref, sh_ref, wrh_ref, wrl_ref,
                    br_ref, x1_ref, h2_ref, lg_ref, *, tm):
    for p in range(tm // OUTPROJ_PIECE):
        p0 = p * OUTPROJ_PIECE
        prow = slice(p0, p0 + OUTPROJ_PIECE)
        y = (jnp.dot(oa_ref[prow, :], wo_ref[:A_WIDTH, :], preferred_element_type=F32)
             + jnp.dot(ob_ref[prow, :], wo_ref[A_WIDTH:, :], preferred_element_type=F32))
        his, los = [], []
        for c in range(OUTPROJ_PIECE // ROW_CHUNK):
            r0 = p0 + c * ROW_CHUNK
            rows = slice(r0, r0 + ROW_CHUNK)
            x1 = x_ref[rows, :] + gt_ref[0] * (_rms(y[c * ROW_CHUNK:(c + 1) * ROW_CHUNK]) * gpost_ref[...])
            x1_ref[rows, :] = x1
            h2 = (_rms(x1) * gpre_ref[...]) * (1.0 + sc_ref[0]) + sh_ref[0]
            hi = h2.astype(BF16)
            his.append(hi)
            los.append((h2 - hi.astype(F32)).astype(BF16))
            for s in range(ROW_SLABS):
                h2_ref[pl.ds(r0 * ROW_SLABS + s, ROW_CHUNK, stride=ROW_SLABS), :] = h2[:, s * LANES:(s + 1) * LANES]
        hi, lo = jnp.concatenate(his, axis=0), jnp.concatenate(los, axis=0)
        lg_ref[prow, :] = (jnp.dot(hi, wrh_ref[...], preferred_element_type=F32)
                           + jnp.dot(lo, wrh_ref[...], preferred_element_type=F32)
                           + jnp.dot(hi, wrl_ref[...], preferred_element_type=F32) + br_ref[...])


def _outproj(oa, ob, w_out, x2d, gt, g_post, g_pre, sc, sh, w_router, b_router, seq, tm=512):
    t, d = x2d.shape
    tm = min(tm, seq)
    per_b = seq // tm
    kern = functools.partial(_outproj_kernel, tm=tm)
    wr_hi = w_router.astype(BF16)
    wr_lo = (w_router - wr_hi.astype(F32)).astype(BF16)
    row = lambda: pl.BlockSpec((1, d), lambda i: (0, 0))
    per_batch = lambda: pl.BlockSpec((1, 1, d), lambda i: (i // per_b, 0, 0))
    return pl.pallas_call(
        kern,
        out_shape=(jax.ShapeDtypeStruct((t, d), F32),
                   jax.ShapeDtypeStruct((t * ROW_SLABS, LANES), F32),
                   jax.ShapeDtypeStruct((t, LANES), F32)),
        grid=(t // tm,),
        in_specs=[pl.BlockSpec((tm, A_WIDTH), lambda i: (i, 0)),
                  pl.BlockSpec((tm, B_WIDTH), lambda i: (i, 0)),
                  pl.BlockSpec((d, d), lambda i: (0, 0)),
                  pl.BlockSpec((tm, d), lambda i: (i, 0)),
                  per_batch(), row(), row(), per_batch(), per_batch(),
                  pl.BlockSpec((d, LANES), lambda i: (0, 0)),
                  pl.BlockSpec((d, LANES), lambda i: (0, 0)),
                  pl.BlockSpec((1, LANES), lambda i: (0, 0))],
        out_specs=(pl.BlockSpec((tm, d), lambda i: (i, 0)),
                   pl.BlockSpec((tm * ROW_SLABS, LANES), lambda i: (i, 0)),
                   pl.BlockSpec((tm, LANES), lambda i: (i, 0))),
        compiler_params=_params("arbitrary"),
        name="out_proj",
    )(oa, ob, w_out, x2d, gt, g_post, g_pre, sc, sh, wr_hi, wr_lo, b_router)


def _route_kernel(lg_ref, rec_ref, cnt_ref, carry_scr, *, tr):
    @pl.when(pl.program_id(0) == 0)
    def _():
        carry_scr[...] = jnp.zeros(carry_scr.shape, F32)

    lg = lg_ref[...]
    lane = lax.broadcasted_iota(jnp.int32, lg.shape, 1)
    big = jnp.int32(LANES)

    def first_lane(mask):
        return jnp.min(jnp.where(mask, lane, big), axis=-1, keepdims=True)

    gmask = lane < N_GROUPS
    gmax = jnp.max(jnp.where(gmask, lg, -jnp.inf), axis=-1, keepdims=True)
    gexp = jnp.where(gmask, jnp.exp(lg - gmax), 0.0)
    gprob = gexp / jnp.sum(gexp, axis=-1, keepdims=True)
    g_val = jnp.max(gprob, axis=-1, keepdims=True)
    g_idx = first_lane(gmask & (gprob == g_val))

    lo = ROUTER_EXPERT_LANE0 + g_idx * EXPERTS_PER_GROUP
    emask = (lane >= lo) & (lane < lo + EXPERTS_PER_GROUP)
    emax = jnp.max(jnp.where(emask, lg, -jnp.inf), axis=-1, keepdims=True)
    eexp = jnp.where(emask, jnp.exp(lg - emax), 0.0)
    eprob = eexp / jnp.sum(eexp, axis=-1, keepdims=True)
    v1 = jnp.max(eprob, axis=-1, keepdims=True)
    i1 = first_lane(emask & (eprob == v1))
    rest = emask & (lane != i1)
    v2 = jnp.max(jnp.where(rest, eprob, -1.0), axis=-1, keepdims=True)
    i2 = first_lane(rest & (eprob == v2))
    w1 = g_val * (v1 / (v1 + v2))
    w2 = g_val * (v2 / (v1 + v2))

    hit1 = lane == i1
    hit2 = lane == i2
    onehot = (hit1 | hit2).astype(BF16)
    r = lax.broadcasted_iota(jnp.int32, (tr, tr), 0)
    c = lax.broadcasted_iota(jnp.int32, (tr, tr), 1)
    before = (c < r).astype(BF16)
    pos = carry_scr[...] + jnp.dot(before, onehot, preferred_element_type=F32)
    rank1 = jnp.sum(jnp.where(hit1, pos, 0.0), axis=-1, keepdims=True)
    rank2 = jnp.sum(jnp.where(hit2, pos, 0.0), axis=-1, keepdims=True)
    carry_scr[...] = carry_scr[...] + jnp.sum(onehot.astype(F32), axis=0, keepdims=True)
    cnt_ref[...] = carry_scr[...]

    e1 = (i1 - ROUTER_EXPERT_LANE0).astype(F32)
    e2 = (i2 - ROUTER_EXPERT_LANE0).astype(F32)
    rec = jnp.zeros(lg.shape, F32)
    for ln, val in ((ROUTE_E1, e1), (ROUTE_E2, e2), (ROUTE_W1, w1), (ROUTE_W2, w2),
                    (ROUTE_R1, rank1), (ROUTE_R2, rank2)):
        rec = jnp.where(lane == ln, val, rec)
    rec_ref[...] = rec


def _route(logits, tr=512):
    t = logits.shape[0]
    tr = min(tr, t)
    kern = functools.partial(_route_kernel, tr=tr)
    return pl.pallas_call(
        kern,
        out_shape=(jax.ShapeDtypeStruct((t, LANES), F32), jax.ShapeDtypeStruct((1, LANES), F32)),
        grid=(t // tr,),
        in_specs=[pl.BlockSpec((tr, LANES), lambda i: (i, 0))],
        out_specs=(pl.BlockSpec((tr, LANES), lambda i: (i, 0)),
                   pl.BlockSpec((1, LANES), lambda i: (0, 0))),
        scratch_shapes=[pltpu.VMEM((1, LANES), F32)],
        compiler_params=_params("arbitrary"),
        name="route",
    )(logits)


def _slab_rows(ref, row):
    return ref.at[pl.ds(pl.multiple_of(row * ROW_SLABS, ROW_SLABS), ROW_SLABS), :]


def _dispatch_kernel(slot_ref, pad_start_ref, pad_len_ref, used_ref, h2_ref, xs_hbm, zero_scr, sems, *, td, tm):
    g = pl.program_id(0)
    tile_rows = tm * ROW_SLABS
    n_tiles = xs_hbm.shape[0] // tile_rows

    def zero_copy(slot):
        return pltpu.make_async_copy(zero_scr.at[pl.ds(0, ROW_SLABS), :], _slab_rows(xs_hbm, slot), sems.at[1])

    def zero_tile_copy(tile):
        rows = pl.ds(pl.multiple_of(tile * tile_rows, tile_rows), tile_rows)
        return pltpu.make_async_copy(zero_scr, xs_hbm.at[rows, :], sems.at[1])

    @pl.when(g == 0)
    def _():
        zero_scr[...] = jnp.zeros(zero_scr.shape, F32)

        def unused_tile(tile, carry):
            zero_tile_copy(tile).start()
            zero_tile_copy(tile).wait()
            return carry

        lax.fori_loop(used_ref[0], n_tiles, unused_tile, 0)

        def per_expert(e, carry):
            start = pad_start_ref[e]

            def issue(r, c):
                zero_copy(start + r).start()
                return c

            def drain(r, c):
                zero_copy(start).wait()
                return c

            lax.fori_loop(0, pad_len_ref[e], issue, 0)
            lax.fori_loop(0, pad_len_ref[e], drain, 0)
            return carry

        lax.fori_loop(0, N_EXPERTS, per_expert, 0)

    def row_copy(r, slot):
        return pltpu.make_async_copy(_slab_rows(h2_ref, r), _slab_rows(xs_hbm, slot), sems.at[0])

    base = g * td

    def issue(r, c):
        tok = base + r
        row_copy(r, slot_ref[2 * tok]).start()
        row_copy(r, slot_ref[2 * tok + 1]).start()
        return c

    lax.fori_loop(0, td, issue, 0, unroll=CHUNK_UNROLL)
    for _ in range(2):
        pltpu.make_async_copy(h2_ref, xs_hbm.at[pl.ds(0, td * ROW_SLABS), :], sems.at[0]).wait()


def _dispatch(slot, pad_start, pad_len, used, h2_rows, n_slots, tm, td=512):
    t = slot.shape[0] // 2
    td = min(td, t)
    kern = functools.partial(_dispatch_kernel, td=td, tm=tm)
    grid_spec = pltpu.PrefetchScalarGridSpec(
        num_scalar_prefetch=4,
        grid=(t // td,),
        in_specs=[pl.BlockSpec((td * ROW_SLABS, LANES), lambda g, sl, ps, pn, us: (g, 0))],
        out_specs=pl.BlockSpec(memory_space=pl.ANY),
        scratch_shapes=[pltpu.VMEM((tm * ROW_SLABS, LANES), F32), pltpu.SemaphoreType.DMA((2,))],
    )
    return pl.pallas_call(
        kern,
        out_shape=jax.ShapeDtypeStruct((n_slots * ROW_SLABS, LANES), F32),
        grid_spec=grid_spec,
        compiler_params=_params("arbitrary"),
        name="dispatch",
    )(slot, pad_start, pad_len, used, h2_rows)


TILE_UNUSED, TILE_USED, TILE_NEW_EXPERT = 0, 1, 2


def _expert_kernel(texp_ref, tblk_ref, tstate_ref, xs_ref, w1_ref, w3_ref, w2_ref, eo_ref,
                   x_scr, w1_scr, w3_scr, w2_scr, *, tm):
    state = tstate_ref[pl.program_id(0)]

    @pl.when(state == TILE_UNUSED)
    def _():
        eo_ref[...] = jnp.zeros(eo_ref.shape, F32)

    @pl.when(state == TILE_NEW_EXPERT)
    def _():
        w1_scr[...] = w1_ref[...].astype(BF16)
        w3_scr[...] = w3_ref[...].astype(BF16)
        w2_scr[...] = w2_ref[...].astype(BF16)

    @pl.when(state != TILE_UNUSED)
    def _():
        for s in range(ROW_SLABS):
            x_scr[:, s * LANES:(s + 1) * LANES] = xs_ref[pl.ds(s, tm, stride=ROW_SLABS), :].astype(BF16)
        x = x_scr[...]
        a = jnp.dot(x, w1_scr[...], preferred_element_type=F32)
        b = jnp.dot(x, w3_scr[...], preferred_element_type=F32)
        hid = (_silu(a) * b).astype(BF16)
        y = jnp.dot(hid, w2_scr[...], preferred_element_type=F32)
        for s in range(ROW_SLABS):
            eo_ref[pl.ds(s, tm, stride=ROW_SLABS), :] = y[:, s * LANES:(s + 1) * LANES]


def _experts(tile_expert, tile_block, tile_used, xs_rows, w1, w3, w2, tm):
    n_tiles = tile_expert.shape[0]
    d, f = w1.shape[1], w1.shape[2]
    kern = functools.partial(_expert_kernel, tm=tm)
    grid_spec = pltpu.PrefetchScalarGridSpec(
        num_scalar_prefetch=3,
        grid=(n_tiles,),
        in_specs=[pl.BlockSpec((tm * ROW_SLABS, LANES), lambda i, te, tb, tu: (tb[i], 0)),
                  pl.BlockSpec((None, d, f), lambda i, te, tb, tu: (te[i], 0, 0)),
                  pl.BlockSpec((None, d, f), lambda i, te, tb, tu: (te[i], 0, 0)),
                  pl.BlockSpec((None, f, d), lambda i, te, tb, tu: (te[i], 0, 0))],
        out_specs=pl.BlockSpec((tm * ROW_SLABS, LANES), lambda i, te, tb, tu: (i, 0)),
        scratch_shapes=[pltpu.VMEM((tm, d), BF16), pltpu.VMEM((d, f), BF16), pltpu.VMEM((d, f), BF16),
                        pltpu.VMEM((f, d), BF16)],
    )
    return pl.pallas_call(
        kern,
        out_shape=jax.ShapeDtypeStruct(xs_rows.shape, F32),
        grid_spec=grid_spec,
        compiler_params=_params("arbitrary"),
        name="expert_mlp",
    )(tile_expert, tile_block, tile_used, xs_rows, w1, w3, w2)


def _final_kernel(slot_ref, eo_hbm, rec_ref, x1_ref, gt_ref, g_ref, o_ref, e_scr, sems, *, tf):
    i = pl.program_id(0)
    par = i % 2

    def start_all(step, buf):
        def body(r, c):
            tok = step * tf + r
            for k in range(2):
                pltpu.make_async_copy(_slab_rows(eo_hbm, slot_ref[2 * tok + k]),
                                      _slab_rows(e_scr.at[buf, k], r), sems.at[buf]).start()
            return c
        lax.fori_loop(0, tf, body, 0, unroll=CHUNK_UNROLL)

    def wait_all(buf):
        for k in range(2):
            pltpu.make_async_copy(eo_hbm.at[pl.ds(0, tf * ROW_SLABS), :], e_scr.at[buf, k], sems.at[buf]).wait()

    @pl.when(i == 0)
    def _():
        start_all(0, 0)

    @pl.when(i + 1 < pl.num_programs(0))
    def _():
        start_all(i + 1, 1 - par)

    wait_all(par)

    def chunk(c, carry):
        r0 = pl.multiple_of(c * ROW_CHUNK, ROW_CHUNK)
        rows = pl.ds(r0, ROW_CHUNK)
        rec = rec_ref[rows, :]
        w1 = rec[:, ROUTE_W1:ROUTE_W1 + 1]
        w2 = rec[:, ROUTE_W2:ROUTE_W2 + 1]
        slabs = []
        for s in range(ROW_SLABS):
            srows = pl.ds(r0 * ROW_SLABS + s, ROW_CHUNK, stride=ROW_SLABS)
            slabs.append(w1 * e_scr[par, 0, srows, :] + w2 * e_scr[par, 1, srows, :])
        y = jnp.concatenate(slabs, axis=1)
        o_ref[rows, :] = x1_ref[rows, :] + gt_ref[0] * (_rms(y) * g_ref[...])
        return carry

    lax.fori_loop(0, tf // ROW_CHUNK, chunk, 0, unroll=CHUNK_UNROLL)


def _final(slot, eo_rows, rec, x1, gt, g_post, seq, tf=256):
    t, d = x1.shape
    tf = min(tf, seq)
    per_b = seq // tf
    kern = functools.partial(_final_kernel, tf=tf)
    grid_spec = pltpu.PrefetchScalarGridSpec(
        num_scalar_prefetch=1,
        grid=(t // tf,),
        in_specs=[pl.BlockSpec(memory_space=pl.ANY),
                  pl.BlockSpec((tf, LANES), lambda i, sl: (i, 0)),
                  pl.BlockSpec((tf, d), lambda i, sl: (i, 0)),
                  pl.BlockSpec((1, 1, d), lambda i, sl: (i // per_b, 0, 0)),
                  pl.BlockSpec((1, d), lambda i, sl: (0, 0))],
        out_specs=pl.BlockSpec((tf, d), lambda i, sl: (i, 0)),
        scratch_shapes=[pltpu.VMEM((2, 2, tf * ROW_SLABS, LANES), F32),
                        pltpu.SemaphoreType.DMA((2,))],
    )
    return pl.pallas_call(
        kern,
        out_shape=jax.ShapeDtypeStruct((t, d), F32),
        grid_spec=grid_spec,
        compiler_params=_params("arbitrary"),
        name="combine_final",
    )(slot, eo_rows, rec, x1, gt, g_post)


def _dispatch_tables(rec, counts, t, tm):
    e = rec[:, ROUTE_E1:ROUTE_E2 + 1].astype(jnp.int32)
    rank = rec[:, ROUTE_R1:ROUTE_R2 + 1].astype(jnp.int32)
    cnt = counts[0, ROUTER_EXPERT_LANE0:ROUTER_EXPERT_LANE0 + N_EXPERTS].astype(jnp.int32)
    tiles_per = (cnt + tm - 1) // tm
    tile_end = jnp.cumsum(tiles_per)
    tile_start = tile_end - tiles_per
    n_tiles = (2 * t + N_EXPERTS * (tm - 1)) // tm
    slot = (tile_start[e] * tm + rank).reshape(-1)
    pad_start = tile_start * tm + cnt
    pad_len = tiles_per * tm - cnt
    tile_id = jnp.arange(n_tiles, dtype=jnp.int32)
    used = tile_end[-1]
    tblk = jnp.minimum(tile_id, used - 1)
    texp = jnp.searchsorted(tile_end, tblk, side="right").astype(jnp.int32)
    tstate = jnp.where(tile_id < used, jnp.where(tile_id == tile_start[texp], TILE_NEW_EXPERT, TILE_USED),
                       TILE_UNUSED).astype(jnp.int32)
    return slot, pad_start, pad_len, used.reshape(1), texp, tblk, tstate, n_tiles * tm


def kernel(x, c, rel_bias, w_ada, b_ada, g_pre_mix, g_post_mix, w_in, w_alpha, b_alpha, lam_q1, lam_k1, lam_q2,
           lam_k2, g_sub_a, g_norm_b, w_out, g_pre_ffn, g_post_ffn, w_router_g, b_router_g, w_router_e,
           b_router_e, w1, w3, w2):
    batch, seq, d = x.shape
    t = batch * seq
    depth = w_in.shape[0]
    tq = min(512, seq)
    tm_e = 256
    xf = x.reshape(t, d)
    for i in range(depth):
        lam_init = 0.8 - 0.6 * math.exp(-0.3 * i)
        c_pad = jnp.pad(c, ((0, 8 - batch % 8 if batch % 8 else 0), (0, 0)))
        ada = _ada(c_pad, w_ada[i], b_ada[i][None, :])[:batch]
        sh_m, sc_m, gt_m, sh_f, sc_f, gt_f = [a[:, None, :] for a in jnp.split(ada, 6, axis=-1)]

        w_in_b = w_in[i].astype(BF16)
        w_main = w_in_b[:, :D_MAIN]
        w_z = jnp.pad(w_in_b[:, D_MAIN:], ((0, 0), (0, LANES - GATE_RANK)))
        proj, zb = _inproj(xf, g_pre_mix[i][None, :], sc_m, sh_m, w_main, w_z, seq)

        oa = _attention(proj, _bias_tiles(rel_bias, tq), lam_q1[i][None, :], lam_k1[i][None, :],
                        lam_q2[i][None, :], lam_k2[i][None, :], g_sub_a[i][:, None], batch, seq, lam_init, tq)
        w_alpha_pad = jnp.pad(w_alpha[i], ((0, LANES - GATE_RANK), (0, 0)))
        ob = _gla(proj, zb, w_alpha_pad, b_alpha[i][None, :], g_norm_b[i][None, :], batch, seq)

        w_router = jnp.pad(jnp.concatenate([w_router_g[i], w_router_e[i]], axis=1),
                           ((0, 0), (0, LANES - N_GROUPS - N_EXPERTS)))
        b_router = jnp.pad(jnp.concatenate([b_router_g[i], b_router_e[i]]),
                           (0, LANES - N_GROUPS - N_EXPERTS))[None, :]
        x1, h2_rows, logits = _outproj(oa, ob, w_out[i].astype(BF16), xf, gt_m, g_post_mix[i][None, :],
                                       g_pre_ffn[i][None, :], sc_f, sh_f, w_router, b_router, seq)

        rec, counts = _route(logits)
        slot, pad_start, pad_len, used, texp, tblk, tstate, n_slots = _dispatch_tables(rec, counts, t, tm_e)
        xs = _dispatch(slot, pad_start, pad_len, used, h2_rows, n_slots, tm_e)
        eo = _experts(texp, tblk, tstate, xs, w1[i], w3[i], w2[i], tm_e)
        xf = _final(slot, eo, rec, x1, gt_f, g_post_ffn[i][None, :], seq)
    return xf.reshape(batch, seq, d)
```

```python
import functools
import math

import jax
import jax.numpy as jnp
from jax import lax
from jax.experimental import pallas as pl
from jax.experimental.pallas import tpu as pltpu

F32 = jnp.float32
BF16 = jnp.bfloat16

D_MODEL = 2048
CHUNK = 64
A_HEADS = 8
A_DK = 64
A_DV = 2 * A_DK
A_WIDTH = A_HEADS * A_DV
B_HEADS = 4
B_WIDTH = D_MODEL - A_WIDTH
B_DV = B_WIDTH // B_HEADS
B_DK = B_DV // 2
GATE_RANK = 16
GATE_TAU = 16.0
N_BUCKETS = 32
MAX_DISTANCE = 256
N_GROUPS = 4
EXPERTS_PER_GROUP = 8
N_EXPERTS = N_GROUPS * EXPERTS_PER_GROUP
D_EXPERT = D_MODEL // 4
EPS = 1e-6
NEG_INF = -1e30
LOG2E = math.log2(math.e)

LANES = 128
U32 = jnp.uint32
HALF_D = D_MODEL // 2
ROW_SLABS = HALF_D // LANES
ROW_CHUNK = 16
CHUNK_UNROLL = 4
OUTPROJ_PIECE = 128
ONES_ROWS = 16
D_MAIN = 3 * A_WIDTH + 2 * B_HEADS * B_DK + 2 * B_WIDTH
COL_QA, COL_KA, COL_VA = 0, A_HEADS, 2 * A_HEADS
COL_QB = 3 * A_HEADS
COL_KB = COL_QB + B_HEADS
COL_VB256 = (3 * A_WIDTH + 2 * B_HEADS * B_DK) // B_DV
COL_RB256 = COL_VB256 + B_HEADS
ROUTE_E1, ROUTE_E2, ROUTE_W1, ROUTE_W2, ROUTE_R1, ROUTE_R2 = 0, 1, 2, 3, 4, 5
ROUTER_EXPERT_LANE0 = N_GROUPS

VMEM_LIMIT = 56 * 1024 * 1024


def _params(*sem):
    return pltpu.CompilerParams(dimension_semantics=sem, vmem_limit_bytes=VMEM_LIMIT)


def _rms(v):
    return v * lax.rsqrt(jnp.mean(v * v, axis=-1, keepdims=True) + EPS)


def _silu(v):
    return v * jax.nn.sigmoid(v)


_HIGH_HALF = 0xFFFF0000


def _pack_bf16_pair(lo, hi):
    lo_bits = lax.bitcast_convert_type(lo.astype(BF16).astype(F32), U32) >> 16
    hi_bits = lax.bitcast_convert_type(hi.astype(BF16).astype(F32), U32) & U32(_HIGH_HALF)
    return hi_bits | lo_bits


def _unpack_bf16_pair(w):
    return (lax.bitcast_convert_type(w << 16, F32), lax.bitcast_convert_type(w & U32(_HIGH_HALF), F32))


def _store_row_slabs(ref, r0, nrows, rows_f32):
    packed = _pack_bf16_pair(rows_f32[:, :HALF_D], rows_f32[:, HALF_D:])
    for s in range(ROW_SLABS):
        ref[pl.ds(r0 * ROW_SLABS + s, nrows, stride=ROW_SLABS), :] = packed[:, s * LANES:(s + 1) * LANES]


def _load_row_slabs(ref, r0, nrows):
    slabs = [_unpack_bf16_pair(ref[pl.ds(r0 * ROW_SLABS + s, nrows, stride=ROW_SLABS), :]) for s in range(ROW_SLABS)]
    return (jnp.concatenate([lo for lo, _ in slabs], axis=1), jnp.concatenate([hi for _, hi in slabs], axis=1))


def _ada_kernel(c_ref, w_ref, b_ref, o_ref):
    s = _silu(c_ref[...])
    o_ref[...] = jnp.dot(s.astype(BF16), w_ref[...].astype(BF16), preferred_element_type=F32) + b_ref[...]


def _ada(c_pad, w, b, tn=1024):
    m, d = c_pad.shape
    n = w.shape[1]
    return pl.pallas_call(
        _ada_kernel,
        out_shape=jax.ShapeDtypeStruct((m, n), F32),
        grid=(n // tn,),
        in_specs=[pl.BlockSpec((m, d), lambda j: (0, 0)),
                  pl.BlockSpec((d, tn), lambda j: (0, j)),
                  pl.BlockSpec((1, tn), lambda j: (0, j))],
        out_specs=pl.BlockSpec((m, tn), lambda j: (0, j)),
        compiler_params=_params("arbitrary"),
        name="ada_proj",
    )(c_pad, w, b)


def _inproj_kernel(x_ref, g_ref, sc_ref, sh_ref, w_ref, wz_ref, o_ref, z_ref, h_scr):
    @pl.when(pl.program_id(1) == 0)
    def _():
        h = _rms(x_ref[...]) * g_ref[...]
        h = h * (1.0 + sc_ref[0]) + sh_ref[0]
        h_scr[...] = h.astype(BF16)
        z_ref[...] = jnp.dot(h_scr[...], wz_ref[...], preferred_element_type=F32)

    o_ref[...] = jnp.dot(h_scr[...], w_ref[...], preferred_element_type=F32).astype(BF16)


def _inproj(x2d, g, sc, sh, w_main, w_z, seq, tm=1024, tn=512):
    t, d = x2d.shape
    tm = min(tm, seq)
    n = w_main.shape[1]
    per_b = seq // tm
    return pl.pallas_call(
        _inproj_kernel,
        out_shape=(jax.ShapeDtypeStruct((t, n), BF16), jax.ShapeDtypeStruct((t, LANES), F32)),
        grid=(t // tm, n // tn),
        in_specs=[pl.BlockSpec((tm, d), lambda i, j: (i, 0)),
                  pl.BlockSpec((1, d), lambda i, j: (0, 0)),
                  pl.BlockSpec((1, 1, d), lambda i, j: (i // per_b, 0, 0)),
                  pl.BlockSpec((1, 1, d), lambda i, j: (i // per_b, 0, 0)),
                  pl.BlockSpec((d, tn), lambda i, j: (0, j)),
                  pl.BlockSpec((d, LANES), lambda i, j: (0, 0))],
        out_specs=(pl.BlockSpec((tm, tn), lambda i, j: (i, j)),
                   pl.BlockSpec((tm, LANES), lambda i, j: (i, 0))),
        scratch_shapes=[pltpu.VMEM((tm, d), BF16)],
        compiler_params=_params("arbitrary", "arbitrary"),
        name="in_proj",
    )(x2d, g, sc, sh, w_main, w_z)


def _t5_bucket(rel):
    nb = N_BUCKETS // 2
    max_exact = nb // 2
    base = jnp.where(rel > 0, nb, 0)
    n = jnp.abs(rel)
    nf = jnp.maximum(n, 1).astype(F32)
    large = max_exact + (jnp.log(nf / max_exact) / math.log(MAX_DISTANCE / max_exact)
                         * (nb - max_exact)).astype(jnp.int32)
    large = jnp.minimum(large, nb - 1)
    return base + jnp.where(n < max_exact, n, large)


def _bias_buckets(tq):
    kj = jnp.arange(tq, dtype=jnp.int32)[:, None]
    qi = jnp.arange(tq, dtype=jnp.int32)[None, :]
    near = _t5_bucket(kj - qi - tq)
    diag = jnp.where((kj // CHUNK) <= (qi // CHUNK), _t5_bucket(kj - qi), N_BUCKETS)
    return jnp.stack([near, diag]).astype(jnp.int32)


def _bias_kernel(rb_ref, bk_ref, o_ref):
    h = pl.program_id(0)
    far = rb_ref[N_BUCKETS // 2 - 1, h]
    bucket = bk_ref[...]
    acc = jnp.full(bucket.shape, NEG_INF, F32)
    for n in range(N_BUCKETS):
        acc = jnp.where(bucket == n, (rb_ref[n, h] - far) * LOG2E, acc)
    o_ref[...] = acc


def _bias_tiles(rel_bias, tq):
    return pl.pallas_call(
        _bias_kernel,
        out_shape=jax.ShapeDtypeStruct((A_HEADS, 2, tq, tq), F32),
        grid=(A_HEADS, 2),
        in_specs=[pl.BlockSpec(memory_space=pltpu.SMEM),
                  pl.BlockSpec((None, tq, tq), lambda h, d: (d, 0, 0))],
        out_specs=pl.BlockSpec((None, None, tq, tq), lambda h, d: (h, d, 0, 0)),
        compiler_params=_params("arbitrary", "arbitrary"),
        name="bias_tiles",
    )(rel_bias, _bias_buckets(tq))


def _attn_kernel(q_ref, k_ref, v_ref, bias_ref, lq1_ref, lk1_ref, lq2_ref, lk2_ref, g_ref, o_ref,
                 vt_scr, sa_scr, sb_scr, m_scr, acc_scr, *, tq, tk, lam_init):
    i = pl.program_id(2)
    nsub = tq // tk
    bufs = (sa_scr, sb_scr)

    @pl.when(i == 0)
    def _():
        ones = jnp.ones((ONES_ROWS, tk), BF16)
        for c in range(vt_scr.shape[0]):
            vt = v_ref[c * tk:(c + 1) * tk, :].astype(F32).T.astype(BF16)
            vt_scr[c] = jnp.concatenate([vt, ones], axis=0)

    lane = lax.broadcasted_iota(jnp.int32, (1, A_DV), 1)
    q = q_ref[...] * (A_DK ** -0.5 * LOG2E)
    zero = jnp.zeros_like(q)
    q2 = jnp.concatenate([jnp.where(lane < A_DK, q, zero), jnp.where(lane >= A_DK, q, zero)], axis=0)

    m_scr[...] = jnp.full(m_scr.shape, NEG_INF, F32)
    acc_scr[...] = jnp.zeros(acc_scr.shape, F32)

    def scores(j):
        k = k_ref[pl.ds(pl.multiple_of(j * tk, tk), tk), :]
        return lax.dot_general(k, q2, (((1,), (1,)), ((), ())), preferred_element_type=F32)

    def softmax_pv(s_ref, j, bias):
        s = s_ref[...]
        if bias is not None:
            s = jnp.concatenate([s[:, :tq] + bias, s[:, tq:] + bias], axis=1)
        m_old = m_scr[...]
        m_new = jnp.maximum(m_old, jnp.max(s, axis=0, keepdims=True))
        alpha = jnp.exp2(m_old - m_new)
        p = jnp.exp2(s - m_new).astype(BF16)
        acc_scr[...] = alpha * acc_scr[...] + jnp.dot(vt_scr[j], p, preferred_element_type=F32)
        m_scr[...] = m_new

    n_far = jnp.maximum(i - 1, 0) * nsub
    sa_scr[...] = scores(0)

    def far_steps(j, count):
        for c in range(count):
            bufs[(c + 1) % 2][...] = scores(j + c + 1)
            softmax_pv(bufs[c % 2], j + c, None)

    def far_quad(jj, carry):
        far_steps(4 * jj, 4)
        return carry

    n_quads = n_far // 4
    lax.fori_loop(0, n_quads, far_quad, 0)

    @pl.when(n_far - 4 * n_quads >= 2)
    def _():
        far_steps(4 * n_quads, 2)

    def biased_steps(first_tile):
        j0 = (i - 1 + first_tile) * nsub
        count = (2 - first_tile) * nsub
        for c in range(count):
            if c + 1 < count:
                bufs[(c + 1) % 2][...] = scores(j0 + c + 1)
            d, r = first_tile + c // nsub, (c % nsub) * tk
            softmax_pv(bufs[c % 2], j0 + c, bias_ref[d, r:r + tk, :])

    @pl.when(i >= 1)
    def _():
        biased_steps(0)

    @pl.when(i == 0)
    def _():
        biased_steps(1)

    lam = (jnp.exp(jnp.sum(lq1_ref[...] * lk1_ref[...], axis=-1, keepdims=True))
           - jnp.exp(jnp.sum(lq2_ref[...] * lk2_ref[...], axis=-1, keepdims=True)) + lam_init)
    on = acc_scr[:A_DV, :] / acc_scr[A_DV:A_DV + 1, :]
    o = on[:, :tq] - lam * on[:, tq:]
    y = o * lax.rsqrt(jnp.mean(o * o, axis=0, keepdims=True) + EPS) * g_ref[...] * (1.0 - lam_init)
    o_ref[...] = y.T.astype(BF16)


def _attention(proj, bias_tiles, lq1, lk1, lq2, lk2, g_sub_col, batch, seq, lam_init, tq, tk=256):
    t = proj.shape[0]
    nq = seq // tq
    assert (tq // tk) % 2 == 0 and tq % tk == 0, "the score pipeline alternates two buffers per query tile"
    kern = functools.partial(_attn_kernel, tq=tq, tk=tk, lam_init=lam_init)
    vec = lambda n: pl.BlockSpec((1, n), lambda b, h, i: (0, 0))
    return pl.pallas_call(
        kern,
        out_shape=jax.ShapeDtypeStruct((t, A_WIDTH), BF16),
        grid=(batch, A_HEADS, nq),
        in_specs=[pl.BlockSpec((tq, A_DV), lambda b, h, i: (b * nq + i, COL_QA + h)),
                  pl.BlockSpec((seq, A_DV), lambda b, h, i: (b, COL_KA + h)),
                  pl.BlockSpec((seq, A_DV), lambda b, h, i: (b, COL_VA + h)),
                  pl.BlockSpec((None, 2, tq, tq), lambda b, h, i: (h, 0, 0, 0)),
                  vec(A_DK), vec(A_DK), vec(A_DK), vec(A_DK),
                  pl.BlockSpec((A_DV, 1), lambda b, h, i: (0, 0))],
        out_specs=pl.BlockSpec((tq, A_DV), lambda b, h, i: (b * nq + i, h)),
        scratch_shapes=[pltpu.VMEM((seq // tk, A_DV + ONES_ROWS, tk), BF16),
                        pltpu.VMEM((tk, 2 * tq), F32), pltpu.VMEM((tk, 2 * tq), F32),
                        pltpu.VMEM((1, 2 * tq), F32),
                        pltpu.VMEM((A_DV + ONES_ROWS, 2 * tq), F32)],
        compiler_params=_params("arbitrary", "arbitrary", "arbitrary"),
        name="diff_attention",
    )(proj, proj, proj, bias_tiles, lq1, lk1, lq2, lk2, g_sub_col)


def _split3(a):
    a1 = a.astype(BF16)
    r1 = a - a1.astype(F32)
    a2 = r1.astype(BF16)
    return a1, a2, (r1 - a2.astype(F32)).astype(BF16)


def _sum3(x):
    return x[:, :B_DK] + x[:, B_DK:2 * B_DK] + x[:, 2 * B_DK:]


def _gla_kernel(q_ref, k_ref, v_ref, r_ref, z_ref, wa_ref, ba_ref, g_ref, o_ref,
                state_scr, mask_scr, kv_scr, st_scr, *, n_chunks):
    lc = n_chunks * CHUNK

    @pl.when(pl.program_id(2) == 0)
    def _():
        state_scr[...] = jnp.zeros(state_scr.shape, F32)
        row = lax.broadcasted_iota(jnp.int32, (lc, lc), 0)
        col = lax.broadcasted_iota(jnp.int32, (lc, lc), 1)
        same = (row // CHUNK) == (col // CHUNK)
        mask_scr[...] = (same & (row >= col)).astype(BF16)

    z = z_ref[...]
    zh = z.astype(BF16)
    zl = (z - zh.astype(F32)).astype(BF16)
    pre = jnp.dot(jnp.concatenate([zh, zl, zh], axis=1), wa_ref[...], preferred_element_type=F32) + ba_ref[...]
    log_a = (jnp.minimum(pre, 0.0) - jnp.log1p(jnp.exp(-jnp.abs(pre)))) * (1.0 / GATE_TAU)
    parts = jnp.concatenate(_split3(log_a), axis=1)
    cum = _sum3(jnp.dot(mask_scr[...], parts, preferred_element_type=F32))
    totals = [cum[(c + 1) * CHUNK - 1:(c + 1) * CHUNK, :] for c in range(n_chunks)]
    total = jnp.concatenate([jnp.broadcast_to(tc, (CHUNK, B_DK)) for tc in totals], axis=0)
    k_dec = (k_ref[...].astype(F32) * jnp.exp(total - cum)).astype(BF16)

    for c in range(n_chunks):
        rows = slice(c * CHUNK, (c + 1) * CHUNK)
        kv_scr[c] = lax.dot_general(v_ref[rows, :], k_dec[rows], (((0,), (0,)), ((), ())),
                                    preferred_element_type=F32)
    state = state_scr[...]
    for c in range(n_chunks):
        state = state * jnp.exp(totals[c]) + kv_scr[c]
        st_scr[c] = state.astype(BF16)
    state_scr[...] = state

    for c in range(n_chunks):
        rows = slice(c * CHUNK, (c + 1) * CHUNK)
        o = lax.dot_general(q_ref[rows, :], st_scr[c], (((1,), (1,)), ((), ())),
                            preferred_element_type=F32) * (B_DK ** -0.5)
        o_ref[rows, :] = (_rms(o) * g_ref[...] * _silu(r_ref[rows, :].astype(F32))).astype(BF16)


def _gla(proj, zb, w_alpha_pad, b_alpha, g_norm, batch, seq, lc=512):
    t = proj.shape[0]
    lc = min(lc, seq)
    nl = seq // lc
    n_chunks = lc // CHUNK
    kern = functools.partial(_gla_kernel, n_chunks=n_chunks)
    wa_hi = w_alpha_pad.astype(BF16)
    wa_lo = (w_alpha_pad - wa_hi.astype(F32)).astype(BF16)
    wa3 = jnp.concatenate([wa_hi, wa_hi, wa_lo], axis=0)
    return pl.pallas_call(
        kern,
        out_shape=jax.ShapeDtypeStruct((t, B_WIDTH), BF16),
        grid=(batch, B_HEADS, nl),
        in_specs=[pl.BlockSpec((lc, B_DK), lambda b, h, l: (b * nl + l, COL_QB + h)),
                  pl.BlockSpec((lc, B_DK), lambda b, h, l: (b * nl + l, COL_KB + h)),
                  pl.BlockSpec((lc, B_DV), lambda b, h, l: (b * nl + l, COL_VB256 + h)),
                  pl.BlockSpec((lc, B_DV), lambda b, h, l: (b * nl + l, COL_RB256 + h)),
                  pl.BlockSpec((lc, LANES), lambda b, h, l: (b * nl + l, 0)),
                  pl.BlockSpec((3 * LANES, B_DK), lambda b, h, l: (0, h)),
                  pl.BlockSpec((1, B_DK), lambda b, h, l: (0, h)),
                  pl.BlockSpec((1, B_DV), lambda b, h, l: (0, 0))],
        out_specs=pl.BlockSpec((lc, B_DV), lambda b, h, l: (b * nl + l, h)),
        scratch_shapes=[pltpu.VMEM((B_DV, B_DK), F32),
                        pltpu.VMEM((lc, lc), BF16),
                        pltpu.VMEM((n_chunks, B_DV, B_DK), F32),
                        pltpu.VMEM((n_chunks, B_DV, B_DK), BF16)],
        compiler_params=_params("arbitrary", "arbitrary", "arbitrary"),
        name="gla",
    )(proj, proj, proj, proj, zb, wa3, b_alpha, g_norm)


def _outproj_kernel(oa_ref, ob_ref, wo_ref, x_ref, gt_ref, gpost_ref, gpre_ref, sc_ref, sh_ref, wr_ref,
                    br_ref, x1_ref, h2_ref, lg_ref, *, tm):
    for p in range(tm // OUTPROJ_PIECE):
        p0 = p * OUTPROJ_PIECE
        prow = slice(p0, p0 + OUTPROJ_PIECE)
        y = (jnp.dot(oa_ref[prow, :], wo_ref[:A_WIDTH, :], preferred_element_type=F32)
             + jnp.dot(ob_ref[prow, :], wo_ref[A_WIDTH:, :], preferred_element_type=F32))
        his, los = [], []
        for c in range(OUTPROJ_PIECE // ROW_CHUNK):
            r0 = p0 + c * ROW_CHUNK
            rows = slice(r0, r0 + ROW_CHUNK)
            x1 = x_ref[rows, :] + gt_ref[0] * (_rms(y[c * ROW_CHUNK:(c + 1) * ROW_CHUNK]) * gpost_ref[...])
            x1_ref[rows, :] = x1
            h2 = (_rms(x1) * gpre_ref[...]) * (1.0 + sc_ref[0]) + sh_ref[0]
            hi = h2.astype(BF16)
            his.append(hi)
            los.append((h2 - hi.astype(F32)).astype(BF16))
            _store_row_slabs(h2_ref, r0, ROW_CHUNK, h2)
        hi, lo = jnp.concatenate(his, axis=0), jnp.concatenate(los, axis=0)
        hw = jnp.dot(hi, wr_ref[...], preferred_element_type=F32)
        lw = jnp.dot(lo, wr_ref[:, :LANES], preferred_element_type=F32)
        lg_ref[prow, :] = hw[:, :LANES] + hw[:, LANES:] + lw + br_ref[...]


def _outproj(oa, ob, w_out, x2d, gt, g_post, g_pre, sc, sh, w_router, b_router, seq, tm=512):
    t, d = x2d.shape
    tm = min(tm, seq)
    per_b = seq // tm
    kern = functools.partial(_outproj_kernel, tm=tm)
    wr_hi = w_router.astype(BF16)
    wr_lo = (w_router - wr_hi.astype(F32)).astype(BF16)
    wr_cat = jnp.concatenate([wr_hi, wr_lo], axis=1)
    row = lambda: pl.BlockSpec((1, d), lambda i: (0, 0))
    per_batch = lambda: pl.BlockSpec((1, 1, d), lambda i: (i // per_b, 0, 0))
    return pl.pallas_call(
        kern,
        out_shape=(jax.ShapeDtypeStruct((t, d), F32),
                   jax.ShapeDtypeStruct((t * ROW_SLABS, LANES), U32),
                   jax.ShapeDtypeStruct((t, LANES), F32)),
        grid=(t // tm,),
        in_specs=[pl.BlockSpec((tm, A_WIDTH), lambda i: (i, 0)),
                  pl.BlockSpec((tm, B_WIDTH), lambda i: (i, 0)),
                  pl.BlockSpec((d, d), lambda i: (0, 0)),
                  pl.BlockSpec((tm, d), lambda i: (i, 0)),
                  per_batch(), row(), row(), per_batch(), per_batch(),
                  pl.BlockSpec((d, 2 * LANES), lambda i: (0, 0)),
                  pl.BlockSpec((1, LANES), lambda i: (0, 0))],
        out_specs=(pl.BlockSpec((tm, d), lambda i: (i, 0)),
                   pl.BlockSpec((tm * ROW_SLABS, LANES), lambda i: (i, 0)),
                   pl.BlockSpec((tm, LANES), lambda i: (i, 0))),
        compiler_params=_params("arbitrary"),
        name="out_proj",
    )(oa, ob, w_out, x2d, gt, g_post, g_pre, sc, sh, wr_cat, b_router)


def _route_kernel(lg_ref, rec_ref, cnt_ref, carry_scr, *, tr):
    @pl.when(pl.program_id(0) == 0)
    def _():
        carry_scr[...] = jnp.zeros(carry_scr.shape, F32)

    lg = lg_ref[...]
    lane = lax.broadcasted_iota(jnp.int32, lg.shape, 1)
    big = jnp.int32(LANES)

    def first_lane(mask):
        return jnp.min(jnp.where(mask, lane, big), axis=-1, keepdims=True)

    gmask = lane < N_GROUPS
    gmax = jnp.max(jnp.where(gmask, lg, -jnp.inf), axis=-1, keepdims=True)
    gexp = jnp.where(gmask, jnp.exp(lg - gmax), 0.0)
    gprob = gexp / jnp.sum(gexp, axis=-1, keepdims=True)
    g_val = jnp.max(gprob, axis=-1, keepdims=True)
    g_idx = first_lane(gmask & (gprob == g_val))

    lo = ROUTER_EXPERT_LANE0 + g_idx * EXPERTS_PER_GROUP
    emask = (lane >= lo) & (lane < lo + EXPERTS_PER_GROUP)
    emax = jnp.max(jnp.where(emask, lg, -jnp.inf), axis=-1, keepdims=True)
    eexp = jnp.where(emask, jnp.exp(lg - emax), 0.0)
    eprob = eexp / jnp.sum(eexp, axis=-1, keepdims=True)
    v1 = jnp.max(eprob, axis=-1, keepdims=True)
    i1 = first_lane(emask & (eprob == v1))
    rest = emask & (lane != i1)
    v2 = jnp.max(jnp.where(rest, eprob, -1.0), axis=-1, keepdims=True)
    i2 = first_lane(rest & (eprob == v2))
    w1 = g_val * (v1 / (v1 + v2))
    w2 = g_val * (v2 / (v1 + v2))

    hit1 = lane == i1
    hit2 = lane == i2
    onehot = (hit1 | hit2).astype(BF16)
    r = lax.broadcasted_iota(jnp.int32, (tr, tr), 0)
    c = lax.broadcasted_iota(jnp.int32, (tr, tr), 1)
    before = (c < r).astype(BF16)
    pos = carry_scr[...] + jnp.dot(before, onehot, preferred_element_type=F32)
    rank1 = jnp.sum(jnp.where(hit1, pos, 0.0), axis=-1, keepdims=True)
    rank2 = jnp.sum(jnp.where(hit2, pos, 0.0), axis=-1, keepdims=True)
    carry_scr[...] = carry_scr[...] + jnp.sum(onehot.astype(F32), axis=0, keepdims=True)
    cnt_ref[...] = carry_scr[...]

    e1 = (i1 - ROUTER_EXPERT_LANE0).astype(F32)
    e2 = (i2 - ROUTER_EXPERT_LANE0).astype(F32)
    rec = jnp.zeros(lg.shape, F32)
    for ln, val in ((ROUTE_E1, e1), (ROUTE_E2, e2), (ROUTE_W1, w1), (ROUTE_W2, w2),
                    (ROUTE_R1, rank1), (ROUTE_R2, rank2)):
        rec = jnp.where(lane == ln, val, rec)
    rec_ref[...] = rec


def _route(logits, tr=512):
    t = logits.shape[0]
    tr = min(tr, t)
    kern = functools.partial(_route_kernel, tr=tr)
    return pl.pallas_call(
        kern,
        out_shape=(jax.ShapeDtypeStruct((t, LANES), F32), jax.ShapeDtypeStruct((1, LANES), F32)),
        grid=(t // tr,),
        in_specs=[pl.BlockSpec((tr, LANES), lambda i: (i, 0))],
        out_specs=(pl.BlockSpec((tr, LANES), lambda i: (i, 0)),
                   pl.BlockSpec((1, LANES), lambda i: (0, 0))),
        scratch_shapes=[pltpu.VMEM((1, LANES), F32)],
        compiler_params=_params("arbitrary"),
        name="route",
    )(logits)


def _slab_rows(ref, row):
    return ref.at[pl.ds(pl.multiple_of(row * ROW_SLABS, ROW_SLABS), ROW_SLABS), :]


def _dispatch_kernel(slot_ref, pad_start_ref, pad_len_ref, used_ref, h2_ref, xs_hbm, zero_scr, sems, *, td, tm):
    g = pl.program_id(0)
    tile_rows = tm * ROW_SLABS
    n_tiles = xs_hbm.shape[0] // tile_rows

    def zero_copy(slot):
        return pltpu.make_async_copy(zero_scr.at[pl.ds(0, ROW_SLABS), :], _slab_rows(xs_hbm, slot), sems.at[1])

    def zero_tile_copy(tile):
        rows = pl.ds(pl.multiple_of(tile * tile_rows, tile_rows), tile_rows)
        return pltpu.make_async_copy(zero_scr, xs_hbm.at[rows, :], sems.at[1])

    @pl.when(g == 0)
    def _():
        zero_scr[...] = jnp.zeros(zero_scr.shape, U32)

        def unused_tile(tile, carry):
            zero_tile_copy(tile).start()
            zero_tile_copy(tile).wait()
            return carry

        lax.fori_loop(used_ref[0], n_tiles, unused_tile, 0)

        def per_expert(e, carry):
            start = pad_start_ref[e]

            def issue(r, c):
                zero_copy(start + r).start()
                return c

            def drain(r, c):
                zero_copy(start).wait()
                return c

            lax.fori_loop(0, pad_len_ref[e], issue, 0)
            lax.fori_loop(0, pad_len_ref[e], drain, 0)
            return carry

        lax.fori_loop(0, N_EXPERTS, per_expert, 0)

    def row_copy(r, slot):
        return pltpu.make_async_copy(_slab_rows(h2_ref, r), _slab_rows(xs_hbm, slot), sems.at[0])

    base = g * td

    def issue(r, c):
        tok = base + r
        row_copy(r, slot_ref[2 * tok]).start()
        row_copy(r, slot_ref[2 * tok + 1]).start()
        return c

    lax.fori_loop(0, td, issue, 0, unroll=CHUNK_UNROLL)
    for _ in range(2):
        pltpu.make_async_copy(h2_ref, xs_hbm.at[pl.ds(0, td * ROW_SLABS), :], sems.at[0]).wait()


def _dispatch(slot, pad_start, pad_len, used, h2_rows, n_slots, tm, td=512):
    t = slot.shape[0] // 2
    td = min(td, t)
    kern = functools.partial(_dispatch_kernel, td=td, tm=tm)
    grid_spec = pltpu.PrefetchScalarGridSpec(
        num_scalar_prefetch=4,
        grid=(t // td,),
        in_specs=[pl.BlockSpec((td * ROW_SLABS, LANES), lambda g, sl, ps, pn, us: (g, 0))],
        out_specs=pl.BlockSpec(memory_space=pl.ANY),
        scratch_shapes=[pltpu.VMEM((tm * ROW_SLABS, LANES), U32), pltpu.SemaphoreType.DMA((2,))],
    )
    return pl.pallas_call(
        kern,
        out_shape=jax.ShapeDtypeStruct((n_slots * ROW_SLABS, LANES), U32),
        grid_spec=grid_spec,
        compiler_params=_params("arbitrary"),
        name="dispatch",
    )(slot, pad_start, pad_len, used, h2_rows)


TILE_UNUSED, TILE_USED, TILE_NEW_EXPERT = 0, 1, 2


def _expert_kernel(texp_ref, tblk_ref, tstate_ref, xs_ref, w1_ref, w3_ref, w2_ref, eo_ref,
                   x_scr, w1_scr, w3_scr, w2_scr, *, tm):
    state = tstate_ref[pl.program_id(0)]

    @pl.when(state == TILE_UNUSED)
    def _():
        eo_ref[...] = jnp.zeros(eo_ref.shape, U32)

    @pl.when(state == TILE_NEW_EXPERT)
    def _():
        w1_scr[...] = w1_ref[...].astype(BF16)
        w3_scr[...] = w3_ref[...].astype(BF16)
        w2_scr[...] = w2_ref[...].astype(BF16)

    @pl.when(state != TILE_UNUSED)
    def _():
        for s in range(ROW_SLABS):
            lo, hi = _unpack_bf16_pair(xs_ref[pl.ds(s, tm, stride=ROW_SLABS), :])
            x_scr[:, s * LANES:(s + 1) * LANES] = lo.astype(BF16)
            x_scr[:, HALF_D + s * LANES:HALF_D + (s + 1) * LANES] = hi.astype(BF16)
        x = x_scr[...]
        a = jnp.dot(x, w1_scr[...], preferred_element_type=F32)
        b = jnp.dot(x, w3_scr[...], preferred_element_type=F32)
        hid = (_silu(a) * b).astype(BF16)
        y = jnp.dot(hid, w2_scr[...], preferred_element_type=F32)
        _store_row_slabs(eo_ref, 0, tm, y)


def _experts(tile_expert, tile_block, tile_used, xs_rows, w1, w3, w2, tm):
    n_tiles = tile_expert.shape[0]
    d, f = w1.shape[1], w1.shape[2]
    kern = functools.partial(_expert_kernel, tm=tm)
    grid_spec = pltpu.PrefetchScalarGridSpec(
        num_scalar_prefetch=3,
        grid=(n_tiles,),
        in_specs=[pl.BlockSpec((tm * ROW_SLABS, LANES), lambda i, te, tb, tu: (tb[i], 0)),
                  pl.BlockSpec((None, d, f), lambda i, te, tb, tu: (te[i], 0, 0)),
                  pl.BlockSpec((None, d, f), lambda i, te, tb, tu: (te[i], 0, 0)),
                  pl.BlockSpec((None, f, d), lambda i, te, tb, tu: (te[i], 0, 0))],
        out_specs=pl.BlockSpec((tm * ROW_SLABS, LANES), lambda i, te, tb, tu: (i, 0)),
        scratch_shapes=[pltpu.VMEM((tm, d), BF16), pltpu.VMEM((d, f), BF16), pltpu.VMEM((d, f), BF16),
                        pltpu.VMEM((f, d), BF16)],
    )
    return pl.pallas_call(
        kern,
        out_shape=jax.ShapeDtypeStruct(xs_rows.shape, U32),
        grid_spec=grid_spec,
        compiler_params=_params("arbitrary"),
        name="expert_mlp",
    )(tile_expert, tile_block, tile_used, xs_rows, w1, w3, w2)


def _final_kernel(slot_ref, eo_hbm, rec_ref, x1_ref, gt_ref, g_ref, o_ref, e_scr, sems, *, tf):
    i = pl.program_id(0)
    par = i % 2

    def start_all(step, buf):
        def body(r, c):
            tok = step * tf + r
            for k in range(2):
                pltpu.make_async_copy(_slab_rows(eo_hbm, slot_ref[2 * tok + k]),
                                      _slab_rows(e_scr.at[buf, k], r), sems.at[buf]).start()
            return c
        lax.fori_loop(0, tf, body, 0, unroll=CHUNK_UNROLL)

    def wait_all(buf):
        for k in range(2):
            pltpu.make_async_copy(eo_hbm.at[pl.ds(0, tf * ROW_SLABS), :], e_scr.at[buf, k], sems.at[buf]).wait()

    @pl.when(i == 0)
    def _():
        start_all(0, 0)

    @pl.when(i + 1 < pl.num_programs(0))
    def _():
        start_all(i + 1, 1 - par)

    wait_all(par)

    def chunk(c, carry):
        r0 = pl.multiple_of(c * ROW_CHUNK, ROW_CHUNK)
        rows = pl.ds(r0, ROW_CHUNK)
        rec = rec_ref[rows, :]
        w1 = rec[:, ROUTE_W1:ROUTE_W1 + 1]
        w2 = rec[:, ROUTE_W2:ROUTE_W2 + 1]
        lo1, hi1 = _load_row_slabs(e_scr.at[par, 0], r0, ROW_CHUNK)
        lo2, hi2 = _load_row_slabs(e_scr.at[par, 1], r0, ROW_CHUNK)
        y = jnp.concatenate([w1 * lo1 + w2 * lo2, w1 * hi1 + w2 * hi2], axis=1)
        o_ref[rows, :] = x1_ref[rows, :] + gt_ref[0] * (_rms(y) * g_ref[...])
        return carry

    lax.fori_loop(0, tf // ROW_CHUNK, chunk, 0, unroll=CHUNK_UNROLL)


def _final(slot, eo_rows, rec, x1, gt, g_post, seq, tf=256):
    t, d = x1.shape
    tf = min(tf, seq)
    per_b = seq // tf
    kern = functools.partial(_final_kernel, tf=tf)
    grid_spec = pltpu.PrefetchScalarGridSpec(
        num_scalar_prefetch=1,
        grid=(t // tf,),
        in_specs=[pl.BlockSpec(memory_space=pl.ANY),
                  pl.BlockSpec((tf, LANES), lambda i, sl: (i, 0)),
                  pl.BlockSpec((tf, d), lambda i, sl: (i, 0)),
                  pl.BlockSpec((1, 1, d), lambda i, sl: (i // per_b, 0, 0)),
                  pl.BlockSpec((1, d), lambda i, sl: (0, 0))],
        out_specs=pl.BlockSpec((tf, d), lambda i, sl: (i, 0)),
        scratch_shapes=[pltpu.VMEM((2, 2, tf * ROW_SLABS, LANES), U32),
                        pltpu.SemaphoreType.DMA((2,))],
    )
    return pl.pallas_call(
        kern,
        out_shape=jax.ShapeDtypeStruct((t, d), F32),
        grid_spec=grid_spec,
        compiler_params=_params("arbitrary"),
        name="combine_final",
    )(slot, eo_rows, rec, x1, gt, g_post)


def _dispatch_tables(rec, counts, t, tm):
    e = rec[:, ROUTE_E1:ROUTE_E2 + 1].astype(jnp.int32)
    rank = rec[:, ROUTE_R1:ROUTE_R2 + 1].astype(jnp.int32)
    cnt = counts[0, ROUTER_EXPERT_LANE0:ROUTER_EXPERT_LANE0 + N_EXPERTS].astype(jnp.int32)
    tiles_per = (cnt + tm - 1) // tm
    tile_end = jnp.cumsum(tiles_per)
    tile_start = tile_end - tiles_per
    n_tiles = (2 * t + N_EXPERTS * (tm - 1)) // tm
    slot = (tile_start[e] * tm + rank).reshape(-1)
    pad_start = tile_start * tm + cnt
    pad_len = tiles_per * tm - cnt
    tile_id = jnp.arange(n_tiles, dtype=jnp.int32)
    used = tile_end[-1]
    tblk = jnp.minimum(tile_id, used - 1)
    texp = jnp.searchsorted(tile_end, tblk, side="right").astype(jnp.int32)
    tstate = jnp.where(tile_id < used, jnp.where(tile_id == tile_start[texp], TILE_NEW_EXPERT, TILE_USED),
                       TILE_UNUSED).astype(jnp.int32)
    return slot, pad_start, pad_len, used.reshape(1), texp, tblk, tstate, n_tiles * tm


def kernel(x, c, rel_bias, w_ada, b_ada, g_pre_mix, g_post_mix, w_in, w_alpha, b_alpha, lam_q1, lam_k1, lam_q2,
           lam_k2, g_sub_a, g_norm_b, w_out, g_pre_ffn, g_post_ffn, w_router_g, b_router_g, w_router_e,
           b_router_e, w1, w3, w2):
    batch, seq, d = x.shape
    t = batch * seq
    depth = w_in.shape[0]
    tq = min(512, seq)
    tm_e = 256
    xf = x.reshape(t, d)
    for i in range(depth):
        lam_init = 0.8 - 0.6 * math.exp(-0.3 * i)
        c_pad = jnp.pad(c, ((0, 8 - batch % 8 if batch % 8 else 0), (0, 0)))
        ada = _ada(c_pad, w_ada[i], b_ada[i][None, :])[:batch]
        sh_m, sc_m, gt_m, sh_f, sc_f, gt_f = [a[:, None, :] for a in jnp.split(ada, 6, axis=-1)]

        w_in_b = w_in[i].astype(BF16)
        w_main = w_in_b[:, :D_MAIN]
        w_z = jnp.pad(w_in_b[:, D_MAIN:], ((0, 0), (0, LANES - GATE_RANK)))
        proj, zb = _inproj(xf, g_pre_mix[i][None, :], sc_m, sh_m, w_main, w_z, seq)

        oa = _attention(proj, _bias_tiles(rel_bias, tq), lam_q1[i][None, :], lam_k1[i][None, :],
                        lam_q2[i][None, :], lam_k2[i][None, :], g_sub_a[i][:, None], batch, seq, lam_init, tq)
        w_alpha_pad = jnp.pad(w_alpha[i], ((0, LANES - GATE_RANK), (0, 0)))
        ob = _gla(proj, zb, w_alpha_pad, b_alpha[i][None, :], g_norm_b[i][None, :], batch, seq)

        w_router = jnp.pad(jnp.concatenate([w_router_g[i], w_router_e[i]], axis=1),
                           ((0, 0), (0, LANES - N_GROUPS - N_EXPERTS)))
        b_router = jnp.pad(jnp.concatenate([b_router_g[i], b_router_e[i]]),
                           (0, LANES - N_GROUPS - N_EXPERTS))[None, :]
        x1, h2_rows, logits = _outproj(oa, ob, w_out[i].astype(BF16), xf, gt_m, g_post_mix[i][None, :],
                                       g_pre_ffn[i][None, :], sc_f, sh_f, w_router, b_router, seq)

        rec, counts = _route(logits)
        slot, pad_start, pad_len, used, texp, tblk, tstate, n_slots = _dispatch_tables(rec, counts, t, tm_e)
        xs = _dispatch(slot, pad_start, pad_len, used, h2_rows, n_slots, tm_e)
        eo = _experts(texp, tblk, tstate, xs, w1[i], w3[i], w2[i], tm_e)
        xf = _final(slot, eo, rec, x1, gt_f, g_post_ffn[i][None, :], seq)
    return xf.reshape(batch, seq, d)
```

```python
import functools
import math

import jax
import jax.numpy as jnp
from jax import lax
from jax.experimental import pallas as pl
from jax.experimental.pallas import tpu as pltpu

F32 = jnp.float32
BF16 = jnp.bfloat16

D_MODEL = 2048
CHUNK = 64
A_HEADS = 8
A_DK = 64
A_DV = 2 * A_DK
A_WIDTH = A_HEADS * A_DV
B_HEADS = 4
B_WIDTH = D_MODEL - A_WIDTH
B_DV = B_WIDTH // B_HEADS
B_DK = B_DV // 2
GATE_RANK = 16
GATE_TAU = 16.0
N_BUCKETS = 32
MAX_DISTANCE = 256
N_GROUPS = 4
EXPERTS_PER_GROUP = 8
N_EXPERTS = N_GROUPS * EXPERTS_PER_GROUP
D_EXPERT = D_MODEL // 4
EPS = 1e-6
NEG_INF = -1e30
LOG2E = math.log2(math.e)

LANES = 128
U32 = jnp.uint32
HALF_D = D_MODEL // 2
ROW_SLABS = HALF_D // LANES
ROW_CHUNK = 16
CHUNK_UNROLL = 4
OUTPROJ_PIECE = 128
ONES_ROWS = 16
D_MAIN = 3 * A_WIDTH + 2 * B_HEADS * B_DK + 2 * B_WIDTH
COL_QA, COL_KA, COL_VA = 0, A_HEADS, 2 * A_HEADS
COL_QB = 3 * A_HEADS
COL_KB = COL_QB + B_HEADS
COL_VB256 = (3 * A_WIDTH + 2 * B_HEADS * B_DK) // B_DV
COL_RB256 = COL_VB256 + B_HEADS
ROUTE_E1, ROUTE_E2, ROUTE_W1, ROUTE_W2, ROUTE_R1, ROUTE_R2 = 0, 1, 2, 3, 4, 5
ROUTER_EXPERT_LANE0 = N_GROUPS

VMEM_LIMIT = 56 * 1024 * 1024
INPROJ_VMEM_LIMIT = 60 * 1024 * 1024


def _params(*sem):
    return pltpu.CompilerParams(dimension_semantics=sem, vmem_limit_bytes=VMEM_LIMIT)


def _rms(v):
    return v * lax.rsqrt(jnp.mean(v * v, axis=-1, keepdims=True) + EPS)


def _silu(v):
    return v * jax.nn.sigmoid(v)


_HIGH_HALF = 0xFFFF0000


def _pack_bf16_pair(lo, hi):
    lo_bits = lax.bitcast_convert_type(lo.astype(BF16).astype(F32), U32) >> 16
    hi_bits = lax.bitcast_convert_type(hi.astype(BF16).astype(F32), U32) & U32(_HIGH_HALF)
    return hi_bits | lo_bits


def _unpack_bf16_pair(w):
    return (lax.bitcast_convert_type(w << 16, F32), lax.bitcast_convert_type(w & U32(_HIGH_HALF), F32))


def _store_row_slabs(ref, r0, nrows, rows_f32):
    packed = _pack_bf16_pair(rows_f32[:, :HALF_D], rows_f32[:, HALF_D:])
    for s in range(ROW_SLABS):
        ref[pl.ds(r0 * ROW_SLABS + s, nrows, stride=ROW_SLABS), :] = packed[:, s * LANES:(s + 1) * LANES]


def _load_row_slabs(ref, r0, nrows):
    slabs = [_unpack_bf16_pair(ref[pl.ds(r0 * ROW_SLABS + s, nrows, stride=ROW_SLABS), :]) for s in range(ROW_SLABS)]
    return (jnp.concatenate([lo for lo, _ in slabs], axis=1), jnp.concatenate([hi for _, hi in slabs], axis=1))


def _ada_kernel(c_ref, w_ref, b_ref, o_ref):
    s = _silu(c_ref[...])
    o_ref[...] = jnp.dot(s.astype(BF16), w_ref[...].astype(BF16), preferred_element_type=F32) + b_ref[...]


def _ada(c_pad, w, b, tn=1024):
    m, d = c_pad.shape
    n = w.shape[1]
    return pl.pallas_call(
        _ada_kernel,
        out_shape=jax.ShapeDtypeStruct((m, n), F32),
        grid=(n // tn,),
        in_specs=[pl.BlockSpec((m, d), lambda j: (0, 0)),
                  pl.BlockSpec((d, tn), lambda j: (0, j)),
                  pl.BlockSpec((1, tn), lambda j: (0, j))],
        out_specs=pl.BlockSpec((m, tn), lambda j: (0, j)),
        compiler_params=_params("arbitrary"),
        name="ada_proj",
    )(c_pad, w, b)


def _inproj_kernel(x_ref, g_ref, sc_ref, sh_ref, w_ref, wz_ref, o_ref, z_ref, h_scr, *, tm, tn):
    def chunk(c, carry):
        rows = pl.ds(pl.multiple_of(c * ROW_CHUNK, ROW_CHUNK), ROW_CHUNK)
        h = _rms(x_ref[rows, :]) * g_ref[...]
        h_scr[rows, :] = (h * (1.0 + sc_ref[0]) + sh_ref[0]).astype(BF16)
        return carry

    lax.fori_loop(0, tm // ROW_CHUNK, chunk, 0, unroll=CHUNK_UNROLL)
    h = h_scr[...]
    z_ref[...] = jnp.dot(h, wz_ref[...], preferred_element_type=F32)
    for c in range(D_MAIN // tn):
        cols = slice(c * tn, (c + 1) * tn)
        o_ref[:, cols] = jnp.dot(h, w_ref[:, cols], preferred_element_type=F32).astype(BF16)


def _inproj(x2d, g, sc, sh, w_all, w_z, seq, tm=512, tn=1024):
    t, d = x2d.shape
    tm = min(tm, seq)
    per_b = seq // tm
    kern = functools.partial(_inproj_kernel, tm=tm, tn=tn)
    return pl.pallas_call(
        kern,
        out_shape=(jax.ShapeDtypeStruct((t, D_MAIN), BF16), jax.ShapeDtypeStruct((t, LANES), F32)),
        grid=(t // tm,),
        in_specs=[pl.BlockSpec((tm, d), lambda i: (i, 0)),
                  pl.BlockSpec((1, d), lambda i: (0, 0)),
                  pl.BlockSpec((1, 1, d), lambda i: (i // per_b, 0, 0)),
                  pl.BlockSpec((1, 1, d), lambda i: (i // per_b, 0, 0)),
                  pl.BlockSpec(w_all.shape, lambda i: (0, 0), pipeline_mode=pl.Buffered(1)),
                  pl.BlockSpec((d, LANES), lambda i: (0, 0), pipeline_mode=pl.Buffered(1))],
        out_specs=(pl.BlockSpec((tm, D_MAIN), lambda i: (i, 0)),
                   pl.BlockSpec((tm, LANES), lambda i: (i, 0))),
        scratch_shapes=[pltpu.VMEM((tm, d), BF16)],
        compiler_params=pltpu.CompilerParams(dimension_semantics=("arbitrary",),
                                             vmem_limit_bytes=INPROJ_VMEM_LIMIT),
        name="in_proj",
    )(x2d, g, sc, sh, w_all, w_z)


def _t5_bucket(rel):
    nb = N_BUCKETS // 2
    max_exact = nb // 2
    base = jnp.where(rel > 0, nb, 0)
    n = jnp.abs(rel)
    nf = jnp.maximum(n, 1).astype(F32)
    large = max_exact + (jnp.log(nf / max_exact) / math.log(MAX_DISTANCE / max_exact)
                         * (nb - max_exact)).astype(jnp.int32)
    large = jnp.minimum(large, nb - 1)
    return base + jnp.where(n < max_exact, n, large)


def _bias_buckets(tq):
    kj = jnp.arange(tq, dtype=jnp.int32)[:, None]
    qi = jnp.arange(tq, dtype=jnp.int32)[None, :]
    near = _t5_bucket(kj - qi - tq)
    diag = jnp.where((kj // CHUNK) <= (qi // CHUNK), _t5_bucket(kj - qi), N_BUCKETS)
    return jnp.stack([near, diag]).astype(jnp.int32)


def _bias_kernel(rb_ref, bk_ref, o_ref):
    h = pl.program_id(0)
    far = rb_ref[N_BUCKETS // 2 - 1, h]
    bucket = bk_ref[...]
    acc = jnp.full(bucket.shape, NEG_INF, F32)
    for n in range(N_BUCKETS):
        acc = jnp.where(bucket == n, (rb_ref[n, h] - far) * LOG2E, acc)
    o_ref[...] = acc


def _bias_tiles(rel_bias, tq):
    return pl.pallas_call(
        _bias_kernel,
        out_shape=jax.ShapeDtypeStruct((A_HEADS, 2, tq, tq), F32),
        grid=(A_HEADS, 2),
        in_specs=[pl.BlockSpec(memory_space=pltpu.SMEM),
                  pl.BlockSpec((None, tq, tq), lambda h, d: (d, 0, 0))],
        out_specs=pl.BlockSpec((None, None, tq, tq), lambda h, d: (h, d, 0, 0)),
        compiler_params=_params("arbitrary", "arbitrary"),
        name="bias_tiles",
    )(rel_bias, _bias_buckets(tq))


def _attn_kernel(q_ref, k_ref, v_ref, bias_ref, lq1_ref, lk1_ref, lq2_ref, lk2_ref, g_ref, o_ref,
                 vt_scr, sa_scr, sb_scr, m_scr, acc_scr, *, tq, tk, lam_init):
    i = pl.program_id(2)
    nsub = tq // tk
    bufs = (sa_scr, sb_scr)

    @pl.when(i == 0)
    def _():
        ones = jnp.ones((ONES_ROWS, tk), BF16)
        for c in range(vt_scr.shape[0]):
            vt = v_ref[c * tk:(c + 1) * tk, :].astype(F32).T.astype(BF16)
            vt_scr[c] = jnp.concatenate([vt, ones], axis=0)

    lane = lax.broadcasted_iota(jnp.int32, (1, A_DV), 1)
    q = q_ref[...] * (A_DK ** -0.5 * LOG2E)
    zero = jnp.zeros_like(q)
    q2 = jnp.concatenate([jnp.where(lane < A_DK, q, zero), jnp.where(lane >= A_DK, q, zero)], axis=0)

    m_scr[...] = jnp.full(m_scr.shape, NEG_INF, F32)
    acc_scr[...] = jnp.zeros(acc_scr.shape, F32)

    def scores(j):
        k = k_ref[pl.ds(pl.multiple_of(j * tk, tk), tk), :]
        return lax.dot_general(k, q2, (((1,), (1,)), ((), ())), preferred_element_type=F32)

    def softmax_pv(s_ref, j, bias):
        s = s_ref[...]
        if bias is not None:
            s = jnp.concatenate([s[:, :tq] + bias, s[:, tq:] + bias], axis=1)
        m_old = m_scr[...]
        m_new = jnp.maximum(m_old, jnp.max(s, axis=0, keepdims=True))
        alpha = jnp.exp2(m_old - m_new)
        p = jnp.exp2(s - m_new).astype(BF16)
        acc_scr[...] = alpha * acc_scr[...] + jnp.dot(vt_scr[j], p, preferred_element_type=F32)
        m_scr[...] = m_new

    n_far = jnp.maximum(i - 1, 0) * nsub
    sa_scr[...] = scores(0)

    def far_steps(j, count):
        for c in range(count):
            bufs[(c + 1) % 2][...] = scores(j + c + 1)
            softmax_pv(bufs[c % 2], j + c, None)

    def far_quad(jj, carry):
        far_steps(4 * jj, 4)
        return carry

    n_quads = n_far // 4
    lax.fori_loop(0, n_quads, far_quad, 0)

    @pl.when(n_far - 4 * n_quads >= 2)
    def _():
        far_steps(4 * n_quads, 2)

    def biased_steps(first_tile):
        j0 = (i - 1 + first_tile) * nsub
        count = (2 - first_tile) * nsub
        for c in range(count):
            if c + 1 < count:
                bufs[(c + 1) % 2][...] = scores(j0 + c + 1)
            d, r = first_tile + c // nsub, (c % nsub) * tk
            softmax_pv(bufs[c % 2], j0 + c, bias_ref[d, r:r + tk, :])

    @pl.when(i >= 1)
    def _():
        biased_steps(0)

    @pl.when(i == 0)
    def _():
        biased_steps(1)

    lam = (jnp.exp(jnp.sum(lq1_ref[...] * lk1_ref[...], axis=-1, keepdims=True))
           - jnp.exp(jnp.sum(lq2_ref[...] * lk2_ref[...], axis=-1, keepdims=True)) + lam_init)
    on = acc_scr[:A_DV, :] / acc_scr[A_DV:A_DV + 1, :]
    o = on[:, :tq] - lam * on[:, tq:]
    y = o * lax.rsqrt(jnp.mean(o * o, axis=0, keepdims=True) + EPS) * g_ref[...] * (1.0 - lam_init)
    o_ref[...] = y.T.astype(BF16)


def _attention(proj, bias_tiles, lq1, lk1, lq2, lk2, g_sub_col, batch, seq, lam_init, tq, tk=256):
    t = proj.shape[0]
    nq = seq // tq
    assert (tq // tk) % 2 == 0 and tq % tk == 0, "the score pipeline alternates two buffers per query tile"
    kern = functools.partial(_attn_kernel, tq=tq, tk=tk, lam_init=lam_init)
    vec = lambda n: pl.BlockSpec((1, n), lambda b, h, i: (0, 0))
    return pl.pallas_call(
        kern,
        out_shape=jax.ShapeDtypeStruct((t, A_WIDTH), BF16),
        grid=(batch, A_HEADS, nq),
        in_specs=[pl.BlockSpec((tq, A_DV), lambda b, h, i: (b * nq + i, COL_QA + h)),
                  pl.BlockSpec((seq, A_DV), lambda b, h, i: (b, COL_KA + h)),
                  pl.BlockSpec((seq, A_DV), lambda b, h, i: (b, COL_VA + h)),
                  pl.BlockSpec((None, 2, tq, tq), lambda b, h, i: (h, 0, 0, 0)),
                  vec(A_DK), vec(A_DK), vec(A_DK), vec(A_DK),
                  pl.BlockSpec((A_DV, 1), lambda b, h, i: (0, 0))],
        out_specs=pl.BlockSpec((tq, A_DV), lambda b, h, i: (b * nq + i, h)),
        scratch_shapes=[pltpu.VMEM((seq // tk, A_DV + ONES_ROWS, tk), BF16),
                        pltpu.VMEM((tk, 2 * tq), F32), pltpu.VMEM((tk, 2 * tq), F32),
                        pltpu.VMEM((1, 2 * tq), F32),
                        pltpu.VMEM((A_DV + ONES_ROWS, 2 * tq), F32)],
        compiler_params=_params("arbitrary", "arbitrary", "arbitrary"),
        name="diff_attention",
    )(proj, proj, proj, bias_tiles, lq1, lk1, lq2, lk2, g_sub_col)


def _split3(a):
    a1 = a.astype(BF16)
    r1 = a - a1.astype(F32)
    a2 = r1.astype(BF16)
    return a1, a2, (r1 - a2.astype(F32)).astype(BF16)


def _sum3(x):
    return x[:, :B_DK] + x[:, B_DK:2 * B_DK] + x[:, 2 * B_DK:]


def _gla_kernel(q_ref, k_ref, v_ref, r_ref, z_ref, wa_ref, ba_ref, g_ref, o_ref,
                state_scr, mask_scr, kv_scr, st_scr, *, n_chunks):
    lc = n_chunks * CHUNK

    @pl.when(pl.program_id(2) == 0)
    def _():
        state_scr[...] = jnp.zeros(state_scr.shape, F32)
        row = lax.broadcasted_iota(jnp.int32, (lc, lc), 0)
        col = lax.broadcasted_iota(jnp.int32, (lc, lc), 1)
        same = (row // CHUNK) == (col // CHUNK)
        mask_scr[...] = (same & (row >= col)).astype(BF16)

    z = z_ref[...]
    zh = z.astype(BF16)
    zl = (z - zh.astype(F32)).astype(BF16)
    pre = jnp.dot(jnp.concatenate([zh, zl, zh], axis=1), wa_ref[...], preferred_element_type=F32) + ba_ref[...]
    log_a = (jnp.minimum(pre, 0.0) - jnp.log1p(jnp.exp(-jnp.abs(pre)))) * (1.0 / GATE_TAU)
    parts = jnp.concatenate(_split3(log_a), axis=1)
    cum = _sum3(jnp.dot(mask_scr[...], parts, preferred_element_type=F32))
    totals = [cum[(c + 1) * CHUNK - 1:(c + 1) * CHUNK, :] for c in range(n_chunks)]
    total = jnp.concatenate([jnp.broadcast_to(tc, (CHUNK, B_DK)) for tc in totals], axis=0)
    k_dec = (k_ref[...].astype(F32) * jnp.exp(total - cum)).astype(BF16)

    for c in range(n_chunks):
        rows = slice(c * CHUNK, (c + 1) * CHUNK)
        kv_scr[c] = lax.dot_general(v_ref[rows, :], k_dec[rows], (((0,), (0,)), ((), ())),
                                    preferred_element_type=F32)
    state = state_scr[...]
    for c in range(n_chunks):
        state = state * jnp.exp(totals[c]) + kv_scr[c]
        st_scr[c] = state.astype(BF16)
    state_scr[...] = state

    for c in range(n_chunks):
        rows = slice(c * CHUNK, (c + 1) * CHUNK)
        o = lax.dot_general(q_ref[rows, :], st_scr[c], (((1,), (1,)), ((), ())),
                            preferred_element_type=F32) * (B_DK ** -0.5)
        o_ref[rows, :] = (_rms(o) * g_ref[...] * _silu(r_ref[rows, :].astype(F32))).astype(BF16)


def _gla(proj, zb, w_alpha_pad, b_alpha, g_norm, batch, seq, lc=512):
    t = proj.shape[0]
    lc = min(lc, seq)
    nl = seq // lc
    n_chunks = lc // CHUNK
    kern = functools.partial(_gla_kernel, n_chunks=n_chunks)
    wa_hi = w_alpha_pad.astype(BF16)
    wa_lo = (w_alpha_pad - wa_hi.astype(F32)).astype(BF16)
    wa3 = jnp.concatenate([wa_hi, wa_hi, wa_lo], axis=0)
    return pl.pallas_call(
        kern,
        out_shape=jax.ShapeDtypeStruct((t, B_WIDTH), BF16),
        grid=(batch, B_HEADS, nl),
        in_specs=[pl.BlockSpec((lc, B_DK), lambda b, h, l: (b * nl + l, COL_QB + h)),
                  pl.BlockSpec((lc, B_DK), lambda b, h, l: (b * nl + l, COL_KB + h)),
                  pl.BlockSpec((lc, B_DV), lambda b, h, l: (b * nl + l, COL_VB256 + h)),
                  pl.BlockSpec((lc, B_DV), lambda b, h, l: (b * nl + l, COL_RB256 + h)),
                  pl.BlockSpec((lc, LANES), lambda b, h, l: (b * nl + l, 0)),
                  pl.BlockSpec((3 * LANES, B_DK), lambda b, h, l: (0, h)),
                  pl.BlockSpec((1, B_DK), lambda b, h, l: (0, h)),
                  pl.BlockSpec((1, B_DV), lambda b, h, l: (0, 0))],
        out_specs=pl.BlockSpec((lc, B_DV), lambda b, h, l: (b * nl + l, h)),
        scratch_shapes=[pltpu.VMEM((B_DV, B_DK), F32),
                        pltpu.VMEM((lc, lc), BF16),
                        pltpu.VMEM((n_chunks, B_DV, B_DK), F32),
                        pltpu.VMEM((n_chunks, B_DV, B_DK), BF16)],
        compiler_params=_params("arbitrary", "arbitrary", "arbitrary"),
        name="gla",
    )(proj, proj, proj, proj, zb, wa3, b_alpha, g_norm)


def _outproj_kernel(oa_ref, ob_ref, wo_ref, x_ref, gt_ref, gpost_ref, gpre_ref, sc_ref, sh_ref, wr_ref,
                    br_ref, x1_ref, h2_ref, lg_ref, *, tm):
    for p in range(tm // OUTPROJ_PIECE):
        p0 = p * OUTPROJ_PIECE
        prow = slice(p0, p0 + OUTPROJ_PIECE)
        y = (jnp.dot(oa_ref[prow, :], wo_ref[:A_WIDTH, :], preferred_element_type=F32)
             + jnp.dot(ob_ref[prow, :], wo_ref[A_WIDTH:, :], preferred_element_type=F32))
        his, los = [], []
        for c in range(OUTPROJ_PIECE // ROW_CHUNK):
            r0 = p0 + c * ROW_CHUNK
            rows = slice(r0, r0 + ROW_CHUNK)
            x1 = x_ref[rows, :] + gt_ref[0] * (_rms(y[c * ROW_CHUNK:(c + 1) * ROW_CHUNK]) * gpost_ref[...])
            x1_ref[rows, :] = x1
            h2 = (_rms(x1) * gpre_ref[...]) * (1.0 + sc_ref[0]) + sh_ref[0]
            hi = h2.astype(BF16)
            his.append(hi)
            los.append((h2 - hi.astype(F32)).astype(BF16))
            _store_row_slabs(h2_ref, r0, ROW_CHUNK, h2)
        hi, lo = jnp.concatenate(his, axis=0), jnp.concatenate(los, axis=0)
        hw = jnp.dot(hi, wr_ref[...], preferred_element_type=F32)
        lw = jnp.dot(lo, wr_ref[:, :LANES], preferred_element_type=F32)
        lg_ref[prow, :] = hw[:, :LANES] + hw[:, LANES:] + lw + br_ref[...]


def _outproj(oa, ob, w_out, x2d, gt, g_post, g_pre, sc, sh, w_router, b_router, seq, tm=512):
    t, d = x2d.shape
    tm = min(tm, seq)
    per_b = seq // tm
    kern = functools.partial(_outproj_kernel, tm=tm)
    wr_hi = w_router.astype(BF16)
    wr_lo = (w_router - wr_hi.astype(F32)).astype(BF16)
    wr_cat = jnp.concatenate([wr_hi, wr_lo], axis=1)
    row = lambda: pl.BlockSpec((1, d), lambda i: (0, 0))
    per_batch = lambda: pl.BlockSpec((1, 1, d), lambda i: (i // per_b, 0, 0))
    return pl.pallas_call(
        kern,
        out_shape=(jax.ShapeDtypeStruct((t, d), F32),
                   jax.ShapeDtypeStruct((t * ROW_SLABS, LANES), U32),
                   jax.ShapeDtypeStruct((t, LANES), F32)),
        grid=(t // tm,),
        in_specs=[pl.BlockSpec((tm, A_WIDTH), lambda i: (i, 0)),
                  pl.BlockSpec((tm, B_WIDTH), lambda i: (i, 0)),
                  pl.BlockSpec((d, d), lambda i: (0, 0)),
                  pl.BlockSpec((tm, d), lambda i: (i, 0)),
                  per_batch(), row(), row(), per_batch(), per_batch(),
                  pl.BlockSpec((d, 2 * LANES), lambda i: (0, 0)),
                  pl.BlockSpec((1, LANES), lambda i: (0, 0))],
        out_specs=(pl.BlockSpec((tm, d), lambda i: (i, 0)),
                   pl.BlockSpec((tm * ROW_SLABS, LANES), lambda i: (i, 0)),
                   pl.BlockSpec((tm, LANES), lambda i: (i, 0))),
        compiler_params=_params("arbitrary"),
        name="out_proj",
    )(oa, ob, w_out, x2d, gt, g_post, g_pre, sc, sh, wr_cat, b_router)


def _route_kernel(lg_ref, rec_ref, cnt_ref, carry_scr, *, tr):
    @pl.when(pl.program_id(0) == 0)
    def _():
        carry_scr[...] = jnp.zeros(carry_scr.shape, F32)

    lg = lg_ref[...]
    lane = lax.broadcasted_iota(jnp.int32, lg.shape, 1)
    big = jnp.int32(LANES)

    def first_lane(mask):
        return jnp.min(jnp.where(mask, lane, big), axis=-1, keepdims=True)

    gmask = lane < N_GROUPS
    gmax = jnp.max(jnp.where(gmask, lg, -jnp.inf), axis=-1, keepdims=True)
    gexp = jnp.where(gmask, jnp.exp(lg - gmax), 0.0)
    gprob = gexp / jnp.sum(gexp, axis=-1, keepdims=True)
    g_val = jnp.max(gprob, axis=-1, keepdims=True)
    g_idx = first_lane(gmask & (gprob == g_val))

    lo = ROUTER_EXPERT_LANE0 + g_idx * EXPERTS_PER_GROUP
    emask = (lane >= lo) & (lane < lo + EXPERTS_PER_GROUP)
    emax = jnp.max(jnp.where(emask, lg, -jnp.inf), axis=-1, keepdims=True)
    eexp = jnp.where(emask, jnp.exp(lg - emax), 0.0)
    eprob = eexp / jnp.sum(eexp, axis=-1, keepdims=True)
    v1 = jnp.max(eprob, axis=-1, keepdims=True)
    i1 = first_lane(emask & (eprob == v1))
    rest = emask & (lane != i1)
    v2 = jnp.max(jnp.where(rest, eprob, -1.0), axis=-1, keepdims=True)
    i2 = first_lane(rest & (eprob == v2))
    w1 = g_val * (v1 / (v1 + v2))
    w2 = g_val * (v2 / (v1 + v2))

    hit1 = lane == i1
    hit2 = lane == i2
    onehot = (hit1 | hit2).astype(BF16)
    r = lax.broadcasted_iota(jnp.int32, (tr, tr), 0)
    c = lax.broadcasted_iota(jnp.int32, (tr, tr), 1)
    before = (c < r).astype(BF16)
    pos = carry_scr[...] + jnp.dot(before, onehot, preferred_element_type=F32)
    rank1 = jnp.sum(jnp.where(hit1, pos, 0.0), axis=-1, keepdims=True)
    rank2 = jnp.sum(jnp.where(hit2, pos, 0.0), axis=-1, keepdims=True)
    carry_scr[...] = carry_scr[...] + jnp.sum(onehot.astype(F32), axis=0, keepdims=True)
    cnt_ref[...] = carry_scr[...]

    e1 = (i1 - ROUTER_EXPERT_LANE0).astype(F32)
    e2 = (i2 - ROUTER_EXPERT_LANE0).astype(F32)
    rec = jnp.zeros(lg.shape, F32)
    for ln, val in ((ROUTE_E1, e1), (ROUTE_E2, e2), (ROUTE_W1, w1), (ROUTE_W2, w2),
                    (ROUTE_R1, rank1), (ROUTE_R2, rank2)):
        rec = jnp.where(lane == ln, val, rec)
    rec_ref[...] = rec


def _route(logits, tr=512):
    t = logits.shape[0]
    tr = min(tr, t)
    kern = functools.partial(_route_kernel, tr=tr)
    return pl.pallas_call(
        kern,
        out_shape=(jax.ShapeDtypeStruct((t, LANES), F32), jax.ShapeDtypeStruct((1, LANES), F32)),
        grid=(t // tr,),
        in_specs=[pl.BlockSpec((tr, LANES), lambda i: (i, 0))],
        out_specs=(pl.BlockSpec((tr, LANES), lambda i: (i, 0)),
                   pl.BlockSpec((1, LANES), lambda i: (0, 0))),
        scratch_shapes=[pltpu.VMEM((1, LANES), F32)],
        compiler_params=_params("arbitrary"),
        name="route",
    )(logits)


def _slab_rows(ref, row):
    return ref.at[pl.ds(pl.multiple_of(row * ROW_SLABS, ROW_SLABS), ROW_SLABS), :]


def _dispatch_kernel(slot_ref, pad_start_ref, pad_len_ref, used_ref, h2_ref, xs_hbm, zero_scr, sems, pad_sems,
                     *, td, tm):
    g = pl.program_id(0)
    tile_rows = tm * ROW_SLABS
    n_tiles = xs_hbm.shape[0] // tile_rows

    def zero_copy(slot, nslots, sem):
        rows = pl.ds(pl.multiple_of(slot * ROW_SLABS, ROW_SLABS), nslots * ROW_SLABS)
        return pltpu.make_async_copy(zero_scr.at[pl.ds(0, nslots * ROW_SLABS), :], xs_hbm.at[rows, :], sem)

    pad_sizes = [1 << b for b in reversed(range((tm - 1).bit_length()))]

    @pl.when(g == 0)
    def _():
        zero_scr[...] = jnp.zeros(zero_scr.shape, U32)

        def unused_tile(tile, carry):
            zero_copy(tile * tm, tm, sems.at[1]).start()
            zero_copy(tile * tm, tm, sems.at[1]).wait()
            return carry

        lax.fori_loop(used_ref[0], n_tiles, unused_tile, 0)

        def per_expert(e, counts):
            off = pad_start_ref[e]
            n = pad_len_ref[e]
            new_counts = []
            for b, size in enumerate(pad_sizes):
                hit = (n & size) != 0

                @pl.when(hit)
                def _():
                    zero_copy(off, size, pad_sems.at[b]).start()

                off = off + jnp.where(hit, size, 0)
                new_counts.append(counts[b] + hit.astype(jnp.int32))
            return tuple(new_counts)

        counts = lax.fori_loop(0, N_EXPERTS, per_expert, tuple(jnp.int32(0) for _ in pad_sizes))
        for b, size in enumerate(pad_sizes):
            def drain(r, c):
                zero_copy(0, size, pad_sems.at[b]).wait()
                return c
            lax.fori_loop(0, counts[b], drain, 0)

    def row_copy(r, slot):
        return pltpu.make_async_copy(_slab_rows(h2_ref, r), _slab_rows(xs_hbm, slot), sems.at[0])

    base = g * td

    def issue(r, c):
        tok = base + r
        row_copy(r, slot_ref[2 * tok]).start()
        row_copy(r, slot_ref[2 * tok + 1]).start()
        return c

    lax.fori_loop(0, td, issue, 0, unroll=CHUNK_UNROLL)
    for _ in range(2):
        pltpu.make_async_copy(h2_ref, xs_hbm.at[pl.ds(0, td * ROW_SLABS), :], sems.at[0]).wait()


def _dispatch(slot, pad_start, pad_len, used, h2_rows, n_slots, tm, td=512):
    t = slot.shape[0] // 2
    td = min(td, t)
    kern = functools.partial(_dispatch_kernel, td=td, tm=tm)
    grid_spec = pltpu.PrefetchScalarGridSpec(
        num_scalar_prefetch=4,
        grid=(t // td,),
        in_specs=[pl.BlockSpec((td * ROW_SLABS, LANES), lambda g, sl, ps, pn, us: (g, 0))],
        out_specs=pl.BlockSpec(memory_space=pl.ANY),
        scratch_shapes=[pltpu.VMEM((tm * ROW_SLABS, LANES), U32), pltpu.SemaphoreType.DMA((2,)),
                        pltpu.SemaphoreType.DMA(((tm - 1).bit_length(),))],
    )
    return pl.pallas_call(
        kern,
        out_shape=jax.ShapeDtypeStruct((n_slots * ROW_SLABS, LANES), U32),
        grid_spec=grid_spec,
        compiler_params=_params("arbitrary"),
        name="dispatch",
    )(slot, pad_start, pad_len, used, h2_rows)


TILE_UNUSED, TILE_USED, TILE_NEW_EXPERT = 0, 1, 2


def _expert_kernel(texp_ref, tblk_ref, tstate_ref, xs_ref, w1_ref, w3_ref, w2_ref, eo_ref,
                   x_scr, w1_scr, w3_scr, w2_scr, *, tm):
    state = tstate_ref[pl.program_id(0)]

    @pl.when(state == TILE_UNUSED)
    def _():
        eo_ref[...] = jnp.zeros(eo_ref.shape, U32)

    @pl.when(state == TILE_NEW_EXPERT)
    def _():
        w1_scr[...] = w1_ref[...].astype(BF16)
        w3_scr[...] = w3_ref[...].astype(BF16)
        w2_scr[...] = w2_ref[...].astype(BF16)

    @pl.when(state != TILE_UNUSED)
    def _():
        for s in range(ROW_SLABS):
            lo, hi = _unpack_bf16_pair(xs_ref[pl.ds(s, tm, stride=ROW_SLABS), :])
            x_scr[:, s * LANES:(s + 1) * LANES] = lo.astype(BF16)
            x_scr[:, HALF_D + s * LANES:HALF_D + (s + 1) * LANES] = hi.astype(BF16)
        x = x_scr[...]
        a = jnp.dot(x, w1_scr[...], preferred_element_type=F32)
        b = jnp.dot(x, w3_scr[...], preferred_element_type=F32)
        hid = (_silu(a) * b).astype(BF16)
        y = jnp.dot(hid, w2_scr[...], preferred_element_type=F32)
        _store_row_slabs(eo_ref, 0, tm, y)


def _experts(tile_expert, tile_block, tile_used, xs_rows, w1, w3, w2, tm):
    n_tiles = tile_expert.shape[0]
    d, f = w1.shape[1], w1.shape[2]
    kern = functools.partial(_expert_kernel, tm=tm)
    grid_spec = pltpu.PrefetchScalarGridSpec(
        num_scalar_prefetch=3,
        grid=(n_tiles,),
        in_specs=[pl.BlockSpec((tm * ROW_SLABS, LANES), lambda i, te, tb, tu: (tb[i], 0)),
                  pl.BlockSpec((None, d, f), lambda i, te, tb, tu: (te[i], 0, 0)),
                  pl.BlockSpec((None, d, f), lambda i, te, tb, tu: (te[i], 0, 0)),
                  pl.BlockSpec((None, f, d), lambda i, te, tb, tu: (te[i], 0, 0))],
        out_specs=pl.BlockSpec((tm * ROW_SLABS, LANES), lambda i, te, tb, tu: (i, 0)),
        scratch_shapes=[pltpu.VMEM((tm, d), BF16), pltpu.VMEM((d, f), BF16), pltpu.VMEM((d, f), BF16),
                        pltpu.VMEM((f, d), BF16)],
    )
    return pl.pallas_call(
        kern,
        out_shape=jax.ShapeDtypeStruct(xs_rows.shape, U32),
        grid_spec=grid_spec,
        compiler_params=_params("arbitrary"),
        name="expert_mlp",
    )(tile_expert, tile_block, tile_used, xs_rows, w1, w3, w2)


def _final_kernel(slot_ref, eo_hbm, rec_ref, x1_ref, gt_ref, g_ref, o_ref, e_scr, sems, *, tf):
    i = pl.program_id(0)
    par = i % 2

    def start_all(step, buf):
        def body(r, c):
            tok = step * tf + r
            for k in range(2):
                pltpu.make_async_copy(_slab_rows(eo_hbm, slot_ref[2 * tok + k]),
                                      _slab_rows(e_scr.at[buf, k], r), sems.at[buf]).start()
            return c
        lax.fori_loop(0, tf, body, 0, unroll=CHUNK_UNROLL)

    def wait_all(buf):
        for k in range(2):
            pltpu.make_async_copy(eo_hbm.at[pl.ds(0, tf * ROW_SLABS), :], e_scr.at[buf, k], sems.at[buf]).wait()

    @pl.when(i == 0)
    def _():
        start_all(0, 0)

    @pl.when(i + 1 < pl.num_programs(0))
    def _():
        start_all(i + 1, 1 - par)

    wait_all(par)

    def chunk(c, carry):
        r0 = pl.multiple_of(c * ROW_CHUNK, ROW_CHUNK)
        rows = pl.ds(r0, ROW_CHUNK)
        rec = rec_ref[rows, :]
        w1 = rec[:, ROUTE_W1:ROUTE_W1 + 1]
        w2 = rec[:, ROUTE_W2:ROUTE_W2 + 1]
        lo1, hi1 = _load_row_slabs(e_scr.at[par, 0], r0, ROW_CHUNK)
        lo2, hi2 = _load_row_slabs(e_scr.at[par, 1], r0, ROW_CHUNK)
        y = jnp.concatenate([w1 * lo1 + w2 * lo2, w1 * hi1 + w2 * hi2], axis=1)
        o_ref[rows, :] = x1_ref[rows, :] + gt_ref[0] * (_rms(y) * g_ref[...])
        return carry

    lax.fori_loop(0, tf // ROW_CHUNK, chunk, 0, unroll=CHUNK_UNROLL)


def _final(slot, eo_rows, rec, x1, gt, g_post, seq, tf=256):
    t, d = x1.shape
    tf = min(tf, seq)
    per_b = seq // tf
    kern = functools.partial(_final_kernel, tf=tf)
    grid_spec = pltpu.PrefetchScalarGridSpec(
        num_scalar_prefetch=1,
        grid=(t // tf,),
        in_specs=[pl.BlockSpec(memory_space=pl.ANY),
                  pl.BlockSpec((tf, LANES), lambda i, sl: (i, 0)),
                  pl.BlockSpec((tf, d), lambda i, sl: (i, 0)),
                  pl.BlockSpec((1, 1, d), lambda i, sl: (i // per_b, 0, 0)),
                  pl.BlockSpec((1, d), lambda i, sl: (0, 0))],
        out_specs=pl.BlockSpec((tf, d), lambda i, sl: (i, 0)),
        scratch_shapes=[pltpu.VMEM((2, 2, tf * ROW_SLABS, LANES), U32),
                        pltpu.SemaphoreType.DMA((2,))],
    )
    return pl.pallas_call(
        kern,
        out_shape=jax.ShapeDtypeStruct((t, d), F32),
        grid_spec=grid_spec,
        compiler_params=_params("arbitrary"),
        name="combine_final",
    )(slot, eo_rows, rec, x1, gt, g_post)


def _dispatch_tables(rec, counts, t, tm):
    e = rec[:, ROUTE_E1:ROUTE_E2 + 1].astype(jnp.int32)
    rank = rec[:, ROUTE_R1:ROUTE_R2 + 1].astype(jnp.int32)
    cnt = counts[0, ROUTER_EXPERT_LANE0:ROUTER_EXPERT_LANE0 + N_EXPERTS].astype(jnp.int32)
    tiles_per = (cnt + tm - 1) // tm
    tile_end = jnp.cumsum(tiles_per)
    tile_start = tile_end - tiles_per
    n_tiles = (2 * t + N_EXPERTS * (tm - 1)) // tm
    experts = jnp.arange(N_EXPERTS, dtype=jnp.int32)
    start_of = jnp.sum(jnp.where(e[..., None] == experts, tile_start, 0), axis=-1)
    slot = (start_of * tm + rank).reshape(-1)
    pad_start = tile_start * tm + cnt
    pad_len = tiles_per * tm - cnt
    tile_id = jnp.arange(n_tiles, dtype=jnp.int32)
    used = tile_end[-1]
    tblk = jnp.minimum(tile_id, used - 1)
    texp = jnp.sum(tile_end[None, :] <= tblk[:, None], axis=-1).astype(jnp.int32)
    tstate = jnp.where(tile_id < used, jnp.where(tile_id == tile_start[texp], TILE_NEW_EXPERT, TILE_USED),
                       TILE_UNUSED).astype(jnp.int32)
    return slot, pad_start, pad_len, used.reshape(1), texp, tblk, tstate, n_tiles * tm


def kernel(x, c, rel_bias, w_ada, b_ada, g_pre_mix, g_post_mix, w_in, w_alpha, b_alpha, lam_q1, lam_k1, lam_q2,
           lam_k2, g_sub_a, g_norm_b, w_out, g_pre_ffn, g_post_ffn, w_router_g, b_router_g, w_router_e,
           b_router_e, w1, w3, w2):
    batch, seq, d = x.shape
    t = batch * seq
    depth = w_in.shape[0]
    tq = min(512, seq)
    tm_e = 512
    xf = x.reshape(t, d)
    for i in range(depth):
        lam_init = 0.8 - 0.6 * math.exp(-0.3 * i)
        c_pad = jnp.pad(c, ((0, 8 - batch % 8 if batch % 8 else 0), (0, 0)))
        ada = _ada(c_pad, w_ada[i], b_ada[i][None, :])[:batch]
        sh_m, sc_m, gt_m, sh_f, sc_f, gt_f = [a[:, None, :] for a in jnp.split(ada, 6, axis=-1)]

        w_in_b = w_in[i].astype(BF16)
        w_z = jnp.pad(w_in_b[:, D_MAIN:], ((0, 0), (0, LANES - GATE_RANK)))
        proj, zb = _inproj(xf, g_pre_mix[i][None, :], sc_m, sh_m, w_in_b, w_z, seq)

        oa = _attention(proj, _bias_tiles(rel_bias, tq), lam_q1[i][None, :], lam_k1[i][None, :],
                        lam_q2[i][None, :], lam_k2[i][None, :], g_sub_a[i][:, None], batch, seq, lam_init, tq)
        w_alpha_pad = jnp.pad(w_alpha[i], ((0, LANES - GATE_RANK), (0, 0)))
        ob = _gla(proj, zb, w_alpha_pad, b_alpha[i][None, :], g_norm_b[i][None, :], batch, seq)

        w_router = jnp.pad(jnp.concatenate([w_router_g[i], w_router_e[i]], axis=1),
                           ((0, 0), (0, LANES - N_GROUPS - N_EXPERTS)))
        b_router = jnp.pad(jnp.concatenate([b_router_g[i], b_router_e[i]]),
                           (0, LANES - N_GROUPS - N_EXPERTS))[None, :]
        x1, h2_rows, logits = _outproj(oa, ob, w_out[i].astype(BF16), xf, gt_m, g_post_mix[i][None, :],
                                       g_pre_ffn[i][None, :], sc_f, sh_f, w_router, b_router, seq)

        rec, counts = _route(logits)
        slot, pad_start, pad_len, used, texp, tblk, tstate, n_slots = _dispatch_tables(rec, counts, t, tm_e)
        xs = _dispatch(slot, pad_start, pad_len, used, h2_rows, n_slots, tm_e)
        eo = _experts(texp, tblk, tstate, xs, w1[i], w3[i], w2[i], tm_e)
        xf = _final(slot, eo, rec, x1, gt_f, g_post_ffn[i][None, :], seq)
    return xf.reshape(batch, seq, d)
```

```python
import functools
import math

import jax
import jax.numpy as jnp
from jax import lax
from jax.experimental import pallas as pl
from jax.experimental.pallas import tpu as pltpu

F32 = jnp.float32
BF16 = jnp.bfloat16

D_MODEL = 2048
CHUNK = 64
A_HEADS = 8
A_DK = 64
A_DV = 2 * A_DK
A_WIDTH = A_HEADS * A_DV
B_HEADS = 4
B_WIDTH = D_MODEL - A_WIDTH
B_DV = B_WIDTH // B_HEADS
B_DK = B_DV // 2
GATE_RANK = 16
GATE_TAU = 16.0
N_BUCKETS = 32
MAX_DISTANCE = 256
N_GROUPS = 4
EXPERTS_PER_GROUP = 8
N_EXPERTS = N_GROUPS * EXPERTS_PER_GROUP
D_EXPERT = D_MODEL // 4
EPS = 1e-6
NEG_INF = -1e30
LOG2E = math.log2(math.e)

LANES = 128
U32 = jnp.uint32
HALF_D = D_MODEL // 2
ROW_SLABS = HALF_D // LANES
ROW_CHUNK = 16
CHUNK_UNROLL = 4
OUTPROJ_PIECE = 128
ONES_ROWS = 16
D_MAIN = 3 * A_WIDTH + 2 * B_HEADS * B_DK + 2 * B_WIDTH
COL_QA, COL_KA, COL_VA = 0, A_HEADS, 2 * A_HEADS
COL_QB = 3 * A_HEADS
COL_KB = COL_QB + B_HEADS
COL_VB256 = (3 * A_WIDTH + 2 * B_HEADS * B_DK) // B_DV
COL_RB256 = COL_VB256 + B_HEADS
ROUTE_E1, ROUTE_E2, ROUTE_W1, ROUTE_W2, ROUTE_R1, ROUTE_R2 = 0, 1, 2, 3, 4, 5
ROUTER_EXPERT_LANE0 = N_GROUPS

VMEM_LIMIT = 56 * 1024 * 1024
INPROJ_VMEM_LIMIT = 60 * 1024 * 1024


def _params(*sem):
    return pltpu.CompilerParams(dimension_semantics=sem, vmem_limit_bytes=VMEM_LIMIT)


def _rms(v):
    return v * lax.rsqrt(jnp.mean(v * v, axis=-1, keepdims=True) + EPS)


def _silu(v):
    return v * jax.nn.sigmoid(v)


_HIGH_HALF = 0xFFFF0000


def _pack_bf16_pair(lo, hi):
    lo_bits = lax.bitcast_convert_type(lo.astype(BF16).astype(F32), U32) >> 16
    hi_bits = lax.bitcast_convert_type(hi.astype(BF16).astype(F32), U32) & U32(_HIGH_HALF)
    return hi_bits | lo_bits


def _unpack_bf16_pair(w):
    return (lax.bitcast_convert_type(w << 16, F32), lax.bitcast_convert_type(w & U32(_HIGH_HALF), F32))


def _store_row_slabs(ref, r0, nrows, rows_f32):
    packed = _pack_bf16_pair(rows_f32[:, :HALF_D], rows_f32[:, HALF_D:])
    for s in range(ROW_SLABS):
        ref[pl.ds(r0 * ROW_SLABS + s, nrows, stride=ROW_SLABS), :] = packed[:, s * LANES:(s + 1) * LANES]


def _load_row_slabs(ref, r0, nrows):
    slabs = [_unpack_bf16_pair(ref[pl.ds(r0 * ROW_SLABS + s, nrows, stride=ROW_SLABS), :]) for s in range(ROW_SLABS)]
    return (jnp.concatenate([lo for lo, _ in slabs], axis=1), jnp.concatenate([hi for _, hi in slabs], axis=1))


def _ada_kernel(c_ref, w_ref, b_ref, o_ref):
    s = _silu(c_ref[...])
    o_ref[...] = jnp.dot(s.astype(BF16), w_ref[...].astype(BF16), preferred_element_type=F32) + b_ref[...]


def _ada(c_pad, w, b, tn=1024):
    m, d = c_pad.shape
    n = w.shape[1]
    return pl.pallas_call(
        _ada_kernel,
        out_shape=jax.ShapeDtypeStruct((m, n), F32),
        grid=(n // tn,),
        in_specs=[pl.BlockSpec((m, d), lambda j: (0, 0)),
                  pl.BlockSpec((d, tn), lambda j: (0, j)),
                  pl.BlockSpec((1, tn), lambda j: (0, j))],
        out_specs=pl.BlockSpec((m, tn), lambda j: (0, j)),
        compiler_params=_params("arbitrary"),
        name="ada_proj",
    )(c_pad, w, b)


def _inproj_kernel(x_ref, xn_ref, g_ref, sc_ref, sh_ref, scn_ref, shn_ref, w_ref, wz_ref, o_ref, z_ref,
                   h_scr, hn_scr, *, tm, tn):
    def normed(src_ref, scale_ref, shift_ref, rows):
        h = _rms(src_ref[rows, :]) * g_ref[...]
        return (h * (1.0 + scale_ref[0]) + shift_ref[0]).astype(BF16)

    @pl.when(pl.program_id(0) == 0)
    def _():
        def chunk(c, carry):
            rows = pl.ds(pl.multiple_of(c * ROW_CHUNK, ROW_CHUNK), ROW_CHUNK)
            h_scr[rows, :] = normed(x_ref, sc_ref, sh_ref, rows)
            return carry
        lax.fori_loop(0, tm // ROW_CHUNK, chunk, 0, unroll=CHUNK_UNROLL)

    h = h_scr[...]
    z_ref[...] = jnp.dot(h, wz_ref[...], preferred_element_type=F32)
    n_col = D_MAIN // tn
    rows_per_col = tm // n_col // ROW_CHUNK * ROW_CHUNK
    next_row = 0
    for c in range(n_col):
        cols = slice(c * tn, (c + 1) * tn)
        o_ref[:, cols] = jnp.dot(h, w_ref[:, cols], preferred_element_type=F32).astype(BF16)
        stop = tm if c == n_col - 1 else next_row + rows_per_col
        for r0 in range(next_row, stop, ROW_CHUNK):
            rows = slice(r0, r0 + ROW_CHUNK)
            hn_scr[rows, :] = normed(xn_ref, scn_ref, shn_ref, rows)
        next_row = stop
    h_scr[...] = hn_scr[...]


def _inproj(x2d, g, sc, sh, w_all, w_z, seq, tm=512, tn=1024):
    t, d = x2d.shape
    tm = min(tm, seq)
    per_b = seq // tm
    n_steps = t // tm
    kern = functools.partial(_inproj_kernel, tm=tm, tn=tn)
    nxt = lambda i: jnp.minimum(i + 1, n_steps - 1)
    per_batch = lambda step: pl.BlockSpec((1, 1, d), lambda i: (step(i) // per_b, 0, 0))
    return pl.pallas_call(
        kern,
        out_shape=(jax.ShapeDtypeStruct((t, D_MAIN), BF16), jax.ShapeDtypeStruct((t, LANES), F32)),
        grid=(n_steps,),
        in_specs=[pl.BlockSpec((tm, d), lambda i: (i, 0)),
                  pl.BlockSpec((tm, d), lambda i: (nxt(i), 0)),
                  pl.BlockSpec((1, d), lambda i: (0, 0)),
                  per_batch(lambda i: i), per_batch(lambda i: i), per_batch(nxt), per_batch(nxt),
                  pl.BlockSpec(w_all.shape, lambda i: (0, 0), pipeline_mode=pl.Buffered(1)),
                  pl.BlockSpec((d, LANES), lambda i: (0, 0), pipeline_mode=pl.Buffered(1))],
        out_specs=(pl.BlockSpec((tm, D_MAIN), lambda i: (i, 0)),
                   pl.BlockSpec((tm, LANES), lambda i: (i, 0))),
        scratch_shapes=[pltpu.VMEM((tm, d), BF16), pltpu.VMEM((tm, d), BF16)],
        compiler_params=pltpu.CompilerParams(dimension_semantics=("arbitrary",),
                                             vmem_limit_bytes=INPROJ_VMEM_LIMIT),
        name="in_proj",
    )(x2d, x2d, g, sc, sh, sc, sh, w_all, w_z)


def _t5_bucket(rel):
    nb = N_BUCKETS // 2
    max_exact = nb // 2
    base = jnp.where(rel > 0, nb, 0)
    n = jnp.abs(rel)
    nf = jnp.maximum(n, 1).astype(F32)
    large = max_exact + (jnp.log(nf / max_exact) / math.log(MAX_DISTANCE / max_exact)
                         * (nb - max_exact)).astype(jnp.int32)
    large = jnp.minimum(large, nb - 1)
    return base + jnp.where(n < max_exact, n, large)


def _bias_buckets(tq):
    kj = jnp.arange(tq, dtype=jnp.int32)[:, None]
    qi = jnp.arange(tq, dtype=jnp.int32)[None, :]
    near = _t5_bucket(kj - qi - tq)
    diag = jnp.where((kj // CHUNK) <= (qi // CHUNK), _t5_bucket(kj - qi), N_BUCKETS)
    return jnp.stack([near, diag]).astype(jnp.int32)


def _bias_kernel(rb_ref, bk_ref, o_ref):
    h = pl.program_id(0)
    far = rb_ref[N_BUCKETS // 2 - 1, h]
    bucket = bk_ref[...]
    acc = jnp.full(bucket.shape, NEG_INF, F32)
    for n in range(N_BUCKETS):
        acc = jnp.where(bucket == n, (rb_ref[n, h] - far) * LOG2E, acc)
    o_ref[...] = acc


def _bias_tiles(rel_bias, tq):
    return pl.pallas_call(
        _bias_kernel,
        out_shape=jax.ShapeDtypeStruct((A_HEADS, 2, tq, tq), F32),
        grid=(A_HEADS, 2),
        in_specs=[pl.BlockSpec(memory_space=pltpu.SMEM),
                  pl.BlockSpec((None, tq, tq), lambda h, d: (d, 0, 0))],
        out_specs=pl.BlockSpec((None, None, tq, tq), lambda h, d: (h, d, 0, 0)),
        compiler_params=_params("arbitrary", "arbitrary"),
        name="bias_tiles",
    )(rel_bias, _bias_buckets(tq))


def _attn_kernel(q_ref, qn_ref, k_ref, v_ref, bias_ref, lq1_ref, lk1_ref, lq2_ref, lk2_ref, g_ref, o_ref,
                 vt_scr, sa_scr, sb_scr, m_scr, acc_scr, *, tq, tk, lam_init):
    i = pl.program_id(2)
    nsub = tq // tk
    bufs = (sa_scr, sb_scr)

    @pl.when(i == 0)
    def _():
        ones = jnp.ones((ONES_ROWS, tk), BF16)
        for c in range(vt_scr.shape[0]):
            vt = v_ref[c * tk:(c + 1) * tk, :].astype(F32).T.astype(BF16)
            vt_scr[c] = jnp.concatenate([vt, ones], axis=0)

    lane = lax.broadcasted_iota(jnp.int32, (1, A_DV), 1)

    def two_map_queries(ref):
        q = ref[...] * (A_DK ** -0.5 * LOG2E)
        zero = jnp.zeros_like(q)
        return jnp.concatenate([jnp.where(lane < A_DK, q, zero), jnp.where(lane >= A_DK, q, zero)], axis=0)

    q2 = two_map_queries(q_ref)
    m_scr[...] = jnp.full(m_scr.shape, NEG_INF, F32)
    acc_scr[...] = jnp.zeros(acc_scr.shape, F32)

    def scores(j, queries=q2):
        k = k_ref[pl.ds(pl.multiple_of(j * tk, tk), tk), :]
        return lax.dot_general(k, queries, (((1,), (1,)), ((), ())), preferred_element_type=F32)

    def softmax_pv(s_ref, j, bias):
        s = s_ref[...]
        if bias is not None:
            s = jnp.concatenate([s[:, :tq] + bias, s[:, tq:] + bias], axis=1)
        m_old = m_scr[...]
        m_new = jnp.maximum(m_old, jnp.max(s, axis=0, keepdims=True))
        alpha = jnp.exp2(m_old - m_new)
        p = jnp.exp2(s - m_new).astype(BF16)
        acc_scr[...] = alpha * acc_scr[...] + jnp.dot(vt_scr[j], p, preferred_element_type=F32)
        m_scr[...] = m_new

    n_far = jnp.maximum(i - 1, 0) * nsub

    @pl.when(i == 0)
    def _():
        sa_scr[...] = scores(0)

    def far_steps(j, count):
        for c in range(count):
            bufs[(c + 1) % 2][...] = scores(j + c + 1)
            softmax_pv(bufs[c % 2], j + c, None)

    def far_quad(jj, carry):
        far_steps(4 * jj, 4)
        return carry

    n_quads = n_far // 4
    lax.fori_loop(0, n_quads, far_quad, 0)

    @pl.when(n_far - 4 * n_quads >= 2)
    def _():
        far_steps(4 * n_quads, 2)

    def biased_steps(first_tile):
        j0 = (i - 1 + first_tile) * nsub
        count = (2 - first_tile) * nsub
        for c in range(count):
            if c + 1 < count:
                bufs[(c + 1) % 2][...] = scores(j0 + c + 1)
            d, r = first_tile + c // nsub, (c % nsub) * tk
            softmax_pv(bufs[c % 2], j0 + c, bias_ref[d, r:r + tk, :])

    def finish():
        lam = (jnp.exp(jnp.sum(lq1_ref[...] * lk1_ref[...], axis=-1, keepdims=True))
               - jnp.exp(jnp.sum(lq2_ref[...] * lk2_ref[...], axis=-1, keepdims=True)) + lam_init)
        on = acc_scr[:A_DV, :] / acc_scr[A_DV:A_DV + 1, :]
        o = on[:, :tq] - lam * on[:, tq:]
        y = o * lax.rsqrt(jnp.mean(o * o, axis=0, keepdims=True) + EPS) * g_ref[...] * (1.0 - lam_init)
        o_ref[...] = y.T.astype(BF16)

    last = i + 1 == pl.num_programs(2)
    for first_tile, applies in ((0, i >= 1), (1, i == 0)):
        @pl.when(applies & jnp.logical_not(last))
        def _():
            biased_steps(first_tile)
            sa_scr[...] = scores(0, two_map_queries(qn_ref))
            finish()

        @pl.when(applies & last)
        def _():
            biased_steps(first_tile)
            finish()


def _attention(proj, bias_tiles, lq1, lk1, lq2, lk2, g_sub_col, batch, seq, lam_init, tq, tk=256):
    t = proj.shape[0]
    nq = seq // tq
    assert (tq // tk) % 2 == 0 and tq % tk == 0, "the score pipeline alternates two buffers per query tile"
    kern = functools.partial(_attn_kernel, tq=tq, tk=tk, lam_init=lam_init)
    vec = lambda n: pl.BlockSpec((1, n), lambda b, h, i: (0, 0))
    return pl.pallas_call(
        kern,
        out_shape=jax.ShapeDtypeStruct((t, A_WIDTH), BF16),
        grid=(batch, A_HEADS, nq),
        in_specs=[pl.BlockSpec((tq, A_DV), lambda b, h, i: (b * nq + i, COL_QA + h)),
                  pl.BlockSpec((tq, A_DV), lambda b, h, i: (b * nq + jnp.minimum(i + 1, nq - 1), COL_QA + h)),
                  pl.BlockSpec((seq, A_DV), lambda b, h, i: (b, COL_KA + h)),
                  pl.BlockSpec((seq, A_DV), lambda b, h, i: (b, COL_VA + h)),
                  pl.BlockSpec((None, 2, tq, tq), lambda b, h, i: (h, 0, 0, 0)),
                  vec(A_DK), vec(A_DK), vec(A_DK), vec(A_DK),
                  pl.BlockSpec((A_DV, 1), lambda b, h, i: (0, 0))],
        out_specs=pl.BlockSpec((tq, A_DV), lambda b, h, i: (b * nq + i, h)),
        scratch_shapes=[pltpu.VMEM((seq // tk, A_DV + ONES_ROWS, tk), BF16),
                        pltpu.VMEM((tk, 2 * tq), F32), pltpu.VMEM((tk, 2 * tq), F32),
                        pltpu.VMEM((1, 2 * tq), F32),
                        pltpu.VMEM((A_DV + ONES_ROWS, 2 * tq), F32)],
        compiler_params=_params("arbitrary", "arbitrary", "arbitrary"),
        name="diff_attention",
    )(proj, proj, proj, proj, bias_tiles, lq1, lk1, lq2, lk2, g_sub_col)


def _split3(a):
    a1 = a.astype(BF16)
    r1 = a - a1.astype(F32)
    a2 = r1.astype(BF16)
    return a1, a2, (r1 - a2.astype(F32)).astype(BF16)


def _sum3(x):
    return x[:, :B_DK] + x[:, B_DK:2 * B_DK] + x[:, 2 * B_DK:]


def _gla_kernel(q_ref, k_ref, v_ref, r_ref, z_ref, wa_ref, ba_ref, g_ref, o_ref,
                state_scr, mask_scr, kv_scr, st_scr, *, n_chunks):
    lc = n_chunks * CHUNK

    @pl.when(pl.program_id(2) == 0)
    def _():
        state_scr[...] = jnp.zeros(state_scr.shape, F32)
        row = lax.broadcasted_iota(jnp.int32, (lc, lc), 0)
        col = lax.broadcasted_iota(jnp.int32, (lc, lc), 1)
        same = (row // CHUNK) == (col // CHUNK)
        mask_scr[...] = (same & (row >= col)).astype(BF16)

    z = z_ref[...]
    zh = z.astype(BF16)
    zl = (z - zh.astype(F32)).astype(BF16)
    pre = jnp.dot(jnp.concatenate([zh, zl, zh], axis=1), wa_ref[...], preferred_element_type=F32) + ba_ref[...]
    log_a = (jnp.minimum(pre, 0.0) - jnp.log1p(jnp.exp(-jnp.abs(pre)))) * (1.0 / GATE_TAU)
    parts = jnp.concatenate(_split3(log_a), axis=1)
    cum = _sum3(jnp.dot(mask_scr[...], parts, preferred_element_type=F32))
    totals = [cum[(c + 1) * CHUNK - 1:(c + 1) * CHUNK, :] for c in range(n_chunks)]
    total = jnp.concatenate([jnp.broadcast_to(tc, (CHUNK, B_DK)) for tc in totals], axis=0)
    k_dec = (k_ref[...].astype(F32) * jnp.exp(total - cum)).astype(BF16)

    for c in range(n_chunks):
        rows = slice(c * CHUNK, (c + 1) * CHUNK)
        kv_scr[c] = lax.dot_general(v_ref[rows, :], k_dec[rows], (((0,), (0,)), ((), ())),
                                    preferred_element_type=F32)
    state = state_scr[...]
    for c in range(n_chunks):
        state = state * jnp.exp(totals[c]) + kv_scr[c]
        st_scr[c] = state.astype(BF16)
    state_scr[...] = state

    for c in range(n_chunks):
        rows = slice(c * CHUNK, (c + 1) * CHUNK)
        o = lax.dot_general(q_ref[rows, :], st_scr[c], (((1,), (1,)), ((), ())),
                            preferred_element_type=F32) * (B_DK ** -0.5)
        o_ref[rows, :] = (_rms(o) * g_ref[...] * _silu(r_ref[rows, :].astype(F32))).astype(BF16)


def _gla(proj, zb, w_alpha_pad, b_alpha, g_norm, batch, seq, lc=512):
    t = proj.shape[0]
    lc = min(lc, seq)
    nl = seq // lc
    n_chunks = lc // CHUNK
    kern = functools.partial(_gla_kernel, n_chunks=n_chunks)
    wa_hi = w_alpha_pad.astype(BF16)
    wa_lo = (w_alpha_pad - wa_hi.astype(F32)).astype(BF16)
    wa3 = jnp.concatenate([wa_hi, wa_hi, wa_lo], axis=0)
    return pl.pallas_call(
        kern,
        out_shape=jax.ShapeDtypeStruct((t, B_WIDTH), BF16),
        grid=(batch, B_HEADS, nl),
        in_specs=[pl.BlockSpec((lc, B_DK), lambda b, h, l: (b * nl + l, COL_QB + h)),
                  pl.BlockSpec((lc, B_DK), lambda b, h, l: (b * nl + l, COL_KB + h)),
                  pl.BlockSpec((lc, B_DV), lambda b, h, l: (b * nl + l, COL_VB256 + h)),
                  pl.BlockSpec((lc, B_DV), lambda b, h, l: (b * nl + l, COL_RB256 + h)),
                  pl.BlockSpec((lc, LANES), lambda b, h, l: (b * nl + l, 0)),
                  pl.BlockSpec((3 * LANES, B_DK), lambda b, h, l: (0, h)),
                  pl.BlockSpec((1, B_DK), lambda b, h, l: (0, h)),
                  pl.BlockSpec((1, B_DV), lambda b, h, l: (0, 0))],
        out_specs=pl.BlockSpec((lc, B_DV), lambda b, h, l: (b * nl + l, h)),
        scratch_shapes=[pltpu.VMEM((B_DV, B_DK), F32),
                        pltpu.VMEM((lc, lc), BF16),
                        pltpu.VMEM((n_chunks, B_DV, B_DK), F32),
                        pltpu.VMEM((n_chunks, B_DV, B_DK), BF16)],
        compiler_params=_params("arbitrary", "arbitrary", "arbitrary"),
        name="gla",
    )(proj, proj, proj, proj, zb, wa3, b_alpha, g_norm)


def _outproj_kernel(oa_ref, ob_ref, wo_ref, x_ref, gt_ref, gpost_ref, gpre_ref, sc_ref, sh_ref, wr_ref,
                    br_ref, x1_ref, h2_ref, lg_ref, *, tm):
    for p in range(tm // OUTPROJ_PIECE):
        p0 = p * OUTPROJ_PIECE
        prow = slice(p0, p0 + OUTPROJ_PIECE)
        y = (jnp.dot(oa_ref[prow, :], wo_ref[:A_WIDTH, :], preferred_element_type=F32)
             + jnp.dot(ob_ref[prow, :], wo_ref[A_WIDTH:, :], preferred_element_type=F32))
        his, los = [], []
        for c in range(OUTPROJ_PIECE // ROW_CHUNK):
            r0 = p0 + c * ROW_CHUNK
            rows = slice(r0, r0 + ROW_CHUNK)
            x1 = x_ref[rows, :] + gt_ref[0] * (_rms(y[c * ROW_CHUNK:(c + 1) * ROW_CHUNK]) * gpost_ref[...])
            x1_ref[rows, :] = x1
            h2 = (_rms(x1) * gpre_ref[...]) * (1.0 + sc_ref[0]) + sh_ref[0]
            hi = h2.astype(BF16)
            his.append(hi)
            los.append((h2 - hi.astype(F32)).astype(BF16))
            _store_row_slabs(h2_ref, r0, ROW_CHUNK, h2)
        hi, lo = jnp.concatenate(his, axis=0), jnp.concatenate(los, axis=0)
        hw = jnp.dot(hi, wr_ref[...], preferred_element_type=F32)
        lw = jnp.dot(lo, wr_ref[:, :LANES], preferred_element_type=F32)
        lg_ref[prow, :] = hw[:, :LANES] + hw[:, LANES:] + lw + br_ref[...]


def _outproj(oa, ob, w_out, x2d, gt, g_post, g_pre, sc, sh, w_router, b_router, seq, tm=512):
    t, d = x2d.shape
    tm = min(tm, seq)
    per_b = seq // tm
    kern = functools.partial(_outproj_kernel, tm=tm)
    wr_hi = w_router.astype(BF16)
    wr_lo = (w_router - wr_hi.astype(F32)).astype(BF16)
    wr_cat = jnp.concatenate([wr_hi, wr_lo], axis=1)
    row = lambda: pl.BlockSpec((1, d), lambda i: (0, 0))
    per_batch = lambda: pl.BlockSpec((1, 1, d), lambda i: (i // per_b, 0, 0))
    return pl.pallas_call(
        kern,
        out_shape=(jax.ShapeDtypeStruct((t, d), F32),
                   jax.ShapeDtypeStruct((t * ROW_SLABS, LANES), U32),
                   jax.ShapeDtypeStruct((t, LANES), F32)),
        grid=(t // tm,),
        in_specs=[pl.BlockSpec((tm, A_WIDTH), lambda i: (i, 0)),
                  pl.BlockSpec((tm, B_WIDTH), lambda i: (i, 0)),
                  pl.BlockSpec((d, d), lambda i: (0, 0)),
                  pl.BlockSpec((tm, d), lambda i: (i, 0)),
                  per_batch(), row(), row(), per_batch(), per_batch(),
                  pl.BlockSpec((d, 2 * LANES), lambda i: (0, 0)),
                  pl.BlockSpec((1, LANES), lambda i: (0, 0))],
        out_specs=(pl.BlockSpec((tm, d), lambda i: (i, 0)),
                   pl.BlockSpec((tm * ROW_SLABS, LANES), lambda i: (i, 0)),
                   pl.BlockSpec((tm, LANES), lambda i: (i, 0))),
        compiler_params=_params("arbitrary"),
        name="out_proj",
    )(oa, ob, w_out, x2d, gt, g_post, g_pre, sc, sh, wr_cat, b_router)


def _route_kernel(lg_ref, rec_ref, cnt_ref, carry_scr, *, tr):
    @pl.when(pl.program_id(0) == 0)
    def _():
        carry_scr[...] = jnp.zeros(carry_scr.shape, F32)

    lg = lg_ref[...]
    lane = lax.broadcasted_iota(jnp.int32, lg.shape, 1)
    big = jnp.int32(LANES)

    def first_lane(mask):
        return jnp.min(jnp.where(mask, lane, big), axis=-1, keepdims=True)

    gmask = lane < N_GROUPS
    gmax = jnp.max(jnp.where(gmask, lg, -jnp.inf), axis=-1, keepdims=True)
    gexp = jnp.where(gmask, jnp.exp(lg - gmax), 0.0)
    gprob = gexp / jnp.sum(gexp, axis=-1, keepdims=True)
    g_val = jnp.max(gprob, axis=-1, keepdims=True)
    g_idx = first_lane(gmask & (gprob == g_val))

    lo = ROUTER_EXPERT_LANE0 + g_idx * EXPERTS_PER_GROUP
    emask = (lane >= lo) & (lane < lo + EXPERTS_PER_GROUP)
    emax = jnp.max(jnp.where(emask, lg, -jnp.inf), axis=-1, keepdims=True)
    eexp = jnp.where(emask, jnp.exp(lg - emax), 0.0)
    eprob = eexp / jnp.sum(eexp, axis=-1, keepdims=True)
    v1 = jnp.max(eprob, axis=-1, keepdims=True)
    i1 = first_lane(emask & (eprob == v1))
    rest = emask & (lane != i1)
    v2 = jnp.max(jnp.where(rest, eprob, -1.0), axis=-1, keepdims=True)
    i2 = first_lane(rest & (eprob == v2))
    w1 = g_val * (v1 / (v1 + v2))
    w2 = g_val * (v2 / (v1 + v2))

    hit1 = lane == i1
    hit2 = lane == i2
    onehot = (hit1 | hit2).astype(BF16)
    r = lax.broadcasted_iota(jnp.int32, (tr, tr), 0)
    c = lax.broadcasted_iota(jnp.int32, (tr, tr), 1)
    before = (c < r).astype(BF16)
    pos = carry_scr[...] + jnp.dot(before, onehot, preferred_element_type=F32)
    rank1 = jnp.sum(jnp.where(hit1, pos, 0.0), axis=-1, keepdims=True)
    rank2 = jnp.sum(jnp.where(hit2, pos, 0.0), axis=-1, keepdims=True)
    carry_scr[...] = carry_scr[...] + jnp.sum(onehot.astype(F32), axis=0, keepdims=True)
    cnt_ref[...] = carry_scr[...]

    e1 = (i1 - ROUTER_EXPERT_LANE0).astype(F32)
    e2 = (i2 - ROUTER_EXPERT_LANE0).astype(F32)
    rec = jnp.zeros(lg.shape, F32)
    for ln, val in ((ROUTE_E1, e1), (ROUTE_E2, e2), (ROUTE_W1, w1), (ROUTE_W2, w2),
                    (ROUTE_R1, rank1), (ROUTE_R2, rank2)):
        rec = jnp.where(lane == ln, val, rec)
    rec_ref[...] = rec


def _route(logits, tr=512):
    t = logits.shape[0]
    tr = min(tr, t)
    kern = functools.partial(_route_kernel, tr=tr)
    return pl.pallas_call(
        kern,
        out_shape=(jax.ShapeDtypeStruct((t, LANES), F32), jax.ShapeDtypeStruct((1, LANES), F32)),
        grid=(t // tr,),
        in_specs=[pl.BlockSpec((tr, LANES), lambda i: (i, 0))],
        out_specs=(pl.BlockSpec((tr, LANES), lambda i: (i, 0)),
                   pl.BlockSpec((1, LANES), lambda i: (0, 0))),
        scratch_shapes=[pltpu.VMEM((1, LANES), F32)],
        compiler_params=_params("arbitrary"),
        name="route",
    )(logits)


def _slab_rows(ref, row):
    return ref.at[pl.ds(pl.multiple_of(row * ROW_SLABS, ROW_SLABS), ROW_SLABS), :]


def _dispatch_kernel(slot_ref, pad_start_ref, pad_len_ref, used_ref, h2_ref, xs_hbm, zero_scr, sems, pad_sems,
                     *, td, tm):
    g = pl.program_id(0)
    tile_rows = tm * ROW_SLABS
    n_tiles = xs_hbm.shape[0] // tile_rows

    def zero_copy(slot, nslots, sem):
        rows = pl.ds(pl.multiple_of(slot * ROW_SLABS, ROW_SLABS), nslots * ROW_SLABS)
        return pltpu.make_async_copy(zero_scr.at[pl.ds(0, nslots * ROW_SLABS), :], xs_hbm.at[rows, :], sem)

    pad_sizes = [1 << b for b in reversed(range((tm - 1).bit_length()))]

    @pl.when(g == 0)
    def _():
        zero_scr[...] = jnp.zeros(zero_scr.shape, U32)

        def unused_tile(tile, carry):
            zero_copy(tile * tm, tm, sems.at[1]).start()
            zero_copy(tile * tm, tm, sems.at[1]).wait()
            return carry

        lax.fori_loop(used_ref[0], n_tiles, unused_tile, 0)

        def per_expert(e, counts):
            off = pad_start_ref[e]
            n = pad_len_ref[e]
            new_counts = []
            for b, size in enumerate(pad_sizes):
                hit = (n & size) != 0

                @pl.when(hit)
                def _():
                    zero_copy(off, size, pad_sems.at[b]).start()

                off = off + jnp.where(hit, size, 0)
                new_counts.append(counts[b] + hit.astype(jnp.int32))
            return tuple(new_counts)

        counts = lax.fori_loop(0, N_EXPERTS, per_expert, tuple(jnp.int32(0) for _ in pad_sizes))
        for b, size in enumerate(pad_sizes):
            def drain(r, c):
                zero_copy(0, size, pad_sems.at[b]).wait()
                return c
            lax.fori_loop(0, counts[b], drain, 0)

    def row_copy(r, slot):
        return pltpu.make_async_copy(_slab_rows(h2_ref, r), _slab_rows(xs_hbm, slot), sems.at[0])

    base = g * td

    def issue(r, c):
        tok = base + r
        row_copy(r, slot_ref[2 * tok]).start()
        row_copy(r, slot_ref[2 * tok + 1]).start()
        return c

    lax.fori_loop(0, td, issue, 0, unroll=CHUNK_UNROLL)
    for _ in range(2):
        pltpu.make_async_copy(h2_ref, xs_hbm.at[pl.ds(0, td * ROW_SLABS), :], sems.at[0]).wait()


def _dispatch(slot, pad_start, pad_len, used, h2_rows, n_slots, tm, td=1024):
    t = slot.shape[0] // 2
    td = min(td, t)
    kern = functools.partial(_dispatch_kernel, td=td, tm=tm)
    grid_spec = pltpu.PrefetchScalarGridSpec(
        num_scalar_prefetch=4,
        grid=(t // td,),
        in_specs=[pl.BlockSpec((td * ROW_SLABS, LANES), lambda g, sl, ps, pn, us: (g, 0))],
        out_specs=pl.BlockSpec(memory_space=pl.ANY),
        scratch_shapes=[pltpu.VMEM((tm * ROW_SLABS, LANES), U32), pltpu.SemaphoreType.DMA((2,)),
                        pltpu.SemaphoreType.DMA(((tm - 1).bit_length(),))],
    )
    return pl.pallas_call(
        kern,
        out_shape=jax.ShapeDtypeStruct((n_slots * ROW_SLABS, LANES), U32),
        grid_spec=grid_spec,
        compiler_params=_params("arbitrary"),
        name="dispatch",
    )(slot, pad_start, pad_len, used, h2_rows)


TILE_UNUSED, TILE_USED, TILE_NEW_EXPERT = 0, 1, 2


def _expert_kernel(texp_ref, tblk_ref, tstate_ref, tnext_ref, tpar_ref, xs_ref, w1_hbm, w3_hbm, w2_hbm, eo_ref,
                   x_scr, w1_scr, w3_scr, w2_scr, w1_stage, w3_stage, w2_stage, sems, *, tm):
    i = pl.program_id(0)
    state = tstate_ref[i]
    slot = tpar_ref[i]

    def weight_copies(expert, dst_slot):
        return [pltpu.make_async_copy(hbm.at[expert], stage.at[dst_slot], sems.at[dst_slot])
                for hbm, stage in ((w1_hbm, w1_stage), (w3_hbm, w3_stage), (w2_hbm, w2_stage))]

    @pl.when(state == TILE_UNUSED)
    def _():
        eo_ref[...] = jnp.zeros(eo_ref.shape, U32)

    @pl.when(i == 0)
    def _():
        for cp in weight_copies(texp_ref[0], slot):
            cp.start()

    @pl.when(state == TILE_NEW_EXPERT)
    def _():
        for cp in weight_copies(texp_ref[i], slot):
            cp.wait()

        @pl.when(tnext_ref[i] >= 0)
        def _():
            for cp in weight_copies(tnext_ref[i], 1 - slot):
                cp.start()

        w1_scr[...] = w1_stage[slot].astype(BF16)
        w3_scr[...] = w3_stage[slot].astype(BF16)
        w2_scr[...] = w2_stage[slot].astype(BF16)

    @pl.when(state != TILE_UNUSED)
    def _():
        for s in range(ROW_SLABS):
            lo, hi = _unpack_bf16_pair(xs_ref[pl.ds(s, tm, stride=ROW_SLABS), :])
            x_scr[:, s * LANES:(s + 1) * LANES] = lo.astype(BF16)
            x_scr[:, HALF_D + s * LANES:HALF_D + (s + 1) * LANES] = hi.astype(BF16)
        x = x_scr[...]
        a = jnp.dot(x, w1_scr[...], preferred_element_type=F32)
        b = jnp.dot(x, w3_scr[...], preferred_element_type=F32)
        hid = (_silu(a) * b).astype(BF16)
        y = jnp.dot(hid, w2_scr[...], preferred_element_type=F32)
        _store_row_slabs(eo_ref, 0, tm, y)


def _experts(tile_expert, tile_block, tile_state, tile_next, tile_slot, xs_rows, w1, w3, w2, tm):
    n_tiles = tile_expert.shape[0]
    d, f = w1.shape[1], w1.shape[2]
    kern = functools.partial(_expert_kernel, tm=tm)
    grid_spec = pltpu.PrefetchScalarGridSpec(
        num_scalar_prefetch=5,
        grid=(n_tiles,),
        in_specs=[pl.BlockSpec((tm * ROW_SLABS, LANES), lambda i, te, tb, ts, tn, tp: (tb[i], 0)),
                  pl.BlockSpec(memory_space=pl.ANY), pl.BlockSpec(memory_space=pl.ANY),
                  pl.BlockSpec(memory_space=pl.ANY)],
        out_specs=pl.BlockSpec((tm * ROW_SLABS, LANES), lambda i, te, tb, ts, tn, tp: (i, 0)),
        scratch_shapes=[pltpu.VMEM((tm, d), BF16), pltpu.VMEM((d, f), BF16), pltpu.VMEM((d, f), BF16),
                        pltpu.VMEM((f, d), BF16),
                        pltpu.VMEM((2, d, f), F32), pltpu.VMEM((2, d, f), F32), pltpu.VMEM((2, f, d), F32),
                        pltpu.SemaphoreType.DMA((2,))],
    )
    return pl.pallas_call(
        kern,
        out_shape=jax.ShapeDtypeStruct(xs_rows.shape, U32),
        grid_spec=grid_spec,
        compiler_params=_params("arbitrary"),
        name="expert_mlp",
    )(tile_expert, tile_block, tile_state, tile_next, tile_slot, xs_rows, w1, w3, w2)


def _final_kernel(slot_ref, eo_hbm, rec_ref, x1_ref, gt_ref, g_ref, o_ref, e_scr, sems, *, tf):
    i = pl.program_id(0)
    par = i % 2

    def start_all(step, buf):
        def body(r, c):
            tok = step * tf + r
            for k in range(2):
                pltpu.make_async_copy(_slab_rows(eo_hbm, slot_ref[2 * tok + k]),
                                      _slab_rows(e_scr.at[buf, k], r), sems.at[buf]).start()
            return c
        lax.fori_loop(0, tf, body, 0, unroll=CHUNK_UNROLL)

    def wait_all(buf):
        for k in range(2):
            pltpu.make_async_copy(eo_hbm.at[pl.ds(0, tf * ROW_SLABS), :], e_scr.at[buf, k], sems.at[buf]).wait()

    @pl.when(i == 0)
    def _():
        start_all(0, 0)

    @pl.when(i + 1 < pl.num_programs(0))
    def _():
        start_all(i + 1, 1 - par)

    wait_all(par)

    def chunk(c, carry):
        r0 = pl.multiple_of(c * ROW_CHUNK, ROW_CHUNK)
        rows = pl.ds(r0, ROW_CHUNK)
        rec = rec_ref[rows, :]
        w1 = rec[:, ROUTE_W1:ROUTE_W1 + 1]
        w2 = rec[:, ROUTE_W2:ROUTE_W2 + 1]
        lo1, hi1 = _load_row_slabs(e_scr.at[par, 0], r0, ROW_CHUNK)
        lo2, hi2 = _load_row_slabs(e_scr.at[par, 1], r0, ROW_CHUNK)
        y = jnp.concatenate([w1 * lo1 + w2 * lo2, w1 * hi1 + w2 * hi2], axis=1)
        o_ref[rows, :] = x1_ref[rows, :] + gt_ref[0] * (_rms(y) * g_ref[...])
        return carry

    lax.fori_loop(0, tf // ROW_CHUNK, chunk, 0, unroll=CHUNK_UNROLL)


def _final(slot, eo_rows, rec, x1, gt, g_post, seq, tf=512):
    t, d = x1.shape
    tf = min(tf, seq)
    per_b = seq // tf
    kern = functools.partial(_final_kernel, tf=tf)
    grid_spec = pltpu.PrefetchScalarGridSpec(
        num_scalar_prefetch=1,
        grid=(t // tf,),
        in_specs=[pl.BlockSpec(memory_space=pl.ANY),
                  pl.BlockSpec((tf, LANES), lambda i, sl: (i, 0)),
                  pl.BlockSpec((tf, d), lambda i, sl: (i, 0)),
                  pl.BlockSpec((1, 1, d), lambda i, sl: (i // per_b, 0, 0)),
                  pl.BlockSpec((1, d), lambda i, sl: (0, 0))],
        out_specs=pl.BlockSpec((tf, d), lambda i, sl: (i, 0)),
        scratch_shapes=[pltpu.VMEM((2, 2, tf * ROW_SLABS, LANES), U32),
                        pltpu.SemaphoreType.DMA((2,))],
    )
    return pl.pallas_call(
        kern,
        out_shape=jax.ShapeDtypeStruct((t, d), F32),
        grid_spec=grid_spec,
        compiler_params=_params("arbitrary"),
        name="combine_final",
    )(slot, eo_rows, rec, x1, gt, g_post)


def _dispatch_tables(rec, counts, t, tm):
    e = rec[:, ROUTE_E1:ROUTE_E2 + 1].astype(jnp.int32)
    rank = rec[:, ROUTE_R1:ROUTE_R2 + 1].astype(jnp.int32)
    cnt = counts[0, ROUTER_EXPERT_LANE0:ROUTER_EXPERT_LANE0 + N_EXPERTS].astype(jnp.int32)
    tiles_per = (cnt + tm - 1) // tm
    tile_end = jnp.cumsum(tiles_per)
    tile_start = tile_end - tiles_per
    n_tiles = (2 * t + N_EXPERTS * (tm - 1)) // tm
    experts = jnp.arange(N_EXPERTS, dtype=jnp.int32)
    start_of = jnp.sum(jnp.where(e[..., None] == experts, tile_start, 0), axis=-1)
    slot = (start_of * tm + rank).reshape(-1)
    pad_start = tile_start * tm + cnt
    pad_len = tiles_per * tm - cnt
    tile_id = jnp.arange(n_tiles, dtype=jnp.int32)
    used = tile_end[-1]
    tblk = jnp.minimum(tile_id, used - 1)
    texp = jnp.sum(tile_end[None, :] <= tblk[:, None], axis=-1).astype(jnp.int32)
    tstate = jnp.where(tile_id < used, jnp.where(tile_id == tile_start[texp], TILE_NEW_EXPERT, TILE_USED),
                       TILE_UNUSED).astype(jnp.int32)
    nonempty = cnt > 0
    ordinal = jnp.cumsum(nonempty.astype(jnp.int32)) - 1
    later = jnp.where(nonempty[None, :] & (experts[None, :] > experts[:, None]), experts[None, :], N_EXPERTS)
    next_expert = jnp.min(later, axis=-1)
    next_expert = jnp.where(next_expert == N_EXPERTS, -1, next_expert)
    tnext = next_expert[texp].astype(jnp.int32)
    tslot = (ordinal[texp] % 2).astype(jnp.int32)
    return slot, pad_start, pad_len, used.reshape(1), texp, tblk, tstate, tnext, tslot, n_tiles * tm


def kernel(x, c, rel_bias, w_ada, b_ada, g_pre_mix, g_post_mix, w_in, w_alpha, b_alpha, lam_q1, lam_k1, lam_q2,
           lam_k2, g_sub_a, g_norm_b, w_out, g_pre_ffn, g_post_ffn, w_router_g, b_router_g, w_router_e,
           b_router_e, w1, w3, w2):
    batch, seq, d = x.shape
    t = batch * seq
    depth = w_in.shape[0]
    tq = min(512, seq)
    tm_e = 512
    xf = x.reshape(t, d)
    for i in range(depth):
        lam_init = 0.8 - 0.6 * math.exp(-0.3 * i)
        c_pad = jnp.pad(c, ((0, 8 - batch % 8 if batch % 8 else 0), (0, 0)))
        ada = _ada(c_pad, w_ada[i], b_ada[i][None, :])[:batch]
        sh_m, sc_m, gt_m, sh_f, sc_f, gt_f = [a[:, None, :] for a in jnp.split(ada, 6, axis=-1)]

        w_in_b = w_in[i].astype(BF16)
        w_z = jnp.pad(w_in_b[:, D_MAIN:], ((0, 0), (0, LANES - GATE_RANK)))
        proj, zb = _inproj(xf, g_pre_mix[i][None, :], sc_m, sh_m, w_in_b, w_z, seq)

        oa = _attention(proj, _bias_tiles(rel_bias, tq), lam_q1[i][None, :], lam_k1[i][None, :],
                        lam_q2[i][None, :], lam_k2[i][None, :], g_sub_a[i][:, None], batch, seq, lam_init, tq)
        w_alpha_pad = jnp.pad(w_alpha[i], ((0, LANES - GATE_RANK), (0, 0)))
        ob = _gla(proj, zb, w_alpha_pad, b_alpha[i][None, :], g_norm_b[i][None, :], batch, seq)

        w_router = jnp.pad(jnp.concatenate([w_router_g[i], w_router_e[i]], axis=1),
                           ((0, 0), (0, LANES - N_GROUPS - N_EXPERTS)))
        b_router = jnp.pad(jnp.concatenate([b_router_g[i], b_router_e[i]]),
                           (0, LANES - N_GROUPS - N_EXPERTS))[None, :]
        x1, h2_rows, logits = _outproj(oa, ob, w_out[i].astype(BF16), xf, gt_m, g_post_mix[i][None, :],
                                       g_pre_ffn[i][None, :], sc_f, sh_f, w_router, b_router, seq)

        rec, counts = _route(logits)
        (slot, pad_start, pad_len, used, texp, tblk, tstate, tnext, tslot,
         n_slots) = _dispatch_tables(rec, counts, t, tm_e)
        xs = _dispatch(slot, pad_start, pad_len, used, h2_rows, n_slots, tm_e)
        eo = _experts(texp, tblk, tstate, tnext, tslot, xs, w1[i], w3[i], w2[i], tm_e)
        xf = _final(slot, eo, rec, x1, gt_f, g_post_ffn[i][None, :], seq)
    return xf.reshape(batch, seq, d)
```

```python
import functools
import math

import jax
import jax.numpy as jnp
from jax import lax
from jax.experimental import pallas as pl
from jax.experimental.pallas import tpu as pltpu

F32 = jnp.float32
BF16 = jnp.bfloat16

D_MODEL = 2048
CHUNK = 64
A_HEADS = 8
A_DK = 64
A_DV = 2 * A_DK
A_WIDTH = A_HEADS * A_DV
B_HEADS = 4
B_WIDTH = D_MODEL - A_WIDTH
B_DV = B_WIDTH // B_HEADS
B_DK = B_DV // 2
GATE_RANK = 16
GATE_TAU = 16.0
N_BUCKETS = 32
MAX_DISTANCE = 256
N_GROUPS = 4
EXPERTS_PER_GROUP = 8
N_EXPERTS = N_GROUPS * EXPERTS_PER_GROUP
D_EXPERT = D_MODEL // 4
EPS = 1e-6
NEG_INF = -1e30
LOG2E = math.log2(math.e)

LANES = 128
U32 = jnp.uint32
HALF_D = D_MODEL // 2
ROW_SLABS = HALF_D // LANES
ROW_CHUNK = 16
CHUNK_UNROLL = 4
ONES_ROWS = 16

ADA_TN = 1024
INPROJ_TM, INPROJ_TN = 512, 1024
ATTN_TQ, ATTN_TK = 512, 256
GLA_LC = 512
OUTPROJ_TM = 512
OUTPROJ_PIECE = 512
ROUTE_TR = 512
EXPERT_TM = 512
DISPATCH_TD = 1024
COMBINE_TF = 256
D_MAIN = 3 * A_WIDTH + 2 * B_HEADS * B_DK + 2 * B_WIDTH
COL_QA, COL_KA, COL_VA = 0, A_HEADS, 2 * A_HEADS
COL_QB = 3 * A_HEADS
COL_KB = COL_QB + B_HEADS
COL_VB256 = (3 * A_WIDTH + 2 * B_HEADS * B_DK) // B_DV
COL_RB256 = COL_VB256 + B_HEADS
ROUTE_E1, ROUTE_E2, ROUTE_W1, ROUTE_W2, ROUTE_R1, ROUTE_R2 = 0, 1, 2, 3, 4, 5
ROUTER_EXPERT_LANE0 = N_GROUPS

VMEM_LIMIT = 56 * 1024 * 1024
INPROJ_VMEM_LIMIT = 60 * 1024 * 1024


def _params(*sem):
    return pltpu.CompilerParams(dimension_semantics=sem, vmem_limit_bytes=VMEM_LIMIT)


def _rms(v):
    return v * lax.rsqrt(jnp.mean(v * v, axis=-1, keepdims=True) + EPS)


def _silu(v):
    return v * jax.nn.sigmoid(v)


_HIGH_HALF = 0xFFFF0000


def _pack_bf16_pair(lo, hi):
    lo_bits = lax.bitcast_convert_type(lo.astype(BF16).astype(F32), U32) >> 16
    hi_bits = lax.bitcast_convert_type(hi.astype(BF16).astype(F32), U32) & U32(_HIGH_HALF)
    return hi_bits | lo_bits


def _unpack_bf16_pair(w):
    return (lax.bitcast_convert_type(w << 16, F32), lax.bitcast_convert_type(w & U32(_HIGH_HALF), F32))


def _store_row_slabs(ref, r0, nrows, rows_f32):
    packed = _pack_bf16_pair(rows_f32[:, :HALF_D], rows_f32[:, HALF_D:])
    for s in range(ROW_SLABS):
        ref[pl.ds(r0 * ROW_SLABS + s, nrows, stride=ROW_SLABS), :] = packed[:, s * LANES:(s + 1) * LANES]


def _load_row_slabs(ref, r0, nrows):
    slabs = [_unpack_bf16_pair(ref[pl.ds(r0 * ROW_SLABS + s, nrows, stride=ROW_SLABS), :]) for s in range(ROW_SLABS)]
    return (jnp.concatenate([lo for lo, _ in slabs], axis=1), jnp.concatenate([hi for _, hi in slabs], axis=1))


def _ada_kernel(c_ref, w_ref, b_ref, o_ref):
    s = _silu(c_ref[...])
    o_ref[...] = jnp.dot(s.astype(BF16), w_ref[...].astype(BF16), preferred_element_type=F32) + b_ref[...]


def _ada(c_pad, w, b, tn=ADA_TN):
    m, d = c_pad.shape
    n = w.shape[1]
    return pl.pallas_call(
        _ada_kernel,
        out_shape=jax.ShapeDtypeStruct((m, n), F32),
        grid=(n // tn,),
        in_specs=[pl.BlockSpec((m, d), lambda j: (0, 0)),
                  pl.BlockSpec((d, tn), lambda j: (0, j)),
                  pl.BlockSpec((1, tn), lambda j: (0, j))],
        out_specs=pl.BlockSpec((m, tn), lambda j: (0, j)),
        compiler_params=_params("arbitrary"),
        name="ada_proj",
    )(c_pad, w, b)


def _inproj_kernel(x_ref, xn_ref, g_ref, sc_ref, sh_ref, scn_ref, shn_ref, w_ref, wz_ref, o_ref, z_ref,
                   h_scr, hn_scr, *, tm, tn):
    def normed(src_ref, scale_ref, shift_ref, rows):
        h = _rms(src_ref[rows, :]) * g_ref[...]
        return (h * (1.0 + scale_ref[0]) + shift_ref[0]).astype(BF16)

    @pl.when(pl.program_id(0) == 0)
    def _():
        def chunk(c, carry):
            rows = pl.ds(pl.multiple_of(c * ROW_CHUNK, ROW_CHUNK), ROW_CHUNK)
            h_scr[rows, :] = normed(x_ref, sc_ref, sh_ref, rows)
            return carry
        lax.fori_loop(0, tm // ROW_CHUNK, chunk, 0, unroll=CHUNK_UNROLL)

    h = h_scr[...]
    z_ref[...] = jnp.dot(h, wz_ref[...], preferred_element_type=F32)
    n_col = D_MAIN // tn
    rows_per_col = tm // n_col // ROW_CHUNK * ROW_CHUNK
    next_row = 0
    for c in range(n_col):
        cols = slice(c * tn, (c + 1) * tn)
        o_ref[:, cols] = jnp.dot(h, w_ref[:, cols], preferred_element_type=F32).astype(BF16)
        stop = tm if c == n_col - 1 else next_row + rows_per_col
        for r0 in range(next_row, stop, ROW_CHUNK):
            rows = slice(r0, r0 + ROW_CHUNK)
            hn_scr[rows, :] = normed(xn_ref, scn_ref, shn_ref, rows)
        next_row = stop
    h_scr[...] = hn_scr[...]


def _inproj(x2d, g, sc, sh, w_all, w_z, seq, tm=INPROJ_TM, tn=INPROJ_TN):
    t, d = x2d.shape
    tm = min(tm, seq)
    per_b = seq // tm
    n_steps = t // tm
    kern = functools.partial(_inproj_kernel, tm=tm, tn=tn)
    nxt = lambda i: jnp.minimum(i + 1, n_steps - 1)
    per_batch = lambda step: pl.BlockSpec((1, 1, d), lambda i: (step(i) // per_b, 0, 0))
    return pl.pallas_call(
        kern,
        out_shape=(jax.ShapeDtypeStruct((t, D_MAIN), BF16), jax.ShapeDtypeStruct((t, LANES), F32)),
        grid=(n_steps,),
        in_specs=[pl.BlockSpec((tm, d), lambda i: (i, 0)),
                  pl.BlockSpec((tm, d), lambda i: (nxt(i), 0)),
                  pl.BlockSpec((1, d), lambda i: (0, 0)),
                  per_batch(lambda i: i), per_batch(lambda i: i), per_batch(nxt), per_batch(nxt),
                  pl.BlockSpec(w_all.shape, lambda i: (0, 0), pipeline_mode=pl.Buffered(1)),
                  pl.BlockSpec((d, LANES), lambda i: (0, 0), pipeline_mode=pl.Buffered(1))],
        out_specs=(pl.BlockSpec((tm, D_MAIN), lambda i: (i, 0)),
                   pl.BlockSpec((tm, LANES), lambda i: (i, 0))),
        scratch_shapes=[pltpu.VMEM((tm, d), BF16), pltpu.VMEM((tm, d), BF16)],
        compiler_params=pltpu.CompilerParams(dimension_semantics=("arbitrary",),
                                             vmem_limit_bytes=INPROJ_VMEM_LIMIT),
        name="in_proj",
    )(x2d, x2d, g, sc, sh, sc, sh, w_all, w_z)


def _t5_bucket(rel):
    nb = N_BUCKETS // 2
    max_exact = nb // 2
    base = jnp.where(rel > 0, nb, 0)
    n = jnp.abs(rel)
    nf = jnp.maximum(n, 1).astype(F32)
    large = max_exact + (jnp.log(nf / max_exact) / math.log(MAX_DISTANCE / max_exact)
                         * (nb - max_exact)).astype(jnp.int32)
    large = jnp.minimum(large, nb - 1)
    return base + jnp.where(n < max_exact, n, large)


def _bias_buckets(tq):
    kj = jnp.arange(tq, dtype=jnp.int32)[:, None]
    qi = jnp.arange(tq, dtype=jnp.int32)[None, :]
    near = _t5_bucket(kj - qi - tq)
    diag = jnp.where((kj // CHUNK) <= (qi // CHUNK), _t5_bucket(kj - qi), N_BUCKETS)
    return jnp.stack([near, diag]).astype(jnp.int32)


def _bias_kernel(rb_ref, bk_ref, o_ref):
    h = pl.program_id(0)
    far = rb_ref[N_BUCKETS // 2 - 1, h]
    bucket = bk_ref[...]
    acc = jnp.full(bucket.shape, NEG_INF, F32)
    for n in range(N_BUCKETS):
        acc = jnp.where(bucket == n, (rb_ref[n, h] - far) * LOG2E, acc)
    o_ref[...] = acc


def _bias_tiles(rel_bias, tq):
    return pl.pallas_call(
        _bias_kernel,
        out_shape=jax.ShapeDtypeStruct((A_HEADS, 2, tq, tq), F32),
        grid=(A_HEADS, 2),
        in_specs=[pl.BlockSpec(memory_space=pltpu.SMEM),
                  pl.BlockSpec((None, tq, tq), lambda h, d: (d, 0, 0))],
        out_specs=pl.BlockSpec((None, None, tq, tq), lambda h, d: (h, d, 0, 0)),
        compiler_params=_params("arbitrary", "arbitrary"),
        name="bias_tiles",
    )(rel_bias, _bias_buckets(tq))


def _attn_kernel(q_ref, qn_ref, k_ref, v_ref, bias_ref, lq1_ref, lk1_ref, lq2_ref, lk2_ref, g_ref, o_ref,
                 vt_scr, sa_scr, sb_scr, m_scr, acc_scr, *, tq, tk, lam_init):
    i = pl.program_id(2)
    nsub = tq // tk
    bufs = (sa_scr, sb_scr)

    @pl.when(i == 0)
    def _():
        ones = jnp.ones((ONES_ROWS, tk), BF16)
        for c in range(vt_scr.shape[0]):
            vt = v_ref[c * tk:(c + 1) * tk, :].astype(F32).T.astype(BF16)
            vt_scr[c] = jnp.concatenate([vt, ones], axis=0)

    lane = lax.broadcasted_iota(jnp.int32, (1, A_DV), 1)

    def two_map_queries(ref):
        q = ref[...] * (A_DK ** -0.5 * LOG2E)
        zero = jnp.zeros_like(q)
        return jnp.concatenate([jnp.where(lane < A_DK, q, zero), jnp.where(lane >= A_DK, q, zero)], axis=0)

    q2 = two_map_queries(q_ref)
    m_scr[...] = jnp.full(m_scr.shape, NEG_INF, F32)
    acc_scr[...] = jnp.zeros(acc_scr.shape, F32)

    def scores(j, queries=q2):
        k = k_ref[pl.ds(pl.multiple_of(j * tk, tk), tk), :]
        return lax.dot_general(k, queries, (((1,), (1,)), ((), ())), preferred_element_type=F32)

    def softmax_pv(s_ref, j, bias):
        s = s_ref[...]
        if bias is not None:
            s = jnp.concatenate([s[:, :tq] + bias, s[:, tq:] + bias], axis=1)
        m_old = m_scr[...]
        m_new = jnp.maximum(m_old, jnp.max(s, axis=0, keepdims=True))
        alpha = jnp.exp2(m_old - m_new)
        p = jnp.exp2(s - m_new).astype(BF16)
        acc_scr[...] = alpha * acc_scr[...] + jnp.dot(vt_scr[j], p, preferred_element_type=F32)
        m_scr[...] = m_new

    n_far = jnp.maximum(i - 1, 0) * nsub

    @pl.when(i == 0)
    def _():
        sa_scr[...] = scores(0)

    def far_steps(j, count):
        for c in range(count):
            bufs[(c + 1) % 2][...] = scores(j + c + 1)
            softmax_pv(bufs[c % 2], j + c, None)

    def far_quad(jj, carry):
        far_steps(4 * jj, 4)
        return carry

    n_quads = n_far // 4
    lax.fori_loop(0, n_quads, far_quad, 0)

    @pl.when(n_far - 4 * n_quads >= 2)
    def _():
        far_steps(4 * n_quads, 2)

    def biased_steps(first_tile):
        j0 = (i - 1 + first_tile) * nsub
        count = (2 - first_tile) * nsub
        for c in range(count):
            if c + 1 < count:
                bufs[(c + 1) % 2][...] = scores(j0 + c + 1)
            d, r = first_tile + c // nsub, (c % nsub) * tk
            softmax_pv(bufs[c % 2], j0 + c, bias_ref[d, r:r + tk, :])

    def finish():
        lam = (jnp.exp(jnp.sum(lq1_ref[...] * lk1_ref[...], axis=-1, keepdims=True))
               - jnp.exp(jnp.sum(lq2_ref[...] * lk2_ref[...], axis=-1, keepdims=True)) + lam_init)
        on = acc_scr[:A_DV, :] / acc_scr[A_DV:A_DV + 1, :]
        o = on[:, :tq] - lam * on[:, tq:]
        y = o * lax.rsqrt(jnp.mean(o * o, axis=0, keepdims=True) + EPS) * g_ref[...] * (1.0 - lam_init)
        o_ref[...] = y.T.astype(BF16)

    last = i + 1 == pl.num_programs(2)
    for first_tile, applies in ((0, i >= 1), (1, i == 0)):
        @pl.when(applies & jnp.logical_not(last))
        def _():
            biased_steps(first_tile)
            sa_scr[...] = scores(0, two_map_queries(qn_ref))
            finish()

        @pl.when(applies & last)
        def _():
            biased_steps(first_tile)
            finish()


def _attention(proj, bias_tiles, lq1, lk1, lq2, lk2, g_sub_col, batch, seq, lam_init, tq, tk=ATTN_TK):
    t = proj.shape[0]
    nq = seq // tq
    assert (tq // tk) % 2 == 0 and tq % tk == 0, "the score pipeline alternates two buffers per query tile"
    kern = functools.partial(_attn_kernel, tq=tq, tk=tk, lam_init=lam_init)
    vec = lambda n: pl.BlockSpec((1, n), lambda b, h, i: (0, 0))
    return pl.pallas_call(
        kern,
        out_shape=jax.ShapeDtypeStruct((t, A_WIDTH), BF16),
        grid=(batch, A_HEADS, nq),
        in_specs=[pl.BlockSpec((tq, A_DV), lambda b, h, i: (b * nq + i, COL_QA + h)),
                  pl.BlockSpec((tq, A_DV), lambda b, h, i: (b * nq + jnp.minimum(i + 1, nq - 1), COL_QA + h)),
                  pl.BlockSpec((seq, A_DV), lambda b, h, i: (b, COL_KA + h)),
                  pl.BlockSpec((seq, A_DV), lambda b, h, i: (b, COL_VA + h)),
                  pl.BlockSpec((None, 2, tq, tq), lambda b, h, i: (h, 0, 0, 0)),
                  vec(A_DK), vec(A_DK), vec(A_DK), vec(A_DK),
                  pl.BlockSpec((A_DV, 1), lambda b, h, i: (0, 0))],
        out_specs=pl.BlockSpec((tq, A_DV), lambda b, h, i: (b * nq + i, h)),
        scratch_shapes=[pltpu.VMEM((seq // tk, A_DV + ONES_ROWS, tk), BF16),
                        pltpu.VMEM((tk, 2 * tq), F32), pltpu.VMEM((tk, 2 * tq), F32),
                        pltpu.VMEM((1, 2 * tq), F32),
                        pltpu.VMEM((A_DV + ONES_ROWS, 2 * tq), F32)],
        compiler_params=_params("arbitrary", "arbitrary", "arbitrary"),
        name="diff_attention",
    )(proj, proj, proj, proj, bias_tiles, lq1, lk1, lq2, lk2, g_sub_col)


def _split3(a):
    a1 = a.astype(BF16)
    r1 = a - a1.astype(F32)
    a2 = r1.astype(BF16)
    return a1, a2, (r1 - a2.astype(F32)).astype(BF16)


def _sum3(x):
    return x[:, :B_DK] + x[:, B_DK:2 * B_DK] + x[:, 2 * B_DK:]


def _gla_kernel(q_ref, k_ref, v_ref, r_ref, z_ref, wa_ref, ba_ref, g_ref, o_ref,
                state_scr, mask_scr, kv_scr, st_scr, *, n_chunks):
    lc = n_chunks * CHUNK

    @pl.when(pl.program_id(2) == 0)
    def _():
        state_scr[...] = jnp.zeros(state_scr.shape, F32)
        row = lax.broadcasted_iota(jnp.int32, (lc, lc), 0)
        col = lax.broadcasted_iota(jnp.int32, (lc, lc), 1)
        same = (row // CHUNK) == (col // CHUNK)
        mask_scr[...] = (same & (row >= col)).astype(BF16)

    z = z_ref[...]
    zh = z.astype(BF16)
    zl = (z - zh.astype(F32)).astype(BF16)
    pre = jnp.dot(jnp.concatenate([zh, zl, zh], axis=1), wa_ref[...], preferred_element_type=F32) + ba_ref[...]
    log_a = (jnp.minimum(pre, 0.0) - jnp.log1p(jnp.exp(-jnp.abs(pre)))) * (1.0 / GATE_TAU)
    parts = jnp.concatenate(_split3(log_a), axis=1)
    cum = _sum3(jnp.dot(mask_scr[...], parts, preferred_element_type=F32))
    totals = [cum[(c + 1) * CHUNK - 1:(c + 1) * CHUNK, :] for c in range(n_chunks)]
    total = jnp.concatenate([jnp.broadcast_to(tc, (CHUNK, B_DK)) for tc in totals], axis=0)
    k_dec = (k_ref[...].astype(F32) * jnp.exp(total - cum)).astype(BF16)

    for c in range(n_chunks):
        rows = slice(c * CHUNK, (c + 1) * CHUNK)
        kv_scr[c] = lax.dot_general(v_ref[rows, :], k_dec[rows], (((0,), (0,)), ((), ())),
                                    preferred_element_type=F32)
    state = state_scr[...]
    for c in range(n_chunks):
        state = state * jnp.exp(totals[c]) + kv_scr[c]
        st_scr[c] = state.astype(BF16)
    state_scr[...] = state

    for c in range(n_chunks):
        rows = slice(c * CHUNK, (c + 1) * CHUNK)
        o = lax.dot_general(q_ref[rows, :], st_scr[c], (((1,), (1,)), ((), ())),
                            preferred_element_type=F32) * (B_DK ** -0.5)
        o_ref[rows, :] = (_rms(o) * g_ref[...] * _silu(r_ref[rows, :].astype(F32))).astype(BF16)


def _gla(proj, zb, w_alpha_pad, b_alpha, g_norm, batch, seq, lc=GLA_LC):
    t = proj.shape[0]
    lc = min(lc, seq)
    nl = seq // lc
    n_chunks = lc // CHUNK
    kern = functools.partial(_gla_kernel, n_chunks=n_chunks)
    wa_hi = w_alpha_pad.astype(BF16)
    wa_lo = (w_alpha_pad - wa_hi.astype(F32)).astype(BF16)
    wa3 = jnp.concatenate([wa_hi, wa_hi, wa_lo], axis=0)
    return pl.pallas_call(
        kern,
        out_shape=jax.ShapeDtypeStruct((t, B_WIDTH), BF16),
        grid=(batch, B_HEADS, nl),
        in_specs=[pl.BlockSpec((lc, B_DK), lambda b, h, l: (b * nl + l, COL_QB + h)),
                  pl.BlockSpec((lc, B_DK), lambda b, h, l: (b * nl + l, COL_KB + h)),
                  pl.BlockSpec((lc, B_DV), lambda b, h, l: (b * nl + l, COL_VB256 + h)),
                  pl.BlockSpec((lc, B_DV), lambda b, h, l: (b * nl + l, COL_RB256 + h)),
                  pl.BlockSpec((lc, LANES), lambda b, h, l: (b * nl + l, 0)),
                  pl.BlockSpec((3 * LANES, B_DK), lambda b, h, l: (0, h)),
                  pl.BlockSpec((1, B_DK), lambda b, h, l: (0, h)),
                  pl.BlockSpec((1, B_DV), lambda b, h, l: (0, 0))],
        out_specs=pl.BlockSpec((lc, B_DV), lambda b, h, l: (b * nl + l, h)),
        scratch_shapes=[pltpu.VMEM((B_DV, B_DK), F32),
                        pltpu.VMEM((lc, lc), BF16),
                        pltpu.VMEM((n_chunks, B_DV, B_DK), F32),
                        pltpu.VMEM((n_chunks, B_DV, B_DK), BF16)],
        compiler_params=_params("arbitrary", "arbitrary", "arbitrary"),
        name="gla",
    )(proj, proj, proj, proj, zb, wa3, b_alpha, g_norm)


def _outproj_kernel(oa_ref, ob_ref, wo_ref, x_ref, gt_ref, gpost_ref, gpre_ref, sc_ref, sh_ref, wr_ref,
                    br_ref, x1_ref, h2_ref, lg_ref, *, tm):
    for p in range(tm // OUTPROJ_PIECE):
        p0 = p * OUTPROJ_PIECE
        prow = slice(p0, p0 + OUTPROJ_PIECE)
        y = (jnp.dot(oa_ref[prow, :], wo_ref[:A_WIDTH, :], preferred_element_type=F32)
             + jnp.dot(ob_ref[prow, :], wo_ref[A_WIDTH:, :], preferred_element_type=F32))
        his, los = [], []
        for c in range(OUTPROJ_PIECE // ROW_CHUNK):
            r0 = p0 + c * ROW_CHUNK
            rows = slice(r0, r0 + ROW_CHUNK)
            x1 = x_ref[rows, :] + gt_ref[0] * (_rms(y[c * ROW_CHUNK:(c + 1) * ROW_CHUNK]) * gpost_ref[...])
            x1_ref[rows, :] = x1
            h2 = (_rms(x1) * gpre_ref[...]) * (1.0 + sc_ref[0]) + sh_ref[0]
            hi = h2.astype(BF16)
            his.append(hi)
            los.append((h2 - hi.astype(F32)).astype(BF16))
            _store_row_slabs(h2_ref, r0, ROW_CHUNK, h2)
        hi, lo = jnp.concatenate(his, axis=0), jnp.concatenate(los, axis=0)
        hw = jnp.dot(hi, wr_ref[...], preferred_element_type=F32)
        lw = jnp.dot(lo, wr_ref[:, :LANES], preferred_element_type=F32)
        lg_ref[prow, :] = hw[:, :LANES] + hw[:, LANES:] + lw + br_ref[...]


def _outproj(oa, ob, w_out, x2d, gt, g_post, g_pre, sc, sh, w_router, b_router, seq, tm=OUTPROJ_TM):
    t, d = x2d.shape
    tm = min(tm, seq)
    per_b = seq // tm
    kern = functools.partial(_outproj_kernel, tm=tm)
    wr_hi = w_router.astype(BF16)
    wr_lo = (w_router - wr_hi.astype(F32)).astype(BF16)
    wr_cat = jnp.concatenate([wr_hi, wr_lo], axis=1)
    row = lambda: pl.BlockSpec((1, d), lambda i: (0, 0))
    per_batch = lambda: pl.BlockSpec((1, 1, d), lambda i: (i // per_b, 0, 0))
    return pl.pallas_call(
        kern,
        out_shape=(jax.ShapeDtypeStruct((t, d), F32),
                   jax.ShapeDtypeStruct((t * ROW_SLABS, LANES), U32),
                   jax.ShapeDtypeStruct((t, LANES), F32)),
        grid=(t // tm,),
        in_specs=[pl.BlockSpec((tm, A_WIDTH), lambda i: (i, 0)),
                  pl.BlockSpec((tm, B_WIDTH), lambda i: (i, 0)),
                  pl.BlockSpec((d, d), lambda i: (0, 0)),
                  pl.BlockSpec((tm, d), lambda i: (i, 0)),
                  per_batch(), row(), row(), per_batch(), per_batch(),
                  pl.BlockSpec((d, 2 * LANES), lambda i: (0, 0)),
                  pl.BlockSpec((1, LANES), lambda i: (0, 0))],
        out_specs=(pl.BlockSpec((tm, d), lambda i: (i, 0)),
                   pl.BlockSpec((tm * ROW_SLABS, LANES), lambda i: (i, 0)),
                   pl.BlockSpec((tm, LANES), lambda i: (i, 0))),
        compiler_params=_params("arbitrary"),
        name="out_proj",
    )(oa, ob, w_out, x2d, gt, g_post, g_pre, sc, sh, wr_cat, b_router)


def _route_kernel(lg_ref, rec_ref, cnt_ref, carry_scr, *, tr):
    @pl.when(pl.program_id(0) == 0)
    def _():
        carry_scr[...] = jnp.zeros(carry_scr.shape, F32)

    lg = lg_ref[...]
    lane = lax.broadcasted_iota(jnp.int32, lg.shape, 1)
    big = jnp.int32(LANES)

    def first_lane(mask):
        return jnp.min(jnp.where(mask, lane, big), axis=-1, keepdims=True)

    gmask = lane < N_GROUPS
    gmax = jnp.max(jnp.where(gmask, lg, -jnp.inf), axis=-1, keepdims=True)
    gexp = jnp.where(gmask, jnp.exp(lg - gmax), 0.0)
    gprob = gexp / jnp.sum(gexp, axis=-1, keepdims=True)
    g_val = jnp.max(gprob, axis=-1, keepdims=True)
    g_idx = first_lane(gmask & (gprob == g_val))

    lo = ROUTER_EXPERT_LANE0 + g_idx * EXPERTS_PER_GROUP
    emask = (lane >= lo) & (lane < lo + EXPERTS_PER_GROUP)
    emax = jnp.max(jnp.where(emask, lg, -jnp.inf), axis=-1, keepdims=True)
    eexp = jnp.where(emask, jnp.exp(lg - emax), 0.0)
    eprob = eexp / jnp.sum(eexp, axis=-1, keepdims=True)
    v1 = jnp.max(eprob, axis=-1, keepdims=True)
    i1 = first_lane(emask & (eprob == v1))
    rest = emask & (lane != i1)
    v2 = jnp.max(jnp.where(rest, eprob, -1.0), axis=-1, keepdims=True)
    i2 = first_lane(rest & (eprob == v2))
    w1 = g_val * (v1 / (v1 + v2))
    w2 = g_val * (v2 / (v1 + v2))

    hit1 = lane == i1
    hit2 = lane == i2
    onehot = (hit1 | hit2).astype(BF16)
    r = lax.broadcasted_iota(jnp.int32, (tr, tr), 0)
    c = lax.broadcasted_iota(jnp.int32, (tr, tr), 1)
    before = (c < r).astype(BF16)
    pos = carry_scr[...] + jnp.dot(before, onehot, preferred_element_type=F32)
    rank1 = jnp.sum(jnp.where(hit1, pos, 0.0), axis=-1, keepdims=True)
    rank2 = jnp.sum(jnp.where(hit2, pos, 0.0), axis=-1, keepdims=True)
    carry_scr[...] = carry_scr[...] + jnp.sum(onehot.astype(F32), axis=0, keepdims=True)
    cnt_ref[...] = carry_scr[...]

    e1 = (i1 - ROUTER_EXPERT_LANE0).astype(F32)
    e2 = (i2 - ROUTER_EXPERT_LANE0).astype(F32)
    rec = jnp.zeros(lg.shape, F32)
    for ln, val in ((ROUTE_E1, e1), (ROUTE_E2, e2), (ROUTE_W1, w1), (ROUTE_W2, w2),
                    (ROUTE_R1, rank1), (ROUTE_R2, rank2)):
        rec = jnp.where(lane == ln, val, rec)
    rec_ref[...] = rec


def _route(logits, tr=ROUTE_TR):
    t = logits.shape[0]
    tr = min(tr, t)
    kern = functools.partial(_route_kernel, tr=tr)
    return pl.pallas_call(
        kern,
        out_shape=(jax.ShapeDtypeStruct((t, LANES), F32), jax.ShapeDtypeStruct((1, LANES), F32)),
        grid=(t // tr,),
        in_specs=[pl.BlockSpec((tr, LANES), lambda i: (i, 0))],
        out_specs=(pl.BlockSpec((tr, LANES), lambda i: (i, 0)),
                   pl.BlockSpec((1, LANES), lambda i: (0, 0))),
        scratch_shapes=[pltpu.VMEM((1, LANES), F32)],
        compiler_params=_params("arbitrary"),
        name="route",
    )(logits)


def _slab_rows(ref, row):
    return ref.at[pl.ds(pl.multiple_of(row * ROW_SLABS, ROW_SLABS), ROW_SLABS), :]


def _dispatch_kernel(slot_ref, pad_start_ref, pad_len_ref, used_ref, h2_ref, xs_hbm, zero_scr, sems, pad_sems,
                     *, td, tm):
    g = pl.program_id(0)
    tile_rows = tm * ROW_SLABS
    n_tiles = xs_hbm.shape[0] // tile_rows

    def zero_copy(slot, nslots, sem):
        rows = pl.ds(pl.multiple_of(slot * ROW_SLABS, ROW_SLABS), nslots * ROW_SLABS)
        return pltpu.make_async_copy(zero_scr.at[pl.ds(0, nslots * ROW_SLABS), :], xs_hbm.at[rows, :], sem)

    pad_sizes = [1 << b for b in reversed(range((tm - 1).bit_length()))]

    @pl.when(g == 0)
    def _():
        zero_scr[...] = jnp.zeros(zero_scr.shape, U32)

        def unused_tile(tile, carry):
            zero_copy(tile * tm, tm, sems.at[1]).start()
            zero_copy(tile * tm, tm, sems.at[1]).wait()
            return carry

        lax.fori_loop(used_ref[0], n_tiles, unused_tile, 0)

        def per_expert(e, counts):
            off = pad_start_ref[e]
            n = pad_len_ref[e]
            new_counts = []
            for b, size in enumerate(pad_sizes):
                hit = (n & size) != 0

                @pl.when(hit)
                def _():
                    zero_copy(off, size, pad_sems.at[b]).start()

                off = off + jnp.where(hit, size, 0)
                new_counts.append(counts[b] + hit.astype(jnp.int32))
            return tuple(new_counts)

        counts = lax.fori_loop(0, N_EXPERTS, per_expert, tuple(jnp.int32(0) for _ in pad_sizes))
        for b, size in enumerate(pad_sizes):
            def drain(r, c):
                zero_copy(0, size, pad_sems.at[b]).wait()
                return c
            lax.fori_loop(0, counts[b], drain, 0)

    def row_copy(r, slot):
        return pltpu.make_async_copy(_slab_rows(h2_ref, r), _slab_rows(xs_hbm, slot), sems.at[0])

    base = g * td

    def issue(r, c):
        tok = base + r
        row_copy(r, slot_ref[2 * tok]).start()
        row_copy(r, slot_ref[2 * tok + 1]).start()
        return c

    lax.fori_loop(0, td, issue, 0, unroll=CHUNK_UNROLL)
    for _ in range(2):
        pltpu.make_async_copy(h2_ref, xs_hbm.at[pl.ds(0, td * ROW_SLABS), :], sems.at[0]).wait()


def _dispatch(slot, pad_start, pad_len, used, h2_rows, n_slots, tm, td=DISPATCH_TD):
    t = slot.shape[0] // 2
    td = min(td, t)
    kern = functools.partial(_dispatch_kernel, td=td, tm=tm)
    grid_spec = pltpu.PrefetchScalarGridSpec(
        num_scalar_prefetch=4,
        grid=(t // td,),
        in_specs=[pl.BlockSpec((td * ROW_SLABS, LANES), lambda g, sl, ps, pn, us: (g, 0))],
        out_specs=pl.BlockSpec(memory_space=pl.ANY),
        scratch_shapes=[pltpu.VMEM((tm * ROW_SLABS, LANES), U32), pltpu.SemaphoreType.DMA((2,)),
                        pltpu.SemaphoreType.DMA(((tm - 1).bit_length(),))],
    )
    return pl.pallas_call(
        kern,
        out_shape=jax.ShapeDtypeStruct((n_slots * ROW_SLABS, LANES), U32),
        grid_spec=grid_spec,
        compiler_params=_params("arbitrary"),
        name="dispatch",
    )(slot, pad_start, pad_len, used, h2_rows)


TILE_UNUSED, TILE_USED, TILE_NEW_EXPERT = 0, 1, 2


def _expert_kernel(texp_ref, tblk_ref, tstate_ref, tnext_ref, tpar_ref, xs_ref, w1_hbm, w3_hbm, w2_hbm, eo_ref,
                   x_scr, w1_scr, w3_scr, w2_scr, w1_stage, w3_stage, w2_stage, sems, *, tm):
    i = pl.program_id(0)
    state = tstate_ref[i]
    slot = tpar_ref[i]

    def weight_copies(expert, dst_slot):
        return [pltpu.make_async_copy(hbm.at[expert], stage.at[dst_slot], sems.at[dst_slot])
                for hbm, stage in ((w1_hbm, w1_stage), (w3_hbm, w3_stage), (w2_hbm, w2_stage))]

    @pl.when(state == TILE_UNUSED)
    def _():
        eo_ref[...] = jnp.zeros(eo_ref.shape, U32)

    @pl.when(i == 0)
    def _():
        for cp in weight_copies(texp_ref[0], slot):
            cp.start()

    @pl.when(state == TILE_NEW_EXPERT)
    def _():
        for cp in weight_copies(texp_ref[i], slot):
            cp.wait()

        @pl.when(tnext_ref[i] >= 0)
        def _():
            for cp in weight_copies(tnext_ref[i], 1 - slot):
                cp.start()

        w1_scr[...] = w1_stage[slot].astype(BF16)
        w3_scr[...] = w3_stage[slot].astype(BF16)
        w2_scr[...] = w2_stage[slot].astype(BF16)

    @pl.when(state != TILE_UNUSED)
    def _():
        for s in range(ROW_SLABS):
            lo, hi = _unpack_bf16_pair(xs_ref[pl.ds(s, tm, stride=ROW_SLABS), :])
            x_scr[:, s * LANES:(s + 1) * LANES] = lo.astype(BF16)
            x_scr[:, HALF_D + s * LANES:HALF_D + (s + 1) * LANES] = hi.astype(BF16)
        x = x_scr[...]
        a = jnp.dot(x, w1_scr[...], preferred_element_type=F32)
        b = jnp.dot(x, w3_scr[...], preferred_element_type=F32)
        hid = (_silu(a) * b).astype(BF16)
        y = jnp.dot(hid, w2_scr[...], preferred_element_type=F32)
        _store_row_slabs(eo_ref, 0, tm, y)


def _experts(tile_expert, tile_block, tile_state, tile_next, tile_slot, xs_rows, w1, w3, w2, tm):
    n_tiles = tile_expert.shape[0]
    d, f = w1.shape[1], w1.shape[2]
    kern = functools.partial(_expert_kernel, tm=tm)
    grid_spec = pltpu.PrefetchScalarGridSpec(
        num_scalar_prefetch=5,
        grid=(n_tiles,),
        in_specs=[pl.BlockSpec((tm * ROW_SLABS, LANES), lambda i, te, tb, ts, tn, tp: (tb[i], 0)),
                  pl.BlockSpec(memory_space=pl.ANY), pl.BlockSpec(memory_space=pl.ANY),
                  pl.BlockSpec(memory_space=pl.ANY)],
        out_specs=pl.BlockSpec((tm * ROW_SLABS, LANES), lambda i, te, tb, ts, tn, tp: (i, 0)),
        scratch_shapes=[pltpu.VMEM((tm, d), BF16), pltpu.VMEM((d, f), BF16), pltpu.VMEM((d, f), BF16),
                        pltpu.VMEM((f, d), BF16),
                        pltpu.VMEM((2, d, f), F32), pltpu.VMEM((2, d, f), F32), pltpu.VMEM((2, f, d), F32),
                        pltpu.SemaphoreType.DMA((2,))],
    )
    return pl.pallas_call(
        kern,
        out_shape=jax.ShapeDtypeStruct(xs_rows.shape, U32),
        grid_spec=grid_spec,
        compiler_params=_params("arbitrary"),
        name="expert_mlp",
    )(tile_expert, tile_block, tile_state, tile_next, tile_slot, xs_rows, w1, w3, w2)


def _final_kernel(slot_ref, eo_hbm, rec_ref, x1_ref, gt_ref, g_ref, o_ref, e_scr, sems, *, tf):
    i = pl.program_id(0)
    par = i % 2

    def start_all(step, buf):
        def body(r, c):
            tok = step * tf + r
            for k in range(2):
                pltpu.make_async_copy(_slab_rows(eo_hbm, slot_ref[2 * tok + k]),
                                      _slab_rows(e_scr.at[buf, k], r), sems.at[buf]).start()
            return c
        lax.fori_loop(0, tf, body, 0, unroll=CHUNK_UNROLL)

    def wait_all(buf):
        for k in range(2):
            pltpu.make_async_copy(eo_hbm.at[pl.ds(0, tf * ROW_SLABS), :], e_scr.at[buf, k], sems.at[buf]).wait()

    @pl.when(i == 0)
    def _():
        start_all(0, 0)

    @pl.when(i + 1 < pl.num_programs(0))
    def _():
        start_all(i + 1, 1 - par)

    wait_all(par)

    def chunk(c, carry):
        r0 = pl.multiple_of(c * ROW_CHUNK, ROW_CHUNK)
        rows = pl.ds(r0, ROW_CHUNK)
        rec = rec_ref[rows, :]
        w1 = rec[:, ROUTE_W1:ROUTE_W1 + 1]
        w2 = rec[:, ROUTE_W2:ROUTE_W2 + 1]
        lo1, hi1 = _load_row_slabs(e_scr.at[par, 0], r0, ROW_CHUNK)
        lo2, hi2 = _load_row_slabs(e_scr.at[par, 1], r0, ROW_CHUNK)
        y = jnp.concatenate([w1 * lo1 + w2 * lo2, w1 * hi1 + w2 * hi2], axis=1)
        o_ref[rows, :] = x1_ref[rows, :] + gt_ref[0] * (_rms(y) * g_ref[...])
        return carry

    lax.fori_loop(0, tf // ROW_CHUNK, chunk, 0, unroll=CHUNK_UNROLL)


def _final(slot, eo_rows, rec, x1, gt, g_post, seq, tf=COMBINE_TF):
    t, d = x1.shape
    tf = min(tf, seq)
    per_b = seq // tf
    kern = functools.partial(_final_kernel, tf=tf)
    grid_spec = pltpu.PrefetchScalarGridSpec(
        num_scalar_prefetch=1,
        grid=(t // tf,),
        in_specs=[pl.BlockSpec(memory_space=pl.ANY),
                  pl.BlockSpec((tf, LANES), lambda i, sl: (i, 0)),
                  pl.BlockSpec((tf, d), lambda i, sl: (i, 0)),
                  pl.BlockSpec((1, 1, d), lambda i, sl: (i // per_b, 0, 0)),
                  pl.BlockSpec((1, d), lambda i, sl: (0, 0))],
        out_specs=pl.BlockSpec((tf, d), lambda i, sl: (i, 0)),
        scratch_shapes=[pltpu.VMEM((2, 2, tf * ROW_SLABS, LANES), U32),
                        pltpu.SemaphoreType.DMA((2,))],
    )
    return pl.pallas_call(
        kern,
        out_shape=jax.ShapeDtypeStruct((t, d), F32),
        grid_spec=grid_spec,
        compiler_params=_params("arbitrary"),
        name="combine_final",
    )(slot, eo_rows, rec, x1, gt, g_post)


def _dispatch_tables(rec, counts, t, tm):
    e = rec[:, ROUTE_E1:ROUTE_E2 + 1].astype(jnp.int32)
    rank = rec[:, ROUTE_R1:ROUTE_R2 + 1].astype(jnp.int32)
    cnt = counts[0, ROUTER_EXPERT_LANE0:ROUTER_EXPERT_LANE0 + N_EXPERTS].astype(jnp.int32)
    tiles_per = (cnt + tm - 1) // tm
    tile_end = jnp.cumsum(tiles_per)
    tile_start = tile_end - tiles_per
    n_tiles = (2 * t + N_EXPERTS * (tm - 1)) // tm
    experts = jnp.arange(N_EXPERTS, dtype=jnp.int32)
    start_of = jnp.sum(jnp.where(e[..., None] == experts, tile_start, 0), axis=-1)
    slot = (start_of * tm + rank).reshape(-1)
    pad_start = tile_start * tm + cnt
    pad_len = tiles_per * tm - cnt
    tile_id = jnp.arange(n_tiles, dtype=jnp.int32)
    used = tile_end[-1]
    tblk = jnp.minimum(tile_id, used - 1)
    texp = jnp.sum(tile_end[None, :] <= tblk[:, None], axis=-1).astype(jnp.int32)
    tstate = jnp.where(tile_id < used, jnp.where(tile_id == tile_start[texp], TILE_NEW_EXPERT, TILE_USED),
                       TILE_UNUSED).astype(jnp.int32)
    nonempty = cnt > 0
    ordinal = jnp.cumsum(nonempty.astype(jnp.int32)) - 1
    later = jnp.where(nonempty[None, :] & (experts[None, :] > experts[:, None]), experts[None, :], N_EXPERTS)
    next_expert = jnp.min(later, axis=-1)
    next_expert = jnp.where(next_expert == N_EXPERTS, -1, next_expert)
    tnext = next_expert[texp].astype(jnp.int32)
    tslot = (ordinal[texp] % 2).astype(jnp.int32)
    return slot, pad_start, pad_len, used.reshape(1), texp, tblk, tstate, tnext, tslot, n_tiles * tm


def kernel(x, c, rel_bias, w_ada, b_ada, g_pre_mix, g_post_mix, w_in, w_alpha, b_alpha, lam_q1, lam_k1, lam_q2,
           lam_k2, g_sub_a, g_norm_b, w_out, g_pre_ffn, g_post_ffn, w_router_g, b_router_g, w_router_e,
           b_router_e, w1, w3, w2):
    batch, seq, d = x.shape
    t = batch * seq
    depth = w_in.shape[0]
    tq = min(ATTN_TQ, seq)
    tm_e = EXPERT_TM
    xf = x.reshape(t, d)
    for i in range(depth):
        lam_init = 0.8 - 0.6 * math.exp(-0.3 * i)
        c_pad = jnp.pad(c, ((0, 8 - batch % 8 if batch % 8 else 0), (0, 0)))
        ada = _ada(c_pad, w_ada[i], b_ada[i][None, :])[:batch]
        sh_m, sc_m, gt_m, sh_f, sc_f, gt_f = [a[:, None, :] for a in jnp.split(ada, 6, axis=-1)]

        w_in_b = w_in[i].astype(BF16)
        w_z = jnp.pad(w_in_b[:, D_MAIN:], ((0, 0), (0, LANES - GATE_RANK)))
        proj, zb = _inproj(xf, g_pre_mix[i][None, :], sc_m, sh_m, w_in_b, w_z, seq)

        oa = _attention(proj, _bias_tiles(rel_bias, tq), lam_q1[i][None, :], lam_k1[i][None, :],
                        lam_q2[i][None, :], lam_k2[i][None, :], g_sub_a[i][:, None], batch, seq, lam_init, tq)
        w_alpha_pad = jnp.pad(w_alpha[i], ((0, LANES - GATE_RANK), (0, 0)))
        ob = _gla(proj, zb, w_alpha_pad, b_alpha[i][None, :], g_norm_b[i][None, :], batch, seq)

        w_router = jnp.pad(jnp.concatenate([w_router_g[i], w_router_e[i]], axis=1),
                           ((0, 0), (0, LANES - N_GROUPS - N_EXPERTS)))
        b_router = jnp.pad(jnp.concatenate([b_router_g[i], b_router_e[i]]),
                           (0, LANES - N_GROUPS - N_EXPERTS))[None, :]
        x1, h2_rows, logits = _outproj(oa, ob, w_out[i].astype(BF16), xf, gt_m, g_post_mix[i][None, :],
                                       g_pre_ffn[i][None, :], sc_f, sh_f, w_router, b_router, seq)

        rec, counts = _route(logits)
        (slot, pad_start, pad_len, used, texp, tblk, tstate, tnext, tslot,
         n_slots) = _dispatch_tables(rec, counts, t, tm_e)
        xs = _dispatch(slot, pad_start, pad_len, used, h2_rows, n_slots, tm_e)
        eo = _experts(texp, tblk, tstate, tnext, tslot, xs, w1[i], w3[i], w2[i], tm_e)
        xf = _final(slot, eo, rec, x1, gt_f, g_post_ffn[i][None, :], seq)
    return xf.reshape(batch, seq, d)
```

```python
import functools
import math

import jax
import jax.numpy as jnp
from jax import lax
from jax.experimental import pallas as pl
from jax.experimental.pallas import tpu as pltpu

F32 = jnp.float32
BF16 = jnp.bfloat16

D_MODEL = 2048
CHUNK = 64
A_HEADS = 8
A_DK = 64
A_DV = 2 * A_DK
A_WIDTH = A_HEADS * A_DV
B_HEADS = 4
B_WIDTH = D_MODEL - A_WIDTH
B_DV = B_WIDTH // B_HEADS
B_DK = B_DV // 2
GATE_RANK = 16
GATE_TAU = 16.0
N_BUCKETS = 32
MAX_DISTANCE = 256
N_GROUPS = 4
EXPERTS_PER_GROUP = 8
N_EXPERTS = N_GROUPS * EXPERTS_PER_GROUP
D_EXPERT = D_MODEL // 4
EPS = 1e-6
NEG_INF = -1e30
LOG2E = math.log2(math.e)

LANES = 128
U32 = jnp.uint32
HALF_D = D_MODEL // 2
ROW_SLABS = HALF_D // LANES
ROW_CHUNK = 16
CHUNK_UNROLL = 4
ONES_ROWS = 16
FAR_BIAS_DISTANCE = 166

ADA_TN = 1024
INPROJ_TM, INPROJ_TN = 512, 1024
ATTN_TQ, ATTN_TK = 512, 256
GLA_LC = 512
OUTPROJ_TM = 512
OUTPROJ_PIECE = 512
ROUTE_TR = 512
EXPERT_TM = 512
DISPATCH_TD = 1024
COMBINE_TF = 256
D_MAIN = 3 * A_WIDTH + 2 * B_HEADS * B_DK + 2 * B_WIDTH
COL_QA, COL_KA, COL_VA = 0, A_HEADS, 2 * A_HEADS
COL_QB = 3 * A_HEADS
COL_KB = COL_QB + B_HEADS
COL_VB256 = (3 * A_WIDTH + 2 * B_HEADS * B_DK) // B_DV
COL_RB256 = COL_VB256 + B_HEADS
ROUTE_E1, ROUTE_E2, ROUTE_W1, ROUTE_W2, ROUTE_R1, ROUTE_R2 = 0, 1, 2, 3, 4, 5
ROUTER_EXPERT_LANE0 = N_GROUPS
ROUTER_ROWS = 48

VMEM_LIMIT = 56 * 1024 * 1024
INPROJ_VMEM_LIMIT = 60 * 1024 * 1024


def _params(*sem):
    return pltpu.CompilerParams(dimension_semantics=sem, vmem_limit_bytes=VMEM_LIMIT)


def _rms(v):
    return v * lax.rsqrt(jnp.mean(v * v, axis=-1, keepdims=True) + EPS)


def _silu(v):
    return v * jax.nn.sigmoid(v)


_HIGH_HALF = 0xFFFF0000


def _pack_bf16_pair(lo, hi):
    lo_bits = lax.bitcast_convert_type(lo.astype(BF16).astype(F32), U32) >> 16
    hi_bits = lax.bitcast_convert_type(hi.astype(BF16).astype(F32), U32) & U32(_HIGH_HALF)
    return hi_bits | lo_bits


def _unpack_bf16_pair(w):
    return (lax.bitcast_convert_type(w << 16, F32), lax.bitcast_convert_type(w & U32(_HIGH_HALF), F32))


def _store_row_slabs(ref, r0, nrows, rows_f32):
    packed = _pack_bf16_pair(rows_f32[:, :HALF_D], rows_f32[:, HALF_D:])
    for s in range(ROW_SLABS):
        ref[pl.ds(r0 * ROW_SLABS + s, nrows, stride=ROW_SLABS), :] = packed[:, s * LANES:(s + 1) * LANES]


def _load_row_slabs(ref, r0, nrows):
    slabs = [_unpack_bf16_pair(ref[pl.ds(r0 * ROW_SLABS + s, nrows, stride=ROW_SLABS), :]) for s in range(ROW_SLABS)]
    return (jnp.concatenate([lo for lo, _ in slabs], axis=1), jnp.concatenate([hi for _, hi in slabs], axis=1))


def _ada_kernel(c_ref, w_ref, b_ref, o_ref):
    s = _silu(c_ref[...])
    o_ref[...] = jnp.dot(s.astype(BF16), w_ref[...].astype(BF16), preferred_element_type=F32) + b_ref[...]


def _ada(c_pad, w, b, tn=ADA_TN):
    m, d = c_pad.shape
    n = w.shape[1]
    return pl.pallas_call(
        _ada_kernel,
        out_shape=jax.ShapeDtypeStruct((m, n), F32),
        grid=(n // tn,),
        in_specs=[pl.BlockSpec((m, d), lambda j: (0, 0)),
                  pl.BlockSpec((d, tn), lambda j: (0, j)),
                  pl.BlockSpec((1, tn), lambda j: (0, j))],
        out_specs=pl.BlockSpec((m, tn), lambda j: (0, j)),
        compiler_params=_params("arbitrary"),
        name="ada_proj",
    )(c_pad, w, b)


def _inproj_kernel(x_ref, xn_ref, g_ref, sc_ref, sh_ref, scn_ref, shn_ref, w_ref, wz_ref, o_ref, z_ref,
                   h_scr, hn_scr, *, tm, tn):
    def normed(src_ref, scale_ref, shift_ref, rows):
        h = _rms(src_ref[rows, :]) * g_ref[...]
        return (h * (1.0 + scale_ref[0]) + shift_ref[0]).astype(BF16)

    @pl.when(pl.program_id(0) == 0)
    def _():
        def chunk(c, carry):
            rows = pl.ds(pl.multiple_of(c * ROW_CHUNK, ROW_CHUNK), ROW_CHUNK)
            h_scr[rows, :] = normed(x_ref, sc_ref, sh_ref, rows)
            return carry
        lax.fori_loop(0, tm // ROW_CHUNK, chunk, 0, unroll=CHUNK_UNROLL)

    h = h_scr[...]
    z_ref[...] = jnp.dot(h, wz_ref[...], preferred_element_type=F32)
    n_col = D_MAIN // tn
    rows_per_col = tm // n_col // ROW_CHUNK * ROW_CHUNK
    next_row = 0
    for c in range(n_col):
        cols = slice(c * tn, (c + 1) * tn)
        o_ref[:, cols] = jnp.dot(h, w_ref[:, cols], preferred_element_type=F32).astype(BF16)
        stop = tm if c == n_col - 1 else next_row + rows_per_col
        for r0 in range(next_row, stop, ROW_CHUNK):
            rows = slice(r0, r0 + ROW_CHUNK)
            hn_scr[rows, :] = normed(xn_ref, scn_ref, shn_ref, rows)
        next_row = stop
    h_scr[...] = hn_scr[...]


def _inproj(x2d, g, sc, sh, w_all, w_z, seq, tm=INPROJ_TM, tn=INPROJ_TN):
    t, d = x2d.shape
    tm = min(tm, seq)
    per_b = seq // tm
    n_steps = t // tm
    kern = functools.partial(_inproj_kernel, tm=tm, tn=tn)
    nxt = lambda i: jnp.minimum(i + 1, n_steps - 1)
    per_batch = lambda step: pl.BlockSpec((1, 1, d), lambda i: (step(i) // per_b, 0, 0))
    return pl.pallas_call(
        kern,
        out_shape=(jax.ShapeDtypeStruct((t, D_MAIN), BF16), jax.ShapeDtypeStruct((t, LANES), F32)),
        grid=(n_steps,),
        in_specs=[pl.BlockSpec((tm, d), lambda i: (i, 0)),
                  pl.BlockSpec((tm, d), lambda i: (nxt(i), 0)),
                  pl.BlockSpec((1, d), lambda i: (0, 0)),
                  per_batch(lambda i: i), per_batch(lambda i: i), per_batch(nxt), per_batch(nxt),
                  pl.BlockSpec(w_all.shape, lambda i: (0, 0), pipeline_mode=pl.Buffered(1)),
                  pl.BlockSpec((d, LANES), lambda i: (0, 0), pipeline_mode=pl.Buffered(1))],
        out_specs=(pl.BlockSpec((tm, D_MAIN), lambda i: (i, 0)),
                   pl.BlockSpec((tm, LANES), lambda i: (i, 0))),
        scratch_shapes=[pltpu.VMEM((tm, d), BF16), pltpu.VMEM((tm, d), BF16)],
        compiler_params=pltpu.CompilerParams(dimension_semantics=("arbitrary",),
                                             vmem_limit_bytes=INPROJ_VMEM_LIMIT),
        name="in_proj",
    )(x2d, x2d, g, sc, sh, sc, sh, w_all, w_z)


def _t5_bucket(rel):
    nb = N_BUCKETS // 2
    max_exact = nb // 2
    base = jnp.where(rel > 0, nb, 0)
    n = jnp.abs(rel)
    nf = jnp.maximum(n, 1).astype(F32)
    large = max_exact + (jnp.log(nf / max_exact) / math.log(MAX_DISTANCE / max_exact)
                         * (nb - max_exact)).astype(jnp.int32)
    large = jnp.minimum(large, nb - 1)
    return base + jnp.where(n < max_exact, n, large)


def _bias_buckets(tq):
    kj = jnp.arange(tq, dtype=jnp.int32)[:, None]
    qi = jnp.arange(tq, dtype=jnp.int32)[None, :]
    near = _t5_bucket(kj - qi - tq)
    diag = jnp.where((kj // CHUNK) <= (qi // CHUNK), _t5_bucket(kj - qi), N_BUCKETS)
    return jnp.stack([near, diag]).astype(jnp.int32)


def _bias_kernel(rb_ref, bk_ref, o_ref):
    h = pl.program_id(0)
    far = rb_ref[N_BUCKETS // 2 - 1, h]
    bucket = bk_ref[...]
    acc = jnp.full(bucket.shape, NEG_INF, F32)
    for n in range(N_BUCKETS):
        acc = jnp.where(bucket == n, (rb_ref[n, h] - far) * LOG2E, acc)
    o_ref[...] = acc


def _bias_tiles(rel_bias, tq):
    return pl.pallas_call(
        _bias_kernel,
        out_shape=jax.ShapeDtypeStruct((A_HEADS, 2, tq, tq), F32),
        grid=(A_HEADS, 2),
        in_specs=[pl.BlockSpec(memory_space=pltpu.SMEM),
                  pl.BlockSpec((None, tq, tq), lambda h, d: (d, 0, 0))],
        out_specs=pl.BlockSpec((None, None, tq, tq), lambda h, d: (h, d, 0, 0)),
        compiler_params=_params("arbitrary", "arbitrary"),
        name="bias_tiles",
    )(rel_bias, _bias_buckets(tq))


def _attn_kernel(q_ref, qn_ref, k_ref, v_ref, bias_ref, lq1_ref, lk1_ref, lq2_ref, lk2_ref, g_ref, o_ref,
                 vt_scr, sa_scr, sb_scr, m_scr, acc_scr, *, tq, tk, lam_init):
    i = pl.program_id(2)
    nsub = tq // tk
    bufs = (sa_scr, sb_scr)

    @pl.when(i == 0)
    def _():
        ones = jnp.ones((ONES_ROWS, tk), BF16)
        for c in range(vt_scr.shape[0]):
            vt = v_ref[c * tk:(c + 1) * tk, :].astype(F32).T.astype(BF16)
            vt_scr[c] = jnp.concatenate([vt, ones], axis=0)

    lane = lax.broadcasted_iota(jnp.int32, (1, A_DV), 1)

    def two_map_queries(ref):
        q = ref[...] * (A_DK ** -0.5 * LOG2E)
        zero = jnp.zeros_like(q)
        return jnp.concatenate([jnp.where(lane < A_DK, q, zero), jnp.where(lane >= A_DK, q, zero)], axis=0)

    q2 = two_map_queries(q_ref)
    m_scr[...] = jnp.full(m_scr.shape, NEG_INF, F32)
    acc_scr[...] = jnp.zeros(acc_scr.shape, F32)

    def scores(j, queries=q2, q_lo=0):
        k = k_ref[pl.ds(pl.multiple_of(j * tk, tk), tk), :]
        if q_lo:
            queries = jnp.concatenate([queries[q_lo:tq], queries[tq + q_lo:]], axis=0)
        return lax.dot_general(k, queries, (((1,), (1,)), ((), ())), preferred_element_type=F32)

    def softmax_pv(s_ref, j, bias, q_lo=0):
        width = tq - q_lo
        halves = (slice(q_lo, tq), slice(tq + q_lo, 2 * tq))
        both = lambda ref: ref[...] if not q_lo else jnp.concatenate([ref[:, h] for h in halves], axis=1)
        s = s_ref[:, :2 * width]
        if bias is not None:
            s = jnp.concatenate([s[:, :width] + bias, s[:, width:] + bias], axis=1)
        m_old = both(m_scr)
        m_new = jnp.maximum(m_old, jnp.max(s, axis=0, keepdims=True))
        alpha = jnp.exp2(m_old - m_new)
        p = jnp.exp2(s - m_new).astype(BF16)
        acc = alpha * both(acc_scr) + jnp.dot(vt_scr[j], p, preferred_element_type=F32)
        if not q_lo:
            acc_scr[...] = acc
            m_scr[...] = m_new
        else:
            for n, h in enumerate(halves):
                acc_scr[:, h] = acc[:, n * width:(n + 1) * width]
                m_scr[:, h] = m_new[:, n * width:(n + 1) * width]

    n_far = jnp.maximum(i - 1, 0) * nsub

    @pl.when(i == 0)
    def _():
        sa_scr[...] = scores(0)

    def far_steps(j, count):
        for c in range(count):
            bufs[(c + 1) % 2][...] = scores(j + c + 1)
            softmax_pv(bufs[c % 2], j + c, None)

    def far_quad(jj, carry):
        far_steps(4 * jj, 4)
        return carry

    n_quads = n_far // 4
    lax.fori_loop(0, n_quads, far_quad, 0)

    @pl.when(n_far - 4 * n_quads >= 2)
    def _():
        far_steps(4 * n_quads, 2)

    def biased_steps(first_tile):
        j0 = (i - 1 + first_tile) * nsub
        count = (2 - first_tile) * nsub

        def geometry(c):
            d, r = first_tile + c // nsub, (c % nsub) * tk
            return d, r, (r if d == 1 else 0)

        for c in range(count):
            if c + 1 < count:
                q_next = geometry(c + 1)[2]
                bufs[(c + 1) % 2][:, :2 * (tq - q_next)] = scores(j0 + c + 1, q_lo=q_next)
            d, r, q_lo = geometry(c)
            no_bias = d == 0 and r + tk - 1 - tq <= -FAR_BIAS_DISTANCE
            bias = None if no_bias else bias_ref[d, r:r + tk, q_lo:]
            softmax_pv(bufs[c % 2], j0 + c, bias, q_lo)

    def finish():
        lam = (jnp.exp(jnp.sum(lq1_ref[...] * lk1_ref[...], axis=-1, keepdims=True))
               - jnp.exp(jnp.sum(lq2_ref[...] * lk2_ref[...], axis=-1, keepdims=True)) + lam_init)
        on = acc_scr[:A_DV, :] / acc_scr[A_DV:A_DV + 1, :]
        o = on[:, :tq] - lam * on[:, tq:]
        y = o * lax.rsqrt(jnp.mean(o * o, axis=0, keepdims=True) + EPS) * g_ref[...] * (1.0 - lam_init)
        o_ref[...] = y.T.astype(BF16)

    last = i + 1 == pl.num_programs(2)
    for first_tile, applies in ((0, i >= 1), (1, i == 0)):
        @pl.when(applies & jnp.logical_not(last))
        def _():
            biased_steps(first_tile)
            sa_scr[...] = scores(0, two_map_queries(qn_ref))
            finish()

        @pl.when(applies & last)
        def _():
            biased_steps(first_tile)
            finish()


def _attention(proj, bias_tiles, lq1, lk1, lq2, lk2, g_sub_col, batch, seq, lam_init, tq, tk=ATTN_TK):
    t = proj.shape[0]
    nq = seq // tq
    assert (tq // tk) % 2 == 0 and tq % tk == 0, "the score pipeline alternates two buffers per query tile"
    assert tq + 1 >= FAR_BIAS_DISTANCE, "key tiles two or more before the query tile must be past the bias horizon"
    kern = functools.partial(_attn_kernel, tq=tq, tk=tk, lam_init=lam_init)
    vec = lambda n: pl.BlockSpec((1, n), lambda b, h, i: (0, 0))
    return pl.pallas_call(
        kern,
        out_shape=jax.ShapeDtypeStruct((t, A_WIDTH), BF16),
        grid=(batch, A_HEADS, nq),
        in_specs=[pl.BlockSpec((tq, A_DV), lambda b, h, i: (b * nq + i, COL_QA + h)),
                  pl.BlockSpec((tq, A_DV), lambda b, h, i: (b * nq + jnp.minimum(i + 1, nq - 1), COL_QA + h)),
                  pl.BlockSpec((seq, A_DV), lambda b, h, i: (b, COL_KA + h)),
                  pl.BlockSpec((seq, A_DV), lambda b, h, i: (b, COL_VA + h)),
                  pl.BlockSpec((None, 2, tq, tq), lambda b, h, i: (h, 0, 0, 0)),
                  vec(A_DK), vec(A_DK), vec(A_DK), vec(A_DK),
                  pl.BlockSpec((A_DV, 1), lambda b, h, i: (0, 0))],
        out_specs=pl.BlockSpec((tq, A_DV), lambda b, h, i: (b * nq + i, h)),
        scratch_shapes=[pltpu.VMEM((seq // tk, A_DV + ONES_ROWS, tk), BF16),
                        pltpu.VMEM((tk, 2 * tq), F32), pltpu.VMEM((tk, 2 * tq), F32),
                        pltpu.VMEM((1, 2 * tq), F32),
                        pltpu.VMEM((A_DV + ONES_ROWS, 2 * tq), F32)],
        compiler_params=_params("arbitrary", "arbitrary", "arbitrary"),
        name="diff_attention",
    )(proj, proj, proj, proj, bias_tiles, lq1, lk1, lq2, lk2, g_sub_col)


def _split3(a):
    a1 = a.astype(BF16)
    r1 = a - a1.astype(F32)
    a2 = r1.astype(BF16)
    return a1, a2, (r1 - a2.astype(F32)).astype(BF16)


def _sum3(x):
    return x[:, :B_DK] + x[:, B_DK:2 * B_DK] + x[:, 2 * B_DK:]


def _gla_kernel(q_ref, k_ref, v_ref, r_ref, z_ref, wa_ref, ba_ref, g_ref, o_ref,
                state_scr, mask_scr, kv_scr, st_scr, *, n_chunks):
    lc = n_chunks * CHUNK

    @pl.when(pl.program_id(2) == 0)
    def _():
        state_scr[...] = jnp.zeros(state_scr.shape, F32)
        row = lax.broadcasted_iota(jnp.int32, (lc, lc), 0)
        col = lax.broadcasted_iota(jnp.int32, (lc, lc), 1)
        same = (row // CHUNK) == (col // CHUNK)
        mask_scr[...] = (same & (row >= col)).astype(BF16)

    z = z_ref[...]
    zh = z.astype(BF16)
    zl = (z - zh.astype(F32)).astype(BF16)
    pre = jnp.dot(jnp.concatenate([zh, zl, zh], axis=1), wa_ref[...], preferred_element_type=F32) + ba_ref[...]
    log_a = (jnp.minimum(pre, 0.0) - jnp.log1p(jnp.exp(-jnp.abs(pre)))) * (1.0 / GATE_TAU)
    parts = jnp.concatenate(_split3(log_a), axis=1)
    cum = _sum3(jnp.dot(mask_scr[...], parts, preferred_element_type=F32))
    totals = [cum[(c + 1) * CHUNK - 1:(c + 1) * CHUNK, :] for c in range(n_chunks)]
    total = jnp.concatenate([jnp.broadcast_to(tc, (CHUNK, B_DK)) for tc in totals], axis=0)
    k_dec = (k_ref[...].astype(F32) * jnp.exp(total - cum)).astype(BF16)

    for c in range(n_chunks):
        rows = slice(c * CHUNK, (c + 1) * CHUNK)
        kv_scr[c] = lax.dot_general(v_ref[rows, :], k_dec[rows], (((0,), (0,)), ((), ())),
                                    preferred_element_type=F32)
    state = state_scr[...]
    for c in range(n_chunks):
        state = state * jnp.exp(totals[c]) + kv_scr[c]
        st_scr[c] = state.astype(BF16)
    state_scr[...] = state

    for c in range(n_chunks):
        rows = slice(c * CHUNK, (c + 1) * CHUNK)
        o = lax.dot_general(q_ref[rows, :], st_scr[c], (((1,), (1,)), ((), ())),
                            preferred_element_type=F32) * (B_DK ** -0.5)
        o_ref[rows, :] = (_rms(o) * g_ref[...] * _silu(r_ref[rows, :].astype(F32))).astype(BF16)


def _gla(proj, zb, w_alpha_pad, b_alpha, g_norm, batch, seq, lc=GLA_LC):
    t = proj.shape[0]
    lc = min(lc, seq)
    nl = seq // lc
    n_chunks = lc // CHUNK
    kern = functools.partial(_gla_kernel, n_chunks=n_chunks)
    wa_hi = w_alpha_pad.astype(BF16)
    wa_lo = (w_alpha_pad - wa_hi.astype(F32)).astype(BF16)
    wa3 = jnp.concatenate([wa_hi, wa_hi, wa_lo], axis=0)
    return pl.pallas_call(
        kern,
        out_shape=jax.ShapeDtypeStruct((t, B_WIDTH), BF16),
        grid=(batch, B_HEADS, nl),
        in_specs=[pl.BlockSpec((lc, B_DK), lambda b, h, l: (b * nl + l, COL_QB + h)),
                  pl.BlockSpec((lc, B_DK), lambda b, h, l: (b * nl + l, COL_KB + h)),
                  pl.BlockSpec((lc, B_DV), lambda b, h, l: (b * nl + l, COL_VB256 + h)),
                  pl.BlockSpec((lc, B_DV), lambda b, h, l: (b * nl + l, COL_RB256 + h)),
                  pl.BlockSpec((lc, LANES), lambda b, h, l: (b * nl + l, 0)),
                  pl.BlockSpec((3 * LANES, B_DK), lambda b, h, l: (0, h)),
                  pl.BlockSpec((1, B_DK), lambda b, h, l: (0, h)),
                  pl.BlockSpec((1, B_DV), lambda b, h, l: (0, 0))],
        out_specs=pl.BlockSpec((lc, B_DV), lambda b, h, l: (b * nl + l, h)),
        scratch_shapes=[pltpu.VMEM((B_DV, B_DK), F32),
                        pltpu.VMEM((lc, lc), BF16),
                        pltpu.VMEM((n_chunks, B_DV, B_DK), F32),
                        pltpu.VMEM((n_chunks, B_DV, B_DK), BF16)],
        compiler_params=_params("arbitrary", "arbitrary", "arbitrary"),
        name="gla",
    )(proj, proj, proj, proj, zb, wa3, b_alpha, g_norm)


def _outproj_kernel(oa_ref, ob_ref, wo_ref, x_ref, gt_ref, gpost_ref, gpre_ref, sc_ref, sh_ref, wr_ref,
                    br_ref, x1_ref, h2_ref, lg_ref, *, tm):
    for p in range(tm // OUTPROJ_PIECE):
        p0 = p * OUTPROJ_PIECE
        prow = slice(p0, p0 + OUTPROJ_PIECE)
        y = (jnp.dot(oa_ref[prow, :], wo_ref[:A_WIDTH, :], preferred_element_type=F32)
             + jnp.dot(ob_ref[prow, :], wo_ref[A_WIDTH:, :], preferred_element_type=F32))
        his, los = [], []
        for c in range(OUTPROJ_PIECE // ROW_CHUNK):
            r0 = p0 + c * ROW_CHUNK
            rows = slice(r0, r0 + ROW_CHUNK)
            x1 = x_ref[rows, :] + gt_ref[0] * (_rms(y[c * ROW_CHUNK:(c + 1) * ROW_CHUNK]) * gpost_ref[...])
            x1_ref[rows, :] = x1
            h2 = (_rms(x1) * gpre_ref[...]) * (1.0 + sc_ref[0]) + sh_ref[0]
            hi = h2.astype(BF16)
            his.append(hi)
            los.append((h2 - hi.astype(F32)).astype(BF16))
            _store_row_slabs(h2_ref, r0, ROW_CHUNK, h2)
        hi, lo = jnp.concatenate(his, axis=0), jnp.concatenate(los, axis=0)
        hw = jnp.dot(hi, wr_ref[...], preferred_element_type=F32)
        lw = jnp.dot(lo, wr_ref[:, :LANES], preferred_element_type=F32)
        lg_ref[prow, :] = hw[:, :LANES] + hw[:, LANES:] + lw + br_ref[...]


def _outproj(oa, ob, w_out, x2d, gt, g_post, g_pre, sc, sh, w_router, b_router, seq, tm=OUTPROJ_TM):
    t, d = x2d.shape
    tm = min(tm, seq)
    per_b = seq // tm
    kern = functools.partial(_outproj_kernel, tm=tm)
    wr_hi = w_router.astype(BF16)
    wr_lo = (w_router - wr_hi.astype(F32)).astype(BF16)
    wr_cat = jnp.concatenate([wr_hi, wr_lo], axis=1)
    row = lambda: pl.BlockSpec((1, d), lambda i: (0, 0))
    per_batch = lambda: pl.BlockSpec((1, 1, d), lambda i: (i // per_b, 0, 0))
    return pl.pallas_call(
        kern,
        out_shape=(jax.ShapeDtypeStruct((t, d), F32),
                   jax.ShapeDtypeStruct((t * ROW_SLABS, LANES), U32),
                   jax.ShapeDtypeStruct((t, LANES), F32)),
        grid=(t // tm,),
        in_specs=[pl.BlockSpec((tm, A_WIDTH), lambda i: (i, 0)),
                  pl.BlockSpec((tm, B_WIDTH), lambda i: (i, 0)),
                  pl.BlockSpec((d, d), lambda i: (0, 0)),
                  pl.BlockSpec((tm, d), lambda i: (i, 0)),
                  per_batch(), row(), row(), per_batch(), per_batch(),
                  pl.BlockSpec((d, 2 * LANES), lambda i: (0, 0)),
                  pl.BlockSpec((1, LANES), lambda i: (0, 0))],
        out_specs=(pl.BlockSpec((tm, d), lambda i: (i, 0)),
                   pl.BlockSpec((tm * ROW_SLABS, LANES), lambda i: (i, 0)),
                   pl.BlockSpec((tm, LANES), lambda i: (i, 0))),
        compiler_params=_params("arbitrary"),
        name="out_proj",
    )(oa, ob, w_out, x2d, gt, g_post, g_pre, sc, sh, wr_cat, b_router)


def _route_kernel(lg_ref, rec_ref, cnt_ref, carry_scr, before_scr, *, tr):
    @pl.when(pl.program_id(0) == 0)
    def _():
        carry_scr[...] = jnp.zeros(carry_scr.shape, F32)
        earlier = lax.broadcasted_iota(jnp.int32, (tr, tr), 0)
        token = lax.broadcasted_iota(jnp.int32, (tr, tr), 1)
        before_scr[...] = (earlier < token).astype(BF16)

    lg = lg_ref[...].T[:ROUTER_ROWS]
    row = lax.broadcasted_iota(jnp.int32, lg.shape, 0)
    big = jnp.int32(ROUTER_ROWS)

    def first_row(mask):
        return jnp.min(jnp.where(mask, row, big), axis=0, keepdims=True)

    gmask = row < N_GROUPS
    gmax = jnp.max(jnp.where(gmask, lg, -jnp.inf), axis=0, keepdims=True)
    gexp = jnp.where(gmask, jnp.exp(lg - gmax), 0.0)
    gprob = gexp / jnp.sum(gexp, axis=0, keepdims=True)
    g_val = jnp.max(gprob, axis=0, keepdims=True)
    g_idx = first_row(gmask & (gprob == g_val))

    lo = ROUTER_EXPERT_LANE0 + g_idx * EXPERTS_PER_GROUP
    emask = (row >= lo) & (row < lo + EXPERTS_PER_GROUP)
    emax = jnp.max(jnp.where(emask, lg, -jnp.inf), axis=0, keepdims=True)
    eexp = jnp.where(emask, jnp.exp(lg - emax), 0.0)
    eprob = eexp / jnp.sum(eexp, axis=0, keepdims=True)
    v1 = jnp.max(eprob, axis=0, keepdims=True)
    i1 = first_row(emask & (eprob == v1))
    rest = emask & (row != i1)
    v2 = jnp.max(jnp.where(rest, eprob, -1.0), axis=0, keepdims=True)
    i2 = first_row(rest & (eprob == v2))
    w1 = g_val * (v1 / (v1 + v2))
    w2 = g_val * (v2 / (v1 + v2))

    hit1 = row == i1
    hit2 = row == i2
    onehot = (hit1 | hit2).astype(BF16)
    pos = carry_scr[...] + jnp.dot(onehot, before_scr[...], preferred_element_type=F32)
    rank1 = jnp.sum(jnp.where(hit1, pos, 0.0), axis=0, keepdims=True)
    rank2 = jnp.sum(jnp.where(hit2, pos, 0.0), axis=0, keepdims=True)
    carry_scr[...] = carry_scr[...] + jnp.sum(onehot.astype(F32), axis=1, keepdims=True)
    cnt_ref[...] = carry_scr[...]

    e1 = (i1 - ROUTER_EXPERT_LANE0).astype(F32)
    e2 = (i2 - ROUTER_EXPERT_LANE0).astype(F32)
    field = lax.broadcasted_iota(jnp.int32, (LANES, tr), 0)
    rec = jnp.zeros((LANES, tr), F32)
    for ln, val in ((ROUTE_E1, e1), (ROUTE_E2, e2), (ROUTE_W1, w1), (ROUTE_W2, w2),
                    (ROUTE_R1, rank1), (ROUTE_R2, rank2)):
        rec = jnp.where(field == ln, val, rec)
    rec_ref[...] = rec.T


def _route(logits, tr=ROUTE_TR):
    t = logits.shape[0]
    tr = min(tr, t)
    kern = functools.partial(_route_kernel, tr=tr)
    return pl.pallas_call(
        kern,
        out_shape=(jax.ShapeDtypeStruct((t, LANES), F32), jax.ShapeDtypeStruct((ROUTER_ROWS, 1), F32)),
        grid=(t // tr,),
        in_specs=[pl.BlockSpec((tr, LANES), lambda i: (i, 0))],
        out_specs=(pl.BlockSpec((tr, LANES), lambda i: (i, 0)),
                   pl.BlockSpec((ROUTER_ROWS, 1), lambda i: (0, 0))),
        scratch_shapes=[pltpu.VMEM((ROUTER_ROWS, 1), F32), pltpu.VMEM((tr, tr), BF16)],
        compiler_params=_params("arbitrary"),
        name="route",
    )(logits)


def _slab_rows(ref, row):
    return ref.at[pl.ds(pl.multiple_of(row * ROW_SLABS, ROW_SLABS), ROW_SLABS), :]


def _dispatch_kernel(slot_ref, pad_start_ref, pad_len_ref, used_ref, h2_ref, xs_hbm, zero_scr, sems, pad_sems,
                     *, td, tm):
    g = pl.program_id(0)
    tile_rows = tm * ROW_SLABS
    n_tiles = xs_hbm.shape[0] // tile_rows

    def zero_copy(slot, nslots, sem):
        rows = pl.ds(pl.multiple_of(slot * ROW_SLABS, ROW_SLABS), nslots * ROW_SLABS)
        return pltpu.make_async_copy(zero_scr.at[pl.ds(0, nslots * ROW_SLABS), :], xs_hbm.at[rows, :], sem)

    pad_sizes = [1 << b for b in reversed(range((tm - 1).bit_length()))]

    @pl.when(g == 0)
    def _():
        zero_scr[...] = jnp.zeros(zero_scr.shape, U32)

        def unused_tile(tile, carry):
            zero_copy(tile * tm, tm, sems.at[1]).start()
            zero_copy(tile * tm, tm, sems.at[1]).wait()
            return carry

        lax.fori_loop(used_ref[0], n_tiles, unused_tile, 0)

        def per_expert(e, counts):
            off = pad_start_ref[e]
            n = pad_len_ref[e]
            new_counts = []
            for b, size in enumerate(pad_sizes):
                hit = (n & size) != 0

                @pl.when(hit)
                def _():
                    zero_copy(off, size, pad_sems.at[b]).start()

                off = off + jnp.where(hit, size, 0)
                new_counts.append(counts[b] + hit.astype(jnp.int32))
            return tuple(new_counts)

        counts = lax.fori_loop(0, N_EXPERTS, per_expert, tuple(jnp.int32(0) for _ in pad_sizes))
        for b, size in enumerate(pad_sizes):
            def drain(r, c):
                zero_copy(0, size, pad_sems.at[b]).wait()
                return c
            lax.fori_loop(0, counts[b], drain, 0)

    def row_copy(r, slot):
        return pltpu.make_async_copy(_slab_rows(h2_ref, r), _slab_rows(xs_hbm, slot), sems.at[0])

    base = g * td

    def issue(r, c):
        tok = base + r
        row_copy(r, slot_ref[2 * tok]).start()
        row_copy(r, slot_ref[2 * tok + 1]).start()
        return c

    lax.fori_loop(0, td, issue, 0, unroll=CHUNK_UNROLL)
    for _ in range(2):
        pltpu.make_async_copy(h2_ref, xs_hbm.at[pl.ds(0, td * ROW_SLABS), :], sems.at[0]).wait()


def _dispatch(slot, pad_start, pad_len, used, h2_rows, n_slots, tm, td=DISPATCH_TD):
    t = slot.shape[0] // 2
    td = min(td, t)
    kern = functools.partial(_dispatch_kernel, td=td, tm=tm)
    grid_spec = pltpu.PrefetchScalarGridSpec(
        num_scalar_prefetch=4,
        grid=(t // td,),
        in_specs=[pl.BlockSpec((td * ROW_SLABS, LANES), lambda g, sl, ps, pn, us: (g, 0))],
        out_specs=pl.BlockSpec(memory_space=pl.ANY),
        scratch_shapes=[pltpu.VMEM((tm * ROW_SLABS, LANES), U32), pltpu.SemaphoreType.DMA((2,)),
                        pltpu.SemaphoreType.DMA(((tm - 1).bit_length(),))],
    )
    return pl.pallas_call(
        kern,
        out_shape=jax.ShapeDtypeStruct((n_slots * ROW_SLABS, LANES), U32),
        grid_spec=grid_spec,
        compiler_params=_params("arbitrary"),
        name="dispatch",
    )(slot, pad_start, pad_len, used, h2_rows)


TILE_UNUSED, TILE_USED, TILE_NEW_EXPERT = 0, 1, 2


def _expert_kernel(texp_ref, tblk_ref, tstate_ref, tnext_ref, tpar_ref, xs_ref, w1_hbm, w3_hbm, w2_hbm, eo_ref,
                   x_scr, w1_scr, w3_scr, w2_scr, w1_stage, w3_stage, w2_stage, sems, *, tm):
    i = pl.program_id(0)
    state = tstate_ref[i]
    slot = tpar_ref[i]

    def weight_copies(expert, dst_slot):
        return [pltpu.make_async_copy(hbm.at[expert], stage.at[dst_slot], sems.at[dst_slot])
                for hbm, stage in ((w1_hbm, w1_stage), (w3_hbm, w3_stage), (w2_hbm, w2_stage))]

    @pl.when(state == TILE_UNUSED)
    def _():
        eo_ref[...] = jnp.zeros(eo_ref.shape, U32)

    @pl.when(i == 0)
    def _():
        for cp in weight_copies(texp_ref[0], slot):
            cp.start()

    @pl.when(state == TILE_NEW_EXPERT)
    def _():
        for cp in weight_copies(texp_ref[i], slot):
            cp.wait()

        @pl.when(tnext_ref[i] >= 0)
        def _():
            for cp in weight_copies(tnext_ref[i], 1 - slot):
                cp.start()

        w1_scr[...] = w1_stage[slot].astype(BF16)
        w3_scr[...] = w3_stage[slot].astype(BF16)
        w2_scr[...] = w2_stage[slot].astype(BF16)

    @pl.when(state != TILE_UNUSED)
    def _():
        for s in range(ROW_SLABS):
            lo, hi = _unpack_bf16_pair(xs_ref[pl.ds(s, tm, stride=ROW_SLABS), :])
            x_scr[:, s * LANES:(s + 1) * LANES] = lo.astype(BF16)
            x_scr[:, HALF_D + s * LANES:HALF_D + (s + 1) * LANES] = hi.astype(BF16)
        x = x_scr[...]
        a = jnp.dot(x, w1_scr[...], preferred_element_type=F32)
        b = jnp.dot(x, w3_scr[...], preferred_element_type=F32)
        hid = (_silu(a) * b).astype(BF16)
        y = jnp.dot(hid, w2_scr[...], preferred_element_type=F32)
        _store_row_slabs(eo_ref, 0, tm, y)


def _experts(tile_expert, tile_block, tile_state, tile_next, tile_slot, xs_rows, w1, w3, w2, tm):
    n_tiles = tile_expert.shape[0]
    d, f = w1.shape[1], w1.shape[2]
    kern = functools.partial(_expert_kernel, tm=tm)
    grid_spec = pltpu.PrefetchScalarGridSpec(
        num_scalar_prefetch=5,
        grid=(n_tiles,),
        in_specs=[pl.BlockSpec((tm * ROW_SLABS, LANES), lambda i, te, tb, ts, tn, tp: (tb[i], 0)),
                  pl.BlockSpec(memory_space=pl.ANY), pl.BlockSpec(memory_space=pl.ANY),
                  pl.BlockSpec(memory_space=pl.ANY)],
        out_specs=pl.BlockSpec((tm * ROW_SLABS, LANES), lambda i, te, tb, ts, tn, tp: (i, 0)),
        scratch_shapes=[pltpu.VMEM((tm, d), BF16), pltpu.VMEM((d, f), BF16), pltpu.VMEM((d, f), BF16),
                        pltpu.VMEM((f, d), BF16),
                        pltpu.VMEM((2, d, f), F32), pltpu.VMEM((2, d, f), F32), pltpu.VMEM((2, f, d), F32),
                        pltpu.SemaphoreType.DMA((2,))],
    )
    return pl.pallas_call(
        kern,
        out_shape=jax.ShapeDtypeStruct(xs_rows.shape, U32),
        grid_spec=grid_spec,
        compiler_params=_params("arbitrary"),
        name="expert_mlp",
    )(tile_expert, tile_block, tile_state, tile_next, tile_slot, xs_rows, w1, w3, w2)


def _final_kernel(slot_ref, eo_hbm, rec_ref, x1_ref, gt_ref, g_ref, o_ref, e_scr, sems, *, tf):
    i = pl.program_id(0)
    par = i % 2

    def start_all(step, buf):
        def body(r, c):
            tok = step * tf + r
            for k in range(2):
                pltpu.make_async_copy(_slab_rows(eo_hbm, slot_ref[2 * tok + k]),
                                      _slab_rows(e_scr.at[buf, k], r), sems.at[buf]).start()
            return c
        lax.fori_loop(0, tf, body, 0, unroll=CHUNK_UNROLL)

    def wait_all(buf):
        for k in range(2):
            pltpu.make_async_copy(eo_hbm.at[pl.ds(0, tf * ROW_SLABS), :], e_scr.at[buf, k], sems.at[buf]).wait()

    @pl.when(i == 0)
    def _():
        start_all(0, 0)

    @pl.when(i + 1 < pl.num_programs(0))
    def _():
        start_all(i + 1, 1 - par)

    wait_all(par)

    def chunk(c, carry):
        r0 = pl.multiple_of(c * ROW_CHUNK, ROW_CHUNK)
        rows = pl.ds(r0, ROW_CHUNK)
        rec = rec_ref[rows, :]
        w1 = rec[:, ROUTE_W1:ROUTE_W1 + 1]
        w2 = rec[:, ROUTE_W2:ROUTE_W2 + 1]
        lo1, hi1 = _load_row_slabs(e_scr.at[par, 0], r0, ROW_CHUNK)
        lo2, hi2 = _load_row_slabs(e_scr.at[par, 1], r0, ROW_CHUNK)
        y = jnp.concatenate([w1 * lo1 + w2 * lo2, w1 * hi1 + w2 * hi2], axis=1)
        o_ref[rows, :] = x1_ref[rows, :] + gt_ref[0] * (_rms(y) * g_ref[...])
        return carry

    lax.fori_loop(0, tf // ROW_CHUNK, chunk, 0, unroll=CHUNK_UNROLL)


def _final(slot, eo_rows, rec, x1, gt, g_post, seq, tf=COMBINE_TF):
    t, d = x1.shape
    tf = min(tf, seq)
    per_b = seq // tf
    kern = functools.partial(_final_kernel, tf=tf)
    grid_spec = pltpu.PrefetchScalarGridSpec(
        num_scalar_prefetch=1,
        grid=(t // tf,),
        in_specs=[pl.BlockSpec(memory_space=pl.ANY),
                  pl.BlockSpec((tf, LANES), lambda i, sl: (i, 0)),
                  pl.BlockSpec((tf, d), lambda i, sl: (i, 0)),
                  pl.BlockSpec((1, 1, d), lambda i, sl: (i // per_b, 0, 0)),
                  pl.BlockSpec((1, d), lambda i, sl: (0, 0))],
        out_specs=pl.BlockSpec((tf, d), lambda i, sl: (i, 0)),
        scratch_shapes=[pltpu.VMEM((2, 2, tf * ROW_SLABS, LANES), U32),
                        pltpu.SemaphoreType.DMA((2,))],
    )
    return pl.pallas_call(
        kern,
        out_shape=jax.ShapeDtypeStruct((t, d), F32),
        grid_spec=grid_spec,
        compiler_params=_params("arbitrary"),
        name="combine_final",
    )(slot, eo_rows, rec, x1, gt, g_post)


def _dispatch_tables(rec, counts, t, tm):
    e = rec[:, ROUTE_E1:ROUTE_E2 + 1].astype(jnp.int32)
    rank = rec[:, ROUTE_R1:ROUTE_R2 + 1].astype(jnp.int32)
    cnt = counts[ROUTER_EXPERT_LANE0:ROUTER_EXPERT_LANE0 + N_EXPERTS, 0].astype(jnp.int32)
    tiles_per = (cnt + tm - 1) // tm
    tile_end = jnp.cumsum(tiles_per)
    tile_start = tile_end - tiles_per
    n_tiles = (2 * t + N_EXPERTS * (tm - 1)) // tm
    experts = jnp.arange(N_EXPERTS, dtype=jnp.int32)
    start_of = jnp.sum(jnp.where(e[..., None] == experts, tile_start, 0), axis=-1)
    slot = (start_of * tm + rank).reshape(-1)
    pad_start = tile_start * tm + cnt
    pad_len = tiles_per * tm - cnt
    tile_id = jnp.arange(n_tiles, dtype=jnp.int32)
    used = tile_end[-1]
    tblk = jnp.minimum(tile_id, used - 1)
    texp = jnp.sum(tile_end[None, :] <= tblk[:, None], axis=-1).astype(jnp.int32)
    tstate = jnp.where(tile_id < used, jnp.where(tile_id == tile_start[texp], TILE_NEW_EXPERT, TILE_USED),
                       TILE_UNUSED).astype(jnp.int32)
    nonempty = cnt > 0
    ordinal = jnp.cumsum(nonempty.astype(jnp.int32)) - 1
    later = jnp.where(nonempty[None, :] & (experts[None, :] > experts[:, None]), experts[None, :], N_EXPERTS)
    next_expert = jnp.min(later, axis=-1)
    next_expert = jnp.where(next_expert == N_EXPERTS, -1, next_expert)
    tnext = next_expert[texp].astype(jnp.int32)
    tslot = (ordinal[texp] % 2).astype(jnp.int32)
    return slot, pad_start, pad_len, used.reshape(1), texp, tblk, tstate, tnext, tslot, n_tiles * tm


def kernel(x, c, rel_bias, w_ada, b_ada, g_pre_mix, g_post_mix, w_in, w_alpha, b_alpha, lam_q1, lam_k1, lam_q2,
           lam_k2, g_sub_a, g_norm_b, w_out, g_pre_ffn, g_post_ffn, w_router_g, b_router_g, w_router_e,
           b_router_e, w1, w3, w2):
    batch, seq, d = x.shape
    t = batch * seq
    depth = w_in.shape[0]
    tq = min(ATTN_TQ, seq)
    tm_e = EXPERT_TM
    xf = x.reshape(t, d)
    for i in range(depth):
        lam_init = 0.8 - 0.6 * math.exp(-0.3 * i)
        c_pad = jnp.pad(c, ((0, 8 - batch % 8 if batch % 8 else 0), (0, 0)))
        ada = _ada(c_pad, w_ada[i], b_ada[i][None, :])[:batch]
        sh_m, sc_m, gt_m, sh_f, sc_f, gt_f = [a[:, None, :] for a in jnp.split(ada, 6, axis=-1)]

        w_in_b = w_in[i].astype(BF16)
        w_z = jnp.pad(w_in_b[:, D_MAIN:], ((0, 0), (0, LANES - GATE_RANK)))
        proj, zb = _inproj(xf, g_pre_mix[i][None, :], sc_m, sh_m, w_in_b, w_z, seq)

        oa = _attention(proj, _bias_tiles(rel_bias, tq), lam_q1[i][None, :], lam_k1[i][None, :],
                        lam_q2[i][None, :], lam_k2[i][None, :], g_sub_a[i][:, None], batch, seq, lam_init, tq)
        w_alpha_pad = jnp.pad(w_alpha[i], ((0, LANES - GATE_RANK), (0, 0)))
        ob = _gla(proj, zb, w_alpha_pad, b_alpha[i][None, :], g_norm_b[i][None, :], batch, seq)

        w_router = jnp.pad(jnp.concatenate([w_router_g[i], w_router_e[i]], axis=1),
                           ((0, 0), (0, LANES - N_GROUPS - N_EXPERTS)))
        b_router = jnp.pad(jnp.concatenate([b_router_g[i], b_router_e[i]]),
                           (0, LANES - N_GROUPS - N_EXPERTS))[None, :]
        x1, h2_rows, logits = _outproj(oa, ob, w_out[i].astype(BF16), xf, gt_m, g_post_mix[i][None, :],
                                       g_pre_ffn[i][None, :], sc_f, sh_f, w_router, b_router, seq)

        rec, counts = _route(logits)
        (slot, pad_start, pad_len, used, texp, tblk, tstate, tnext, tslot,
         n_slots) = _dispatch_tables(rec, counts, t, tm_e)
        xs = _dispatch(slot, pad_start, pad_len, used, h2_rows, n_slots, tm_e)
        eo = _experts(texp, tblk, tstate, tnext, tslot, xs, w1[i], w3[i], w2[i], tm_e)
        xf = _final(slot, eo, rec, x1, gt_f, g_post_ffn[i][None, :], seq)
    return xf.reshape(batch, seq, d)
```

```python
import functools
import math

import jax
import jax.numpy as jnp
from jax import lax
from jax.experimental import pallas as pl
from jax.experimental.pallas import tpu as pltpu

F32 = jnp.float32
BF16 = jnp.bfloat16

D_MODEL = 2048
CHUNK = 64
A_HEADS = 8
A_DK = 64
A_DV = 2 * A_DK
A_WIDTH = A_HEADS * A_DV
B_HEADS = 4
B_WIDTH = D_MODEL - A_WIDTH
B_DV = B_WIDTH // B_HEADS
B_DK = B_DV // 2
GATE_RANK = 16
GATE_TAU = 16.0
N_BUCKETS = 32
MAX_DISTANCE = 256
N_GROUPS = 4
EXPERTS_PER_GROUP = 8
N_EXPERTS = N_GROUPS * EXPERTS_PER_GROUP
D_EXPERT = D_MODEL // 4
EPS = 1e-6
NEG_INF = -1e30
LOG2E = math.log2(math.e)

LANES = 128
U32 = jnp.uint32
HALF_D = D_MODEL // 2
ROW_SLABS = HALF_D // LANES
ROW_CHUNK = 16
CHUNK_UNROLL = 4
ONES_ROWS = 16
FAR_BIAS_DISTANCE = 166

ADA_TN = 1024
INPROJ_TM, INPROJ_TN = 512, 1024
ATTN_TQ, ATTN_TK = 512, 256
GLA_LC = 512
GLA_HEADS_PER_STEP = 4
OUTPROJ_TM = 512
OUTPROJ_PIECE = 128
ROUTE_TR = 512
EXPERT_TM = 512
DISPATCH_TD = 1024
COMBINE_TF = 256
D_MAIN = 3 * A_WIDTH + 2 * B_HEADS * B_DK + 2 * B_WIDTH
COL_QA, COL_KA, COL_VA = 0, A_HEADS, 2 * A_HEADS
COL_QB = 3 * A_HEADS
COL_KB = COL_QB + B_HEADS
COL_VB256 = (3 * A_WIDTH + 2 * B_HEADS * B_DK) // B_DV
COL_RB256 = COL_VB256 + B_HEADS
ROUTE_E1, ROUTE_E2, ROUTE_W1, ROUTE_W2, ROUTE_R1, ROUTE_R2 = 0, 1, 2, 3, 4, 5
ROUTER_EXPERT_LANE0 = N_GROUPS
ROUTER_ROWS = 48

VMEM_LIMIT = 56 * 1024 * 1024
INPROJ_VMEM_LIMIT = 60 * 1024 * 1024


def _params(*sem):
    return pltpu.CompilerParams(dimension_semantics=sem, vmem_limit_bytes=VMEM_LIMIT)


def _rms(v):
    return v * lax.rsqrt(jnp.mean(v * v, axis=-1, keepdims=True) + EPS)


def _silu(v):
    return v * jax.nn.sigmoid(v)


_HIGH_HALF = 0xFFFF0000


def _pack_bf16_pair(lo, hi):
    lo_bits = lax.bitcast_convert_type(lo.astype(BF16).astype(F32), U32) >> 16
    hi_bits = lax.bitcast_convert_type(hi.astype(BF16).astype(F32), U32) & U32(_HIGH_HALF)
    return hi_bits | lo_bits


def _unpack_bf16_pair(w):
    return (lax.bitcast_convert_type(w << 16, F32), lax.bitcast_convert_type(w & U32(_HIGH_HALF), F32))


def _store_row_slabs(ref, r0, nrows, rows_f32):
    packed = _pack_bf16_pair(rows_f32[:, :HALF_D], rows_f32[:, HALF_D:])
    for s in range(ROW_SLABS):
        ref[pl.ds(r0 * ROW_SLABS + s, nrows, stride=ROW_SLABS), :] = packed[:, s * LANES:(s + 1) * LANES]


def _load_row_slabs(ref, r0, nrows):
    slabs = [_unpack_bf16_pair(ref[pl.ds(r0 * ROW_SLABS + s, nrows, stride=ROW_SLABS), :]) for s in range(ROW_SLABS)]
    return (jnp.concatenate([lo for lo, _ in slabs], axis=1), jnp.concatenate([hi for _, hi in slabs], axis=1))


def _ada_kernel(c_ref, w_ref, b_ref, o_ref):
    s = _silu(c_ref[...])
    o_ref[...] = jnp.dot(s.astype(BF16), w_ref[...].astype(BF16), preferred_element_type=F32) + b_ref[...]


def _ada(c_pad, w, b, tn=ADA_TN):
    m, d = c_pad.shape
    n = w.shape[1]
    return pl.pallas_call(
        _ada_kernel,
        out_shape=jax.ShapeDtypeStruct((m, n), F32),
        grid=(n // tn,),
        in_specs=[pl.BlockSpec((m, d), lambda j: (0, 0)),
                  pl.BlockSpec((d, tn), lambda j: (0, j)),
                  pl.BlockSpec((1, tn), lambda j: (0, j))],
        out_specs=pl.BlockSpec((m, tn), lambda j: (0, j)),
        compiler_params=_params("arbitrary"),
        name="ada_proj",
    )(c_pad, w, b)


def _inproj_kernel(x_ref, xn_ref, g_ref, sc_ref, sh_ref, scn_ref, shn_ref, w_ref, wz_ref, o_ref, z_ref,
                   h_scr, hn_scr, *, tm, tn):
    def normed(src_ref, scale_ref, shift_ref, rows):
        h = _rms(src_ref[rows, :]) * g_ref[...]
        return (h * (1.0 + scale_ref[0]) + shift_ref[0]).astype(BF16)

    @pl.when(pl.program_id(0) == 0)
    def _():
        def chunk(c, carry):
            rows = pl.ds(pl.multiple_of(c * ROW_CHUNK, ROW_CHUNK), ROW_CHUNK)
            h_scr[rows, :] = normed(x_ref, sc_ref, sh_ref, rows)
            return carry
        lax.fori_loop(0, tm // ROW_CHUNK, chunk, 0, unroll=CHUNK_UNROLL)

    h = h_scr[...]
    z_ref[...] = jnp.dot(h, wz_ref[...], preferred_element_type=F32)
    n_col = D_MAIN // tn
    rows_per_col = tm // n_col // ROW_CHUNK * ROW_CHUNK
    next_row = 0
    for c in range(n_col):
        cols = slice(c * tn, (c + 1) * tn)
        o_ref[:, cols] = jnp.dot(h, w_ref[:, cols], preferred_element_type=F32).astype(BF16)
        stop = tm if c == n_col - 1 else next_row + rows_per_col
        for r0 in range(next_row, stop, ROW_CHUNK):
            rows = slice(r0, r0 + ROW_CHUNK)
            hn_scr[rows, :] = normed(xn_ref, scn_ref, shn_ref, rows)
        next_row = stop
    h_scr[...] = hn_scr[...]


def _inproj(x2d, g, sc, sh, w_all, w_z, seq, tm=INPROJ_TM, tn=INPROJ_TN):
    t, d = x2d.shape
    tm = min(tm, seq)
    per_b = seq // tm
    n_steps = t // tm
    kern = functools.partial(_inproj_kernel, tm=tm, tn=tn)
    nxt = lambda i: jnp.minimum(i + 1, n_steps - 1)
    per_batch = lambda step: pl.BlockSpec((1, 1, d), lambda i: (step(i) // per_b, 0, 0))
    return pl.pallas_call(
        kern,
        out_shape=(jax.ShapeDtypeStruct((t, D_MAIN), BF16), jax.ShapeDtypeStruct((t, LANES), F32)),
        grid=(n_steps,),
        in_specs=[pl.BlockSpec((tm, d), lambda i: (i, 0)),
                  pl.BlockSpec((tm, d), lambda i: (nxt(i), 0)),
                  pl.BlockSpec((1, d), lambda i: (0, 0)),
                  per_batch(lambda i: i), per_batch(lambda i: i), per_batch(nxt), per_batch(nxt),
                  pl.BlockSpec(w_all.shape, lambda i: (0, 0), pipeline_mode=pl.Buffered(1)),
                  pl.BlockSpec((d, LANES), lambda i: (0, 0), pipeline_mode=pl.Buffered(1))],
        out_specs=(pl.BlockSpec((tm, D_MAIN), lambda i: (i, 0)),
                   pl.BlockSpec((tm, LANES), lambda i: (i, 0))),
        scratch_shapes=[pltpu.VMEM((tm, d), BF16), pltpu.VMEM((tm, d), BF16)],
        compiler_params=pltpu.CompilerParams(dimension_semantics=("arbitrary",),
                                             vmem_limit_bytes=INPROJ_VMEM_LIMIT),
        name="in_proj",
    )(x2d, x2d, g, sc, sh, sc, sh, w_all, w_z)


def _t5_bucket(rel):
    nb = N_BUCKETS // 2
    max_exact = nb // 2
    base = jnp.where(rel > 0, nb, 0)
    n = jnp.abs(rel)
    nf = jnp.maximum(n, 1).astype(F32)
    large = max_exact + (jnp.log(nf / max_exact) / math.log(MAX_DISTANCE / max_exact)
                         * (nb - max_exact)).astype(jnp.int32)
    large = jnp.minimum(large, nb - 1)
    return base + jnp.where(n < max_exact, n, large)


def _bias_buckets(tq):
    kj = jnp.arange(tq, dtype=jnp.int32)[:, None]
    qi = jnp.arange(tq, dtype=jnp.int32)[None, :]
    near = _t5_bucket(kj - qi - tq)
    diag = jnp.where((kj // CHUNK) <= (qi // CHUNK), _t5_bucket(kj - qi), N_BUCKETS)
    return jnp.stack([near, diag]).astype(jnp.int32)


def _bias_kernel(rb_ref, bk_ref, o_ref):
    h = pl.program_id(0)
    far = rb_ref[N_BUCKETS // 2 - 1, h]
    bucket = bk_ref[...]
    acc = jnp.full(bucket.shape, NEG_INF, F32)
    for n in range(N_BUCKETS):
        acc = jnp.where(bucket == n, (rb_ref[n, h] - far) * LOG2E, acc)
    o_ref[...] = acc


def _bias_tiles(rel_bias, tq):
    return pl.pallas_call(
        _bias_kernel,
        out_shape=jax.ShapeDtypeStruct((A_HEADS, 2, tq, tq), F32),
        grid=(A_HEADS, 2),
        in_specs=[pl.BlockSpec(memory_space=pltpu.SMEM),
                  pl.BlockSpec((None, tq, tq), lambda h, d: (d, 0, 0))],
        out_specs=pl.BlockSpec((None, None, tq, tq), lambda h, d: (h, d, 0, 0)),
        compiler_params=_params("arbitrary", "arbitrary"),
        name="bias_tiles",
    )(rel_bias, _bias_buckets(tq))


def _attn_kernel(q_ref, qn_ref, k_ref, v_ref, bias_ref, lq1_ref, lk1_ref, lq2_ref, lk2_ref, g_ref, o_ref,
                 vt_scr, sa_scr, sb_scr, m_scr, acc_scr, *, tq, tk, lam_init):
    i = pl.program_id(2)
    nsub = tq // tk
    bufs = (sa_scr, sb_scr)

    @pl.when(i == 0)
    def _():
        ones = jnp.ones((ONES_ROWS, tk), BF16)
        for c in range(vt_scr.shape[0]):
            vt = v_ref[c * tk:(c + 1) * tk, :].astype(F32).T.astype(BF16)
            vt_scr[c] = jnp.concatenate([vt, ones], axis=0)

    lane = lax.broadcasted_iota(jnp.int32, (1, A_DV), 1)

    def two_map_queries(ref):
        q = ref[...] * (A_DK ** -0.5 * LOG2E)
        zero = jnp.zeros_like(q)
        return jnp.concatenate([jnp.where(lane < A_DK, q, zero), jnp.where(lane >= A_DK, q, zero)], axis=0)

    q2 = two_map_queries(q_ref)
    m_scr[...] = jnp.full(m_scr.shape, NEG_INF, F32)
    acc_scr[...] = jnp.zeros(acc_scr.shape, F32)

    def scores(j, queries=q2, q_lo=0):
        k = k_ref[pl.ds(pl.multiple_of(j * tk, tk), tk), :]
        if q_lo:
            queries = jnp.concatenate([queries[q_lo:tq], queries[tq + q_lo:]], axis=0)
        return lax.dot_general(k, queries, (((1,), (1,)), ((), ())), preferred_element_type=F32)

    def softmax_pv(s_ref, j, bias, q_lo=0):
        width = tq - q_lo
        halves = (slice(q_lo, tq), slice(tq + q_lo, 2 * tq))
        both = lambda ref: ref[...] if not q_lo else jnp.concatenate([ref[:, h] for h in halves], axis=1)
        s = s_ref[:, :2 * width]
        if bias is not None:
            s = jnp.concatenate([s[:, :width] + bias, s[:, width:] + bias], axis=1)
        m_old = both(m_scr)
        m_new = jnp.maximum(m_old, jnp.max(s, axis=0, keepdims=True))
        alpha = jnp.exp2(m_old - m_new)
        p = jnp.exp2(s - m_new).astype(BF16)
        acc = alpha * both(acc_scr) + jnp.dot(vt_scr[j], p, preferred_element_type=F32)
        if not q_lo:
            acc_scr[...] = acc
            m_scr[...] = m_new
        else:
            for n, h in enumerate(halves):
                acc_scr[:, h] = acc[:, n * width:(n + 1) * width]
                m_scr[:, h] = m_new[:, n * width:(n + 1) * width]

    n_far = jnp.maximum(i - 1, 0) * nsub

    @pl.when(i == 0)
    def _():
        sa_scr[...] = scores(0)

    def far_steps(j, count):
        for c in range(count):
            bufs[(c + 1) % 2][...] = scores(j + c + 1)
            softmax_pv(bufs[c % 2], j + c, None)

    def far_quad(jj, carry):
        far_steps(4 * jj, 4)
        return carry

    n_quads = n_far // 4
    lax.fori_loop(0, n_quads, far_quad, 0)

    @pl.when(n_far - 4 * n_quads >= 2)
    def _():
        far_steps(4 * n_quads, 2)

    def biased_steps(first_tile):
        j0 = (i - 1 + first_tile) * nsub
        count = (2 - first_tile) * nsub

        def geometry(c):
            d, r = first_tile + c // nsub, (c % nsub) * tk
            return d, r, (r if d == 1 else 0)

        for c in range(count):
            if c + 1 < count:
                q_next = geometry(c + 1)[2]
                bufs[(c + 1) % 2][:, :2 * (tq - q_next)] = scores(j0 + c + 1, q_lo=q_next)
            d, r, q_lo = geometry(c)
            no_bias = d == 0 and r + tk - 1 - tq <= -FAR_BIAS_DISTANCE
            bias = None if no_bias else bias_ref[d, r:r + tk, q_lo:]
            softmax_pv(bufs[c % 2], j0 + c, bias, q_lo)

    def finish():
        lam = (jnp.exp(jnp.sum(lq1_ref[...] * lk1_ref[...], axis=-1, keepdims=True))
               - jnp.exp(jnp.sum(lq2_ref[...] * lk2_ref[...], axis=-1, keepdims=True)) + lam_init)
        on = acc_scr[:A_DV, :] / acc_scr[A_DV:A_DV + 1, :]
        o = on[:, :tq] - lam * on[:, tq:]
        y = o * lax.rsqrt(jnp.mean(o * o, axis=0, keepdims=True) + EPS) * g_ref[...] * (1.0 - lam_init)
        o_ref[...] = y.T.astype(BF16)

    last = i + 1 == pl.num_programs(2)
    for first_tile, applies in ((0, i >= 1), (1, i == 0)):
        @pl.when(applies & jnp.logical_not(last))
        def _():
            biased_steps(first_tile)
            sa_scr[...] = scores(0, two_map_queries(qn_ref))
            finish()

        @pl.when(applies & last)
        def _():
            biased_steps(first_tile)
            finish()


def _attention(proj, bias_tiles, lq1, lk1, lq2, lk2, g_sub_col, batch, seq, lam_init, tq, tk=ATTN_TK):
    t = proj.shape[0]
    nq = seq // tq
    assert (tq // tk) % 2 == 0 and tq % tk == 0, "the score pipeline alternates two buffers per query tile"
    assert tq + 1 >= FAR_BIAS_DISTANCE, "key tiles two or more before the query tile must be past the bias horizon"
    kern = functools.partial(_attn_kernel, tq=tq, tk=tk, lam_init=lam_init)
    vec = lambda n: pl.BlockSpec((1, n), lambda b, h, i: (0, 0))
    return pl.pallas_call(
        kern,
        out_shape=jax.ShapeDtypeStruct((t, A_WIDTH), BF16),
        grid=(batch, A_HEADS, nq),
        in_specs=[pl.BlockSpec((tq, A_DV), lambda b, h, i: (b * nq + i, COL_QA + h)),
                  pl.BlockSpec((tq, A_DV), lambda b, h, i: (b * nq + jnp.minimum(i + 1, nq - 1), COL_QA + h)),
                  pl.BlockSpec((seq, A_DV), lambda b, h, i: (b, COL_KA + h)),
                  pl.BlockSpec((seq, A_DV), lambda b, h, i: (b, COL_VA + h)),
                  pl.BlockSpec((None, 2, tq, tq), lambda b, h, i: (h, 0, 0, 0)),
                  vec(A_DK), vec(A_DK), vec(A_DK), vec(A_DK),
                  pl.BlockSpec((A_DV, 1), lambda b, h, i: (0, 0))],
        out_specs=pl.BlockSpec((tq, A_DV), lambda b, h, i: (b * nq + i, h)),
        scratch_shapes=[pltpu.VMEM((seq // tk, A_DV + ONES_ROWS, tk), BF16),
                        pltpu.VMEM((tk, 2 * tq), F32), pltpu.VMEM((tk, 2 * tq), F32),
                        pltpu.VMEM((1, 2 * tq), F32),
                        pltpu.VMEM((A_DV + ONES_ROWS, 2 * tq), F32)],
        compiler_params=_params("arbitrary", "arbitrary", "arbitrary"),
        name="diff_attention",
    )(proj, proj, proj, proj, bias_tiles, lq1, lk1, lq2, lk2, g_sub_col)


def _split3(a):
    a1 = a.astype(BF16)
    r1 = a - a1.astype(F32)
    a2 = r1.astype(BF16)
    return a1, a2, (r1 - a2.astype(F32)).astype(BF16)


def _sum3(x):
    return x[:, :B_DK] + x[:, B_DK:2 * B_DK] + x[:, 2 * B_DK:]


def _gla_kernel(q_ref, k_ref, v_ref, r_ref, z_ref, wa_ref, ba_ref, g_ref, o_ref,
                state_scr, mask_scr, kv_scr, st_scr, *, n_chunks):
    lc = n_chunks * CHUNK

    @pl.when(pl.program_id(2) == 0)
    def _():
        state_scr[...] = jnp.zeros(state_scr.shape, F32)
        row = lax.broadcasted_iota(jnp.int32, (lc, lc), 0)
        col = lax.broadcasted_iota(jnp.int32, (lc, lc), 1)
        same = (row // CHUNK) == (col // CHUNK)
        mask_scr[...] = (same & (row >= col)).astype(BF16)

    z = z_ref[...]
    zh = z.astype(BF16)
    zl = (z - zh.astype(F32)).astype(BF16)
    z3 = jnp.concatenate([zh, zl, zh], axis=1)
    kcols = lambda hh: slice(hh * B_DK, (hh + 1) * B_DK)
    vcols = lambda hh: slice(hh * B_DV, (hh + 1) * B_DV)
    chunk_rows = lambda c: slice(c * CHUNK, (c + 1) * CHUNK)
    live = [dict() for _ in range(GLA_HEADS_PER_STEP)]

    def gate_stage(hh):
        pre = jnp.dot(z3, wa_ref[:, kcols(hh)], preferred_element_type=F32) + ba_ref[:, kcols(hh)]
        log_a = (jnp.minimum(pre, 0.0) - jnp.log1p(jnp.exp(-jnp.abs(pre)))) * (1.0 / GATE_TAU)
        live[hh]["parts"] = jnp.concatenate(_split3(log_a), axis=1)

    def decay_stage(hh):
        cum = _sum3(jnp.dot(mask_scr[...], live[hh].pop("parts"), preferred_element_type=F32))
        totals = [cum[(c + 1) * CHUNK - 1:(c + 1) * CHUNK, :] for c in range(n_chunks)]
        total = jnp.concatenate([jnp.broadcast_to(tc, (CHUNK, B_DK)) for tc in totals], axis=0)
        live[hh]["totals"] = totals
        live[hh]["k_dec"] = (k_ref[:, kcols(hh)].astype(F32) * jnp.exp(total - cum)).astype(BF16)

    def kv_stage(hh):
        k_dec = live[hh].pop("k_dec")
        for c in range(n_chunks):
            kv_scr[hh, c] = lax.dot_general(v_ref[chunk_rows(c), vcols(hh)], k_dec[chunk_rows(c)],
                                            (((0,), (0,)), ((), ())), preferred_element_type=F32)

    def state_stage(hh):
        totals = live[hh].pop("totals")
        state = state_scr[hh]
        for c in range(n_chunks):
            state = state * jnp.exp(totals[c]) + kv_scr[hh, c]
            st_scr[hh, c] = state.astype(BF16)
        state_scr[hh] = state

    def output_stage(hh):
        for c in range(n_chunks):
            rows = chunk_rows(c)
            o = lax.dot_general(q_ref[rows, kcols(hh)], st_scr[hh, c], (((1,), (1,)), ((), ())),
                                preferred_element_type=F32) * (B_DK ** -0.5)
            o_ref[rows, vcols(hh)] = (_rms(o) * g_ref[...]
                                      * _silu(r_ref[rows, vcols(hh)].astype(F32))).astype(BF16)

    stages = (gate_stage, decay_stage, kv_stage, state_stage, output_stage)
    for tick in range(GLA_HEADS_PER_STEP + len(stages) - 1):
        for k, stage in enumerate(stages):
            if 0 <= tick - k < GLA_HEADS_PER_STEP:
                stage(tick - k)


def _gla(proj, zb, w_alpha_pad, b_alpha, g_norm, batch, seq, lc=GLA_LC):
    t = proj.shape[0]
    lc = min(lc, seq)
    nl = seq // lc
    n_chunks = lc // CHUNK
    hps = GLA_HEADS_PER_STEP
    assert B_HEADS % hps == 0 and COL_QB % hps == 0 and COL_KB % hps == 0
    kern = functools.partial(_gla_kernel, n_chunks=n_chunks)
    wa_hi = w_alpha_pad.astype(BF16)
    wa_lo = (w_alpha_pad - wa_hi.astype(F32)).astype(BF16)
    wa3 = jnp.concatenate([wa_hi, wa_hi, wa_lo], axis=0)
    kblock = lambda col0: pl.BlockSpec((lc, hps * B_DK), lambda b, h, l: (b * nl + l, col0 // hps + h))
    vblock = lambda col0: pl.BlockSpec((lc, hps * B_DV), lambda b, h, l: (b * nl + l, col0 // hps + h))
    return pl.pallas_call(
        kern,
        out_shape=jax.ShapeDtypeStruct((t, B_WIDTH), BF16),
        grid=(batch, B_HEADS // hps, nl),
        in_specs=[kblock(COL_QB), kblock(COL_KB), vblock(COL_VB256), vblock(COL_RB256),
                  pl.BlockSpec((lc, LANES), lambda b, h, l: (b * nl + l, 0)),
                  pl.BlockSpec((3 * LANES, hps * B_DK), lambda b, h, l: (0, h)),
                  pl.BlockSpec((1, hps * B_DK), lambda b, h, l: (0, h)),
                  pl.BlockSpec((1, B_DV), lambda b, h, l: (0, 0))],
        out_specs=pl.BlockSpec((lc, hps * B_DV), lambda b, h, l: (b * nl + l, h)),
        scratch_shapes=[pltpu.VMEM((hps, B_DV, B_DK), F32),
                        pltpu.VMEM((lc, lc), BF16),
                        pltpu.VMEM((hps, n_chunks, B_DV, B_DK), F32),
                        pltpu.VMEM((hps, n_chunks, B_DV, B_DK), BF16)],
        compiler_params=_params("arbitrary", "arbitrary", "arbitrary"),
        name="gla",
    )(proj, proj, proj, proj, zb, wa3, b_alpha, g_norm)


def _outproj_kernel(oa_ref, ob_ref, wo_ref, x_ref, gt_ref, gpost_ref, gpre_ref, sc_ref, sh_ref, wr_ref,
                    br_ref, x1_ref, h2_ref, lg_ref, *, tm):
    n_pieces = tm // OUTPROJ_PIECE

    def project(p):
        prow = slice(p * OUTPROJ_PIECE, (p + 1) * OUTPROJ_PIECE)
        return (jnp.dot(oa_ref[prow, :], wo_ref[:A_WIDTH, :], preferred_element_type=F32)
                + jnp.dot(ob_ref[prow, :], wo_ref[A_WIDTH:, :], preferred_element_type=F32))

    y_next = project(0)
    for p in range(n_pieces):
        p0 = p * OUTPROJ_PIECE
        prow = slice(p0, p0 + OUTPROJ_PIECE)
        y = y_next
        if p + 1 < n_pieces:
            y_next = project(p + 1)
        his, los = [], []
        for c in range(OUTPROJ_PIECE // ROW_CHUNK):
            r0 = p0 + c * ROW_CHUNK
            rows = slice(r0, r0 + ROW_CHUNK)
            x1 = x_ref[rows, :] + gt_ref[0] * (_rms(y[c * ROW_CHUNK:(c + 1) * ROW_CHUNK]) * gpost_ref[...])
            x1_ref[rows, :] = x1
            h2 = (_rms(x1) * gpre_ref[...]) * (1.0 + sc_ref[0]) + sh_ref[0]
            hi = h2.astype(BF16)
            his.append(hi)
            los.append((h2 - hi.astype(F32)).astype(BF16))
            _store_row_slabs(h2_ref, r0, ROW_CHUNK, h2)
        hi, lo = jnp.concatenate(his, axis=0), jnp.concatenate(los, axis=0)
        hw = jnp.dot(hi, wr_ref[...], preferred_element_type=F32)
        lw = jnp.dot(lo, wr_ref[:, :LANES], preferred_element_type=F32)
        lg_ref[prow, :] = hw[:, :LANES] + hw[:, LANES:] + lw + br_ref[...]


def _outproj(oa, ob, w_out, x2d, gt, g_post, g_pre, sc, sh, w_router, b_router, seq, tm=OUTPROJ_TM):
    t, d = x2d.shape
    tm = min(tm, seq)
    per_b = seq // tm
    kern = functools.partial(_outproj_kernel, tm=tm)
    wr_hi = w_router.astype(BF16)
    wr_lo = (w_router - wr_hi.astype(F32)).astype(BF16)
    wr_cat = jnp.concatenate([wr_hi, wr_lo], axis=1)
    row = lambda: pl.BlockSpec((1, d), lambda i: (0, 0))
    per_batch = lambda: pl.BlockSpec((1, 1, d), lambda i: (i // per_b, 0, 0))
    return pl.pallas_call(
        kern,
        out_shape=(jax.ShapeDtypeStruct((t, d), F32),
                   jax.ShapeDtypeStruct((t * ROW_SLABS, LANES), U32),
                   jax.ShapeDtypeStruct((t, LANES), F32)),
        grid=(t // tm,),
        in_specs=[pl.BlockSpec((tm, A_WIDTH), lambda i: (i, 0)),
                  pl.BlockSpec((tm, B_WIDTH), lambda i: (i, 0)),
                  pl.BlockSpec((d, d), lambda i: (0, 0)),
                  pl.BlockSpec((tm, d), lambda i: (i, 0)),
                  per_batch(), row(), row(), per_batch(), per_batch(),
                  pl.BlockSpec((d, 2 * LANES), lambda i: (0, 0)),
                  pl.BlockSpec((1, LANES), lambda i: (0, 0))],
        out_specs=(pl.BlockSpec((tm, d), lambda i: (i, 0)),
                   pl.BlockSpec((tm * ROW_SLABS, LANES), lambda i: (i, 0)),
                   pl.BlockSpec((tm, LANES), lambda i: (i, 0))),
        compiler_params=_params("arbitrary"),
        name="out_proj",
    )(oa, ob, w_out, x2d, gt, g_post, g_pre, sc, sh, wr_cat, b_router)


def _route_kernel(lg_ref, rec_ref, cnt_ref, carry_scr, before_scr, *, tr):
    @pl.when(pl.program_id(0) == 0)
    def _():
        carry_scr[...] = jnp.zeros(carry_scr.shape, F32)
        earlier = lax.broadcasted_iota(jnp.int32, (tr, tr), 0)
        token = lax.broadcasted_iota(jnp.int32, (tr, tr), 1)
        before_scr[...] = (earlier < token).astype(BF16)

    lg = lg_ref[...].T[:ROUTER_ROWS]
    row = lax.broadcasted_iota(jnp.int32, lg.shape, 0)
    big = jnp.int32(ROUTER_ROWS)

    def first_row(mask):
        return jnp.min(jnp.where(mask, row, big), axis=0, keepdims=True)

    gmask = row < N_GROUPS
    gmax = jnp.max(jnp.where(gmask, lg, -jnp.inf), axis=0, keepdims=True)
    gexp = jnp.where(gmask, jnp.exp(lg - gmax), 0.0)
    gprob = gexp / jnp.sum(gexp, axis=0, keepdims=True)
    g_val = jnp.max(gprob, axis=0, keepdims=True)
    g_idx = first_row(gmask & (gprob == g_val))

    lo = ROUTER_EXPERT_LANE0 + g_idx * EXPERTS_PER_GROUP
    emask = (row >= lo) & (row < lo + EXPERTS_PER_GROUP)
    emax = jnp.max(jnp.where(emask, lg, -jnp.inf), axis=0, keepdims=True)
    eexp = jnp.where(emask, jnp.exp(lg - emax), 0.0)
    eprob = eexp / jnp.sum(eexp, axis=0, keepdims=True)
    v1 = jnp.max(eprob, axis=0, keepdims=True)
    i1 = first_row(emask & (eprob == v1))
    rest = emask & (row != i1)
    v2 = jnp.max(jnp.where(rest, eprob, -1.0), axis=0, keepdims=True)
    i2 = first_row(rest & (eprob == v2))
    w1 = g_val * (v1 / (v1 + v2))
    w2 = g_val * (v2 / (v1 + v2))

    hit1 = row == i1
    hit2 = row == i2
    onehot = (hit1 | hit2).astype(BF16)
    pos = carry_scr[...] + jnp.dot(onehot, before_scr[...], preferred_element_type=F32)
    rank1 = jnp.sum(jnp.where(hit1, pos, 0.0), axis=0, keepdims=True)
    rank2 = jnp.sum(jnp.where(hit2, pos, 0.0), axis=0, keepdims=True)
    carry_scr[...] = carry_scr[...] + jnp.sum(onehot.astype(F32), axis=1, keepdims=True)
    cnt_ref[...] = carry_scr[...]

    e1 = (i1 - ROUTER_EXPERT_LANE0).astype(F32)
    e2 = (i2 - ROUTER_EXPERT_LANE0).astype(F32)
    field = lax.broadcasted_iota(jnp.int32, (LANES, tr), 0)
    rec = jnp.zeros((LANES, tr), F32)
    for ln, val in ((ROUTE_E1, e1), (ROUTE_E2, e2), (ROUTE_W1, w1), (ROUTE_W2, w2),
                    (ROUTE_R1, rank1), (ROUTE_R2, rank2)):
        rec = jnp.where(field == ln, val, rec)
    rec_ref[...] = rec.T


def _route(logits, tr=ROUTE_TR):
    t = logits.shape[0]
    tr = min(tr, t)
    kern = functools.partial(_route_kernel, tr=tr)
    return pl.pallas_call(
        kern,
        out_shape=(jax.ShapeDtypeStruct((t, LANES), F32), jax.ShapeDtypeStruct((ROUTER_ROWS, 1), F32)),
        grid=(t // tr,),
        in_specs=[pl.BlockSpec((tr, LANES), lambda i: (i, 0))],
        out_specs=(pl.BlockSpec((tr, LANES), lambda i: (i, 0)),
                   pl.BlockSpec((ROUTER_ROWS, 1), lambda i: (0, 0))),
        scratch_shapes=[pltpu.VMEM((ROUTER_ROWS, 1), F32), pltpu.VMEM((tr, tr), BF16)],
        compiler_params=_params("arbitrary"),
        name="route",
    )(logits)


def _slab_rows(ref, row):
    return ref.at[pl.ds(pl.multiple_of(row * ROW_SLABS, ROW_SLABS), ROW_SLABS), :]


def _dispatch_kernel(slot_ref, pad_start_ref, pad_len_ref, used_ref, h2_ref, xs_hbm, zero_scr, sems, pad_sems,
                     *, td, tm):
    g = pl.program_id(0)
    tile_rows = tm * ROW_SLABS
    n_tiles = xs_hbm.shape[0] // tile_rows

    def zero_copy(slot, nslots, sem):
        rows = pl.ds(pl.multiple_of(slot * ROW_SLABS, ROW_SLABS), nslots * ROW_SLABS)
        return pltpu.make_async_copy(zero_scr.at[pl.ds(0, nslots * ROW_SLABS), :], xs_hbm.at[rows, :], sem)

    pad_sizes = [1 << b for b in reversed(range((tm - 1).bit_length()))]

    @pl.when(g == 0)
    def _():
        zero_scr[...] = jnp.zeros(zero_scr.shape, U32)

        def unused_tile(tile, carry):
            zero_copy(tile * tm, tm, sems.at[1]).start()
            zero_copy(tile * tm, tm, sems.at[1]).wait()
            return carry

        lax.fori_loop(used_ref[0], n_tiles, unused_tile, 0)

        def per_expert(e, counts):
            off = pad_start_ref[e]
            n = pad_len_ref[e]
            new_counts = []
            for b, size in enumerate(pad_sizes):
                hit = (n & size) != 0

                @pl.when(hit)
                def _():
                    zero_copy(off, size, pad_sems.at[b]).start()

                off = off + jnp.where(hit, size, 0)
                new_counts.append(counts[b] + hit.astype(jnp.int32))
            return tuple(new_counts)

        counts = lax.fori_loop(0, N_EXPERTS, per_expert, tuple(jnp.int32(0) for _ in pad_sizes))
        for b, size in enumerate(pad_sizes):
            def drain(r, c):
                zero_copy(0, size, pad_sems.at[b]).wait()
                return c
            lax.fori_loop(0, counts[b], drain, 0)

    def row_copy(r, slot):
        return pltpu.make_async_copy(_slab_rows(h2_ref, r), _slab_rows(xs_hbm, slot), sems.at[0])

    base = g * td

    def issue(r, c):
        tok = base + r
        row_copy(r, slot_ref[2 * tok]).start()
        row_copy(r, slot_ref[2 * tok + 1]).start()
        return c

    lax.fori_loop(0, td, issue, 0, unroll=CHUNK_UNROLL)
    for _ in range(2):
        pltpu.make_async_copy(h2_ref, xs_hbm.at[pl.ds(0, td * ROW_SLABS), :], sems.at[0]).wait()


def _dispatch(slot, pad_start, pad_len, used, h2_rows, n_slots, tm, td=DISPATCH_TD):
    t = slot.shape[0] // 2
    td = min(td, t)
    kern = functools.partial(_dispatch_kernel, td=td, tm=tm)
    grid_spec = pltpu.PrefetchScalarGridSpec(
        num_scalar_prefetch=4,
        grid=(t // td,),
        in_specs=[pl.BlockSpec((td * ROW_SLABS, LANES), lambda g, sl, ps, pn, us: (g, 0))],
        out_specs=pl.BlockSpec(memory_space=pl.ANY),
        scratch_shapes=[pltpu.VMEM((tm * ROW_SLABS, LANES), U32), pltpu.SemaphoreType.DMA((2,)),
                        pltpu.SemaphoreType.DMA(((tm - 1).bit_length(),))],
    )
    return pl.pallas_call(
        kern,
        out_shape=jax.ShapeDtypeStruct((n_slots * ROW_SLABS, LANES), U32),
        grid_spec=grid_spec,
        compiler_params=_params("arbitrary"),
        name="dispatch",
    )(slot, pad_start, pad_len, used, h2_rows)


TILE_UNUSED, TILE_USED, TILE_NEW_EXPERT = 0, 1, 2


def _expert_kernel(texp_ref, tblk_ref, tstate_ref, tnext_ref, tpar_ref, xs_ref, w1_hbm, w3_hbm, w2_hbm, eo_ref,
                   x_scr, w1_scr, w3_scr, w2_scr, w1_stage, w3_stage, w2_stage, sems, *, tm):
    i = pl.program_id(0)
    state = tstate_ref[i]
    slot = tpar_ref[i]

    def weight_copies(expert, dst_slot):
        return [pltpu.make_async_copy(hbm.at[expert], stage.at[dst_slot], sems.at[dst_slot])
                for hbm, stage in ((w1_hbm, w1_stage), (w3_hbm, w3_stage), (w2_hbm, w2_stage))]

    @pl.when(state == TILE_UNUSED)
    def _():
        eo_ref[...] = jnp.zeros(eo_ref.shape, U32)

    @pl.when(i == 0)
    def _():
        for cp in weight_copies(texp_ref[0], slot):
            cp.start()

    @pl.when(state == TILE_NEW_EXPERT)
    def _():
        for cp in weight_copies(texp_ref[i], slot):
            cp.wait()

        @pl.when(tnext_ref[i] >= 0)
        def _():
            for cp in weight_copies(tnext_ref[i], 1 - slot):
                cp.start()

        w1_scr[...] = w1_stage[slot].astype(BF16)
        w3_scr[...] = w3_stage[slot].astype(BF16)
        w2_scr[...] = w2_stage[slot].astype(BF16)

    @pl.when(state != TILE_UNUSED)
    def _():
        for s in range(ROW_SLABS):
            lo, hi = _unpack_bf16_pair(xs_ref[pl.ds(s, tm, stride=ROW_SLABS), :])
            x_scr[:, s * LANES:(s + 1) * LANES] = lo.astype(BF16)
            x_scr[:, HALF_D + s * LANES:HALF_D + (s + 1) * LANES] = hi.astype(BF16)
        x = x_scr[...]
        a = jnp.dot(x, w1_scr[...], preferred_element_type=F32)
        b = jnp.dot(x, w3_scr[...], preferred_element_type=F32)
        hid = (_silu(a) * b).astype(BF16)
        y = jnp.dot(hid, w2_scr[...], preferred_element_type=F32)
        _store_row_slabs(eo_ref, 0, tm, y)


def _experts(tile_expert, tile_block, tile_state, tile_next, tile_slot, xs_rows, w1, w3, w2, tm):
    n_tiles = tile_expert.shape[0]
    d, f = w1.shape[1], w1.shape[2]
    kern = functools.partial(_expert_kernel, tm=tm)
    grid_spec = pltpu.PrefetchScalarGridSpec(
        num_scalar_prefetch=5,
        grid=(n_tiles,),
        in_specs=[pl.BlockSpec((tm * ROW_SLABS, LANES), lambda i, te, tb, ts, tn, tp: (tb[i], 0)),
                  pl.BlockSpec(memory_space=pl.ANY), pl.BlockSpec(memory_space=pl.ANY),
                  pl.BlockSpec(memory_space=pl.ANY)],
        out_specs=pl.BlockSpec((tm * ROW_SLABS, LANES), lambda i, te, tb, ts, tn, tp: (i, 0)),
        scratch_shapes=[pltpu.VMEM((tm, d), BF16), pltpu.VMEM((d, f), BF16), pltpu.VMEM((d, f), BF16),
                        pltpu.VMEM((f, d), BF16),
                        pltpu.VMEM((2, d, f), F32), pltpu.VMEM((2, d, f), F32), pltpu.VMEM((2, f, d), F32),
                        pltpu.SemaphoreType.DMA((2,))],
    )
    return pl.pallas_call(
        kern,
        out_shape=jax.ShapeDtypeStruct(xs_rows.shape, U32),
        grid_spec=grid_spec,
        compiler_params=_params("arbitrary"),
        name="expert_mlp",
    )(tile_expert, tile_block, tile_state, tile_next, tile_slot, xs_rows, w1, w3, w2)


def _final_kernel(slot_ref, eo_hbm, rec_ref, x1_ref, gt_ref, g_ref, o_ref, e_scr, sems, *, tf):
    i = pl.program_id(0)
    par = i % 2

    def start_all(step, buf):
        def body(r, c):
            tok = step * tf + r
            for k in range(2):
                pltpu.make_async_copy(_slab_rows(eo_hbm, slot_ref[2 * tok + k]),
                                      _slab_rows(e_scr.at[buf, k], r), sems.at[buf]).start()
            return c
        lax.fori_loop(0, tf, body, 0, unroll=CHUNK_UNROLL)

    def wait_all(buf):
        for k in range(2):
            pltpu.make_async_copy(eo_hbm.at[pl.ds(0, tf * ROW_SLABS), :], e_scr.at[buf, k], sems.at[buf]).wait()

    @pl.when(i == 0)
    def _():
        start_all(0, 0)

    @pl.when(i + 1 < pl.num_programs(0))
    def _():
        start_all(i + 1, 1 - par)

    wait_all(par)

    def chunk(c, carry):
        r0 = pl.multiple_of(c * ROW_CHUNK, ROW_CHUNK)
        rows = pl.ds(r0, ROW_CHUNK)
        rec = rec_ref[rows, :]
        w1 = rec[:, ROUTE_W1:ROUTE_W1 + 1]
        w2 = rec[:, ROUTE_W2:ROUTE_W2 + 1]
        lo1, hi1 = _load_row_slabs(e_scr.at[par, 0], r0, ROW_CHUNK)
        lo2, hi2 = _load_row_slabs(e_scr.at[par, 1], r0, ROW_CHUNK)
        y = jnp.concatenate([w1 * lo1 + w2 * lo2, w1 * hi1 + w2 * hi2], axis=1)
        o_ref[rows, :] = x1_ref[rows, :] + gt_ref[0] * (_rms(y) * g_ref[...])
        return carry

    lax.fori_loop(0, tf // ROW_CHUNK, chunk, 0, unroll=CHUNK_UNROLL)


def _final(slot, eo_rows, rec, x1, gt, g_post, seq, tf=COMBINE_TF):
    t, d = x1.shape
    tf = min(tf, seq)
    per_b = seq // tf
    kern = functools.partial(_final_kernel, tf=tf)
    grid_spec = pltpu.PrefetchScalarGridSpec(
        num_scalar_prefetch=1,
        grid=(t // tf,),
        in_specs=[pl.BlockSpec(memory_space=pl.ANY),
                  pl.BlockSpec((tf, LANES), lambda i, sl: (i, 0)),
                  pl.BlockSpec((tf, d), lambda i, sl: (i, 0)),
                  pl.BlockSpec((1, 1, d), lambda i, sl: (i // per_b, 0, 0)),
                  pl.BlockSpec((1, d), lambda i, sl: (0, 0))],
        out_specs=pl.BlockSpec((tf, d), lambda i, sl: (i, 0)),
        scratch_shapes=[pltpu.VMEM((2, 2, tf * ROW_SLABS, LANES), U32),
                        pltpu.SemaphoreType.DMA((2,))],
    )
    return pl.pallas_call(
        kern,
        out_shape=jax.ShapeDtypeStruct((t, d), F32),
        grid_spec=grid_spec,
        compiler_params=_params("arbitrary"),
        name="combine_final",
    )(slot, eo_rows, rec, x1, gt, g_post)


def _dispatch_tables(rec, counts, t, tm):
    e = rec[:, ROUTE_E1:ROUTE_E2 + 1].astype(jnp.int32)
    rank = rec[:, ROUTE_R1:ROUTE_R2 + 1].astype(jnp.int32)
    cnt = counts[ROUTER_EXPERT_LANE0:ROUTER_EXPERT_LANE0 + N_EXPERTS, 0].astype(jnp.int32)
    tiles_per = (cnt + tm - 1) // tm
    tile_end = jnp.cumsum(tiles_per)
    tile_start = tile_end - tiles_per
    n_tiles = (2 * t + N_EXPERTS * (tm - 1)) // tm
    experts = jnp.arange(N_EXPERTS, dtype=jnp.int32)
    start_of = jnp.sum(jnp.where(e[..., None] == experts, tile_start, 0), axis=-1)
    slot = (start_of * tm + rank).reshape(-1)
    pad_start = tile_start * tm + cnt
    pad_len = tiles_per * tm - cnt
    tile_id = jnp.arange(n_tiles, dtype=jnp.int32)
    used = tile_end[-1]
    tblk = jnp.minimum(tile_id, used - 1)
    texp = jnp.sum(tile_end[None, :] <= tblk[:, None], axis=-1).astype(jnp.int32)
    tstate = jnp.where(tile_id < used, jnp.where(tile_id == tile_start[texp], TILE_NEW_EXPERT, TILE_USED),
                       TILE_UNUSED).astype(jnp.int32)
    nonempty = cnt > 0
    ordinal = jnp.cumsum(nonempty.astype(jnp.int32)) - 1
    later = jnp.where(nonempty[None, :] & (experts[None, :] > experts[:, None]), experts[None, :], N_EXPERTS)
    next_expert = jnp.min(later, axis=-1)
    next_expert = jnp.where(next_expert == N_EXPERTS, -1, next_expert)
    tnext = next_expert[texp].astype(jnp.int32)
    tslot = (ordinal[texp] % 2).astype(jnp.int32)
    return slot, pad_start, pad_len, used.reshape(1), texp, tblk, tstate, tnext, tslot, n_tiles * tm


def kernel(x, c, rel_bias, w_ada, b_ada, g_pre_mix, g_post_mix, w_in, w_alpha, b_alpha, lam_q1, lam_k1, lam_q2,
           lam_k2, g_sub_a, g_norm_b, w_out, g_pre_ffn, g_post_ffn, w_router_g, b_router_g, w_router_e,
           b_router_e, w1, w3, w2):
    batch, seq, d = x.shape
    t = batch * seq
    depth = w_in.shape[0]
    tq = min(ATTN_TQ, seq)
    tm_e = EXPERT_TM
    xf = x.reshape(t, d)
    for i in range(depth):
        lam_init = 0.8 - 0.6 * math.exp(-0.3 * i)
        c_pad = jnp.pad(c, ((0, 8 - batch % 8 if batch % 8 else 0), (0, 0)))
        ada = _ada(c_pad, w_ada[i], b_ada[i][None, :])[:batch]
        sh_m, sc_m, gt_m, sh_f, sc_f, gt_f = [a[:, None, :] for a in jnp.split(ada, 6, axis=-1)]

        w_in_b = w_in[i].astype(BF16)
        w_z = jnp.pad(w_in_b[:, D_MAIN:], ((0, 0), (0, LANES - GATE_RANK)))
        proj, zb = _inproj(xf, g_pre_mix[i][None, :], sc_m, sh_m, w_in_b, w_z, seq)

        oa = _attention(proj, _bias_tiles(rel_bias, tq), lam_q1[i][None, :], lam_k1[i][None, :],
                        lam_q2[i][None, :], lam_k2[i][None, :], g_sub_a[i][:, None], batch, seq, lam_init, tq)
        w_alpha_pad = jnp.pad(w_alpha[i], ((0, LANES - GATE_RANK), (0, 0)))
        ob = _gla(proj, zb, w_alpha_pad, b_alpha[i][None, :], g_norm_b[i][None, :], batch, seq)

        w_router = jnp.pad(jnp.concatenate([w_router_g[i], w_router_e[i]], axis=1),
                           ((0, 0), (0, LANES - N_GROUPS - N_EXPERTS)))
        b_router = jnp.pad(jnp.concatenate([b_router_g[i], b_router_e[i]]),
                           (0, LANES - N_GROUPS - N_EXPERTS))[None, :]
        x1, h2_rows, logits = _outproj(oa, ob, w_out[i].astype(BF16), xf, gt_m, g_post_mix[i][None, :],
                                       g_pre_ffn[i][None, :], sc_f, sh_f, w_router, b_router, seq)

        rec, counts = _route(logits)
        (slot, pad_start, pad_len, used, texp, tblk, tstate, tnext, tslot,
         n_slots) = _dispatch_tables(rec, counts, t, tm_e)
        xs = _dispatch(slot, pad_start, pad_len, used, h2_rows, n_slots, tm_e)
        eo = _experts(texp, tblk, tstate, tnext, tslot, xs, w1[i], w3[i], w2[i], tm_e)
        xf = _final(slot, eo, rec, x1, gt_f, g_post_ffn[i][None, :], seq)
    return xf.reshape(batch, seq, d)
```

```python
import functools
import math

import jax
import jax.numpy as jnp
from jax import lax
from jax.experimental import pallas as pl
from jax.experimental.pallas import tpu as pltpu

F32 = jnp.float32
BF16 = jnp.bfloat16

D_MODEL = 2048
CHUNK = 64
A_HEADS = 8
A_DK = 64
A_DV = 2 * A_DK
A_WIDTH = A_HEADS * A_DV
B_HEADS = 4
B_WIDTH = D_MODEL - A_WIDTH
B_DV = B_WIDTH // B_HEADS
B_DK = B_DV // 2
GATE_RANK = 16
GATE_TAU = 16.0
N_BUCKETS = 32
MAX_DISTANCE = 256
N_GROUPS = 4
EXPERTS_PER_GROUP = 8
N_EXPERTS = N_GROUPS * EXPERTS_PER_GROUP
D_EXPERT = D_MODEL // 4
EPS = 1e-6
NEG_INF = -1e30
LOG2E = math.log2(math.e)

LANES = 128
U32 = jnp.uint32
HALF_D = D_MODEL // 2
ROW_SLABS = HALF_D // LANES
ROW_CHUNK = 16
CHUNK_UNROLL = 4
ONES_ROWS = 16
FAR_BIAS_DISTANCE = 166

ADA_TN = 1024
INPROJ_TM, INPROJ_TN = 512, 1024
ATTN_TQ, ATTN_TK = 512, 256
GLA_LC = 512
GLA_HEADS_PER_STEP = 4
OUTPROJ_TM = 512
OUTPROJ_PIECE = 128
ROUTE_TR = 512
EXPERT_TM = 512
DISPATCH_TD = 1024
COMBINE_TF = 256
D_MAIN = 3 * A_WIDTH + 2 * B_HEADS * B_DK + 2 * B_WIDTH
COL_QA, COL_KA, COL_VA = 0, A_HEADS, 2 * A_HEADS
COL_QB = 3 * A_HEADS
COL_KB = COL_QB + B_HEADS
COL_VB256 = (3 * A_WIDTH + 2 * B_HEADS * B_DK) // B_DV
COL_RB256 = COL_VB256 + B_HEADS
ROUTE_E1, ROUTE_E2, ROUTE_W1, ROUTE_W2, ROUTE_R1, ROUTE_R2 = 0, 1, 2, 3, 4, 5
ROUTER_EXPERT_LANE0 = N_GROUPS
ROUTER_ROWS = 48

VMEM_LIMIT = 56 * 1024 * 1024
INPROJ_VMEM_LIMIT = 60 * 1024 * 1024


def _params(*sem):
    return pltpu.CompilerParams(dimension_semantics=sem, vmem_limit_bytes=VMEM_LIMIT)


def _rms(v):
    return v * lax.rsqrt(jnp.mean(v * v, axis=-1, keepdims=True) + EPS)


def _silu(v):
    return v * jax.nn.sigmoid(v)


_HIGH_HALF = 0xFFFF0000


def _pack_bf16_pair(lo, hi):
    lo_bits = lax.bitcast_convert_type(lo.astype(BF16).astype(F32), U32) >> 16
    hi_bits = lax.bitcast_convert_type(hi.astype(BF16).astype(F32), U32) & U32(_HIGH_HALF)
    return hi_bits | lo_bits


def _unpack_bf16_pair(w):
    return (lax.bitcast_convert_type(w << 16, F32), lax.bitcast_convert_type(w & U32(_HIGH_HALF), F32))


def _store_row_slabs(ref, r0, nrows, rows_f32):
    packed = _pack_bf16_pair(rows_f32[:, :HALF_D], rows_f32[:, HALF_D:])
    for s in range(ROW_SLABS):
        ref[pl.ds(r0 * ROW_SLABS + s, nrows, stride=ROW_SLABS), :] = packed[:, s * LANES:(s + 1) * LANES]


def _load_row_slabs(ref, r0, nrows):
    slabs = [_unpack_bf16_pair(ref[pl.ds(r0 * ROW_SLABS + s, nrows, stride=ROW_SLABS), :]) for s in range(ROW_SLABS)]
    return (jnp.concatenate([lo for lo, _ in slabs], axis=1), jnp.concatenate([hi for _, hi in slabs], axis=1))


def _ada_kernel(c_ref, w_ref, b_ref, o_ref):
    s = _silu(c_ref[...])
    o_ref[...] = jnp.dot(s.astype(BF16), w_ref[...].astype(BF16), preferred_element_type=F32) + b_ref[...]


def _ada(c_pad, w, b, tn=ADA_TN):
    m, d = c_pad.shape
    n = w.shape[1]
    return pl.pallas_call(
        _ada_kernel,
        out_shape=jax.ShapeDtypeStruct((m, n), F32),
        grid=(n // tn,),
        in_specs=[pl.BlockSpec((m, d), lambda j: (0, 0)),
                  pl.BlockSpec((d, tn), lambda j: (0, j)),
                  pl.BlockSpec((1, tn), lambda j: (0, j))],
        out_specs=pl.BlockSpec((m, tn), lambda j: (0, j)),
        compiler_params=_params("arbitrary"),
        name="ada_proj",
    )(c_pad, w, b)


def _inproj_kernel(x_ref, xn_ref, g_ref, sc_ref, sh_ref, scn_ref, shn_ref, w_ref, wz_ref, o_ref, z_ref,
                   h_scr, hn_scr, *, tm, tn):
    def normed(src_ref, scale_ref, shift_ref, rows):
        h = _rms(src_ref[rows, :]) * g_ref[...]
        return (h * (1.0 + scale_ref[0]) + shift_ref[0]).astype(BF16)

    @pl.when(pl.program_id(0) == 0)
    def _():
        def chunk(c, carry):
            rows = pl.ds(pl.multiple_of(c * ROW_CHUNK, ROW_CHUNK), ROW_CHUNK)
            h_scr[rows, :] = normed(x_ref, sc_ref, sh_ref, rows)
            return carry
        lax.fori_loop(0, tm // ROW_CHUNK, chunk, 0, unroll=CHUNK_UNROLL)

    h = h_scr[...]
    z_ref[...] = jnp.dot(h, wz_ref[...], preferred_element_type=F32)
    n_col = D_MAIN // tn
    rows_per_col = tm // n_col // ROW_CHUNK * ROW_CHUNK
    next_row = 0
    for c in range(n_col):
        cols = slice(c * tn, (c + 1) * tn)
        o_ref[:, cols] = jnp.dot(h, w_ref[:, cols], preferred_element_type=F32).astype(BF16)
        stop = tm if c == n_col - 1 else next_row + rows_per_col
        for r0 in range(next_row, stop, ROW_CHUNK):
            rows = slice(r0, r0 + ROW_CHUNK)
            hn_scr[rows, :] = normed(xn_ref, scn_ref, shn_ref, rows)
        next_row = stop
    h_scr[...] = hn_scr[...]


def _inproj(x2d, g, sc, sh, w_all, w_z, seq, tm=INPROJ_TM, tn=INPROJ_TN):
    t, d = x2d.shape
    tm = min(tm, seq)
    per_b = seq // tm
    n_steps = t // tm
    kern = functools.partial(_inproj_kernel, tm=tm, tn=tn)
    nxt = lambda i: jnp.minimum(i + 1, n_steps - 1)
    per_batch = lambda step: pl.BlockSpec((1, 1, d), lambda i: (step(i) // per_b, 0, 0))
    return pl.pallas_call(
        kern,
        out_shape=(jax.ShapeDtypeStruct((t, D_MAIN), BF16), jax.ShapeDtypeStruct((t, LANES), F32)),
        grid=(n_steps,),
        in_specs=[pl.BlockSpec((tm, d), lambda i: (i, 0)),
                  pl.BlockSpec((tm, d), lambda i: (nxt(i), 0)),
                  pl.BlockSpec((1, d), lambda i: (0, 0)),
                  per_batch(lambda i: i), per_batch(lambda i: i), per_batch(nxt), per_batch(nxt),
                  pl.BlockSpec(w_all.shape, lambda i: (0, 0), pipeline_mode=pl.Buffered(1)),
                  pl.BlockSpec((d, LANES), lambda i: (0, 0), pipeline_mode=pl.Buffered(1))],
        out_specs=(pl.BlockSpec((tm, D_MAIN), lambda i: (i, 0)),
                   pl.BlockSpec((tm, LANES), lambda i: (i, 0))),
        scratch_shapes=[pltpu.VMEM((tm, d), BF16), pltpu.VMEM((tm, d), BF16)],
        compiler_params=pltpu.CompilerParams(dimension_semantics=("arbitrary",),
                                             vmem_limit_bytes=INPROJ_VMEM_LIMIT),
        name="in_proj",
    )(x2d, x2d, g, sc, sh, sc, sh, w_all, w_z)


def _t5_bucket(rel):
    nb = N_BUCKETS // 2
    max_exact = nb // 2
    base = jnp.where(rel > 0, nb, 0)
    n = jnp.abs(rel)
    nf = jnp.maximum(n, 1).astype(F32)
    large = max_exact + (jnp.log(nf / max_exact) / math.log(MAX_DISTANCE / max_exact)
                         * (nb - max_exact)).astype(jnp.int32)
    large = jnp.minimum(large, nb - 1)
    return base + jnp.where(n < max_exact, n, large)


def _bias_buckets(tq):
    kj = jnp.arange(tq, dtype=jnp.int32)[:, None]
    qi = jnp.arange(tq, dtype=jnp.int32)[None, :]
    near = _t5_bucket(kj - qi - tq)
    diag = jnp.where((kj // CHUNK) <= (qi // CHUNK), _t5_bucket(kj - qi), N_BUCKETS)
    return jnp.stack([near, diag]).astype(jnp.int32)


def _bias_kernel(rb_ref, bk_ref, o_ref):
    h = pl.program_id(0)
    far = rb_ref[N_BUCKETS // 2 - 1, h]
    bucket = bk_ref[...]
    acc = jnp.full(bucket.shape, NEG_INF, F32)
    for n in range(N_BUCKETS):
        acc = jnp.where(bucket == n, (rb_ref[n, h] - far) * LOG2E, acc)
    o_ref[...] = acc


def _bias_tiles(rel_bias, tq):
    return pl.pallas_call(
        _bias_kernel,
        out_shape=jax.ShapeDtypeStruct((A_HEADS, 2, tq, tq), F32),
        grid=(A_HEADS, 2),
        in_specs=[pl.BlockSpec(memory_space=pltpu.SMEM),
                  pl.BlockSpec((None, tq, tq), lambda h, d: (d, 0, 0))],
        out_specs=pl.BlockSpec((None, None, tq, tq), lambda h, d: (h, d, 0, 0)),
        compiler_params=_params("arbitrary", "arbitrary"),
        name="bias_tiles",
    )(rel_bias, _bias_buckets(tq))


def _attn_kernel(q_ref, k_ref, v_ref, bias_ref, lq1_ref, lk1_ref, lq2_ref, lk2_ref, g_ref, o_ref,
                 vt_scr, sa_scr, sb_scr, m_scr, acc_scr, *, tq, tk, lam_init):
    i = pl.program_id(2)
    nsub = tq // tk
    bufs = (sa_scr, sb_scr)

    @pl.when(i == 0)
    def _():
        ones = jnp.ones((ONES_ROWS, tk), BF16)
        for c in range(vt_scr.shape[0]):
            vt = v_ref[c * tk:(c + 1) * tk, :].astype(F32).T.astype(BF16)
            vt_scr[c] = jnp.concatenate([vt, ones], axis=0)

    lane = lax.broadcasted_iota(jnp.int32, (1, A_DV), 1)

    def tile_rows(tile):
        return pl.ds(pl.multiple_of(tile * tq, tq), tq)

    def two_map_queries(tile):
        q = q_ref[tile_rows(tile), :] * (A_DK ** -0.5 * LOG2E)
        zero = jnp.zeros_like(q)
        return jnp.concatenate([jnp.where(lane < A_DK, q, zero), jnp.where(lane >= A_DK, q, zero)], axis=0)

    q2 = two_map_queries(i)
    m_scr[...] = jnp.full(m_scr.shape, NEG_INF, F32)
    acc_scr[...] = jnp.zeros(acc_scr.shape, F32)

    def scores(j, queries=q2, q_lo=0):
        k = k_ref[pl.ds(pl.multiple_of(j * tk, tk), tk), :]
        if q_lo:
            queries = jnp.concatenate([queries[q_lo:tq], queries[tq + q_lo:]], axis=0)
        return lax.dot_general(k, queries, (((1,), (1,)), ((), ())), preferred_element_type=F32)

    def softmax_pv(s_ref, j, bias, q_lo=0):
        width = tq - q_lo
        halves = (slice(q_lo, tq), slice(tq + q_lo, 2 * tq))
        both = lambda ref: ref[...] if not q_lo else jnp.concatenate([ref[:, h] for h in halves], axis=1)
        s = s_ref[:, :2 * width]
        if bias is not None:
            s = jnp.concatenate([s[:, :width] + bias, s[:, width:] + bias], axis=1)
        m_old = both(m_scr)
        m_new = jnp.maximum(m_old, jnp.max(s, axis=0, keepdims=True))
        alpha = jnp.exp2(m_old - m_new)
        p = jnp.exp2(s - m_new).astype(BF16)
        acc = alpha * both(acc_scr) + jnp.dot(vt_scr[j], p, preferred_element_type=F32)
        if not q_lo:
            acc_scr[...] = acc
            m_scr[...] = m_new
        else:
            for n, h in enumerate(halves):
                acc_scr[:, h] = acc[:, n * width:(n + 1) * width]
                m_scr[:, h] = m_new[:, n * width:(n + 1) * width]

    n_far = jnp.maximum(i - 1, 0) * nsub

    @pl.when(i == 0)
    def _():
        sa_scr[...] = scores(0)

    def far_steps(j, count):
        for c in range(count):
            bufs[(c + 1) % 2][...] = scores(j + c + 1)
            softmax_pv(bufs[c % 2], j + c, None)

    def far_quad(jj, carry):
        far_steps(4 * jj, 4)
        return carry

    n_quads = n_far // 4
    lax.fori_loop(0, n_quads, far_quad, 0)

    @pl.when(n_far - 4 * n_quads >= 2)
    def _():
        far_steps(4 * n_quads, 2)

    def biased_steps(first_tile):
        j0 = (i - 1 + first_tile) * nsub
        count = (2 - first_tile) * nsub

        def geometry(c):
            d, r = first_tile + c // nsub, (c % nsub) * tk
            return d, r, (r if d == 1 else 0)

        for c in range(count):
            if c + 1 < count:
                q_next = geometry(c + 1)[2]
                bufs[(c + 1) % 2][:, :2 * (tq - q_next)] = scores(j0 + c + 1, q_lo=q_next)
            d, r, q_lo = geometry(c)
            no_bias = d == 0 and r + tk - 1 - tq <= -FAR_BIAS_DISTANCE
            bias = None if no_bias else bias_ref[d, r:r + tk, q_lo:]
            softmax_pv(bufs[c % 2], j0 + c, bias, q_lo)

    def finish():
        lam = (jnp.exp(jnp.sum(lq1_ref[...] * lk1_ref[...], axis=-1, keepdims=True))
               - jnp.exp(jnp.sum(lq2_ref[...] * lk2_ref[...], axis=-1, keepdims=True)) + lam_init)
        on = acc_scr[:A_DV, :] / acc_scr[A_DV:A_DV + 1, :]
        o = on[:, :tq] - lam * on[:, tq:]
        y = o * lax.rsqrt(jnp.mean(o * o, axis=0, keepdims=True) + EPS) * g_ref[...] * (1.0 - lam_init)
        o_ref[tile_rows(i), :] = y.T.astype(BF16)

    last = i + 1 == pl.num_programs(2)
    for first_tile, applies in ((0, i >= 1), (1, i == 0)):
        @pl.when(applies & jnp.logical_not(last))
        def _():
            biased_steps(first_tile)
            sa_scr[...] = scores(0, two_map_queries(i + 1))
            finish()

        @pl.when(applies & last)
        def _():
            biased_steps(first_tile)
            finish()


def _attention(proj, bias_tiles, lq1, lk1, lq2, lk2, g_sub_col, batch, seq, lam_init, tq, tk=ATTN_TK):
    t = proj.shape[0]
    nq = seq // tq
    assert (tq // tk) % 2 == 0 and tq % tk == 0, "the score pipeline alternates two buffers per query tile"
    assert tq + 1 >= FAR_BIAS_DISTANCE, "key tiles two or more before the query tile must be past the bias horizon"
    kern = functools.partial(_attn_kernel, tq=tq, tk=tk, lam_init=lam_init)
    vec = lambda n: pl.BlockSpec((1, n), lambda b, h, i: (0, 0))
    return pl.pallas_call(
        kern,
        out_shape=jax.ShapeDtypeStruct((t, A_WIDTH), BF16),
        grid=(batch, A_HEADS, nq),
        in_specs=[pl.BlockSpec((seq, A_DV), lambda b, h, i: (b, COL_QA + h)),
                  pl.BlockSpec((seq, A_DV), lambda b, h, i: (b, COL_KA + h)),
                  pl.BlockSpec((seq, A_DV), lambda b, h, i: (b, COL_VA + h)),
                  pl.BlockSpec((None, 2, tq, tq), lambda b, h, i: (h, 0, 0, 0)),
                  vec(A_DK), vec(A_DK), vec(A_DK), vec(A_DK),
                  pl.BlockSpec((A_DV, 1), lambda b, h, i: (0, 0))],
        out_specs=pl.BlockSpec((seq, A_DV), lambda b, h, i: (b, h)),
        scratch_shapes=[pltpu.VMEM((seq // tk, A_DV + ONES_ROWS, tk), BF16),
                        pltpu.VMEM((tk, 2 * tq), F32), pltpu.VMEM((tk, 2 * tq), F32),
                        pltpu.VMEM((1, 2 * tq), F32),
                        pltpu.VMEM((A_DV + ONES_ROWS, 2 * tq), F32)],
        compiler_params=_params("arbitrary", "arbitrary", "arbitrary"),
        name="diff_attention",
    )(proj, proj, proj, bias_tiles, lq1, lk1, lq2, lk2, g_sub_col)


def _split3(a):
    a1 = a.astype(BF16)
    r1 = a - a1.astype(F32)
    a2 = r1.astype(BF16)
    return a1, a2, (r1 - a2.astype(F32)).astype(BF16)


def _sum3(x):
    return x[:, :B_DK] + x[:, B_DK:2 * B_DK] + x[:, 2 * B_DK:]


def _gla_kernel(q_ref, k_ref, v_ref, r_ref, z_ref, wa_ref, ba_ref, g_ref, o_ref,
                state_scr, mask_scr, kv_scr, st_scr, *, n_chunks):
    lc = n_chunks * CHUNK

    @pl.when(pl.program_id(2) == 0)
    def _():
        state_scr[...] = jnp.zeros(state_scr.shape, F32)
        row = lax.broadcasted_iota(jnp.int32, (lc, lc), 0)
        col = lax.broadcasted_iota(jnp.int32, (lc, lc), 1)
        same = (row // CHUNK) == (col // CHUNK)
        mask_scr[...] = (same & (row >= col)).astype(BF16)

    z = z_ref[...]
    zh = z.astype(BF16)
    zl = (z - zh.astype(F32)).astype(BF16)
    z3 = jnp.concatenate([zh, zl, zh], axis=1)
    kcols = lambda hh: slice(hh * B_DK, (hh + 1) * B_DK)
    vcols = lambda hh: slice(hh * B_DV, (hh + 1) * B_DV)
    chunk_rows = lambda c: slice(c * CHUNK, (c + 1) * CHUNK)
    live = [dict() for _ in range(GLA_HEADS_PER_STEP)]

    def gate_stage(hh):
        pre = jnp.dot(z3, wa_ref[:, kcols(hh)], preferred_element_type=F32) + ba_ref[:, kcols(hh)]
        log_a = (jnp.minimum(pre, 0.0) - jnp.log1p(jnp.exp(-jnp.abs(pre)))) * (1.0 / GATE_TAU)
        live[hh]["parts"] = jnp.concatenate(_split3(log_a), axis=1)

    def decay_stage(hh):
        cum = _sum3(jnp.dot(mask_scr[...], live[hh].pop("parts"), preferred_element_type=F32))
        totals = [cum[(c + 1) * CHUNK - 1:(c + 1) * CHUNK, :] for c in range(n_chunks)]
        total = jnp.concatenate([jnp.broadcast_to(tc, (CHUNK, B_DK)) for tc in totals], axis=0)
        live[hh]["totals"] = totals
        live[hh]["k_dec"] = (k_ref[:, kcols(hh)].astype(F32) * jnp.exp(total - cum)).astype(BF16)

    def kv_stage(hh):
        k_dec = live[hh].pop("k_dec")
        for c in range(n_chunks):
            kv_scr[hh, c] = lax.dot_general(v_ref[chunk_rows(c), vcols(hh)], k_dec[chunk_rows(c)],
                                            (((0,), (0,)), ((), ())), preferred_element_type=F32)

    def state_stage(hh):
        totals = live[hh].pop("totals")
        state = state_scr[hh]
        for c in range(n_chunks):
            state = state * jnp.exp(totals[c]) + kv_scr[hh, c]
            st_scr[hh, c] = state.astype(BF16)
        state_scr[hh] = state

    def output_stage(hh):
        for c in range(n_chunks):
            rows = chunk_rows(c)
            o = lax.dot_general(q_ref[rows, kcols(hh)], st_scr[hh, c], (((1,), (1,)), ((), ())),
                                preferred_element_type=F32) * (B_DK ** -0.5)
            o_ref[rows, vcols(hh)] = (_rms(o) * g_ref[...]
                                      * _silu(r_ref[rows, vcols(hh)].astype(F32))).astype(BF16)

    stages = (gate_stage, decay_stage, kv_stage, state_stage, output_stage)
    for tick in range(GLA_HEADS_PER_STEP + len(stages) - 1):
        for k, stage in enumerate(stages):
            if 0 <= tick - k < GLA_HEADS_PER_STEP:
                stage(tick - k)


def _gla(proj, zb, w_alpha_pad, b_alpha, g_norm, batch, seq, lc=GLA_LC):
    t = proj.shape[0]
    lc = min(lc, seq)
    nl = seq // lc
    n_chunks = lc // CHUNK
    hps = GLA_HEADS_PER_STEP
    assert B_HEADS % hps == 0 and COL_QB % hps == 0 and COL_KB % hps == 0
    kern = functools.partial(_gla_kernel, n_chunks=n_chunks)
    wa_hi = w_alpha_pad.astype(BF16)
    wa_lo = (w_alpha_pad - wa_hi.astype(F32)).astype(BF16)
    wa3 = jnp.concatenate([wa_hi, wa_hi, wa_lo], axis=0)
    kblock = lambda col0: pl.BlockSpec((lc, hps * B_DK), lambda b, h, l: (b * nl + l, col0 // hps + h))
    vblock = lambda col0: pl.BlockSpec((lc, hps * B_DV), lambda b, h, l: (b * nl + l, col0 // hps + h))
    return pl.pallas_call(
        kern,
        out_shape=jax.ShapeDtypeStruct((t, B_WIDTH), BF16),
        grid=(batch, B_HEADS // hps, nl),
        in_specs=[kblock(COL_QB), kblock(COL_KB), vblock(COL_VB256), vblock(COL_RB256),
                  pl.BlockSpec((lc, LANES), lambda b, h, l: (b * nl + l, 0)),
                  pl.BlockSpec((3 * LANES, hps * B_DK), lambda b, h, l: (0, h)),
                  pl.BlockSpec((1, hps * B_DK), lambda b, h, l: (0, h)),
                  pl.BlockSpec((1, B_DV), lambda b, h, l: (0, 0))],
        out_specs=pl.BlockSpec((lc, hps * B_DV), lambda b, h, l: (b * nl + l, h)),
        scratch_shapes=[pltpu.VMEM((hps, B_DV, B_DK), F32),
                        pltpu.VMEM((lc, lc), BF16),
                        pltpu.VMEM((hps, n_chunks, B_DV, B_DK), F32),
                        pltpu.VMEM((hps, n_chunks, B_DV, B_DK), BF16)],
        compiler_params=_params("arbitrary", "arbitrary", "arbitrary"),
        name="gla",
    )(proj, proj, proj, proj, zb, wa3, b_alpha, g_norm)


def _outproj_kernel(oa_ref, ob_ref, wo_ref, x_ref, gt_ref, gpost_ref, gpre_ref, sc_ref, sh_ref, wr_ref,
                    br_ref, x1_ref, h2_ref, lg_ref, *, tm):
    n_pieces = tm // OUTPROJ_PIECE

    def project(p):
        prow = slice(p * OUTPROJ_PIECE, (p + 1) * OUTPROJ_PIECE)
        return (jnp.dot(oa_ref[prow, :], wo_ref[:A_WIDTH, :], preferred_element_type=F32)
                + jnp.dot(ob_ref[prow, :], wo_ref[A_WIDTH:, :], preferred_element_type=F32))

    y_next = project(0)
    for p in range(n_pieces):
        p0 = p * OUTPROJ_PIECE
        prow = slice(p0, p0 + OUTPROJ_PIECE)
        y = y_next
        if p + 1 < n_pieces:
            y_next = project(p + 1)
        his, los = [], []
        for c in range(OUTPROJ_PIECE // ROW_CHUNK):
            r0 = p0 + c * ROW_CHUNK
            rows = slice(r0, r0 + ROW_CHUNK)
            x1 = x_ref[rows, :] + gt_ref[0] * (_rms(y[c * ROW_CHUNK:(c + 1) * ROW_CHUNK]) * gpost_ref[...])
            x1_ref[rows, :] = x1
            h2 = (_rms(x1) * gpre_ref[...]) * (1.0 + sc_ref[0]) + sh_ref[0]
            hi = h2.astype(BF16)
            his.append(hi)
            los.append((h2 - hi.astype(F32)).astype(BF16))
            _store_row_slabs(h2_ref, r0, ROW_CHUNK, h2)
        hi, lo = jnp.concatenate(his, axis=0), jnp.concatenate(los, axis=0)
        hw = jnp.dot(hi, wr_ref[...], preferred_element_type=F32)
        lw = jnp.dot(lo, wr_ref[:, :LANES], preferred_element_type=F32)
        lg_ref[prow, :] = hw[:, :LANES] + hw[:, LANES:] + lw + br_ref[...]


def _outproj(oa, ob, w_out, x2d, gt, g_post, g_pre, sc, sh, w_router, b_router, seq, tm=OUTPROJ_TM):
    t, d = x2d.shape
    tm = min(tm, seq)
    per_b = seq // tm
    kern = functools.partial(_outproj_kernel, tm=tm)
    wr_hi = w_router.astype(BF16)
    wr_lo = (w_router - wr_hi.astype(F32)).astype(BF16)
    wr_cat = jnp.concatenate([wr_hi, wr_lo], axis=1)
    row = lambda: pl.BlockSpec((1, d), lambda i: (0, 0))
    per_batch = lambda: pl.BlockSpec((1, 1, d), lambda i: (i // per_b, 0, 0))
    return pl.pallas_call(
        kern,
        out_shape=(jax.ShapeDtypeStruct((t, d), F32),
                   jax.ShapeDtypeStruct((t * ROW_SLABS, LANES), U32),
                   jax.ShapeDtypeStruct((t, LANES), F32)),
        grid=(t // tm,),
        in_specs=[pl.BlockSpec((tm, A_WIDTH), lambda i: (i, 0)),
                  pl.BlockSpec((tm, B_WIDTH), lambda i: (i, 0)),
                  pl.BlockSpec((d, d), lambda i: (0, 0)),
                  pl.BlockSpec((tm, d), lambda i: (i, 0)),
                  per_batch(), row(), row(), per_batch(), per_batch(),
                  pl.BlockSpec((d, 2 * LANES), lambda i: (0, 0)),
                  pl.BlockSpec((1, LANES), lambda i: (0, 0))],
        out_specs=(pl.BlockSpec((tm, d), lambda i: (i, 0)),
                   pl.BlockSpec((tm * ROW_SLABS, LANES), lambda i: (i, 0)),
                   pl.BlockSpec((tm, LANES), lambda i: (i, 0))),
        compiler_params=_params("arbitrary"),
        name="out_proj",
    )(oa, ob, w_out, x2d, gt, g_post, g_pre, sc, sh, wr_cat, b_router)


def _route_kernel(lg_ref, rec_ref, cnt_ref, carry_scr, before_scr, *, tr):
    @pl.when(pl.program_id(0) == 0)
    def _():
        carry_scr[...] = jnp.zeros(carry_scr.shape, F32)
        earlier = lax.broadcasted_iota(jnp.int32, (tr, tr), 0)
        token = lax.broadcasted_iota(jnp.int32, (tr, tr), 1)
        before_scr[...] = (earlier < token).astype(BF16)

    lg = lg_ref[...].T[:ROUTER_ROWS]
    row = lax.broadcasted_iota(jnp.int32, lg.shape, 0)
    big = jnp.int32(ROUTER_ROWS)

    def first_row(mask):
        return jnp.min(jnp.where(mask, row, big), axis=0, keepdims=True)

    gmask = row < N_GROUPS
    gmax = jnp.max(jnp.where(gmask, lg, -jnp.inf), axis=0, keepdims=True)
    gexp = jnp.where(gmask, jnp.exp(lg - gmax), 0.0)
    gprob = gexp / jnp.sum(gexp, axis=0, keepdims=True)
    g_val = jnp.max(gprob, axis=0, keepdims=True)
    g_idx = first_row(gmask & (gprob == g_val))

    lo = ROUTER_EXPERT_LANE0 + g_idx * EXPERTS_PER_GROUP
    emask = (row >= lo) & (row < lo + EXPERTS_PER_GROUP)
    emax = jnp.max(jnp.where(emask, lg, -jnp.inf), axis=0, keepdims=True)
    eexp = jnp.where(emask, jnp.exp(lg - emax), 0.0)
    eprob = eexp / jnp.sum(eexp, axis=0, keepdims=True)
    v1 = jnp.max(eprob, axis=0, keepdims=True)
    i1 = first_row(emask & (eprob == v1))
    rest = emask & (row != i1)
    v2 = jnp.max(jnp.where(rest, eprob, -1.0), axis=0, keepdims=True)
    i2 = first_row(rest & (eprob == v2))
    w1 = g_val * (v1 / (v1 + v2))
    w2 = g_val * (v2 / (v1 + v2))

    hit1 = row == i1
    hit2 = row == i2
    onehot = (hit1 | hit2).astype(BF16)
    pos = carry_scr[...] + jnp.dot(onehot, before_scr[...], preferred_element_type=F32)
    rank1 = jnp.sum(jnp.where(hit1, pos, 0.0), axis=0, keepdims=True)
    rank2 = jnp.sum(jnp.where(hit2, pos, 0.0), axis=0, keepdims=True)
    carry_scr[...] = carry_scr[...] + jnp.sum(onehot.astype(F32), axis=1, keepdims=True)
    cnt_ref[...] = carry_scr[...]

    e1 = (i1 - ROUTER_EXPERT_LANE0).astype(F32)
    e2 = (i2 - ROUTER_EXPERT_LANE0).astype(F32)
    field = lax.broadcasted_iota(jnp.int32, (LANES, tr), 0)
    rec = jnp.zeros((LANES, tr), F32)
    for ln, val in ((ROUTE_E1, e1), (ROUTE_E2, e2), (ROUTE_W1, w1), (ROUTE_W2, w2),
                    (ROUTE_R1, rank1), (ROUTE_R2, rank2)):
        rec = jnp.where(field == ln, val, rec)
    rec_ref[...] = rec.T


def _route(logits, tr=ROUTE_TR):
    t = logits.shape[0]
    tr = min(tr, t)
    kern = functools.partial(_route_kernel, tr=tr)
    return pl.pallas_call(
        kern,
        out_shape=(jax.ShapeDtypeStruct((t, LANES), F32), jax.ShapeDtypeStruct((ROUTER_ROWS, 1), F32)),
        grid=(t // tr,),
        in_specs=[pl.BlockSpec((tr, LANES), lambda i: (i, 0))],
        out_specs=(pl.BlockSpec((tr, LANES), lambda i: (i, 0)),
                   pl.BlockSpec((ROUTER_ROWS, 1), lambda i: (0, 0))),
        scratch_shapes=[pltpu.VMEM((ROUTER_ROWS, 1), F32), pltpu.VMEM((tr, tr), BF16)],
        compiler_params=_params("arbitrary"),
        name="route",
    )(logits)


def _slab_rows(ref, row):
    return ref.at[pl.ds(pl.multiple_of(row * ROW_SLABS, ROW_SLABS), ROW_SLABS), :]


def _dispatch_kernel(slot_ref, pad_start_ref, pad_len_ref, used_ref, h2_ref, xs_hbm, zero_scr, sems, pad_sems,
                     *, td, tm):
    g = pl.program_id(0)
    tile_rows = tm * ROW_SLABS
    n_tiles = xs_hbm.shape[0] // tile_rows

    def zero_copy(slot, nslots, sem):
        rows = pl.ds(pl.multiple_of(slot * ROW_SLABS, ROW_SLABS), nslots * ROW_SLABS)
        return pltpu.make_async_copy(zero_scr.at[pl.ds(0, nslots * ROW_SLABS), :], xs_hbm.at[rows, :], sem)

    pad_sizes = [1 << b for b in reversed(range((tm - 1).bit_length()))]

    @pl.when(g == 0)
    def _():
        zero_scr[...] = jnp.zeros(zero_scr.shape, U32)

        def unused_tile(tile, carry):
            zero_copy(tile * tm, tm, sems.at[1]).start()
            zero_copy(tile * tm, tm, sems.at[1]).wait()
            return carry

        lax.fori_loop(used_ref[0], n_tiles, unused_tile, 0)

        def per_expert(e, counts):
            off = pad_start_ref[e]
            n = pad_len_ref[e]
            new_counts = []
            for b, size in enumerate(pad_sizes):
                hit = (n & size) != 0

                @pl.when(hit)
                def _():
                    zero_copy(off, size, pad_sems.at[b]).start()

                off = off + jnp.where(hit, size, 0)
                new_counts.append(counts[b] + hit.astype(jnp.int32))
            return tuple(new_counts)

        counts = lax.fori_loop(0, N_EXPERTS, per_expert, tuple(jnp.int32(0) for _ in pad_sizes))
        for b, size in enumerate(pad_sizes):
            def drain(r, c):
                zero_copy(0, size, pad_sems.at[b]).wait()
                return c
            lax.fori_loop(0, counts[b], drain, 0)

    def row_copy(r, slot):
        return pltpu.make_async_copy(_slab_rows(h2_ref, r), _slab_rows(xs_hbm, slot), sems.at[0])

    base = g * td

    def issue(r, c):
        tok = base + r
        row_copy(r, slot_ref[2 * tok]).start()
        row_copy(r, slot_ref[2 * tok + 1]).start()
        return c

    lax.fori_loop(0, td, issue, 0, unroll=CHUNK_UNROLL)
    for _ in range(2):
        pltpu.make_async_copy(h2_ref, xs_hbm.at[pl.ds(0, td * ROW_SLABS), :], sems.at[0]).wait()


def _dispatch(slot, pad_start, pad_len, used, h2_rows, n_slots, tm, td=DISPATCH_TD):
    t = slot.shape[0] // 2
    td = min(td, t)
    kern = functools.partial(_dispatch_kernel, td=td, tm=tm)
    grid_spec = pltpu.PrefetchScalarGridSpec(
        num_scalar_prefetch=4,
        grid=(t // td,),
        in_specs=[pl.BlockSpec((td * ROW_SLABS, LANES), lambda g, sl, ps, pn, us: (g, 0))],
        out_specs=pl.BlockSpec(memory_space=pl.ANY),
        scratch_shapes=[pltpu.VMEM((tm * ROW_SLABS, LANES), U32), pltpu.SemaphoreType.DMA((2,)),
                        pltpu.SemaphoreType.DMA(((tm - 1).bit_length(),))],
    )
    return pl.pallas_call(
        kern,
        out_shape=jax.ShapeDtypeStruct((n_slots * ROW_SLABS, LANES), U32),
        grid_spec=grid_spec,
        compiler_params=_params("arbitrary"),
        name="dispatch",
    )(slot, pad_start, pad_len, used, h2_rows)


TILE_UNUSED, TILE_USED, TILE_NEW_EXPERT = 0, 1, 2


def _expert_kernel(texp_ref, tblk_ref, tstate_ref, tnext_ref, tpar_ref, xs_ref, w1_hbm, w3_hbm, w2_hbm, eo_ref,
                   x_scr, w1_scr, w3_scr, w2_scr, w1_stage, w3_stage, w2_stage, sems, *, tm):
    i = pl.program_id(0)
    state = tstate_ref[i]
    slot = tpar_ref[i]

    def weight_copies(expert, dst_slot):
        return [pltpu.make_async_copy(hbm.at[expert], stage.at[dst_slot], sems.at[dst_slot])
                for hbm, stage in ((w1_hbm, w1_stage), (w3_hbm, w3_stage), (w2_hbm, w2_stage))]

    @pl.when(state == TILE_UNUSED)
    def _():
        eo_ref[...] = jnp.zeros(eo_ref.shape, U32)

    @pl.when(i == 0)
    def _():
        for cp in weight_copies(texp_ref[0], slot):
            cp.start()

    @pl.when(state == TILE_NEW_EXPERT)
    def _():
        for cp in weight_copies(texp_ref[i], slot):
            cp.wait()

        @pl.when(tnext_ref[i] >= 0)
        def _():
            for cp in weight_copies(tnext_ref[i], 1 - slot):
                cp.start()

        w1_scr[...] = w1_stage[slot].astype(BF16)
        w3_scr[...] = w3_stage[slot].astype(BF16)
        w2_scr[...] = w2_stage[slot].astype(BF16)

    @pl.when(state != TILE_UNUSED)
    def _():
        for s in range(ROW_SLABS):
            lo, hi = _unpack_bf16_pair(xs_ref[pl.ds(s, tm, stride=ROW_SLABS), :])
            x_scr[:, s * LANES:(s + 1) * LANES] = lo.astype(BF16)
            x_scr[:, HALF_D + s * LANES:HALF_D + (s + 1) * LANES] = hi.astype(BF16)
        x = x_scr[...]
        a = jnp.dot(x, w1_scr[...], preferred_element_type=F32)
        b = jnp.dot(x, w3_scr[...], preferred_element_type=F32)
        hid = (_silu(a) * b).astype(BF16)
        y = jnp.dot(hid, w2_scr[...], preferred_element_type=F32)
        _store_row_slabs(eo_ref, 0, tm, y)


def _experts(tile_expert, tile_block, tile_state, tile_next, tile_slot, xs_rows, w1, w3, w2, tm):
    n_tiles = tile_expert.shape[0]
    d, f = w1.shape[1], w1.shape[2]
    kern = functools.partial(_expert_kernel, tm=tm)
    grid_spec = pltpu.PrefetchScalarGridSpec(
        num_scalar_prefetch=5,
        grid=(n_tiles,),
        in_specs=[pl.BlockSpec((tm * ROW_SLABS, LANES), lambda i, te, tb, ts, tn, tp: (tb[i], 0)),
                  pl.BlockSpec(memory_space=pl.ANY), pl.BlockSpec(memory_space=pl.ANY),
                  pl.BlockSpec(memory_space=pl.ANY)],
        out_specs=pl.BlockSpec((tm * ROW_SLABS, LANES), lambda i, te, tb, ts, tn, tp: (i, 0)),
        scratch_shapes=[pltpu.VMEM((tm, d), BF16), pltpu.VMEM((d, f), BF16), pltpu.VMEM((d, f), BF16),
                        pltpu.VMEM((f, d), BF16),
                        pltpu.VMEM((2, d, f), F32), pltpu.VMEM((2, d, f), F32), pltpu.VMEM((2, f, d), F32),
                        pltpu.SemaphoreType.DMA((2,))],
    )
    return pl.pallas_call(
        kern,
        out_shape=jax.ShapeDtypeStruct(xs_rows.shape, U32),
        grid_spec=grid_spec,
        compiler_params=_params("arbitrary"),
        name="expert_mlp",
    )(tile_expert, tile_block, tile_state, tile_next, tile_slot, xs_rows, w1, w3, w2)


def _final_kernel(slot_ref, eo_hbm, rec_ref, x1_ref, gt_ref, g_ref, o_ref, e_scr, sems, *, tf):
    i = pl.program_id(0)
    par = i % 2

    def start_all(step, buf):
        def body(r, c):
            tok = step * tf + r
            for k in range(2):
                pltpu.make_async_copy(_slab_rows(eo_hbm, slot_ref[2 * tok + k]),
                                      _slab_rows(e_scr.at[buf, k], r), sems.at[buf]).start()
            return c
        lax.fori_loop(0, tf, body, 0, unroll=CHUNK_UNROLL)

    def wait_all(buf):
        for k in range(2):
            pltpu.make_async_copy(eo_hbm.at[pl.ds(0, tf * ROW_SLABS), :], e_scr.at[buf, k], sems.at[buf]).wait()

    @pl.when(i == 0)
    def _():
        start_all(0, 0)

    @pl.when(i + 1 < pl.num_programs(0))
    def _():
        start_all(i + 1, 1 - par)

    wait_all(par)

    def chunk(c, carry):
        r0 = pl.multiple_of(c * ROW_CHUNK, ROW_CHUNK)
        rows = pl.ds(r0, ROW_CHUNK)
        rec = rec_ref[rows, :]
        w1 = rec[:, ROUTE_W1:ROUTE_W1 + 1]
        w2 = rec[:, ROUTE_W2:ROUTE_W2 + 1]
        lo1, hi1 = _load_row_slabs(e_scr.at[par, 0], r0, ROW_CHUNK)
        lo2, hi2 = _load_row_slabs(e_scr.at[par, 1], r0, ROW_CHUNK)
        y = jnp.concatenate([w1 * lo1 + w2 * lo2, w1 * hi1 + w2 * hi2], axis=1)
        o_ref[rows, :] = x1_ref[rows, :] + gt_ref[0] * (_rms(y) * g_ref[...])
        return carry

    lax.fori_loop(0, tf // ROW_CHUNK, chunk, 0, unroll=CHUNK_UNROLL)


def _final(slot, eo_rows, rec, x1, gt, g_post, seq, tf=COMBINE_TF):
    t, d = x1.shape
    tf = min(tf, seq)
    per_b = seq // tf
    kern = functools.partial(_final_kernel, tf=tf)
    grid_spec = pltpu.PrefetchScalarGridSpec(
        num_scalar_prefetch=1,
        grid=(t // tf,),
        in_specs=[pl.BlockSpec(memory_space=pl.ANY),
                  pl.BlockSpec((tf, LANES), lambda i, sl: (i, 0)),
                  pl.BlockSpec((tf, d), lambda i, sl: (i, 0)),
                  pl.BlockSpec((1, 1, d), lambda i, sl: (i // per_b, 0, 0)),
                  pl.BlockSpec((1, d), lambda i, sl: (0, 0))],
        out_specs=pl.BlockSpec((tf, d), lambda i, sl: (i, 0)),
        scratch_shapes=[pltpu.VMEM((2, 2, tf * ROW_SLABS, LANES), U32),
                        pltpu.SemaphoreType.DMA((2,))],
    )
    return pl.pallas_call(
        kern,
        out_shape=jax.ShapeDtypeStruct((t, d), F32),
        grid_spec=grid_spec,
        compiler_params=_params("arbitrary"),
        name="combine_final",
    )(slot, eo_rows, rec, x1, gt, g_post)


def _dispatch_tables(rec, counts, t, tm):
    e = rec[:, ROUTE_E1:ROUTE_E2 + 1].astype(jnp.int32)
    rank = rec[:, ROUTE_R1:ROUTE_R2 + 1].astype(jnp.int32)
    cnt = counts[ROUTER_EXPERT_LANE0:ROUTER_EXPERT_LANE0 + N_EXPERTS, 0].astype(jnp.int32)
    tiles_per = (cnt + tm - 1) // tm
    tile_end = jnp.cumsum(tiles_per)
    tile_start = tile_end - tiles_per
    n_tiles = (2 * t + N_EXPERTS * (tm - 1)) // tm
    experts = jnp.arange(N_EXPERTS, dtype=jnp.int32)
    start_of = jnp.sum(jnp.where(e[..., None] == experts, tile_start, 0), axis=-1)
    slot = (start_of * tm + rank).reshape(-1)
    pad_start = tile_start * tm + cnt
    pad_len = tiles_per * tm - cnt
    tile_id = jnp.arange(n_tiles, dtype=jnp.int32)
    used = tile_end[-1]
    tblk = jnp.minimum(tile_id, used - 1)
    texp = jnp.sum(tile_end[None, :] <= tblk[:, None], axis=-1).astype(jnp.int32)
    tstate = jnp.where(tile_id < used, jnp.where(tile_id == tile_start[texp], TILE_NEW_EXPERT, TILE_USED),
                       TILE_UNUSED).astype(jnp.int32)
    nonempty = cnt > 0
    ordinal = jnp.cumsum(nonempty.astype(jnp.int32)) - 1
    later = jnp.where(nonempty[None, :] & (experts[None, :] > experts[:, None]), experts[None, :], N_EXPERTS)
    next_expert = jnp.min(later, axis=-1)
    next_expert = jnp.where(next_expert == N_EXPERTS, -1, next_expert)
    tnext = next_expert[texp].astype(jnp.int32)
    tslot = (ordinal[texp] % 2).astype(jnp.int32)
    return slot, pad_start, pad_len, used.reshape(1), texp, tblk, tstate, tnext, tslot, n_tiles * tm


def kernel(x, c, rel_bias, w_ada, b_ada, g_pre_mix, g_post_mix, w_in, w_alpha, b_alpha, lam_q1, lam_k1, lam_q2,
           lam_k2, g_sub_a, g_norm_b, w_out, g_pre_ffn, g_post_ffn, w_router_g, b_router_g, w_router_e,
           b_router_e, w1, w3, w2):
    batch, seq, d = x.shape
    t = batch * seq
    depth = w_in.shape[0]
    tq = min(ATTN_TQ, seq)
    tm_e = EXPERT_TM
    xf = x.reshape(t, d)
    for i in range(depth):
        lam_init = 0.8 - 0.6 * math.exp(-0.3 * i)
        c_pad = jnp.pad(c, ((0, 8 - batch % 8 if batch % 8 else 0), (0, 0)))
        ada = _ada(c_pad, w_ada[i], b_ada[i][None, :])[:batch]
        sh_m, sc_m, gt_m, sh_f, sc_f, gt_f = [a[:, None, :] for a in jnp.split(ada, 6, axis=-1)]

        w_in_b = w_in[i].astype(BF16)
        w_z = jnp.pad(w_in_b[:, D_MAIN:], ((0, 0), (0, LANES - GATE_RANK)))
        proj, zb = _inproj(xf, g_pre_mix[i][None, :], sc_m, sh_m, w_in_b, w_z, seq)

        oa = _attention(proj, _bias_tiles(rel_bias, tq), lam_q1[i][None, :], lam_k1[i][None, :],
                        lam_q2[i][None, :], lam_k2[i][None, :], g_sub_a[i][:, None], batch, seq, lam_init, tq)
        w_alpha_pad = jnp.pad(w_alpha[i], ((0, LANES - GATE_RANK), (0, 0)))
        ob = _gla(proj, zb, w_alpha_pad, b_alpha[i][None, :], g_norm_b[i][None, :], batch, seq)

        w_router = jnp.pad(jnp.concatenate([w_router_g[i], w_router_e[i]], axis=1),
                           ((0, 0), (0, LANES - N_GROUPS - N_EXPERTS)))
        b_router = jnp.pad(jnp.concatenate([b_router_g[i], b_router_e[i]]),
                           (0, LANES - N_GROUPS - N_EXPERTS))[None, :]
        x1, h2_rows, logits = _outproj(oa, ob, w_out[i].astype(BF16), xf, gt_m, g_post_mix[i][None, :],
                                       g_pre_ffn[i][None, :], sc_f, sh_f, w_router, b_router, seq)

        rec, counts = _route(logits)
        (slot, pad_start, pad_len, used, texp, tblk, tstate, tnext, tslot,
         n_slots) = _dispatch_tables(rec, counts, t, tm_e)
        xs = _dispatch(slot, pad_start, pad_len, used, h2_rows, n_slots, tm_e)
        eo = _experts(texp, tblk, tstate, tnext, tslot, xs, w1[i], w3[i], w2[i], tm_e)
        xf = _final(slot, eo, rec, x1, gt_f, g_post_ffn[i][None, :], seq)
    return xf.reshape(batch, seq, d)
```

```python
import functools
import math

import jax
import jax.numpy as jnp
from jax import lax
from jax.experimental import pallas as pl
from jax.experimental.pallas import tpu as pltpu

F32 = jnp.float32
BF16 = jnp.bfloat16

D_MODEL = 2048
CHUNK = 64
A_HEADS = 8
A_DK = 64
A_DV = 2 * A_DK
A_WIDTH = A_HEADS * A_DV
B_HEADS = 4
B_WIDTH = D_MODEL - A_WIDTH
B_DV = B_WIDTH // B_HEADS
B_DK = B_DV // 2
GATE_RANK = 16
GATE_TAU = 16.0
N_BUCKETS = 32
MAX_DISTANCE = 256
N_GROUPS = 4
EXPERTS_PER_GROUP = 8
N_EXPERTS = N_GROUPS * EXPERTS_PER_GROUP
EPS = 1e-6
NEG_INF = -1e30
LOG2E = math.log2(math.e)

LANES = 128
U32 = jnp.uint32
HALF_D = D_MODEL // 2
ROW_SLABS = HALF_D // LANES
ROW_CHUNK = 16
CHUNK_UNROLL = 4
ONES_ROWS = 16
FAR_BIAS_DISTANCE = 166

ADA_TN = 1024
INPROJ_TM, INPROJ_TN = 512, 1024
ATTN_TQ, ATTN_TK = 512, 256
GLA_LC = 512
GLA_HEADS_PER_STEP = 4
OUTPROJ_TM = 512
OUTPROJ_PIECE = 128
ROUTE_TR = 512
EXPERT_TM = 512
DISPATCH_TD = 1024
COMBINE_TF = 256
D_MAIN = 3 * A_WIDTH + 2 * B_HEADS * B_DK + 2 * B_WIDTH
COL_QA, COL_KA, COL_VA = 0, A_HEADS, 2 * A_HEADS
COL_QB = 3 * A_HEADS
COL_KB = COL_QB + B_HEADS
COL_VB256 = (3 * A_WIDTH + 2 * B_HEADS * B_DK) // B_DV
COL_RB256 = COL_VB256 + B_HEADS
ROUTE_E1, ROUTE_E2, ROUTE_W1, ROUTE_W2, ROUTE_R1, ROUTE_R2 = 0, 1, 2, 3, 4, 5
ROUTE_FIELD_ROWS = 8
ROUTER_EXPERT_LANE0 = N_GROUPS
ROUTER_ROWS = 48

VMEM_LIMIT = 56 * 1024 * 1024
INPROJ_VMEM_LIMIT = 60 * 1024 * 1024


def _params(*sem):
    return pltpu.CompilerParams(dimension_semantics=sem, vmem_limit_bytes=VMEM_LIMIT)


def _rms(v):
    return v * lax.rsqrt(jnp.mean(v * v, axis=-1, keepdims=True) + EPS)


def _silu(v):
    return v * jax.nn.sigmoid(v)


_HIGH_HALF = 0xFFFF0000


def _pack_bf16_pair(lo, hi):
    lo_bits = lax.bitcast_convert_type(lo.astype(BF16).astype(F32), U32) >> 16
    hi_bits = lax.bitcast_convert_type(hi.astype(BF16).astype(F32), U32) & U32(_HIGH_HALF)
    return hi_bits | lo_bits


def _unpack_bf16_pair(w):
    return (lax.bitcast_convert_type(w << 16, F32), lax.bitcast_convert_type(w & U32(_HIGH_HALF), F32))


def _store_row_slabs(ref, r0, nrows, rows_f32):
    packed = _pack_bf16_pair(rows_f32[:, :HALF_D], rows_f32[:, HALF_D:])
    for s in range(ROW_SLABS):
        ref[pl.ds(r0 * ROW_SLABS + s, nrows, stride=ROW_SLABS), :] = packed[:, s * LANES:(s + 1) * LANES]


def _load_row_slabs(ref, r0, nrows):
    slabs = [_unpack_bf16_pair(ref[pl.ds(r0 * ROW_SLABS + s, nrows, stride=ROW_SLABS), :]) for s in range(ROW_SLABS)]
    return (jnp.concatenate([lo for lo, _ in slabs], axis=1), jnp.concatenate([hi for _, hi in slabs], axis=1))


def _ada_kernel(c_ref, w_ref, b_ref, o_ref):
    s = _silu(c_ref[...])
    o_ref[...] = jnp.dot(s.astype(BF16), w_ref[...].astype(BF16), preferred_element_type=F32) + b_ref[...]


def _ada(c_pad, w, b, tn=ADA_TN):
    m, d = c_pad.shape
    n = w.shape[1]
    return pl.pallas_call(
        _ada_kernel,
        out_shape=jax.ShapeDtypeStruct((m, n), F32),
        grid=(n // tn,),
        in_specs=[pl.BlockSpec((m, d), lambda j: (0, 0)),
                  pl.BlockSpec((d, tn), lambda j: (0, j)),
                  pl.BlockSpec((1, tn), lambda j: (0, j))],
        out_specs=pl.BlockSpec((m, tn), lambda j: (0, j)),
        compiler_params=_params("arbitrary"),
        name="ada_proj",
    )(c_pad, w, b)


def _inproj_kernel(x_ref, xn_ref, g_ref, sc_ref, sh_ref, scn_ref, shn_ref, w_ref, wz_ref, o_ref, z_ref,
                   h_scr, hn_scr, *, tm, tn):
    def normed(src_ref, scale_ref, shift_ref, rows):
        h = _rms(src_ref[rows, :]) * g_ref[...]
        return (h * (1.0 + scale_ref[0]) + shift_ref[0]).astype(BF16)

    @pl.when(pl.program_id(0) == 0)
    def _():
        def chunk(c, carry):
            rows = pl.ds(pl.multiple_of(c * ROW_CHUNK, ROW_CHUNK), ROW_CHUNK)
            h_scr[rows, :] = normed(x_ref, sc_ref, sh_ref, rows)
            return carry
        lax.fori_loop(0, tm // ROW_CHUNK, chunk, 0, unroll=CHUNK_UNROLL)

    h = h_scr[...]
    z_ref[...] = jnp.dot(h, wz_ref[...], preferred_element_type=F32)
    n_col = D_MAIN // tn
    rows_per_col = tm // n_col // ROW_CHUNK * ROW_CHUNK
    next_row = 0
    for c in range(n_col):
        cols = slice(c * tn, (c + 1) * tn)
        o_ref[:, cols] = jnp.dot(h, w_ref[:, cols], preferred_element_type=F32).astype(BF16)
        stop = tm if c == n_col - 1 else next_row + rows_per_col
        for r0 in range(next_row, stop, ROW_CHUNK):
            rows = slice(r0, r0 + ROW_CHUNK)
            hn_scr[rows, :] = normed(xn_ref, scn_ref, shn_ref, rows)
        next_row = stop
    h_scr[...] = hn_scr[...]


def _inproj(x2d, g, sc, sh, w_all, w_z, seq, tm=INPROJ_TM, tn=INPROJ_TN):
    t, d = x2d.shape
    tm = min(tm, seq)
    per_b = seq // tm
    n_steps = t // tm
    kern = functools.partial(_inproj_kernel, tm=tm, tn=tn)
    nxt = lambda i: jnp.minimum(i + 1, n_steps - 1)
    per_batch = lambda step: pl.BlockSpec((1, 1, d), lambda i: (step(i) // per_b, 0, 0))
    return pl.pallas_call(
        kern,
        out_shape=(jax.ShapeDtypeStruct((t, D_MAIN), BF16), jax.ShapeDtypeStruct((t, LANES), F32)),
        grid=(n_steps,),
        in_specs=[pl.BlockSpec((tm, d), lambda i: (i, 0)),
                  pl.BlockSpec((tm, d), lambda i: (nxt(i), 0)),
                  pl.BlockSpec((1, d), lambda i: (0, 0)),
                  per_batch(lambda i: i), per_batch(lambda i: i), per_batch(nxt), per_batch(nxt),
                  pl.BlockSpec(w_all.shape, lambda i: (0, 0), pipeline_mode=pl.Buffered(1)),
                  pl.BlockSpec((d, LANES), lambda i: (0, 0), pipeline_mode=pl.Buffered(1))],
        out_specs=(pl.BlockSpec((tm, D_MAIN), lambda i: (i, 0)),
                   pl.BlockSpec((tm, LANES), lambda i: (i, 0))),
        scratch_shapes=[pltpu.VMEM((tm, d), BF16), pltpu.VMEM((tm, d), BF16)],
        compiler_params=pltpu.CompilerParams(dimension_semantics=("arbitrary",),
                                             vmem_limit_bytes=INPROJ_VMEM_LIMIT),
        name="in_proj",
    )(x2d, x2d, g, sc, sh, sc, sh, w_all, w_z)


def _t5_bucket(rel):
    nb = N_BUCKETS // 2
    max_exact = nb // 2
    base = jnp.where(rel > 0, nb, 0)
    n = jnp.abs(rel)
    nf = jnp.maximum(n, 1).astype(F32)
    large = max_exact + (jnp.log(nf / max_exact) / math.log(MAX_DISTANCE / max_exact)
                         * (nb - max_exact)).astype(jnp.int32)
    large = jnp.minimum(large, nb - 1)
    return base + jnp.where(n < max_exact, n, large)


def _bias_buckets(tq):
    kj = jnp.arange(tq, dtype=jnp.int32)[:, None]
    qi = jnp.arange(tq, dtype=jnp.int32)[None, :]
    near = _t5_bucket(kj - qi - tq)
    diag = jnp.where((kj // CHUNK) <= (qi // CHUNK), _t5_bucket(kj - qi), N_BUCKETS)
    return jnp.stack([near, diag]).astype(jnp.int32)


def _bias_kernel(rb_ref, bk_ref, o_ref):
    h = pl.program_id(0)
    far = rb_ref[N_BUCKETS // 2 - 1, h]
    bucket = bk_ref[...]
    acc = jnp.full(bucket.shape, NEG_INF, F32)
    for n in range(N_BUCKETS):
        acc = jnp.where(bucket == n, (rb_ref[n, h] - far) * LOG2E, acc)
    o_ref[...] = acc


def _bias_tiles(rel_bias, tq):
    return pl.pallas_call(
        _bias_kernel,
        out_shape=jax.ShapeDtypeStruct((A_HEADS, 2, tq, tq), F32),
        grid=(A_HEADS, 2),
        in_specs=[pl.BlockSpec(memory_space=pltpu.SMEM),
                  pl.BlockSpec((None, tq, tq), lambda h, d: (d, 0, 0))],
        out_specs=pl.BlockSpec((None, None, tq, tq), lambda h, d: (h, d, 0, 0)),
        compiler_params=_params("arbitrary", "arbitrary"),
        name="bias_tiles",
    )(rel_bias, _bias_buckets(tq))


def _attn_kernel(q_ref, k_ref, v_ref, bias_ref, lq1_ref, lk1_ref, lq2_ref, lk2_ref, g_ref, o_ref,
                 vt_scr, sa_scr, sb_scr, m_scr, acc_scr, *, tq, tk, lam_init):
    nsub = tq // tk
    nq = q_ref.shape[0] // tq
    bufs = (sa_scr, sb_scr)

    ones = jnp.ones((ONES_ROWS, tk), BF16)
    for c in range(vt_scr.shape[0]):
        vt = v_ref[c * tk:(c + 1) * tk, :].astype(F32).T.astype(BF16)
        vt_scr[c] = jnp.concatenate([vt, ones], axis=0)

    lane = lax.broadcasted_iota(jnp.int32, (1, A_DV), 1)

    def query_tile(i, carry):
        def tile_rows(tile):
            return pl.ds(pl.multiple_of(tile * tq, tq), tq)

        def two_map_queries(tile):
            q = q_ref[tile_rows(tile), :] * (A_DK ** -0.5 * LOG2E)
            zero = jnp.zeros_like(q)
            return jnp.concatenate([jnp.where(lane < A_DK, q, zero), jnp.where(lane >= A_DK, q, zero)], axis=0)

        q2 = two_map_queries(i)
        m_scr[...] = jnp.full(m_scr.shape, NEG_INF, F32)
        acc_scr[...] = jnp.zeros(acc_scr.shape, F32)

        def scores(j, queries=q2, q_lo=0):
            k = k_ref[pl.ds(pl.multiple_of(j * tk, tk), tk), :]
            if q_lo:
                queries = jnp.concatenate([queries[q_lo:tq], queries[tq + q_lo:]], axis=0)
            return lax.dot_general(k, queries, (((1,), (1,)), ((), ())), preferred_element_type=F32)

        def softmax_pv(s_ref, j, bias, q_lo=0):
            width = tq - q_lo
            halves = (slice(q_lo, tq), slice(tq + q_lo, 2 * tq))
            both = lambda ref: ref[...] if not q_lo else jnp.concatenate([ref[:, h] for h in halves], axis=1)
            s = s_ref[:, :2 * width]
            if bias is not None:
                s = jnp.concatenate([s[:, :width] + bias, s[:, width:] + bias], axis=1)
            m_old = both(m_scr)
            m_new = jnp.maximum(m_old, jnp.max(s, axis=0, keepdims=True))
            alpha = jnp.exp2(m_old - m_new)
            p = jnp.exp2(s - m_new).astype(BF16)
            acc = alpha * both(acc_scr) + jnp.dot(vt_scr[j], p, preferred_element_type=F32)
            if not q_lo:
                acc_scr[...] = acc
                m_scr[...] = m_new
            else:
                for n, h in enumerate(halves):
                    acc_scr[:, h] = acc[:, n * width:(n + 1) * width]
                    m_scr[:, h] = m_new[:, n * width:(n + 1) * width]

        n_far = jnp.maximum(i - 1, 0) * nsub

        @pl.when(i == 0)
        def _():
            sa_scr[...] = scores(0)

        def far_steps(j, count):
            for c in range(count):
                bufs[(c + 1) % 2][...] = scores(j + c + 1)
                softmax_pv(bufs[c % 2], j + c, None)

        def far_quad(jj, inner):
            far_steps(4 * jj, 4)
            return inner

        n_quads = n_far // 4
        lax.fori_loop(0, n_quads, far_quad, 0)

        @pl.when(n_far - 4 * n_quads >= 2)
        def _():
            far_steps(4 * n_quads, 2)

        def biased_steps(first_tile):
            j0 = (i - 1 + first_tile) * nsub
            count = (2 - first_tile) * nsub

            def geometry(c):
                d, r = first_tile + c // nsub, (c % nsub) * tk
                return d, r, (r if d == 1 else 0)

            for c in range(count):
                if c + 1 < count:
                    q_next = geometry(c + 1)[2]
                    bufs[(c + 1) % 2][:, :2 * (tq - q_next)] = scores(j0 + c + 1, q_lo=q_next)
                d, r, q_lo = geometry(c)
                no_bias = d == 0 and r + tk - 1 - tq <= -FAR_BIAS_DISTANCE
                bias = None if no_bias else bias_ref[d, r:r + tk, q_lo:]
                softmax_pv(bufs[c % 2], j0 + c, bias, q_lo)

        def finish():
            lam = (jnp.exp(jnp.sum(lq1_ref[...] * lk1_ref[...], axis=-1, keepdims=True))
                   - jnp.exp(jnp.sum(lq2_ref[...] * lk2_ref[...], axis=-1, keepdims=True)) + lam_init)
            on = acc_scr[:A_DV, :] / acc_scr[A_DV:A_DV + 1, :]
            o = on[:, :tq] - lam * on[:, tq:]
            y = o * lax.rsqrt(jnp.mean(o * o, axis=0, keepdims=True) + EPS) * g_ref[...] * (1.0 - lam_init)
            o_ref[tile_rows(i), :] = y.T.astype(BF16)

        last = i + 1 == nq
        for first_tile, applies in ((0, i >= 1), (1, i == 0)):
            @pl.when(applies & jnp.logical_not(last))
            def _():
                biased_steps(first_tile)
                sa_scr[...] = scores(0, two_map_queries(i + 1))
                finish()

            @pl.when(applies & last)
            def _():
                biased_steps(first_tile)
                finish()
        return carry

    lax.fori_loop(0, nq, query_tile, 0)


def _attention(proj, bias_tiles, lq1, lk1, lq2, lk2, g_sub_col, batch, seq, lam_init, tq, tk=ATTN_TK):
    t = proj.shape[0]
    assert (tq // tk) % 2 == 0 and tq % tk == 0, "the score pipeline alternates two buffers per query tile"
    assert tq + 1 >= FAR_BIAS_DISTANCE, "key tiles two or more before the query tile must be past the bias horizon"
    kern = functools.partial(_attn_kernel, tq=tq, tk=tk, lam_init=lam_init)
    vec = lambda n: pl.BlockSpec((1, n), lambda b, h: (0, 0))
    return pl.pallas_call(
        kern,
        out_shape=jax.ShapeDtypeStruct((t, A_WIDTH), BF16),
        grid=(batch, A_HEADS),
        in_specs=[pl.BlockSpec((seq, A_DV), lambda b, h: (b, COL_QA + h)),
                  pl.BlockSpec((seq, A_DV), lambda b, h: (b, COL_KA + h)),
                  pl.BlockSpec((seq, A_DV), lambda b, h: (b, COL_VA + h)),
                  pl.BlockSpec((None, 2, tq, tq), lambda b, h: (h, 0, 0, 0)),
                  vec(A_DK), vec(A_DK), vec(A_DK), vec(A_DK),
                  pl.BlockSpec((A_DV, 1), lambda b, h: (0, 0))],
        out_specs=pl.BlockSpec((seq, A_DV), lambda b, h: (b, h)),
        scratch_shapes=[pltpu.VMEM((seq // tk, A_DV + ONES_ROWS, tk), BF16),
                        pltpu.VMEM((tk, 2 * tq), F32), pltpu.VMEM((tk, 2 * tq), F32),
                        pltpu.VMEM((1, 2 * tq), F32),
                        pltpu.VMEM((A_DV + ONES_ROWS, 2 * tq), F32)],
        compiler_params=_params("arbitrary", "arbitrary"),
        name="diff_attention",
    )(proj, proj, proj, bias_tiles, lq1, lk1, lq2, lk2, g_sub_col)


def _split3(a):
    a1 = a.astype(BF16)
    r1 = a - a1.astype(F32)
    a2 = r1.astype(BF16)
    return a1, a2, (r1 - a2.astype(F32)).astype(BF16)


def _sum3(x):
    return x[:, :B_DK] + x[:, B_DK:2 * B_DK] + x[:, 2 * B_DK:]


def _gla_kernel(q_ref, k_ref, v_ref, r_ref, z_ref, wa_ref, ba_ref, g_ref, o_ref,
                state_scr, mask_scr, kv_scr, st_scr, *, n_chunks):
    lc = n_chunks * CHUNK

    @pl.when(pl.program_id(2) == 0)
    def _():
        state_scr[...] = jnp.zeros(state_scr.shape, F32)
        row = lax.broadcasted_iota(jnp.int32, (lc, lc), 0)
        col = lax.broadcasted_iota(jnp.int32, (lc, lc), 1)
        same = (row // CHUNK) == (col // CHUNK)
        mask_scr[...] = (same & (row >= col)).astype(BF16)

    z = z_ref[...]
    zh = z.astype(BF16)
    zl = (z - zh.astype(F32)).astype(BF16)
    z3 = jnp.concatenate([zh, zl, zh], axis=1)
    kcols = lambda hh: slice(hh * B_DK, (hh + 1) * B_DK)
    vcols = lambda hh: slice(hh * B_DV, (hh + 1) * B_DV)
    chunk_rows = lambda c: slice(c * CHUNK, (c + 1) * CHUNK)
    live = [dict() for _ in range(GLA_HEADS_PER_STEP)]

    def gate_stage(hh):
        pre = jnp.dot(z3, wa_ref[:, kcols(hh)], preferred_element_type=F32) + ba_ref[:, kcols(hh)]
        log_a = (jnp.minimum(pre, 0.0) - jnp.log1p(jnp.exp(-jnp.abs(pre)))) * (1.0 / GATE_TAU)
        live[hh]["parts"] = jnp.concatenate(_split3(log_a), axis=1)

    def decay_stage(hh):
        cum = _sum3(jnp.dot(mask_scr[...], live[hh].pop("parts"), preferred_element_type=F32))
        totals = [cum[(c + 1) * CHUNK - 1:(c + 1) * CHUNK, :] for c in range(n_chunks)]
        total = jnp.concatenate([jnp.broadcast_to(tc, (CHUNK, B_DK)) for tc in totals], axis=0)
        live[hh]["totals"] = totals
        live[hh]["k_dec"] = (k_ref[:, kcols(hh)].astype(F32) * jnp.exp(total - cum)).astype(BF16)

    def kv_stage(hh):
        k_dec = live[hh].pop("k_dec")
        for c in range(n_chunks):
            kv_scr[hh, c] = lax.dot_general(v_ref[chunk_rows(c), vcols(hh)], k_dec[chunk_rows(c)],
                                            (((0,), (0,)), ((), ())), preferred_element_type=F32)

    def state_stage(hh):
        totals = live[hh].pop("totals")
        state = state_scr[hh]
        for c in range(n_chunks):
            state = state * jnp.exp(totals[c]) + kv_scr[hh, c]
            st_scr[hh, c] = state.astype(BF16)
        state_scr[hh] = state

    def output_stage(hh):
        for c in range(n_chunks):
            rows = chunk_rows(c)
            o = lax.dot_general(q_ref[rows, kcols(hh)], st_scr[hh, c], (((1,), (1,)), ((), ())),
                                preferred_element_type=F32) * (B_DK ** -0.5)
            o_ref[rows, vcols(hh)] = (_rms(o) * g_ref[...]
                                      * _silu(r_ref[rows, vcols(hh)].astype(F32))).astype(BF16)

    stages = (gate_stage, decay_stage, kv_stage, state_stage, output_stage)
    for tick in range(GLA_HEADS_PER_STEP + len(stages) - 1):
        for k, stage in enumerate(stages):
            if 0 <= tick - k < GLA_HEADS_PER_STEP:
                stage(tick - k)


def _gla(proj, zb, w_alpha_pad, b_alpha, g_norm, batch, seq, lc=GLA_LC):
    t = proj.shape[0]
    lc = min(lc, seq)
    nl = seq // lc
    n_chunks = lc // CHUNK
    hps = GLA_HEADS_PER_STEP
    assert B_HEADS % hps == 0 and COL_QB % hps == 0 and COL_KB % hps == 0
    kern = functools.partial(_gla_kernel, n_chunks=n_chunks)
    wa_hi = w_alpha_pad.astype(BF16)
    wa_lo = (w_alpha_pad - wa_hi.astype(F32)).astype(BF16)
    wa3 = jnp.concatenate([wa_hi, wa_hi, wa_lo], axis=0)
    kblock = lambda col0: pl.BlockSpec((lc, hps * B_DK), lambda b, h, l: (b * nl + l, col0 // hps + h))
    vblock = lambda col0: pl.BlockSpec((lc, hps * B_DV), lambda b, h, l: (b * nl + l, col0 // hps + h))
    return pl.pallas_call(
        kern,
        out_shape=jax.ShapeDtypeStruct((t, B_WIDTH), BF16),
        grid=(batch, B_HEADS // hps, nl),
        in_specs=[kblock(COL_QB), kblock(COL_KB), vblock(COL_VB256), vblock(COL_RB256),
                  pl.BlockSpec((lc, LANES), lambda b, h, l: (b * nl + l, 0)),
                  pl.BlockSpec((3 * LANES, hps * B_DK), lambda b, h, l: (0, h)),
                  pl.BlockSpec((1, hps * B_DK), lambda b, h, l: (0, h)),
                  pl.BlockSpec((1, B_DV), lambda b, h, l: (0, 0))],
        out_specs=pl.BlockSpec((lc, hps * B_DV), lambda b, h, l: (b * nl + l, h)),
        scratch_shapes=[pltpu.VMEM((hps, B_DV, B_DK), F32),
                        pltpu.VMEM((lc, lc), BF16),
                        pltpu.VMEM((hps, n_chunks, B_DV, B_DK), F32),
                        pltpu.VMEM((hps, n_chunks, B_DV, B_DK), BF16)],
        compiler_params=_params("arbitrary", "arbitrary", "arbitrary"),
        name="gla",
    )(proj, proj, proj, proj, zb, wa3, b_alpha, g_norm)


def _outproj_kernel(oa_ref, ob_ref, wo_ref, x_ref, gt_ref, gpost_ref, gpre_ref, sc_ref, sh_ref, wr_ref,
                    br_ref, x1_ref, h2_ref, lg_ref, *, tm):
    n_pieces = tm // OUTPROJ_PIECE

    def project(p):
        prow = slice(p * OUTPROJ_PIECE, (p + 1) * OUTPROJ_PIECE)
        return (jnp.dot(oa_ref[prow, :], wo_ref[:A_WIDTH, :], preferred_element_type=F32)
                + jnp.dot(ob_ref[prow, :], wo_ref[A_WIDTH:, :], preferred_element_type=F32))

    y_next = project(0)
    for p in range(n_pieces):
        p0 = p * OUTPROJ_PIECE
        prow = slice(p0, p0 + OUTPROJ_PIECE)
        y = y_next
        if p + 1 < n_pieces:
            y_next = project(p + 1)
        his, los = [], []
        for c in range(OUTPROJ_PIECE // ROW_CHUNK):
            r0 = p0 + c * ROW_CHUNK
            rows = slice(r0, r0 + ROW_CHUNK)
            x1 = x_ref[rows, :] + gt_ref[0] * (_rms(y[c * ROW_CHUNK:(c + 1) * ROW_CHUNK]) * gpost_ref[...])
            x1_ref[rows, :] = x1
            h2 = (_rms(x1) * gpre_ref[...]) * (1.0 + sc_ref[0]) + sh_ref[0]
            hi = h2.astype(BF16)
            his.append(hi)
            los.append((h2 - hi.astype(F32)).astype(BF16))
            _store_row_slabs(h2_ref, r0, ROW_CHUNK, h2)
        hi, lo = jnp.concatenate(his, axis=0), jnp.concatenate(los, axis=0)
        hw = jnp.dot(hi, wr_ref[...], preferred_element_type=F32)
        lw = jnp.dot(lo, wr_ref[:, :LANES], preferred_element_type=F32)
        lg_ref[prow, :] = hw[:, :LANES] + hw[:, LANES:] + lw + br_ref[...]


def _outproj(oa, ob, w_out, x2d, gt, g_post, g_pre, sc, sh, w_router, b_router, seq, tm=OUTPROJ_TM):
    t, d = x2d.shape
    tm = min(tm, seq)
    per_b = seq // tm
    kern = functools.partial(_outproj_kernel, tm=tm)
    wr_hi = w_router.astype(BF16)
    wr_lo = (w_router - wr_hi.astype(F32)).astype(BF16)
    wr_cat = jnp.concatenate([wr_hi, wr_lo], axis=1)
    row = lambda: pl.BlockSpec((1, d), lambda i: (0, 0))
    per_batch = lambda: pl.BlockSpec((1, 1, d), lambda i: (i // per_b, 0, 0))
    return pl.pallas_call(
        kern,
        out_shape=(jax.ShapeDtypeStruct((t, d), F32),
                   jax.ShapeDtypeStruct((t * ROW_SLABS, LANES), U32),
                   jax.ShapeDtypeStruct((t, LANES), F32)),
        grid=(t // tm,),
        in_specs=[pl.BlockSpec((tm, A_WIDTH), lambda i: (i, 0)),
                  pl.BlockSpec((tm, B_WIDTH), lambda i: (i, 0)),
                  pl.BlockSpec((d, d), lambda i: (0, 0)),
                  pl.BlockSpec((tm, d), lambda i: (i, 0)),
                  per_batch(), row(), row(), per_batch(), per_batch(),
                  pl.BlockSpec((d, 2 * LANES), lambda i: (0, 0)),
                  pl.BlockSpec((1, LANES), lambda i: (0, 0))],
        out_specs=(pl.BlockSpec((tm, d), lambda i: (i, 0)),
                   pl.BlockSpec((tm * ROW_SLABS, LANES), lambda i: (i, 0)),
                   pl.BlockSpec((tm, LANES), lambda i: (i, 0))),
        compiler_params=_params("arbitrary"),
        name="out_proj",
    )(oa, ob, w_out, x2d, gt, g_post, g_pre, sc, sh, wr_cat, b_router)


def _route_kernel(lg_ref, rec_ref, rows_ref, cnt_ref, carry_scr, before_scr, *, tr):
    @pl.when(pl.program_id(0) == 0)
    def _():
        carry_scr[...] = jnp.zeros(carry_scr.shape, F32)
        earlier = lax.broadcasted_iota(jnp.int32, (tr, tr), 0)
        token = lax.broadcasted_iota(jnp.int32, (tr, tr), 1)
        before_scr[...] = (earlier < token).astype(BF16)

    lg = lg_ref[...].T[:ROUTER_ROWS]
    row = lax.broadcasted_iota(jnp.int32, lg.shape, 0)
    big = jnp.int32(ROUTER_ROWS)

    def first_row(mask):
        return jnp.min(jnp.where(mask, row, big), axis=0, keepdims=True)

    gmask = row < N_GROUPS
    gmax = jnp.max(jnp.where(gmask, lg, -jnp.inf), axis=0, keepdims=True)
    gexp = jnp.where(gmask, jnp.exp(lg - gmax), 0.0)
    gprob = gexp / jnp.sum(gexp, axis=0, keepdims=True)
    g_val = jnp.max(gprob, axis=0, keepdims=True)
    g_idx = first_row(gmask & (gprob == g_val))

    lo = ROUTER_EXPERT_LANE0 + g_idx * EXPERTS_PER_GROUP
    emask = (row >= lo) & (row < lo + EXPERTS_PER_GROUP)
    emax = jnp.max(jnp.where(emask, lg, -jnp.inf), axis=0, keepdims=True)
    eexp = jnp.where(emask, jnp.exp(lg - emax), 0.0)
    eprob = eexp / jnp.sum(eexp, axis=0, keepdims=True)
    v1 = jnp.max(eprob, axis=0, keepdims=True)
    i1 = first_row(emask & (eprob == v1))
    rest = emask & (row != i1)
    v2 = jnp.max(jnp.where(rest, eprob, -1.0), axis=0, keepdims=True)
    i2 = first_row(rest & (eprob == v2))
    w1 = g_val * (v1 / (v1 + v2))
    w2 = g_val * (v2 / (v1 + v2))

    hit1 = row == i1
    hit2 = row == i2
    onehot = (hit1 | hit2).astype(BF16)
    pos = carry_scr[...] + jnp.dot(onehot, before_scr[...], preferred_element_type=F32)
    rank1 = jnp.sum(jnp.where(hit1, pos, 0.0), axis=0, keepdims=True)
    rank2 = jnp.sum(jnp.where(hit2, pos, 0.0), axis=0, keepdims=True)
    carry_scr[...] = carry_scr[...] + jnp.sum(onehot.astype(F32), axis=1, keepdims=True)
    cnt_ref[...] = carry_scr[...]

    e1 = (i1 - ROUTER_EXPERT_LANE0).astype(F32)
    e2 = (i2 - ROUTER_EXPERT_LANE0).astype(F32)
    field = lax.broadcasted_iota(jnp.int32, (LANES, tr), 0)
    rec = jnp.zeros((LANES, tr), F32)
    for ln, val in ((ROUTE_E1, e1), (ROUTE_E2, e2), (ROUTE_W1, w1), (ROUTE_W2, w2),
                    (ROUTE_R1, rank1), (ROUTE_R2, rank2)):
        rec = jnp.where(field == ln, val, rec)
    rec_ref[...] = rec.T
    rows_ref[...] = rec[:ROUTE_FIELD_ROWS]


def _route(logits, tr=ROUTE_TR):
    t = logits.shape[0]
    tr = min(tr, t)
    kern = functools.partial(_route_kernel, tr=tr)
    return pl.pallas_call(
        kern,
        out_shape=(jax.ShapeDtypeStruct((t, LANES), F32), jax.ShapeDtypeStruct((ROUTE_FIELD_ROWS, t), F32),
                   jax.ShapeDtypeStruct((ROUTER_ROWS, 1), F32)),
        grid=(t // tr,),
        in_specs=[pl.BlockSpec((tr, LANES), lambda i: (i, 0))],
        out_specs=(pl.BlockSpec((tr, LANES), lambda i: (i, 0)),
                   pl.BlockSpec((ROUTE_FIELD_ROWS, tr), lambda i: (0, i)),
                   pl.BlockSpec((ROUTER_ROWS, 1), lambda i: (0, 0))),
        scratch_shapes=[pltpu.VMEM((ROUTER_ROWS, 1), F32), pltpu.VMEM((tr, tr), BF16)],
        compiler_params=_params("arbitrary"),
        name="route",
    )(logits)


def _slab_rows(ref, row):
    return ref.at[pl.ds(pl.multiple_of(row * ROW_SLABS, ROW_SLABS), ROW_SLABS), :]


def _dispatch_kernel(slot_ref, pad_start_ref, pad_len_ref, used_ref, h2_ref, xs_hbm, zero_scr, sems, pad_sems,
                     *, td, tm):
    g = pl.program_id(0)
    tile_rows = tm * ROW_SLABS
    n_tiles = xs_hbm.shape[0] // tile_rows

    def zero_copy(slot, nslots, sem):
        rows = pl.ds(pl.multiple_of(slot * ROW_SLABS, ROW_SLABS), nslots * ROW_SLABS)
        return pltpu.make_async_copy(zero_scr.at[pl.ds(0, nslots * ROW_SLABS), :], xs_hbm.at[rows, :], sem)

    pad_sizes = [1 << b for b in reversed(range((tm - 1).bit_length()))]

    @pl.when(g == 0)
    def _():
        zero_scr[...] = jnp.zeros(zero_scr.shape, U32)

        def unused_tile(tile, carry):
            zero_copy(tile * tm, tm, sems.at[1]).start()
            zero_copy(tile * tm, tm, sems.at[1]).wait()
            return carry

        lax.fori_loop(used_ref[0], n_tiles, unused_tile, 0)

        def per_expert(e, counts):
            off = pad_start_ref[e]
            n = pad_len_ref[e]
            new_counts = []
            for b, size in enumerate(pad_sizes):
                hit = (n & size) != 0

                @pl.when(hit)
                def _():
                    zero_copy(off, size, pad_sems.at[b]).start()

                off = off + jnp.where(hit, size, 0)
                new_counts.append(counts[b] + hit.astype(jnp.int32))
            return tuple(new_counts)

        counts = lax.fori_loop(0, N_EXPERTS, per_expert, tuple(jnp.int32(0) for _ in pad_sizes))
        for b, size in enumerate(pad_sizes):
            def drain(r, c):
                zero_copy(0, size, pad_sems.at[b]).wait()
                return c
            lax.fori_loop(0, counts[b], drain, 0)

    def row_copy(r, slot):
        return pltpu.make_async_copy(_slab_rows(h2_ref, r), _slab_rows(xs_hbm, slot), sems.at[0])

    base = g * td
    n_tok = slot_ref.shape[0] // 2

    def issue(r, c):
        tok = base + r
        row_copy(r, slot_ref[tok]).start()
        row_copy(r, slot_ref[n_tok + tok]).start()
        return c

    lax.fori_loop(0, td, issue, 0, unroll=CHUNK_UNROLL)
    for _ in range(2):
        pltpu.make_async_copy(h2_ref, xs_hbm.at[pl.ds(0, td * ROW_SLABS), :], sems.at[0]).wait()


def _dispatch(slot, pad_start, pad_len, used, h2_rows, n_slots, tm, td=DISPATCH_TD):
    t = slot.shape[0] // 2
    td = min(td, t)
    kern = functools.partial(_dispatch_kernel, td=td, tm=tm)
    grid_spec = pltpu.PrefetchScalarGridSpec(
        num_scalar_prefetch=4,
        grid=(t // td,),
        in_specs=[pl.BlockSpec((td * ROW_SLABS, LANES), lambda g, sl, ps, pn, us: (g, 0))],
        out_specs=pl.BlockSpec(memory_space=pl.ANY),
        scratch_shapes=[pltpu.VMEM((tm * ROW_SLABS, LANES), U32), pltpu.SemaphoreType.DMA((2,)),
                        pltpu.SemaphoreType.DMA(((tm - 1).bit_length(),))],
    )
    return pl.pallas_call(
        kern,
        out_shape=jax.ShapeDtypeStruct((n_slots * ROW_SLABS, LANES), U32),
        grid_spec=grid_spec,
        compiler_params=_params("arbitrary"),
        name="dispatch",
    )(slot, pad_start, pad_len, used, h2_rows)


TILE_UNUSED, TILE_USED, TILE_NEW_EXPERT = 0, 1, 2


def _expert_kernel(texp_ref, tblk_ref, tstate_ref, tnext_ref, tpar_ref, xs_ref, w1_hbm, w3_hbm, w2_hbm, eo_ref,
                   x_scr, w1_scr, w3_scr, w2_scr, w1_stage, w3_stage, w2_stage, sems, *, tm):
    i = pl.program_id(0)
    state = tstate_ref[i]
    slot = tpar_ref[i]

    def weight_copies(expert, dst_slot):
        return [pltpu.make_async_copy(hbm.at[expert], stage.at[dst_slot], sems.at[dst_slot])
                for hbm, stage in ((w1_hbm, w1_stage), (w3_hbm, w3_stage), (w2_hbm, w2_stage))]

    @pl.when(state == TILE_UNUSED)
    def _():
        eo_ref[...] = jnp.zeros(eo_ref.shape, U32)

    @pl.when(i == 0)
    def _():
        for cp in weight_copies(texp_ref[0], slot):
            cp.start()

    @pl.when(state == TILE_NEW_EXPERT)
    def _():
        for cp in weight_copies(texp_ref[i], slot):
            cp.wait()

        @pl.when(tnext_ref[i] >= 0)
        def _():
            for cp in weight_copies(tnext_ref[i], 1 - slot):
                cp.start()

        w1_scr[...] = w1_stage[slot].astype(BF16)
        w3_scr[...] = w3_stage[slot].astype(BF16)
        w2_scr[...] = w2_stage[slot].astype(BF16)

    @pl.when(state != TILE_UNUSED)
    def _():
        for s in range(ROW_SLABS):
            lo, hi = _unpack_bf16_pair(xs_ref[pl.ds(s, tm, stride=ROW_SLABS), :])
            x_scr[:, s * LANES:(s + 1) * LANES] = lo.astype(BF16)
            x_scr[:, HALF_D + s * LANES:HALF_D + (s + 1) * LANES] = hi.astype(BF16)
        x = x_scr[...]
        a = jnp.dot(x, w1_scr[...], preferred_element_type=F32)
        b = jnp.dot(x, w3_scr[...], preferred_element_type=F32)
        hid = (_silu(a) * b).astype(BF16)
        y = jnp.dot(hid, w2_scr[...], preferred_element_type=F32)
        _store_row_slabs(eo_ref, 0, tm, y)


def _experts(tile_expert, tile_block, tile_state, tile_next, tile_slot, xs_rows, w1, w3, w2, tm):
    n_tiles = tile_expert.shape[0]
    d, f = w1.shape[1], w1.shape[2]
    kern = functools.partial(_expert_kernel, tm=tm)
    grid_spec = pltpu.PrefetchScalarGridSpec(
        num_scalar_prefetch=5,
        grid=(n_tiles,),
        in_specs=[pl.BlockSpec((tm * ROW_SLABS, LANES), lambda i, te, tb, ts, tn, tp: (tb[i], 0)),
                  pl.BlockSpec(memory_space=pl.ANY), pl.BlockSpec(memory_space=pl.ANY),
                  pl.BlockSpec(memory_space=pl.ANY)],
        out_specs=pl.BlockSpec((tm * ROW_SLABS, LANES), lambda i, te, tb, ts, tn, tp: (i, 0)),
        scratch_shapes=[pltpu.VMEM((tm, d), BF16), pltpu.VMEM((d, f), BF16), pltpu.VMEM((d, f), BF16),
                        pltpu.VMEM((f, d), BF16),
                        pltpu.VMEM((2, d, f), F32), pltpu.VMEM((2, d, f), F32), pltpu.VMEM((2, f, d), F32),
                        pltpu.SemaphoreType.DMA((2,))],
    )
    return pl.pallas_call(
        kern,
        out_shape=jax.ShapeDtypeStruct(xs_rows.shape, U32),
        grid_spec=grid_spec,
        compiler_params=_params("arbitrary"),
        name="expert_mlp",
    )(tile_expert, tile_block, tile_state, tile_next, tile_slot, xs_rows, w1, w3, w2)


def _final_kernel(slot_ref, eo_hbm, rec_ref, x1_ref, gt_ref, g_ref, o_ref, e_scr, sems, *, tf):
    i = pl.program_id(0)
    par = i % 2
    n_tok = slot_ref.shape[0] // 2

    def start_all(step, buf):
        def body(r, c):
            tok = step * tf + r
            for k in range(2):
                pltpu.make_async_copy(_slab_rows(eo_hbm, slot_ref[k * n_tok + tok]),
                                      _slab_rows(e_scr.at[buf, k], r), sems.at[buf]).start()
            return c
        lax.fori_loop(0, tf, body, 0, unroll=CHUNK_UNROLL)

    def wait_all(buf):
        for k in range(2):
            pltpu.make_async_copy(eo_hbm.at[pl.ds(0, tf * ROW_SLABS), :], e_scr.at[buf, k], sems.at[buf]).wait()

    @pl.when(i == 0)
    def _():
        start_all(0, 0)

    @pl.when(i + 1 < pl.num_programs(0))
    def _():
        start_all(i + 1, 1 - par)

    wait_all(par)

    def chunk(c, carry):
        r0 = pl.multiple_of(c * ROW_CHUNK, ROW_CHUNK)
        rows = pl.ds(r0, ROW_CHUNK)
        rec = rec_ref[rows, :]
        w1 = rec[:, ROUTE_W1:ROUTE_W1 + 1]
        w2 = rec[:, ROUTE_W2:ROUTE_W2 + 1]
        lo1, hi1 = _load_row_slabs(e_scr.at[par, 0], r0, ROW_CHUNK)
        lo2, hi2 = _load_row_slabs(e_scr.at[par, 1], r0, ROW_CHUNK)
        y = jnp.concatenate([w1 * lo1 + w2 * lo2, w1 * hi1 + w2 * hi2], axis=1)
        o_ref[rows, :] = x1_ref[rows, :] + gt_ref[0] * (_rms(y) * g_ref[...])
        return carry

    lax.fori_loop(0, tf // ROW_CHUNK, chunk, 0, unroll=CHUNK_UNROLL)


def _final(slot, eo_rows, rec, x1, gt, g_post, seq, tf=COMBINE_TF):
    t, d = x1.shape
    tf = min(tf, seq)
    per_b = seq // tf
    kern = functools.partial(_final_kernel, tf=tf)
    grid_spec = pltpu.PrefetchScalarGridSpec(
        num_scalar_prefetch=1,
        grid=(t // tf,),
        in_specs=[pl.BlockSpec(memory_space=pl.ANY),
                  pl.BlockSpec((tf, LANES), lambda i, sl: (i, 0)),
                  pl.BlockSpec((tf, d), lambda i, sl: (i, 0)),
                  pl.BlockSpec((1, 1, d), lambda i, sl: (i // per_b, 0, 0)),
                  pl.BlockSpec((1, d), lambda i, sl: (0, 0))],
        out_specs=pl.BlockSpec((tf, d), lambda i, sl: (i, 0)),
        scratch_shapes=[pltpu.VMEM((2, 2, tf * ROW_SLABS, LANES), U32),
                        pltpu.SemaphoreType.DMA((2,))],
    )
    return pl.pallas_call(
        kern,
        out_shape=jax.ShapeDtypeStruct((t, d), F32),
        grid_spec=grid_spec,
        compiler_params=_params("arbitrary"),
        name="combine_final",
    )(slot, eo_rows, rec, x1, gt, g_post)


def _dispatch_tables(rows, counts, t, tm):
    e = rows[ROUTE_E1:ROUTE_E2 + 1].astype(jnp.int32)
    rank = rows[ROUTE_R1:ROUTE_R2 + 1].astype(jnp.int32)
    cnt = counts[ROUTER_EXPERT_LANE0:ROUTER_EXPERT_LANE0 + N_EXPERTS, 0].astype(jnp.int32)
    tiles_per = (cnt + tm - 1) // tm
    tile_end = jnp.cumsum(tiles_per)
    tile_start = tile_end - tiles_per
    n_tiles = (2 * t + N_EXPERTS * (tm - 1)) // tm
    experts = jnp.arange(N_EXPERTS, dtype=jnp.int32)
    start_of = jnp.sum(jnp.where(e[None] == experts[:, None, None], tile_start[:, None, None], 0),
                       axis=0)
    slot = (start_of * tm + rank).reshape(-1)
    pad_start = tile_start * tm + cnt
    pad_len = tiles_per * tm - cnt
    tile_id = jnp.arange(n_tiles, dtype=jnp.int32)
    used = tile_end[-1]
    tblk = jnp.minimum(tile_id, used - 1)
    texp = jnp.sum(tile_end[None, :] <= tblk[:, None], axis=-1).astype(jnp.int32)
    tstate = jnp.where(tile_id < used, jnp.where(tile_id == tile_start[texp], TILE_NEW_EXPERT, TILE_USED),
                       TILE_UNUSED).astype(jnp.int32)
    nonempty = cnt > 0
    ordinal = jnp.cumsum(nonempty.astype(jnp.int32)) - 1
    later = jnp.where(nonempty[None, :] & (experts[None, :] > experts[:, None]), experts[None, :], N_EXPERTS)
    next_expert = jnp.min(later, axis=-1)
    next_expert = jnp.where(next_expert == N_EXPERTS, -1, next_expert)
    tnext = next_expert[texp].astype(jnp.int32)
    tslot = (ordinal[texp] % 2).astype(jnp.int32)
    return slot, pad_start, pad_len, used.reshape(1), texp, tblk, tstate, tnext, tslot, n_tiles * tm


def kernel(x, c, rel_bias, w_ada, b_ada, g_pre_mix, g_post_mix, w_in, w_alpha, b_alpha, lam_q1, lam_k1, lam_q2,
           lam_k2, g_sub_a, g_norm_b, w_out, g_pre_ffn, g_post_ffn, w_router_g, b_router_g, w_router_e,
           b_router_e, w1, w3, w2):
    batch, seq, d = x.shape
    t = batch * seq
    depth = w_in.shape[0]
    tq = min(ATTN_TQ, seq)
    tm_e = EXPERT_TM
    xf = x.reshape(t, d)
    for i in range(depth):
        lam_init = 0.8 - 0.6 * math.exp(-0.3 * i)
        c_pad = jnp.pad(c, ((0, 8 - batch % 8 if batch % 8 else 0), (0, 0)))
        ada = _ada(c_pad, w_ada[i], b_ada[i][None, :])[:batch]
        sh_m, sc_m, gt_m, sh_f, sc_f, gt_f = [a[:, None, :] for a in jnp.split(ada, 6, axis=-1)]

        w_in_b = w_in[i].astype(BF16)
        w_z = jnp.pad(w_in_b[:, D_MAIN:], ((0, 0), (0, LANES - GATE_RANK)))
        proj, zb = _inproj(xf, g_pre_mix[i][None, :], sc_m, sh_m, w_in_b, w_z, seq)

        oa = _attention(proj, _bias_tiles(rel_bias, tq), lam_q1[i][None, :], lam_k1[i][None, :],
                        lam_q2[i][None, :], lam_k2[i][None, :], g_sub_a[i][:, None], batch, seq, lam_init, tq)
        w_alpha_pad = jnp.pad(w_alpha[i], ((0, LANES - GATE_RANK), (0, 0)))
        ob = _gla(proj, zb, w_alpha_pad, b_alpha[i][None, :], g_norm_b[i][None, :], batch, seq)

        w_router = jnp.pad(jnp.concatenate([w_router_g[i], w_router_e[i]], axis=1),
                           ((0, 0), (0, LANES - N_GROUPS - N_EXPERTS)))
        b_router = jnp.pad(jnp.concatenate([b_router_g[i], b_router_e[i]]),
                           (0, LANES - N_GROUPS - N_EXPERTS))[None, :]
        x1, h2_rows, logits = _outproj(oa, ob, w_out[i].astype(BF16), xf, gt_m, g_post_mix[i][None, :],
                                       g_pre_ffn[i][None, :], sc_f, sh_f, w_router, b_router, seq)

        rec, rec_rows, counts = _route(logits)
        (slot, pad_start, pad_len, used, texp, tblk, tstate, tnext, tslot,
         n_slots) = _dispatch_tables(rec_rows, counts, t, tm_e)
        xs = _dispatch(slot, pad_start, pad_len, used, h2_rows, n_slots, tm_e)
        eo = _experts(texp, tblk, tstate, tnext, tslot, xs, w1[i], w3[i], w2[i], tm_e)
        xf = _final(slot, eo, rec, x1, gt_f, g_post_ffn[i][None, :], seq)
    return xf.reshape(batch, seq, d)
```

```python
import functools
import math

import jax
import jax.numpy as jnp
from jax import lax
from jax.experimental import pallas as pl
from jax.experimental.pallas import tpu as pltpu

F32 = jnp.float32
BF16 = jnp.bfloat16

D_MODEL = 2048
CHUNK = 64
A_HEADS = 8
A_DK = 64
A_DV = 2 * A_DK
A_WIDTH = A_HEADS * A_DV
B_HEADS = 4
B_WIDTH = D_MODEL - A_WIDTH
B_DV = B_WIDTH // B_HEADS
B_DK = B_DV // 2
GATE_RANK = 16
GATE_TAU = 16.0
N_BUCKETS = 32
MAX_DISTANCE = 256
N_GROUPS = 4
EXPERTS_PER_GROUP = 8
N_EXPERTS = N_GROUPS * EXPERTS_PER_GROUP
EPS = 1e-6
NEG_INF = -1e30
LOG2E = math.log2(math.e)

LANES = 128
U32 = jnp.uint32
HALF_D = D_MODEL // 2
ROW_SLABS = HALF_D // LANES
ROW_CHUNK = 16
CHUNK_UNROLL = 8
ONES_ROWS = 16
FAR_BIAS_DISTANCE = 166

ADA_TN = 1024
INPROJ_TM, INPROJ_TN = 512, 1024
ATTN_TQ, ATTN_TK = 512, 256
GLA_LC = 512
GLA_HEADS_PER_STEP = 4
OUTPROJ_TM = 512
OUTPROJ_PIECE = 256
ROUTE_TR = 1024
EXPERT_TM = 512
DISPATCH_TD = 2048
COMBINE_TF = 256
D_MAIN = 3 * A_WIDTH + 2 * B_HEADS * B_DK + 2 * B_WIDTH
COL_QA, COL_KA, COL_VA = 0, A_HEADS, 2 * A_HEADS
COL_QB = 3 * A_HEADS
COL_KB = COL_QB + B_HEADS
COL_VB256 = (3 * A_WIDTH + 2 * B_HEADS * B_DK) // B_DV
COL_RB256 = COL_VB256 + B_HEADS
ROUTE_E1, ROUTE_E2, ROUTE_W1, ROUTE_W2, ROUTE_R1, ROUTE_R2 = 0, 1, 2, 3, 4, 5
ROUTE_FIELD_ROWS = 8
ROUTER_EXPERT_LANE0 = N_GROUPS
ROUTER_ROWS = 48

VMEM_LIMIT = 56 * 1024 * 1024
INPROJ_VMEM_LIMIT = 60 * 1024 * 1024


def _params(*sem):
    return pltpu.CompilerParams(dimension_semantics=sem, vmem_limit_bytes=VMEM_LIMIT)


def _rms(v):
    return v * lax.rsqrt(jnp.mean(v * v, axis=-1, keepdims=True) + EPS)


def _silu(v):
    return v * jax.nn.sigmoid(v)


_HIGH_HALF = 0xFFFF0000


def _pack_bf16_pair(lo, hi):
    lo_bits = lax.bitcast_convert_type(lo.astype(BF16).astype(F32), U32) >> 16
    hi_bits = lax.bitcast_convert_type(hi.astype(BF16).astype(F32), U32) & U32(_HIGH_HALF)
    return hi_bits | lo_bits


def _unpack_bf16_pair(w):
    return (lax.bitcast_convert_type(w << 16, F32), lax.bitcast_convert_type(w & U32(_HIGH_HALF), F32))


def _store_row_slabs(ref, r0, nrows, rows_f32):
    packed = _pack_bf16_pair(rows_f32[:, :HALF_D], rows_f32[:, HALF_D:])
    for s in range(ROW_SLABS):
        ref[pl.ds(r0 * ROW_SLABS + s, nrows, stride=ROW_SLABS), :] = packed[:, s * LANES:(s + 1) * LANES]


def _load_row_slabs(ref, r0, nrows):
    slabs = [_unpack_bf16_pair(ref[pl.ds(r0 * ROW_SLABS + s, nrows, stride=ROW_SLABS), :]) for s in range(ROW_SLABS)]
    return (jnp.concatenate([lo for lo, _ in slabs], axis=1), jnp.concatenate([hi for _, hi in slabs], axis=1))


def _ada_kernel(c_ref, w_ref, b_ref, o_ref):
    s = _silu(c_ref[...])
    o_ref[...] = jnp.dot(s.astype(BF16), w_ref[...].astype(BF16), preferred_element_type=F32) + b_ref[...]


def _ada(c_pad, w, b, tn=ADA_TN):
    m, d = c_pad.shape
    n = w.shape[1]
    return pl.pallas_call(
        _ada_kernel,
        out_shape=jax.ShapeDtypeStruct((m, n), F32),
        grid=(n // tn,),
        in_specs=[pl.BlockSpec((m, d), lambda j: (0, 0)),
                  pl.BlockSpec((d, tn), lambda j: (0, j)),
                  pl.BlockSpec((1, tn), lambda j: (0, j))],
        out_specs=pl.BlockSpec((m, tn), lambda j: (0, j)),
        compiler_params=_params("arbitrary"),
        name="ada_proj",
    )(c_pad, w, b)


def _inproj_kernel(x_ref, xn_ref, g_ref, sc_ref, sh_ref, scn_ref, shn_ref, w_ref, wz_ref, o_ref, z_ref,
                   h_scr, hn_scr, *, tm, tn):
    def normed(src_ref, scale_ref, shift_ref, rows):
        h = _rms(src_ref[rows, :]) * g_ref[...]
        return (h * (1.0 + scale_ref[0]) + shift_ref[0]).astype(BF16)

    @pl.when(pl.program_id(0) == 0)
    def _():
        def chunk(c, carry):
            rows = pl.ds(pl.multiple_of(c * ROW_CHUNK, ROW_CHUNK), ROW_CHUNK)
            h_scr[rows, :] = normed(x_ref, sc_ref, sh_ref, rows)
            return carry
        lax.fori_loop(0, tm // ROW_CHUNK, chunk, 0, unroll=CHUNK_UNROLL)

    h = h_scr[...]
    z_ref[...] = jnp.dot(h, wz_ref[...], preferred_element_type=F32)
    n_col = D_MAIN // tn
    rows_per_col = tm // n_col // ROW_CHUNK * ROW_CHUNK
    next_row = 0
    for c in range(n_col):
        cols = slice(c * tn, (c + 1) * tn)
        o_ref[:, cols] = jnp.dot(h, w_ref[:, cols], preferred_element_type=F32).astype(BF16)
        stop = tm if c == n_col - 1 else next_row + rows_per_col
        for r0 in range(next_row, stop, ROW_CHUNK):
            rows = slice(r0, r0 + ROW_CHUNK)
            hn_scr[rows, :] = normed(xn_ref, scn_ref, shn_ref, rows)
        next_row = stop
    h_scr[...] = hn_scr[...]


def _inproj(x2d, g, sc, sh, w_all, w_z, seq, tm=INPROJ_TM, tn=INPROJ_TN):
    t, d = x2d.shape
    tm = min(tm, seq)
    per_b = seq // tm
    n_steps = t // tm
    kern = functools.partial(_inproj_kernel, tm=tm, tn=tn)
    nxt = lambda i: jnp.minimum(i + 1, n_steps - 1)
    per_batch = lambda step: pl.BlockSpec((1, 1, d), lambda i: (step(i) // per_b, 0, 0))
    return pl.pallas_call(
        kern,
        out_shape=(jax.ShapeDtypeStruct((t, D_MAIN), BF16), jax.ShapeDtypeStruct((t, LANES), F32)),
        grid=(n_steps,),
        in_specs=[pl.BlockSpec((tm, d), lambda i: (i, 0)),
                  pl.BlockSpec((tm, d), lambda i: (nxt(i), 0)),
                  pl.BlockSpec((1, d), lambda i: (0, 0)),
                  per_batch(lambda i: i), per_batch(lambda i: i), per_batch(nxt), per_batch(nxt),
                  pl.BlockSpec(w_all.shape, lambda i: (0, 0), pipeline_mode=pl.Buffered(1)),
                  pl.BlockSpec((d, LANES), lambda i: (0, 0), pipeline_mode=pl.Buffered(1))],
        out_specs=(pl.BlockSpec((tm, D_MAIN), lambda i: (i, 0)),
                   pl.BlockSpec((tm, LANES), lambda i: (i, 0))),
        scratch_shapes=[pltpu.VMEM((tm, d), BF16), pltpu.VMEM((tm, d), BF16)],
        compiler_params=pltpu.CompilerParams(dimension_semantics=("arbitrary",),
                                             vmem_limit_bytes=INPROJ_VMEM_LIMIT),
        name="in_proj",
    )(x2d, x2d, g, sc, sh, sc, sh, w_all, w_z)


def _t5_bucket(rel):
    nb = N_BUCKETS // 2
    max_exact = nb // 2
    base = jnp.where(rel > 0, nb, 0)
    n = jnp.abs(rel)
    nf = jnp.maximum(n, 1).astype(F32)
    large = max_exact + (jnp.log(nf / max_exact) / math.log(MAX_DISTANCE / max_exact)
                         * (nb - max_exact)).astype(jnp.int32)
    large = jnp.minimum(large, nb - 1)
    return base + jnp.where(n < max_exact, n, large)


def _bias_buckets(tq):
    kj = jnp.arange(tq, dtype=jnp.int32)[:, None]
    qi = jnp.arange(tq, dtype=jnp.int32)[None, :]
    near = _t5_bucket(kj - qi - tq)
    diag = jnp.where((kj // CHUNK) <= (qi // CHUNK), _t5_bucket(kj - qi), N_BUCKETS)
    return jnp.stack([near, diag]).astype(jnp.int32)


def _bias_kernel(rb_ref, bk_ref, o_ref):
    h = pl.program_id(0)
    far = rb_ref[N_BUCKETS // 2 - 1, h]
    bucket = bk_ref[...]
    acc = jnp.full(bucket.shape, NEG_INF, F32)
    for n in range(N_BUCKETS):
        acc = jnp.where(bucket == n, (rb_ref[n, h] - far) * LOG2E, acc)
    o_ref[...] = acc


def _bias_tiles(rel_bias, tq):
    return pl.pallas_call(
        _bias_kernel,
        out_shape=jax.ShapeDtypeStruct((A_HEADS, 2, tq, tq), F32),
        grid=(A_HEADS, 2),
        in_specs=[pl.BlockSpec(memory_space=pltpu.SMEM),
                  pl.BlockSpec((None, tq, tq), lambda h, d: (d, 0, 0))],
        out_specs=pl.BlockSpec((None, None, tq, tq), lambda h, d: (h, d, 0, 0)),
        compiler_params=_params("arbitrary", "arbitrary"),
        name="bias_tiles",
    )(rel_bias, _bias_buckets(tq))


def _attn_kernel(q_ref, k_ref, v_ref, bias_ref, lq1_ref, lk1_ref, lq2_ref, lk2_ref, g_ref, o_ref,
                 vt_scr, sa_scr, sb_scr, m_scr, acc_scr, *, tq, tk, lam_init):
    nsub = tq // tk
    nq = q_ref.shape[0] // tq
    bufs = (sa_scr, sb_scr)

    ones = jnp.ones((ONES_ROWS, tk), BF16)
    for c in range(vt_scr.shape[0]):
        vt = v_ref[c * tk:(c + 1) * tk, :].astype(F32).T.astype(BF16)
        vt_scr[c] = jnp.concatenate([vt, ones], axis=0)

    lane = lax.broadcasted_iota(jnp.int32, (1, A_DV), 1)

    def query_tile(i, carry):
        def tile_rows(tile):
            return pl.ds(pl.multiple_of(tile * tq, tq), tq)

        def two_map_queries(tile):
            q = q_ref[tile_rows(tile), :] * (A_DK ** -0.5 * LOG2E)
            zero = jnp.zeros_like(q)
            return jnp.concatenate([jnp.where(lane < A_DK, q, zero), jnp.where(lane >= A_DK, q, zero)], axis=0)

        q2 = two_map_queries(i)
        m_scr[...] = jnp.full(m_scr.shape, NEG_INF, F32)
        acc_scr[...] = jnp.zeros(acc_scr.shape, F32)

        def scores(j, queries=q2, q_lo=0):
            k = k_ref[pl.ds(pl.multiple_of(j * tk, tk), tk), :]
            if q_lo:
                queries = jnp.concatenate([queries[q_lo:tq], queries[tq + q_lo:]], axis=0)
            return lax.dot_general(k, queries, (((1,), (1,)), ((), ())), preferred_element_type=F32)

        def softmax_pv(s_ref, j, bias, q_lo=0):
            width = tq - q_lo
            halves = (slice(q_lo, tq), slice(tq + q_lo, 2 * tq))
            both = lambda ref: ref[...] if not q_lo else jnp.concatenate([ref[:, h] for h in halves], axis=1)
            s = s_ref[:, :2 * width]
            if bias is not None:
                s = jnp.concatenate([s[:, :width] + bias, s[:, width:] + bias], axis=1)
            m_old = both(m_scr)
            m_new = jnp.maximum(m_old, jnp.max(s, axis=0, keepdims=True))
            alpha = jnp.exp2(m_old - m_new)
            p = jnp.exp2(s - m_new).astype(BF16)
            acc = alpha * both(acc_scr) + jnp.dot(vt_scr[j], p, preferred_element_type=F32)
            if not q_lo:
                acc_scr[...] = acc
                m_scr[...] = m_new
            else:
                for n, h in enumerate(halves):
                    acc_scr[:, h] = acc[:, n * width:(n + 1) * width]
                    m_scr[:, h] = m_new[:, n * width:(n + 1) * width]

        n_far = jnp.maximum(i - 1, 0) * nsub

        @pl.when(i == 0)
        def _():
            sa_scr[...] = scores(0)

        def far_steps(j, count):
            for c in range(count):
                bufs[(c + 1) % 2][...] = scores(j + c + 1)
                softmax_pv(bufs[c % 2], j + c, None)

        def far_quad(jj, inner):
            far_steps(4 * jj, 4)
            return inner

        n_quads = n_far // 4
        lax.fori_loop(0, n_quads, far_quad, 0)

        @pl.when(n_far - 4 * n_quads >= 2)
        def _():
            far_steps(4 * n_quads, 2)

        def biased_steps(first_tile):
            j0 = (i - 1 + first_tile) * nsub
            count = (2 - first_tile) * nsub

            def geometry(c):
                d, r = first_tile + c // nsub, (c % nsub) * tk
                return d, r, (r if d == 1 else 0)

            for c in range(count):
                if c + 1 < count:
                    q_next = geometry(c + 1)[2]
                    bufs[(c + 1) % 2][:, :2 * (tq - q_next)] = scores(j0 + c + 1, q_lo=q_next)
                d, r, q_lo = geometry(c)
                no_bias = d == 0 and r + tk - 1 - tq <= -FAR_BIAS_DISTANCE
                bias = None if no_bias else bias_ref[d, r:r + tk, q_lo:]
                softmax_pv(bufs[c % 2], j0 + c, bias, q_lo)

        def finish():
            lam = (jnp.exp(jnp.sum(lq1_ref[...] * lk1_ref[...], axis=-1, keepdims=True))
                   - jnp.exp(jnp.sum(lq2_ref[...] * lk2_ref[...], axis=-1, keepdims=True)) + lam_init)
            on = acc_scr[:A_DV, :] / acc_scr[A_DV:A_DV + 1, :]
            o = on[:, :tq] - lam * on[:, tq:]
            y = o * lax.rsqrt(jnp.mean(o * o, axis=0, keepdims=True) + EPS) * g_ref[...] * (1.0 - lam_init)
            o_ref[tile_rows(i), :] = y.T.astype(BF16)

        last = i + 1 == nq
        for first_tile, applies in ((0, i >= 1), (1, i == 0)):
            @pl.when(applies & jnp.logical_not(last))
            def _():
                biased_steps(first_tile)
                sa_scr[...] = scores(0, two_map_queries(i + 1))
                finish()

            @pl.when(applies & last)
            def _():
                biased_steps(first_tile)
                finish()
        return carry

    lax.fori_loop(0, nq, query_tile, 0)


def _attention(proj, bias_tiles, lq1, lk1, lq2, lk2, g_sub_col, batch, seq, lam_init, tq, tk=ATTN_TK):
    t = proj.shape[0]
    assert (tq // tk) % 2 == 0 and tq % tk == 0, "the score pipeline alternates two buffers per query tile"
    assert tq + 1 >= FAR_BIAS_DISTANCE, "key tiles two or more before the query tile must be past the bias horizon"
    kern = functools.partial(_attn_kernel, tq=tq, tk=tk, lam_init=lam_init)
    vec = lambda n: pl.BlockSpec((1, n), lambda b, h: (0, 0))
    return pl.pallas_call(
        kern,
        out_shape=jax.ShapeDtypeStruct((t, A_WIDTH), BF16),
        grid=(batch, A_HEADS),
        in_specs=[pl.BlockSpec((seq, A_DV), lambda b, h: (b, COL_QA + h)),
                  pl.BlockSpec((seq, A_DV), lambda b, h: (b, COL_KA + h)),
                  pl.BlockSpec((seq, A_DV), lambda b, h: (b, COL_VA + h)),
                  pl.BlockSpec((None, 2, tq, tq), lambda b, h: (h, 0, 0, 0)),
                  vec(A_DK), vec(A_DK), vec(A_DK), vec(A_DK),
                  pl.BlockSpec((A_DV, 1), lambda b, h: (0, 0))],
        out_specs=pl.BlockSpec((seq, A_DV), lambda b, h: (b, h)),
        scratch_shapes=[pltpu.VMEM((seq // tk, A_DV + ONES_ROWS, tk), BF16),
                        pltpu.VMEM((tk, 2 * tq), F32), pltpu.VMEM((tk, 2 * tq), F32),
                        pltpu.VMEM((1, 2 * tq), F32),
                        pltpu.VMEM((A_DV + ONES_ROWS, 2 * tq), F32)],
        compiler_params=_params("arbitrary", "arbitrary"),
        name="diff_attention",
    )(proj, proj, proj, bias_tiles, lq1, lk1, lq2, lk2, g_sub_col)


def _split3(a):
    a1 = a.astype(BF16)
    r1 = a - a1.astype(F32)
    a2 = r1.astype(BF16)
    return a1, a2, (r1 - a2.astype(F32)).astype(BF16)


def _sum3(x):
    return x[:, :B_DK] + x[:, B_DK:2 * B_DK] + x[:, 2 * B_DK:]


def _gla_kernel(q_ref, k_ref, v_ref, r_ref, z_ref, wa_ref, ba_ref, g_ref, o_ref,
                state_scr, mask_scr, kv_scr, st_scr, *, n_chunks):
    lc = n_chunks * CHUNK

    @pl.when(pl.program_id(2) == 0)
    def _():
        state_scr[...] = jnp.zeros(state_scr.shape, F32)
        row = lax.broadcasted_iota(jnp.int32, (lc, lc), 0)
        col = lax.broadcasted_iota(jnp.int32, (lc, lc), 1)
        same = (row // CHUNK) == (col // CHUNK)
        mask_scr[...] = (same & (row >= col)).astype(BF16)

    z = z_ref[...]
    zh = z.astype(BF16)
    zl = (z - zh.astype(F32)).astype(BF16)
    z3 = jnp.concatenate([zh, zl, zh], axis=1)
    kcols = lambda hh: slice(hh * B_DK, (hh + 1) * B_DK)
    vcols = lambda hh: slice(hh * B_DV, (hh + 1) * B_DV)
    chunk_rows = lambda c: slice(c * CHUNK, (c + 1) * CHUNK)
    live = [dict() for _ in range(GLA_HEADS_PER_STEP)]

    def gate_stage(hh):
        pre = jnp.dot(z3, wa_ref[:, kcols(hh)], preferred_element_type=F32) + ba_ref[:, kcols(hh)]
        log_a = (jnp.minimum(pre, 0.0) - jnp.log1p(jnp.exp(-jnp.abs(pre)))) * (1.0 / GATE_TAU)
        live[hh]["parts"] = jnp.concatenate(_split3(log_a), axis=1)

    def decay_stage(hh):
        cum = _sum3(jnp.dot(mask_scr[...], live[hh].pop("parts"), preferred_element_type=F32))
        totals = [cum[(c + 1) * CHUNK - 1:(c + 1) * CHUNK, :] for c in range(n_chunks)]
        total = jnp.concatenate([jnp.broadcast_to(tc, (CHUNK, B_DK)) for tc in totals], axis=0)
        live[hh]["totals"] = totals
        live[hh]["k_dec"] = (k_ref[:, kcols(hh)].astype(F32) * jnp.exp(total - cum)).astype(BF16)

    def kv_stage(hh):
        k_dec = live[hh].pop("k_dec")
        for c in range(n_chunks):
            kv_scr[hh, c] = lax.dot_general(v_ref[chunk_rows(c), vcols(hh)], k_dec[chunk_rows(c)],
                                            (((0,), (0,)), ((), ())), preferred_element_type=F32)

    def state_stage(hh):
        totals = live[hh].pop("totals")
        state = state_scr[hh]
        for c in range(n_chunks):
            state = state * jnp.exp(totals[c]) + kv_scr[hh, c]
            st_scr[hh, c] = state.astype(BF16)
        state_scr[hh] = state

    def output_stage(hh):
        for c in range(n_chunks):
            rows = chunk_rows(c)
            o = lax.dot_general(q_ref[rows, kcols(hh)], st_scr[hh, c], (((1,), (1,)), ((), ())),
                                preferred_element_type=F32) * (B_DK ** -0.5)
            o_ref[rows, vcols(hh)] = (_rms(o) * g_ref[...]
                                      * _silu(r_ref[rows, vcols(hh)].astype(F32))).astype(BF16)

    stages = (gate_stage, decay_stage, kv_stage, state_stage, output_stage)
    for tick in range(GLA_HEADS_PER_STEP + len(stages) - 1):
        for k, stage in enumerate(stages):
            if 0 <= tick - k < GLA_HEADS_PER_STEP:
                stage(tick - k)


def _gla(proj, zb, w_alpha_pad, b_alpha, g_norm, batch, seq, lc=GLA_LC):
    t = proj.shape[0]
    lc = min(lc, seq)
    nl = seq // lc
    n_chunks = lc // CHUNK
    hps = GLA_HEADS_PER_STEP
    assert B_HEADS % hps == 0 and COL_QB % hps == 0 and COL_KB % hps == 0
    kern = functools.partial(_gla_kernel, n_chunks=n_chunks)
    wa_hi = w_alpha_pad.astype(BF16)
    wa_lo = (w_alpha_pad - wa_hi.astype(F32)).astype(BF16)
    wa3 = jnp.concatenate([wa_hi, wa_hi, wa_lo], axis=0)
    kblock = lambda col0: pl.BlockSpec((lc, hps * B_DK), lambda b, h, l: (b * nl + l, col0 // hps + h))
    vblock = lambda col0: pl.BlockSpec((lc, hps * B_DV), lambda b, h, l: (b * nl + l, col0 // hps + h))
    return pl.pallas_call(
        kern,
        out_shape=jax.ShapeDtypeStruct((t, B_WIDTH), BF16),
        grid=(batch, B_HEADS // hps, nl),
        in_specs=[kblock(COL_QB), kblock(COL_KB), vblock(COL_VB256), vblock(COL_RB256),
                  pl.BlockSpec((lc, LANES), lambda b, h, l: (b * nl + l, 0)),
                  pl.BlockSpec((3 * LANES, hps * B_DK), lambda b, h, l: (0, h)),
                  pl.BlockSpec((1, hps * B_DK), lambda b, h, l: (0, h)),
                  pl.BlockSpec((1, B_DV), lambda b, h, l: (0, 0))],
        out_specs=pl.BlockSpec((lc, hps * B_DV), lambda b, h, l: (b * nl + l, h)),
        scratch_shapes=[pltpu.VMEM((hps, B_DV, B_DK), F32),
                        pltpu.VMEM((lc, lc), BF16),
                        pltpu.VMEM((hps, n_chunks, B_DV, B_DK), F32),
                        pltpu.VMEM((hps, n_chunks, B_DV, B_DK), BF16)],
        compiler_params=_params("arbitrary", "arbitrary", "arbitrary"),
        name="gla",
    )(proj, proj, proj, proj, zb, wa3, b_alpha, g_norm)


def _outproj_kernel(oa_ref, ob_ref, wo_ref, x_ref, gt_ref, gpost_ref, gpre_ref, sc_ref, sh_ref, wr_ref,
                    br_ref, x1_ref, h2_ref, lg_ref, *, tm):
    n_pieces = tm // OUTPROJ_PIECE

    def project(p):
        prow = slice(p * OUTPROJ_PIECE, (p + 1) * OUTPROJ_PIECE)
        return (jnp.dot(oa_ref[prow, :], wo_ref[:A_WIDTH, :], preferred_element_type=F32)
                + jnp.dot(ob_ref[prow, :], wo_ref[A_WIDTH:, :], preferred_element_type=F32))

    y_next = project(0)
    for p in range(n_pieces):
        p0 = p * OUTPROJ_PIECE
        prow = slice(p0, p0 + OUTPROJ_PIECE)
        y = y_next
        if p + 1 < n_pieces:
            y_next = project(p + 1)
        his, los = [], []
        for c in range(OUTPROJ_PIECE // ROW_CHUNK):
            r0 = p0 + c * ROW_CHUNK
            rows = slice(r0, r0 + ROW_CHUNK)
            x1 = x_ref[rows, :] + gt_ref[0] * (_rms(y[c * ROW_CHUNK:(c + 1) * ROW_CHUNK]) * gpost_ref[...])
            x1_ref[rows, :] = x1
            h2 = (_rms(x1) * gpre_ref[...]) * (1.0 + sc_ref[0]) + sh_ref[0]
            hi = h2.astype(BF16)
            his.append(hi)
            los.append((h2 - hi.astype(F32)).astype(BF16))
            _store_row_slabs(h2_ref, r0, ROW_CHUNK, h2)
        hi, lo = jnp.concatenate(his, axis=0), jnp.concatenate(los, axis=0)
        hw = jnp.dot(hi, wr_ref[...], preferred_element_type=F32)
        lw = jnp.dot(lo, wr_ref[:, :LANES], preferred_element_type=F32)
        lg_ref[prow, :] = hw[:, :LANES] + hw[:, LANES:] + lw + br_ref[...]


def _outproj(oa, ob, w_out, x2d, gt, g_post, g_pre, sc, sh, w_router, b_router, seq, tm=OUTPROJ_TM):
    t, d = x2d.shape
    tm = min(tm, seq)
    per_b = seq // tm
    kern = functools.partial(_outproj_kernel, tm=tm)
    wr_hi = w_router.astype(BF16)
    wr_lo = (w_router - wr_hi.astype(F32)).astype(BF16)
    wr_cat = jnp.concatenate([wr_hi, wr_lo], axis=1)
    row = lambda: pl.BlockSpec((1, d), lambda i: (0, 0))
    per_batch = lambda: pl.BlockSpec((1, 1, d), lambda i: (i // per_b, 0, 0))
    return pl.pallas_call(
        kern,
        out_shape=(jax.ShapeDtypeStruct((t, d), F32),
                   jax.ShapeDtypeStruct((t * ROW_SLABS, LANES), U32),
                   jax.ShapeDtypeStruct((t, LANES), F32)),
        grid=(t // tm,),
        in_specs=[pl.BlockSpec((tm, A_WIDTH), lambda i: (i, 0)),
                  pl.BlockSpec((tm, B_WIDTH), lambda i: (i, 0)),
                  pl.BlockSpec((d, d), lambda i: (0, 0)),
                  pl.BlockSpec((tm, d), lambda i: (i, 0)),
                  per_batch(), row(), row(), per_batch(), per_batch(),
                  pl.BlockSpec((d, 2 * LANES), lambda i: (0, 0)),
                  pl.BlockSpec((1, LANES), lambda i: (0, 0))],
        out_specs=(pl.BlockSpec((tm, d), lambda i: (i, 0)),
                   pl.BlockSpec((tm * ROW_SLABS, LANES), lambda i: (i, 0)),
                   pl.BlockSpec((tm, LANES), lambda i: (i, 0))),
        compiler_params=_params("arbitrary"),
        name="out_proj",
    )(oa, ob, w_out, x2d, gt, g_post, g_pre, sc, sh, wr_cat, b_router)


def _route_kernel(lg_ref, rec_ref, rows_ref, cnt_ref, carry_scr, before_scr, *, tr):
    @pl.when(pl.program_id(0) == 0)
    def _():
        carry_scr[...] = jnp.zeros(carry_scr.shape, F32)
        earlier = lax.broadcasted_iota(jnp.int32, (tr, tr), 0)
        token = lax.broadcasted_iota(jnp.int32, (tr, tr), 1)
        before_scr[...] = (earlier < token).astype(BF16)

    lg = lg_ref[...].T[:ROUTER_ROWS]
    row = lax.broadcasted_iota(jnp.int32, lg.shape, 0)
    big = jnp.int32(ROUTER_ROWS)

    def first_row(mask):
        return jnp.min(jnp.where(mask, row, big), axis=0, keepdims=True)

    gmask = row < N_GROUPS
    gmax = jnp.max(jnp.where(gmask, lg, -jnp.inf), axis=0, keepdims=True)
    gexp = jnp.where(gmask, jnp.exp(lg - gmax), 0.0)
    gprob = gexp / jnp.sum(gexp, axis=0, keepdims=True)
    g_val = jnp.max(gprob, axis=0, keepdims=True)
    g_idx = first_row(gmask & (gprob == g_val))

    lo = ROUTER_EXPERT_LANE0 + g_idx * EXPERTS_PER_GROUP
    emask = (row >= lo) & (row < lo + EXPERTS_PER_GROUP)
    emax = jnp.max(jnp.where(emask, lg, -jnp.inf), axis=0, keepdims=True)
    eexp = jnp.where(emask, jnp.exp(lg - emax), 0.0)
    eprob = eexp / jnp.sum(eexp, axis=0, keepdims=True)
    v1 = jnp.max(eprob, axis=0, keepdims=True)
    i1 = first_row(emask & (eprob == v1))
    rest = emask & (row != i1)
    v2 = jnp.max(jnp.where(rest, eprob, -1.0), axis=0, keepdims=True)
    i2 = first_row(rest & (eprob == v2))
    w1 = g_val * (v1 / (v1 + v2))
    w2 = g_val * (v2 / (v1 + v2))

    hit1 = row == i1
    hit2 = row == i2
    onehot = (hit1 | hit2).astype(BF16)
    pos = carry_scr[...] + jnp.dot(onehot, before_scr[...], preferred_element_type=F32)
    rank1 = jnp.sum(jnp.where(hit1, pos, 0.0), axis=0, keepdims=True)
    rank2 = jnp.sum(jnp.where(hit2, pos, 0.0), axis=0, keepdims=True)
    carry_scr[...] = carry_scr[...] + jnp.sum(onehot.astype(F32), axis=1, keepdims=True)
    cnt_ref[...] = carry_scr[...]

    e1 = (i1 - ROUTER_EXPERT_LANE0).astype(F32)
    e2 = (i2 - ROUTER_EXPERT_LANE0).astype(F32)
    field = lax.broadcasted_iota(jnp.int32, (LANES, tr), 0)
    rec = jnp.zeros((LANES, tr), F32)
    for ln, val in ((ROUTE_E1, e1), (ROUTE_E2, e2), (ROUTE_W1, w1), (ROUTE_W2, w2),
                    (ROUTE_R1, rank1), (ROUTE_R2, rank2)):
        rec = jnp.where(field == ln, val, rec)
    rec_ref[...] = rec.T
    rows_ref[...] = rec[:ROUTE_FIELD_ROWS]


def _route(logits, tr=ROUTE_TR):
    t = logits.shape[0]
    tr = min(tr, t)
    kern = functools.partial(_route_kernel, tr=tr)
    return pl.pallas_call(
        kern,
        out_shape=(jax.ShapeDtypeStruct((t, LANES), F32), jax.ShapeDtypeStruct((ROUTE_FIELD_ROWS, t), F32),
                   jax.ShapeDtypeStruct((ROUTER_ROWS, 1), F32)),
        grid=(t // tr,),
        in_specs=[pl.BlockSpec((tr, LANES), lambda i: (i, 0))],
        out_specs=(pl.BlockSpec((tr, LANES), lambda i: (i, 0)),
                   pl.BlockSpec((ROUTE_FIELD_ROWS, tr), lambda i: (0, i)),
                   pl.BlockSpec((ROUTER_ROWS, 1), lambda i: (0, 0))),
        scratch_shapes=[pltpu.VMEM((ROUTER_ROWS, 1), F32), pltpu.VMEM((tr, tr), BF16)],
        compiler_params=_params("arbitrary"),
        name="route",
    )(logits)


def _slab_rows(ref, row):
    return ref.at[pl.ds(pl.multiple_of(row * ROW_SLABS, ROW_SLABS), ROW_SLABS), :]


def _dispatch_kernel(slot_ref, pad_start_ref, pad_len_ref, used_ref, h2_ref, xs_hbm, zero_scr, sems, pad_sems,
                     *, td, tm):
    g = pl.program_id(0)
    tile_rows = tm * ROW_SLABS
    n_tiles = xs_hbm.shape[0] // tile_rows

    def zero_copy(slot, nslots, sem):
        rows = pl.ds(pl.multiple_of(slot * ROW_SLABS, ROW_SLABS), nslots * ROW_SLABS)
        return pltpu.make_async_copy(zero_scr.at[pl.ds(0, nslots * ROW_SLABS), :], xs_hbm.at[rows, :], sem)

    pad_sizes = [1 << b for b in reversed(range((tm - 1).bit_length()))]

    @pl.when(g == 0)
    def _():
        zero_scr[...] = jnp.zeros(zero_scr.shape, U32)

        def unused_tile(tile, carry):
            zero_copy(tile * tm, tm, sems.at[1]).start()
            zero_copy(tile * tm, tm, sems.at[1]).wait()
            return carry

        lax.fori_loop(used_ref[0], n_tiles, unused_tile, 0)

        def per_expert(e, counts):
            off = pad_start_ref[e]
            n = pad_len_ref[e]
            new_counts = []
            for b, size in enumerate(pad_sizes):
                hit = (n & size) != 0

                @pl.when(hit)
                def _():
                    zero_copy(off, size, pad_sems.at[b]).start()

                off = off + jnp.where(hit, size, 0)
                new_counts.append(counts[b] + hit.astype(jnp.int32))
            return tuple(new_counts)

        counts = lax.fori_loop(0, N_EXPERTS, per_expert, tuple(jnp.int32(0) for _ in pad_sizes))
        for b, size in enumerate(pad_sizes):
            def drain(r, c):
                zero_copy(0, size, pad_sems.at[b]).wait()
                return c
            lax.fori_loop(0, counts[b], drain, 0)

    def row_copy(r, slot):
        return pltpu.make_async_copy(_slab_rows(h2_ref, r), _slab_rows(xs_hbm, slot), sems.at[0])

    base = g * td
    n_tok = slot_ref.shape[0] // 2

    def issue(r, c):
        tok = base + r
        row_copy(r, slot_ref[tok]).start()
        row_copy(r, slot_ref[n_tok + tok]).start()
        return c

    lax.fori_loop(0, td, issue, 0, unroll=CHUNK_UNROLL)
    for _ in range(2):
        pltpu.make_async_copy(h2_ref, xs_hbm.at[pl.ds(0, td * ROW_SLABS), :], sems.at[0]).wait()


def _dispatch(slot, pad_start, pad_len, used, h2_rows, n_slots, tm, td=DISPATCH_TD):
    t = slot.shape[0] // 2
    td = min(td, t)
    kern = functools.partial(_dispatch_kernel, td=td, tm=tm)
    grid_spec = pltpu.PrefetchScalarGridSpec(
        num_scalar_prefetch=4,
        grid=(t // td,),
        in_specs=[pl.BlockSpec((td * ROW_SLABS, LANES), lambda g, sl, ps, pn, us: (g, 0))],
        out_specs=pl.BlockSpec(memory_space=pl.ANY),
        scratch_shapes=[pltpu.VMEM((tm * ROW_SLABS, LANES), U32), pltpu.SemaphoreType.DMA((2,)),
                        pltpu.SemaphoreType.DMA(((tm - 1).bit_length(),))],
    )
    return pl.pallas_call(
        kern,
        out_shape=jax.ShapeDtypeStruct((n_slots * ROW_SLABS, LANES), U32),
        grid_spec=grid_spec,
        compiler_params=_params("arbitrary"),
        name="dispatch",
    )(slot, pad_start, pad_len, used, h2_rows)


TILE_UNUSED, TILE_USED, TILE_NEW_EXPERT = 0, 1, 2


def _expert_kernel(texp_ref, tblk_ref, tstate_ref, tnext_ref, tpar_ref, xs_ref, w1_hbm, w3_hbm, w2_hbm, eo_ref,
                   x_scr, w1_scr, w3_scr, w2_scr, w1_stage, w3_stage, w2_stage, sems, *, tm):
    i = pl.program_id(0)
    state = tstate_ref[i]
    slot = tpar_ref[i]

    def weight_copies(expert, dst_slot):
        return [pltpu.make_async_copy(hbm.at[expert], stage.at[dst_slot], sems.at[dst_slot])
                for hbm, stage in ((w1_hbm, w1_stage), (w3_hbm, w3_stage), (w2_hbm, w2_stage))]

    @pl.when(state == TILE_UNUSED)
    def _():
        eo_ref[...] = jnp.zeros(eo_ref.shape, U32)

    @pl.when(i == 0)
    def _():
        for cp in weight_copies(texp_ref[0], slot):
            cp.start()

    @pl.when(state == TILE_NEW_EXPERT)
    def _():
        for cp in weight_copies(texp_ref[i], slot):
            cp.wait()

        @pl.when(tnext_ref[i] >= 0)
        def _():
            for cp in weight_copies(tnext_ref[i], 1 - slot):
                cp.start()

        w1_scr[...] = w1_stage[slot].astype(BF16)
        w3_scr[...] = w3_stage[slot].astype(BF16)
        w2_scr[...] = w2_stage[slot].astype(BF16)

    @pl.when(state != TILE_UNUSED)
    def _():
        for s in range(ROW_SLABS):
            lo, hi = _unpack_bf16_pair(xs_ref[pl.ds(s, tm, stride=ROW_SLABS), :])
            x_scr[:, s * LANES:(s + 1) * LANES] = lo.astype(BF16)
            x_scr[:, HALF_D + s * LANES:HALF_D + (s + 1) * LANES] = hi.astype(BF16)
        x = x_scr[...]
        a = jnp.dot(x, w1_scr[...], preferred_element_type=F32)
        b = jnp.dot(x, w3_scr[...], preferred_element_type=F32)
        hid = (_silu(a) * b).astype(BF16)
        y = jnp.dot(hid, w2_scr[...], preferred_element_type=F32)
        _store_row_slabs(eo_ref, 0, tm, y)


def _experts(tile_expert, tile_block, tile_state, tile_next, tile_slot, xs_rows, w1, w3, w2, tm):
    n_tiles = tile_expert.shape[0]
    d, f = w1.shape[1], w1.shape[2]
    kern = functools.partial(_expert_kernel, tm=tm)
    grid_spec = pltpu.PrefetchScalarGridSpec(
        num_scalar_prefetch=5,
        grid=(n_tiles,),
        in_specs=[pl.BlockSpec((tm * ROW_SLABS, LANES), lambda i, te, tb, ts, tn, tp: (tb[i], 0)),
                  pl.BlockSpec(memory_space=pl.ANY), pl.BlockSpec(memory_space=pl.ANY),
                  pl.BlockSpec(memory_space=pl.ANY)],
        out_specs=pl.BlockSpec((tm * ROW_SLABS, LANES), lambda i, te, tb, ts, tn, tp: (i, 0)),
        scratch_shapes=[pltpu.VMEM((tm, d), BF16), pltpu.VMEM((d, f), BF16), pltpu.VMEM((d, f), BF16),
                        pltpu.VMEM((f, d), BF16),
                        pltpu.VMEM((2, d, f), F32), pltpu.VMEM((2, d, f), F32), pltpu.VMEM((2, f, d), F32),
                        pltpu.SemaphoreType.DMA((2,))],
    )
    return pl.pallas_call(
        kern,
        out_shape=jax.ShapeDtypeStruct(xs_rows.shape, U32),
        grid_spec=grid_spec,
        compiler_params=_params("arbitrary"),
        name="expert_mlp",
    )(tile_expert, tile_block, tile_state, tile_next, tile_slot, xs_rows, w1, w3, w2)


def _final_kernel(slot_ref, eo_hbm, rec_ref, x1_ref, gt_ref, g_ref, o_ref, e_scr, sems, *, tf):
    i = pl.program_id(0)
    par = i % 2
    n_tok = slot_ref.shape[0] // 2

    def start_all(step, buf):
        def body(r, c):
            tok = step * tf + r
            for k in range(2):
                pltpu.make_async_copy(_slab_rows(eo_hbm, slot_ref[k * n_tok + tok]),
                                      _slab_rows(e_scr.at[buf, k], r), sems.at[buf]).start()
            return c
        lax.fori_loop(0, tf, body, 0, unroll=CHUNK_UNROLL)

    def wait_all(buf):
        for k in range(2):
            pltpu.make_async_copy(eo_hbm.at[pl.ds(0, tf * ROW_SLABS), :], e_scr.at[buf, k], sems.at[buf]).wait()

    @pl.when(i == 0)
    def _():
        start_all(0, 0)

    @pl.when(i + 1 < pl.num_programs(0))
    def _():
        start_all(i + 1, 1 - par)

    wait_all(par)

    def chunk(c, carry):
        r0 = pl.multiple_of(c * ROW_CHUNK, ROW_CHUNK)
        rows = pl.ds(r0, ROW_CHUNK)
        rec = rec_ref[rows, :]
        w1 = rec[:, ROUTE_W1:ROUTE_W1 + 1]
        w2 = rec[:, ROUTE_W2:ROUTE_W2 + 1]
        lo1, hi1 = _load_row_slabs(e_scr.at[par, 0], r0, ROW_CHUNK)
        lo2, hi2 = _load_row_slabs(e_scr.at[par, 1], r0, ROW_CHUNK)
        y = jnp.concatenate([w1 * lo1 + w2 * lo2, w1 * hi1 + w2 * hi2], axis=1)
        o_ref[rows, :] = x1_ref[rows, :] + gt_ref[0] * (_rms(y) * g_ref[...])
        return carry

    lax.fori_loop(0, tf // ROW_CHUNK, chunk, 0, unroll=CHUNK_UNROLL)


def _final(slot, eo_rows, rec, x1, gt, g_post, seq, tf=COMBINE_TF):
    t, d = x1.shape
    tf = min(tf, seq)
    per_b = seq // tf
    kern = functools.partial(_final_kernel, tf=tf)
    grid_spec = pltpu.PrefetchScalarGridSpec(
        num_scalar_prefetch=1,
        grid=(t // tf,),
        in_specs=[pl.BlockSpec(memory_space=pl.ANY),
                  pl.BlockSpec((tf, LANES), lambda i, sl: (i, 0)),
                  pl.BlockSpec((tf, d), lambda i, sl: (i, 0)),
                  pl.BlockSpec((1, 1, d), lambda i, sl: (i // per_b, 0, 0)),
                  pl.BlockSpec((1, d), lambda i, sl: (0, 0))],
        out_specs=pl.BlockSpec((tf, d), lambda i, sl: (i, 0)),
        scratch_shapes=[pltpu.VMEM((2, 2, tf * ROW_SLABS, LANES), U32),
                        pltpu.SemaphoreType.DMA((2,))],
    )
    return pl.pallas_call(
        kern,
        out_shape=jax.ShapeDtypeStruct((t, d), F32),
        grid_spec=grid_spec,
        compiler_params=_params("arbitrary"),
        name="combine_final",
    )(slot, eo_rows, rec, x1, gt, g_post)


def _dispatch_tables(rows, counts, t, tm):
    e = rows[ROUTE_E1:ROUTE_E2 + 1].astype(jnp.int32)
    rank = rows[ROUTE_R1:ROUTE_R2 + 1].astype(jnp.int32)
    cnt = counts[ROUTER_EXPERT_LANE0:ROUTER_EXPERT_LANE0 + N_EXPERTS, 0].astype(jnp.int32)
    tiles_per = (cnt + tm - 1) // tm
    tile_end = jnp.cumsum(tiles_per)
    tile_start = tile_end - tiles_per
    n_tiles = (2 * t + N_EXPERTS * (tm - 1)) // tm
    experts = jnp.arange(N_EXPERTS, dtype=jnp.int32)
    start_of = jnp.sum(jnp.where(e[None] == experts[:, None, None], tile_start[:, None, None], 0),
                       axis=0)
    slot = (start_of * tm + rank).reshape(-1)
    pad_start = tile_start * tm + cnt
    pad_len = tiles_per * tm - cnt
    tile_id = jnp.arange(n_tiles, dtype=jnp.int32)
    used = tile_end[-1]
    tblk = jnp.minimum(tile_id, used - 1)
    texp = jnp.sum(tile_end[None, :] <= tblk[:, None], axis=-1).astype(jnp.int32)
    tstate = jnp.where(tile_id < used, jnp.where(tile_id == tile_start[texp], TILE_NEW_EXPERT, TILE_USED),
                       TILE_UNUSED).astype(jnp.int32)
    nonempty = cnt > 0
    ordinal = jnp.cumsum(nonempty.astype(jnp.int32)) - 1
    later = jnp.where(nonempty[None, :] & (experts[None, :] > experts[:, None]), experts[None, :], N_EXPERTS)
    next_expert = jnp.min(later, axis=-1)
    next_expert = jnp.where(next_expert == N_EXPERTS, -1, next_expert)
    tnext = next_expert[texp].astype(jnp.int32)
    tslot = (ordinal[texp] % 2).astype(jnp.int32)
    return slot, pad_start, pad_len, used.reshape(1), texp, tblk, tstate, tnext, tslot, n_tiles * tm


def kernel(x, c, rel_bias, w_ada, b_ada, g_pre_mix, g_post_mix, w_in, w_alpha, b_alpha, lam_q1, lam_k1, lam_q2,
           lam_k2, g_sub_a, g_norm_b, w_out, g_pre_ffn, g_post_ffn, w_router_g, b_router_g, w_router_e,
           b_router_e, w1, w3, w2):
    batch, seq, d = x.shape
    t = batch * seq
    depth = w_in.shape[0]
    tq = min(ATTN_TQ, seq)
    tm_e = EXPERT_TM
    xf = x.reshape(t, d)
    for i in range(depth):
        lam_init = 0.8 - 0.6 * math.exp(-0.3 * i)
        c_pad = jnp.pad(c, ((0, 8 - batch % 8 if batch % 8 else 0), (0, 0)))
        ada = _ada(c_pad, w_ada[i], b_ada[i][None, :])[:batch]
        sh_m, sc_m, gt_m, sh_f, sc_f, gt_f = [a[:, None, :] for a in jnp.split(ada, 6, axis=-1)]

        w_in_b = w_in[i].astype(BF16)
        w_z = jnp.pad(w_in_b[:, D_MAIN:], ((0, 0), (0, LANES - GATE_RANK)))
        proj, zb = _inproj(xf, g_pre_mix[i][None, :], sc_m, sh_m, w_in_b, w_z, seq)

        oa = _attention(proj, _bias_tiles(rel_bias, tq), lam_q1[i][None, :], lam_k1[i][None, :],
                        lam_q2[i][None, :], lam_k2[i][None, :], g_sub_a[i][:, None], batch, seq, lam_init, tq)
        w_alpha_pad = jnp.pad(w_alpha[i], ((0, LANES - GATE_RANK), (0, 0)))
        ob = _gla(proj, zb, w_alpha_pad, b_alpha[i][None, :], g_norm_b[i][None, :], batch, seq)

        w_router = jnp.pad(jnp.concatenate([w_router_g[i], w_router_e[i]], axis=1),
                           ((0, 0), (0, LANES - N_GROUPS - N_EXPERTS)))
        b_router = jnp.pad(jnp.concatenate([b_router_g[i], b_router_e[i]]),
                           (0, LANES - N_GROUPS - N_EXPERTS))[None, :]
        x1, h2_rows, logits = _outproj(oa, ob, w_out[i].astype(BF16), xf, gt_m, g_post_mix[i][None, :],
                                       g_pre_ffn[i][None, :], sc_f, sh_f, w_router, b_router, seq)

        rec, rec_rows, counts = _route(logits)
        (slot, pad_start, pad_len, used, texp, tblk, tstate, tnext, tslot,
         n_slots) = _dispatch_tables(rec_rows, counts, t, tm_e)
        xs = _dispatch(slot, pad_start, pad_len, used, h2_rows, n_slots, tm_e)
        eo = _experts(texp, tblk, tstate, tnext, tslot, xs, w1[i], w3[i], w2[i], tm_e)
        xf = _final(slot, eo, rec, x1, gt_f, g_post_ffn[i][None, :], seq)
    return xf.reshape(batch, seq, d)
```

```python
import functools
import math

import jax
import jax.numpy as jnp
from jax import lax
from jax.experimental import pallas as pl
from jax.experimental.pallas import tpu as pltpu

F32 = jnp.float32
BF16 = jnp.bfloat16

D_MODEL = 2048
CHUNK = 64
A_HEADS = 8
A_DK = 64
A_DV = 2 * A_DK
A_WIDTH = A_HEADS * A_DV
B_HEADS = 4
B_WIDTH = D_MODEL - A_WIDTH
B_DV = B_WIDTH // B_HEADS
B_DK = B_DV // 2
GATE_RANK = 16
GATE_TAU = 16.0
N_BUCKETS = 32
MAX_DISTANCE = 256
N_GROUPS = 4
EXPERTS_PER_GROUP = 8
N_EXPERTS = N_GROUPS * EXPERTS_PER_GROUP
EPS = 1e-6
NEG_INF = -1e30
LOG2E = math.log2(math.e)

LANES = 128
U32 = jnp.uint32
HALF_D = D_MODEL // 2
ROW_SLABS = HALF_D // LANES
ROW_CHUNK = 16
CHUNK_UNROLL = 8
ONES_ROWS = 16
FAR_BIAS_DISTANCE = 166

ADA_TN = 1024
INPROJ_TM, INPROJ_TN = 512, 1024
ATTN_TQ, ATTN_TK = 512, 256
GLA_LC = 512
GLA_HEADS_PER_STEP = 4
OUTPROJ_TM = 512
OUTPROJ_PIECE = 256
ROUTE_TR = 1024
EXPERT_TM = 512
DISPATCH_TD = 2048
COMBINE_TF = 256
D_MAIN = 3 * A_WIDTH + 2 * B_HEADS * B_DK + 2 * B_WIDTH
COL_QA, COL_KA, COL_VA = 0, A_HEADS, 2 * A_HEADS
COL_QB = 3 * A_HEADS
COL_KB = COL_QB + B_HEADS
COL_VB256 = (3 * A_WIDTH + 2 * B_HEADS * B_DK) // B_DV
COL_RB256 = COL_VB256 + B_HEADS
ROUTE_E1, ROUTE_E2, ROUTE_W1, ROUTE_W2, ROUTE_R1, ROUTE_R2 = 0, 1, 2, 3, 4, 5
ROUTE_FIELD_ROWS = 8
ROUTER_EXPERT_LANE0 = N_GROUPS
ROUTER_ROWS = 48

VMEM_LIMIT = 56 * 1024 * 1024
INPROJ_VMEM_LIMIT = 60 * 1024 * 1024


def _params(*sem):
    return pltpu.CompilerParams(dimension_semantics=sem, vmem_limit_bytes=VMEM_LIMIT)


def _rms(v):
    return v * lax.rsqrt(jnp.mean(v * v, axis=-1, keepdims=True) + EPS)


def _silu(v):
    return v * jax.nn.sigmoid(v)


_HIGH_HALF = 0xFFFF0000


def _pack_bf16_pair(lo, hi):
    lo_bits = lax.bitcast_convert_type(lo.astype(BF16).astype(F32), U32) >> 16
    hi_bits = lax.bitcast_convert_type(hi.astype(BF16).astype(F32), U32) & U32(_HIGH_HALF)
    return hi_bits | lo_bits


def _unpack_bf16_pair(w):
    return (lax.bitcast_convert_type(w << 16, F32), lax.bitcast_convert_type(w & U32(_HIGH_HALF), F32))


def _store_row_slabs(ref, r0, nrows, rows_f32):
    packed = _pack_bf16_pair(rows_f32[:, :HALF_D], rows_f32[:, HALF_D:])
    for s in range(ROW_SLABS):
        ref[pl.ds(r0 * ROW_SLABS + s, nrows, stride=ROW_SLABS), :] = packed[:, s * LANES:(s + 1) * LANES]


def _load_row_slabs(ref, r0, nrows):
    slabs = [_unpack_bf16_pair(ref[pl.ds(r0 * ROW_SLABS + s, nrows, stride=ROW_SLABS), :]) for s in range(ROW_SLABS)]
    return (jnp.concatenate([lo for lo, _ in slabs], axis=1), jnp.concatenate([hi for _, hi in slabs], axis=1))


def _ada_kernel(c_ref, w_ref, b_ref, o_ref):
    s = _silu(c_ref[...])
    o_ref[...] = jnp.dot(s.astype(BF16), w_ref[...].astype(BF16), preferred_element_type=F32) + b_ref[...]


def _ada(c_pad, w, b, tn=ADA_TN):
    m, d = c_pad.shape
    n = w.shape[1]
    return pl.pallas_call(
        _ada_kernel,
        out_shape=jax.ShapeDtypeStruct((m, n), F32),
        grid=(n // tn,),
        in_specs=[pl.BlockSpec((m, d), lambda j: (0, 0)),
                  pl.BlockSpec((d, tn), lambda j: (0, j)),
                  pl.BlockSpec((1, tn), lambda j: (0, j))],
        out_specs=pl.BlockSpec((m, tn), lambda j: (0, j)),
        compiler_params=_params("arbitrary"),
        name="ada_proj",
    )(c_pad, w, b)


def _inproj_kernel(x_ref, xn_ref, g_ref, sc_ref, sh_ref, scn_ref, shn_ref, w_ref, wz_ref, o_ref, z_ref,
                   h_scr, hn_scr, *, tm, tn):
    def normed(src_ref, scale_ref, shift_ref, rows):
        h = _rms(src_ref[rows, :]) * g_ref[...]
        return (h * (1.0 + scale_ref[0]) + shift_ref[0]).astype(BF16)

    @pl.when(pl.program_id(0) == 0)
    def _():
        def chunk(c, carry):
            rows = pl.ds(pl.multiple_of(c * ROW_CHUNK, ROW_CHUNK), ROW_CHUNK)
            h_scr[rows, :] = normed(x_ref, sc_ref, sh_ref, rows)
            return carry
        lax.fori_loop(0, tm // ROW_CHUNK, chunk, 0, unroll=CHUNK_UNROLL)

    h = h_scr[...]
    z_ref[...] = jnp.dot(h, wz_ref[...], preferred_element_type=F32)
    n_col = D_MAIN // tn
    rows_per_col = tm // n_col // ROW_CHUNK * ROW_CHUNK
    next_row = 0
    for c in range(n_col):
        cols = slice(c * tn, (c + 1) * tn)
        o_ref[:, cols] = jnp.dot(h, w_ref[:, cols], preferred_element_type=F32).astype(BF16)
        stop = tm if c == n_col - 1 else next_row + rows_per_col
        for r0 in range(next_row, stop, ROW_CHUNK):
            rows = slice(r0, r0 + ROW_CHUNK)
            hn_scr[rows, :] = normed(xn_ref, scn_ref, shn_ref, rows)
        next_row = stop
    h_scr[...] = hn_scr[...]


def _inproj(x2d, g, sc, sh, w_all, w_z, seq, tm=INPROJ_TM, tn=INPROJ_TN):
    t, d = x2d.shape
    tm = min(tm, seq)
    per_b = seq // tm
    n_steps = t // tm
    kern = functools.partial(_inproj_kernel, tm=tm, tn=tn)
    nxt = lambda i: jnp.minimum(i + 1, n_steps - 1)
    per_batch = lambda step: pl.BlockSpec((1, 1, d), lambda i: (step(i) // per_b, 0, 0))
    return pl.pallas_call(
        kern,
        out_shape=(jax.ShapeDtypeStruct((t, D_MAIN), BF16), jax.ShapeDtypeStruct((t, LANES), F32)),
        grid=(n_steps,),
        in_specs=[pl.BlockSpec((tm, d), lambda i: (i, 0)),
                  pl.BlockSpec((tm, d), lambda i: (nxt(i), 0)),
                  pl.BlockSpec((1, d), lambda i: (0, 0)),
                  per_batch(lambda i: i), per_batch(lambda i: i), per_batch(nxt), per_batch(nxt),
                  pl.BlockSpec(w_all.shape, lambda i: (0, 0), pipeline_mode=pl.Buffered(1)),
                  pl.BlockSpec((d, LANES), lambda i: (0, 0), pipeline_mode=pl.Buffered(1))],
        out_specs=(pl.BlockSpec((tm, D_MAIN), lambda i: (i, 0)),
                   pl.BlockSpec((tm, LANES), lambda i: (i, 0))),
        scratch_shapes=[pltpu.VMEM((tm, d), BF16), pltpu.VMEM((tm, d), BF16)],
        compiler_params=pltpu.CompilerParams(dimension_semantics=("arbitrary",),
                                             vmem_limit_bytes=INPROJ_VMEM_LIMIT),
        name="in_proj",
    )(x2d, x2d, g, sc, sh, sc, sh, w_all, w_z)


def _t5_bucket(rel):
    nb = N_BUCKETS // 2
    max_exact = nb // 2
    base = jnp.where(rel > 0, nb, 0)
    n = jnp.abs(rel)
    nf = jnp.maximum(n, 1).astype(F32)
    large = max_exact + (jnp.log(nf / max_exact) / math.log(MAX_DISTANCE / max_exact)
                         * (nb - max_exact)).astype(jnp.int32)
    large = jnp.minimum(large, nb - 1)
    return base + jnp.where(n < max_exact, n, large)


def _bias_buckets(tq):
    kj = jnp.arange(tq, dtype=jnp.int32)[:, None]
    qi = jnp.arange(tq, dtype=jnp.int32)[None, :]
    near = _t5_bucket(kj - qi - tq)
    diag = jnp.where((kj // CHUNK) <= (qi // CHUNK), _t5_bucket(kj - qi), N_BUCKETS)
    return jnp.stack([near, diag]).astype(jnp.int32)


def _bias_kernel(rb_ref, bk_ref, o_ref):
    h = pl.program_id(0)
    far = rb_ref[N_BUCKETS // 2 - 1, h]
    bucket = bk_ref[...]
    acc = jnp.full(bucket.shape, NEG_INF, F32)
    for n in range(N_BUCKETS):
        acc = jnp.where(bucket == n, (rb_ref[n, h] - far) * LOG2E, acc)
    o_ref[...] = acc


def _bias_tiles(rel_bias, tq):
    return pl.pallas_call(
        _bias_kernel,
        out_shape=jax.ShapeDtypeStruct((A_HEADS, 2, tq, tq), F32),
        grid=(A_HEADS, 2),
        in_specs=[pl.BlockSpec(memory_space=pltpu.SMEM),
                  pl.BlockSpec((None, tq, tq), lambda h, d: (d, 0, 0))],
        out_specs=pl.BlockSpec((None, None, tq, tq), lambda h, d: (h, d, 0, 0)),
        compiler_params=_params("arbitrary", "arbitrary"),
        name="bias_tiles",
    )(rel_bias, _bias_buckets(tq))


def _attn_kernel(q_ref, k_ref, v_ref, bias_ref, lq1_ref, lk1_ref, lq2_ref, lk2_ref, g_ref, o_ref,
                 vt_scr, sa_scr, sb_scr, m_scr, acc_scr, *, tq, tk, lam_init):
    nsub = tq // tk
    nq = q_ref.shape[0] // tq
    bufs = (sa_scr, sb_scr)

    ones = jnp.ones((ONES_ROWS, tk), BF16)
    for c in range(vt_scr.shape[0]):
        vt = v_ref[c * tk:(c + 1) * tk, :].astype(F32).T.astype(BF16)
        vt_scr[c] = jnp.concatenate([vt, ones], axis=0)

    lane = lax.broadcasted_iota(jnp.int32, (1, A_DV), 1)

    def query_tile(i, carry):
        def tile_rows(tile):
            return pl.ds(pl.multiple_of(tile * tq, tq), tq)

        def two_map_queries(tile):
            q = q_ref[tile_rows(tile), :] * (A_DK ** -0.5 * LOG2E)
            zero = jnp.zeros_like(q)
            return jnp.concatenate([jnp.where(lane < A_DK, q, zero), jnp.where(lane >= A_DK, q, zero)], axis=0)

        q2 = two_map_queries(i)
        m_scr[...] = jnp.full(m_scr.shape, NEG_INF, F32)
        acc_scr[...] = jnp.zeros(acc_scr.shape, F32)

        def scores(j, queries=q2, q_lo=0):
            k = k_ref[pl.ds(pl.multiple_of(j * tk, tk), tk), :]
            if q_lo:
                queries = jnp.concatenate([queries[q_lo:tq], queries[tq + q_lo:]], axis=0)
            return lax.dot_general(k, queries, (((1,), (1,)), ((), ())), preferred_element_type=F32)

        def softmax_pv(s_ref, j, bias, q_lo=0):
            width = tq - q_lo
            halves = (slice(q_lo, tq), slice(tq + q_lo, 2 * tq))
            both = lambda ref: ref[...] if not q_lo else jnp.concatenate([ref[:, h] for h in halves], axis=1)
            s = s_ref[:, :2 * width]
            if bias is not None:
                s = jnp.concatenate([s[:, :width] + bias, s[:, width:] + bias], axis=1)
            m_old = both(m_scr)
            m_new = jnp.maximum(m_old, jnp.max(s, axis=0, keepdims=True))
            alpha = jnp.exp2(m_old - m_new)
            p = jnp.exp2(s - m_new).astype(BF16)
            acc = alpha * both(acc_scr) + jnp.dot(vt_scr[j], p, preferred_element_type=F32)
            if not q_lo:
                acc_scr[...] = acc
                m_scr[...] = m_new
            else:
                for n, h in enumerate(halves):
                    acc_scr[:, h] = acc[:, n * width:(n + 1) * width]
                    m_scr[:, h] = m_new[:, n * width:(n + 1) * width]

        n_far = jnp.maximum(i - 1, 0) * nsub

        @pl.when(i == 0)
        def _():
            sa_scr[...] = scores(0)

        def far_steps(j, count):
            for c in range(count):
                bufs[(c + 1) % 2][...] = scores(j + c + 1)
                softmax_pv(bufs[c % 2], j + c, None)

        def far_quad(jj, inner):
            far_steps(4 * jj, 4)
            return inner

        n_quads = n_far // 4
        lax.fori_loop(0, n_quads, far_quad, 0)

        @pl.when(n_far - 4 * n_quads >= 2)
        def _():
            far_steps(4 * n_quads, 2)

        def biased_steps(first_tile):
            j0 = (i - 1 + first_tile) * nsub
            count = (2 - first_tile) * nsub

            def geometry(c):
                d, r = first_tile + c // nsub, (c % nsub) * tk
                return d, r, (r if d == 1 else 0)

            for c in range(count):
                if c + 1 < count:
                    q_next = geometry(c + 1)[2]
                    bufs[(c + 1) % 2][:, :2 * (tq - q_next)] = scores(j0 + c + 1, q_lo=q_next)
                d, r, q_lo = geometry(c)
                no_bias = d == 0 and r + tk - 1 - tq <= -FAR_BIAS_DISTANCE
                bias = None if no_bias else bias_ref[d, r:r + tk, q_lo:]
                softmax_pv(bufs[c % 2], j0 + c, bias, q_lo)

        def finish():
            lam = (jnp.exp(jnp.sum(lq1_ref[...] * lk1_ref[...], axis=-1, keepdims=True))
                   - jnp.exp(jnp.sum(lq2_ref[...] * lk2_ref[...], axis=-1, keepdims=True)) + lam_init)
            on = acc_scr[:A_DV, :] / acc_scr[A_DV:A_DV + 1, :]
            o = on[:, :tq] - lam * on[:, tq:]
            y = o * lax.rsqrt(jnp.mean(o * o, axis=0, keepdims=True) + EPS) * g_ref[...] * (1.0 - lam_init)
            o_ref[tile_rows(i), :] = y.T.astype(BF16)

        last = i + 1 == nq
        for first_tile, applies in ((0, i >= 1), (1, i == 0)):
            @pl.when(applies & jnp.logical_not(last))
            def _():
                biased_steps(first_tile)
                sa_scr[...] = scores(0, two_map_queries(i + 1))
                finish()

            @pl.when(applies & last)
            def _():
                biased_steps(first_tile)
                finish()
        return carry

    lax.fori_loop(0, nq, query_tile, 0)


def _attention(proj, bias_tiles, lq1, lk1, lq2, lk2, g_sub_col, batch, seq, lam_init, tq, tk=ATTN_TK):
    t = proj.shape[0]
    assert (tq // tk) % 2 == 0 and tq % tk == 0, "the score pipeline alternates two buffers per query tile"
    assert tq + 1 >= FAR_BIAS_DISTANCE, "key tiles two or more before the query tile must be past the bias horizon"
    kern = functools.partial(_attn_kernel, tq=tq, tk=tk, lam_init=lam_init)
    vec = lambda n: pl.BlockSpec((1, n), lambda b, h: (0, 0))
    return pl.pallas_call(
        kern,
        out_shape=jax.ShapeDtypeStruct((t, A_WIDTH), BF16),
        grid=(batch, A_HEADS),
        in_specs=[pl.BlockSpec((seq, A_DV), lambda b, h: (b, COL_QA + h)),
                  pl.BlockSpec((seq, A_DV), lambda b, h: (b, COL_KA + h)),
                  pl.BlockSpec((seq, A_DV), lambda b, h: (b, COL_VA + h)),
                  pl.BlockSpec((None, 2, tq, tq), lambda b, h: (h, 0, 0, 0)),
                  vec(A_DK), vec(A_DK), vec(A_DK), vec(A_DK),
                  pl.BlockSpec((A_DV, 1), lambda b, h: (0, 0))],
        out_specs=pl.BlockSpec((seq, A_DV), lambda b, h: (b, h)),
        scratch_shapes=[pltpu.VMEM((seq // tk, A_DV + ONES_ROWS, tk), BF16),
                        pltpu.VMEM((tk, 2 * tq), F32), pltpu.VMEM((tk, 2 * tq), F32),
                        pltpu.VMEM((1, 2 * tq), F32),
                        pltpu.VMEM((A_DV + ONES_ROWS, 2 * tq), F32)],
        compiler_params=_params("arbitrary", "arbitrary"),
        name="diff_attention",
    )(proj, proj, proj, bias_tiles, lq1, lk1, lq2, lk2, g_sub_col)


def _split3(a):
    a1 = a.astype(BF16)
    r1 = a - a1.astype(F32)
    a2 = r1.astype(BF16)
    return a1, a2, (r1 - a2.astype(F32)).astype(BF16)


def _sum3(x):
    return x[:, :B_DK] + x[:, B_DK:2 * B_DK] + x[:, 2 * B_DK:]


def _gla_kernel(q_ref, k_ref, v_ref, r_ref, z_ref, wa_ref, ba_ref, g_ref, o_ref,
                state_scr, mask_scr, kv_scr, st_scr, *, n_chunks):
    lc = n_chunks * CHUNK

    @pl.when(pl.program_id(2) == 0)
    def _():
        state_scr[...] = jnp.zeros(state_scr.shape, F32)
        row = lax.broadcasted_iota(jnp.int32, (lc, lc), 0)
        col = lax.broadcasted_iota(jnp.int32, (lc, lc), 1)
        same = (row // CHUNK) == (col // CHUNK)
        mask_scr[...] = (same & (row >= col)).astype(BF16)

    z = z_ref[...]
    zh = z.astype(BF16)
    zl = (z - zh.astype(F32)).astype(BF16)
    z3 = jnp.concatenate([zh, zl, zh], axis=1)
    kcols = lambda hh: slice(hh * B_DK, (hh + 1) * B_DK)
    vcols = lambda hh: slice(hh * B_DV, (hh + 1) * B_DV)
    chunk_rows = lambda c: slice(c * CHUNK, (c + 1) * CHUNK)
    live = [dict() for _ in range(GLA_HEADS_PER_STEP)]

    def gate_stage(hh):
        pre = jnp.dot(z3, wa_ref[:, kcols(hh)], preferred_element_type=F32) + ba_ref[:, kcols(hh)]
        log_a = (jnp.minimum(pre, 0.0) - jnp.log1p(jnp.exp(-jnp.abs(pre)))) * (1.0 / GATE_TAU)
        live[hh]["parts"] = jnp.concatenate(_split3(log_a), axis=1)

    def decay_stage(hh):
        cum = _sum3(jnp.dot(mask_scr[...], live[hh].pop("parts"), preferred_element_type=F32))
        totals = [cum[(c + 1) * CHUNK - 1:(c + 1) * CHUNK, :] for c in range(n_chunks)]
        total = jnp.concatenate([jnp.broadcast_to(tc, (CHUNK, B_DK)) for tc in totals], axis=0)
        live[hh]["totals"] = totals
        live[hh]["k_dec"] = (k_ref[:, kcols(hh)].astype(F32) * jnp.exp(total - cum)).astype(BF16)

    def kv_stage(hh):
        k_dec = live[hh].pop("k_dec")
        for c in range(n_chunks):
            kv_scr[hh, c] = lax.dot_general(v_ref[chunk_rows(c), vcols(hh)], k_dec[chunk_rows(c)],
                                            (((0,), (0,)), ((), ())), preferred_element_type=F32)

    def state_stage(hh):
        totals = live[hh].pop("totals")
        state = state_scr[hh]
        for c in range(n_chunks):
            state = state * jnp.exp(totals[c]) + kv_scr[hh, c]
            st_scr[hh, c] = state.astype(BF16)
        state_scr[hh] = state

    def output_stage(hh):
        for c in range(n_chunks):
            rows = chunk_rows(c)
            o = lax.dot_general(q_ref[rows, kcols(hh)], st_scr[hh, c], (((1,), (1,)), ((), ())),
                                preferred_element_type=F32) * (B_DK ** -0.5)
            o_ref[rows, vcols(hh)] = (_rms(o) * g_ref[...]
                                      * _silu(r_ref[rows, vcols(hh)].astype(F32))).astype(BF16)

    stages = (gate_stage, decay_stage, kv_stage, state_stage, output_stage)
    for tick in range(GLA_HEADS_PER_STEP + len(stages) - 1):
        for k, stage in enumerate(stages):
            if 0 <= tick - k < GLA_HEADS_PER_STEP:
                stage(tick - k)


def _gla(proj, zb, w_alpha_pad, b_alpha, g_norm, batch, seq, lc=GLA_LC):
    t = proj.shape[0]
    lc = min(lc, seq)
    nl = seq // lc
    n_chunks = lc // CHUNK
    hps = GLA_HEADS_PER_STEP
    assert B_HEADS % hps == 0 and COL_QB % hps == 0 and COL_KB % hps == 0
    kern = functools.partial(_gla_kernel, n_chunks=n_chunks)
    wa_hi = w_alpha_pad.astype(BF16)
    wa_lo = (w_alpha_pad - wa_hi.astype(F32)).astype(BF16)
    wa3 = jnp.concatenate([wa_hi, wa_hi, wa_lo], axis=0)
    kblock = lambda col0: pl.BlockSpec((lc, hps * B_DK), lambda b, h, l: (b * nl + l, col0 // hps + h))
    vblock = lambda col0: pl.BlockSpec((lc, hps * B_DV), lambda b, h, l: (b * nl + l, col0 // hps + h))
    return pl.pallas_call(
        kern,
        out_shape=jax.ShapeDtypeStruct((t, B_WIDTH), BF16),
        grid=(batch, B_HEADS // hps, nl),
        in_specs=[kblock(COL_QB), kblock(COL_KB), vblock(COL_VB256), vblock(COL_RB256),
                  pl.BlockSpec((lc, LANES), lambda b, h, l: (b * nl + l, 0)),
                  pl.BlockSpec((3 * LANES, hps * B_DK), lambda b, h, l: (0, h)),
                  pl.BlockSpec((1, hps * B_DK), lambda b, h, l: (0, h)),
                  pl.BlockSpec((1, B_DV), lambda b, h, l: (0, 0))],
        out_specs=pl.BlockSpec((lc, hps * B_DV), lambda b, h, l: (b * nl + l, h)),
        scratch_shapes=[pltpu.VMEM((hps, B_DV, B_DK), F32),
                        pltpu.VMEM((lc, lc), BF16),
                        pltpu.VMEM((hps, n_chunks, B_DV, B_DK), F32),
                        pltpu.VMEM((hps, n_chunks, B_DV, B_DK), BF16)],
        compiler_params=_params("arbitrary", "arbitrary", "arbitrary"),
        name="gla",
    )(proj, proj, proj, proj, zb, wa3, b_alpha, g_norm)


def _outproj_kernel(oa_ref, ob_ref, wo_ref, x_ref, gt_ref, gpost_ref, gpre_ref, sc_ref, sh_ref, wr_ref,
                    br_ref, x1_ref, h2_ref, lg_ref, *, tm):
    n_pieces = tm // OUTPROJ_PIECE

    def project(p):
        prow = slice(p * OUTPROJ_PIECE, (p + 1) * OUTPROJ_PIECE)
        return (jnp.dot(oa_ref[prow, :], wo_ref[:A_WIDTH, :], preferred_element_type=F32)
                + jnp.dot(ob_ref[prow, :], wo_ref[A_WIDTH:, :], preferred_element_type=F32))

    y_next = project(0)
    for p in range(n_pieces):
        p0 = p * OUTPROJ_PIECE
        prow = slice(p0, p0 + OUTPROJ_PIECE)
        y = y_next
        if p + 1 < n_pieces:
            y_next = project(p + 1)
        his, los = [], []
        for c in range(OUTPROJ_PIECE // ROW_CHUNK):
            r0 = p0 + c * ROW_CHUNK
            rows = slice(r0, r0 + ROW_CHUNK)
            x1 = x_ref[rows, :] + gt_ref[0] * (_rms(y[c * ROW_CHUNK:(c + 1) * ROW_CHUNK]) * gpost_ref[...])
            x1_ref[rows, :] = x1
            h2 = (_rms(x1) * gpre_ref[...]) * (1.0 + sc_ref[0]) + sh_ref[0]
            hi = h2.astype(BF16)
            his.append(hi)
            los.append((h2 - hi.astype(F32)).astype(BF16))
            _store_row_slabs(h2_ref, r0, ROW_CHUNK, h2)
        hi, lo = jnp.concatenate(his, axis=0), jnp.concatenate(los, axis=0)
        hw = jnp.dot(hi, wr_ref[...], preferred_element_type=F32)
        lw = jnp.dot(lo, wr_ref[:, :LANES], preferred_element_type=F32)
        lg_ref[prow, :] = hw[:, :LANES] + hw[:, LANES:] + lw + br_ref[...]


def _outproj(oa, ob, w_out, x2d, gt, g_post, g_pre, sc, sh, w_router, b_router, seq, tm=OUTPROJ_TM):
    t, d = x2d.shape
    tm = min(tm, seq)
    per_b = seq // tm
    kern = functools.partial(_outproj_kernel, tm=tm)
    wr_hi = w_router.astype(BF16)
    wr_lo = (w_router - wr_hi.astype(F32)).astype(BF16)
    wr_cat = jnp.concatenate([wr_hi, wr_lo], axis=1)
    row = lambda: pl.BlockSpec((1, d), lambda i: (0, 0))
    per_batch = lambda: pl.BlockSpec((1, 1, d), lambda i: (i // per_b, 0, 0))
    return pl.pallas_call(
        kern,
        out_shape=(jax.ShapeDtypeStruct((t, d), F32),
                   jax.ShapeDtypeStruct((t * ROW_SLABS, LANES), U32),
                   jax.ShapeDtypeStruct((t, LANES), F32)),
        grid=(t // tm,),
        in_specs=[pl.BlockSpec((tm, A_WIDTH), lambda i: (i, 0)),
                  pl.BlockSpec((tm, B_WIDTH), lambda i: (i, 0)),
                  pl.BlockSpec((d, d), lambda i: (0, 0)),
                  pl.BlockSpec((tm, d), lambda i: (i, 0)),
                  per_batch(), row(), row(), per_batch(), per_batch(),
                  pl.BlockSpec((d, 2 * LANES), lambda i: (0, 0)),
                  pl.BlockSpec((1, LANES), lambda i: (0, 0))],
        out_specs=(pl.BlockSpec((tm, d), lambda i: (i, 0)),
                   pl.BlockSpec((tm * ROW_SLABS, LANES), lambda i: (i, 0)),
                   pl.BlockSpec((tm, LANES), lambda i: (i, 0))),
        compiler_params=_params("arbitrary"),
        name="out_proj",
    )(oa, ob, w_out, x2d, gt, g_post, g_pre, sc, sh, wr_cat, b_router)


def _route_kernel(lg_ref, rec_ref, rows_ref, cnt_ref, carry_scr, before_scr, *, tr):
    @pl.when(pl.program_id(0) == 0)
    def _():
        carry_scr[...] = jnp.zeros(carry_scr.shape, F32)
        earlier = lax.broadcasted_iota(jnp.int32, (tr, tr), 0)
        token = lax.broadcasted_iota(jnp.int32, (tr, tr), 1)
        before_scr[...] = (earlier < token).astype(BF16)

    lg = lg_ref[...].T[:ROUTER_ROWS]
    row = lax.broadcasted_iota(jnp.int32, lg.shape, 0)
    big = jnp.int32(ROUTER_ROWS)

    def first_row(mask):
        return jnp.min(jnp.where(mask, row, big), axis=0, keepdims=True)

    gmask = row < N_GROUPS
    gmax = jnp.max(jnp.where(gmask, lg, -jnp.inf), axis=0, keepdims=True)
    gexp = jnp.where(gmask, jnp.exp(lg - gmax), 0.0)
    gprob = gexp / jnp.sum(gexp, axis=0, keepdims=True)
    g_val = jnp.max(gprob, axis=0, keepdims=True)
    g_idx = first_row(gmask & (gprob == g_val))

    lo = ROUTER_EXPERT_LANE0 + g_idx * EXPERTS_PER_GROUP
    emask = (row >= lo) & (row < lo + EXPERTS_PER_GROUP)
    emax = jnp.max(jnp.where(emask, lg, -jnp.inf), axis=0, keepdims=True)
    eexp = jnp.where(emask, jnp.exp(lg - emax), 0.0)
    eprob = eexp / jnp.sum(eexp, axis=0, keepdims=True)
    v1 = jnp.max(eprob, axis=0, keepdims=True)
    i1 = first_row(emask & (eprob == v1))
    rest = emask & (row != i1)
    v2 = jnp.max(jnp.where(rest, eprob, -1.0), axis=0, keepdims=True)
    i2 = first_row(rest & (eprob == v2))
    w1 = g_val * (v1 / (v1 + v2))
    w2 = g_val * (v2 / (v1 + v2))

    hit1 = row == i1
    hit2 = row == i2
    onehot = (hit1 | hit2).astype(BF16)
    pos = carry_scr[...] + jnp.dot(onehot, before_scr[...], preferred_element_type=F32)
    rank1 = jnp.sum(jnp.where(hit1, pos, 0.0), axis=0, keepdims=True)
    rank2 = jnp.sum(jnp.where(hit2, pos, 0.0), axis=0, keepdims=True)
    carry_scr[...] = carry_scr[...] + jnp.sum(onehot.astype(F32), axis=1, keepdims=True)
    cnt_ref[...] = carry_scr[...]

    e1 = (i1 - ROUTER_EXPERT_LANE0).astype(F32)
    e2 = (i2 - ROUTER_EXPERT_LANE0).astype(F32)
    field = lax.broadcasted_iota(jnp.int32, (LANES, tr), 0)
    rec = jnp.zeros((LANES, tr), F32)
    for ln, val in ((ROUTE_E1, e1), (ROUTE_E2, e2), (ROUTE_W1, w1), (ROUTE_W2, w2),
                    (ROUTE_R1, rank1), (ROUTE_R2, rank2)):
        rec = jnp.where(field == ln, val, rec)
    rec_ref[...] = rec.T
    rows_ref[...] = rec[:ROUTE_FIELD_ROWS]


def _route(logits, tr=ROUTE_TR):
    t = logits.shape[0]
    tr = min(tr, t)
    kern = functools.partial(_route_kernel, tr=tr)
    return pl.pallas_call(
        kern,
        out_shape=(jax.ShapeDtypeStruct((t, LANES), F32), jax.ShapeDtypeStruct((ROUTE_FIELD_ROWS, t), F32),
                   jax.ShapeDtypeStruct((ROUTER_ROWS, 1), F32)),
        grid=(t // tr,),
        in_specs=[pl.BlockSpec((tr, LANES), lambda i: (i, 0))],
        out_specs=(pl.BlockSpec((tr, LANES), lambda i: (i, 0)),
                   pl.BlockSpec((ROUTE_FIELD_ROWS, tr), lambda i: (0, i)),
                   pl.BlockSpec((ROUTER_ROWS, 1), lambda i: (0, 0))),
        scratch_shapes=[pltpu.VMEM((ROUTER_ROWS, 1), F32), pltpu.VMEM((tr, tr), BF16)],
        compiler_params=_params("arbitrary"),
        name="route",
    )(logits)


def _slab_rows(ref, row):
    return ref.at[pl.ds(pl.multiple_of(row * ROW_SLABS, ROW_SLABS), ROW_SLABS), :]


def _dispatch_kernel(slot_ref, pad_start_ref, pad_len_ref, used_ref, h2_ref, xs_hbm, zero_scr, sems, pad_sems,
                     *, td, tm):
    g = pl.program_id(0)
    tile_rows = tm * ROW_SLABS
    n_tiles = xs_hbm.shape[0] // tile_rows

    def zero_copy(slot, nslots, sem):
        rows = pl.ds(pl.multiple_of(slot * ROW_SLABS, ROW_SLABS), nslots * ROW_SLABS)
        return pltpu.make_async_copy(zero_scr.at[pl.ds(0, nslots * ROW_SLABS), :], xs_hbm.at[rows, :], sem)

    pad_sizes = [1 << b for b in reversed(range((tm - 1).bit_length()))]

    @pl.when(g == 0)
    def _():
        zero_scr[...] = jnp.zeros(zero_scr.shape, U32)

        def start_unused(tile, carry):
            zero_copy(tile * tm, tm, sems.at[1]).start()
            return carry

        def drain_unused(tile, carry):
            zero_copy(0, tm, sems.at[1]).wait()
            return carry

        lax.fori_loop(used_ref[0], n_tiles, start_unused, 0)

        def per_expert(e, counts):
            off = pad_start_ref[e]
            n = pad_len_ref[e]
            new_counts = []
            for b, size in enumerate(pad_sizes):
                hit = (n & size) != 0

                @pl.when(hit)
                def _():
                    zero_copy(off, size, pad_sems.at[b]).start()

                off = off + jnp.where(hit, size, 0)
                new_counts.append(counts[b] + hit.astype(jnp.int32))
            return tuple(new_counts)

        counts = lax.fori_loop(0, N_EXPERTS, per_expert, tuple(jnp.int32(0) for _ in pad_sizes))
        for b, size in enumerate(pad_sizes):
            def drain(r, c):
                zero_copy(0, size, pad_sems.at[b]).wait()
                return c
            lax.fori_loop(0, counts[b], drain, 0)
        lax.fori_loop(used_ref[0], n_tiles, drain_unused, 0)

    def row_copy(r, slot):
        return pltpu.make_async_copy(_slab_rows(h2_ref, r), _slab_rows(xs_hbm, slot), sems.at[0])

    base = g * td
    n_tok = slot_ref.shape[0] // 2

    def issue(r, c):
        tok = base + r
        row_copy(r, slot_ref[tok]).start()
        row_copy(r, slot_ref[n_tok + tok]).start()
        return c

    lax.fori_loop(0, td, issue, 0, unroll=CHUNK_UNROLL)
    for _ in range(2):
        pltpu.make_async_copy(h2_ref, xs_hbm.at[pl.ds(0, td * ROW_SLABS), :], sems.at[0]).wait()


def _dispatch(slot, pad_start, pad_len, used, h2_rows, n_slots, tm, td=DISPATCH_TD):
    t = slot.shape[0] // 2
    td = min(td, t)
    kern = functools.partial(_dispatch_kernel, td=td, tm=tm)
    grid_spec = pltpu.PrefetchScalarGridSpec(
        num_scalar_prefetch=4,
        grid=(t // td,),
        in_specs=[pl.BlockSpec((td * ROW_SLABS, LANES), lambda g, sl, ps, pn, us: (g, 0))],
        out_specs=pl.BlockSpec(memory_space=pl.ANY),
        scratch_shapes=[pltpu.VMEM((tm * ROW_SLABS, LANES), U32), pltpu.SemaphoreType.DMA((2,)),
                        pltpu.SemaphoreType.DMA(((tm - 1).bit_length(),))],
    )
    return pl.pallas_call(
        kern,
        out_shape=jax.ShapeDtypeStruct((n_slots * ROW_SLABS, LANES), U32),
        grid_spec=grid_spec,
        compiler_params=_params("arbitrary"),
        name="dispatch",
    )(slot, pad_start, pad_len, used, h2_rows)


TILE_UNUSED, TILE_USED, TILE_NEW_EXPERT = 0, 1, 2


def _expert_kernel(texp_ref, tblk_ref, tstate_ref, tnext_ref, tpar_ref, xs_ref, w1_hbm, w3_hbm, w2_hbm, eo_ref,
                   x_scr, w1_scr, w3_scr, w2_scr, w1_stage, w3_stage, w2_stage, sems, *, tm):
    i = pl.program_id(0)
    state = tstate_ref[i]
    slot = tpar_ref[i]

    def weight_copies(expert, dst_slot):
        return [pltpu.make_async_copy(hbm.at[expert], stage.at[dst_slot], sems.at[dst_slot])
                for hbm, stage in ((w1_hbm, w1_stage), (w3_hbm, w3_stage), (w2_hbm, w2_stage))]

    @pl.when(state == TILE_UNUSED)
    def _():
        eo_ref[...] = jnp.zeros(eo_ref.shape, U32)

    @pl.when(i == 0)
    def _():
        for cp in weight_copies(texp_ref[0], slot):
            cp.start()

    @pl.when(state == TILE_NEW_EXPERT)
    def _():
        for cp in weight_copies(texp_ref[i], slot):
            cp.wait()

        @pl.when(tnext_ref[i] >= 0)
        def _():
            for cp in weight_copies(tnext_ref[i], 1 - slot):
                cp.start()

        w1_scr[...] = w1_stage[slot].astype(BF16)
        w3_scr[...] = w3_stage[slot].astype(BF16)
        w2_scr[...] = w2_stage[slot].astype(BF16)

    @pl.when(state != TILE_UNUSED)
    def _():
        for s in range(ROW_SLABS):
            lo, hi = _unpack_bf16_pair(xs_ref[pl.ds(s, tm, stride=ROW_SLABS), :])
            x_scr[:, s * LANES:(s + 1) * LANES] = lo.astype(BF16)
            x_scr[:, HALF_D + s * LANES:HALF_D + (s + 1) * LANES] = hi.astype(BF16)
        x = x_scr[...]
        a = jnp.dot(x, w1_scr[...], preferred_element_type=F32)
        b = jnp.dot(x, w3_scr[...], preferred_element_type=F32)
        hid = (_silu(a) * b).astype(BF16)
        y = jnp.dot(hid, w2_scr[...], preferred_element_type=F32)
        _store_row_slabs(eo_ref, 0, tm, y)


def _experts(tile_expert, tile_block, tile_state, tile_next, tile_slot, xs_rows, w1, w3, w2, tm):
    n_tiles = tile_expert.shape[0]
    d, f = w1.shape[1], w1.shape[2]
    kern = functools.partial(_expert_kernel, tm=tm)
    grid_spec = pltpu.PrefetchScalarGridSpec(
        num_scalar_prefetch=5,
        grid=(n_tiles,),
        in_specs=[pl.BlockSpec((tm * ROW_SLABS, LANES), lambda i, te, tb, ts, tn, tp: (tb[i], 0)),
                  pl.BlockSpec(memory_space=pl.ANY), pl.BlockSpec(memory_space=pl.ANY),
                  pl.BlockSpec(memory_space=pl.ANY)],
        out_specs=pl.BlockSpec((tm * ROW_SLABS, LANES), lambda i, te, tb, ts, tn, tp: (i, 0)),
        scratch_shapes=[pltpu.VMEM((tm, d), BF16), pltpu.VMEM((d, f), BF16), pltpu.VMEM((d, f), BF16),
                        pltpu.VMEM((f, d), BF16),
                        pltpu.VMEM((2, d, f), F32), pltpu.VMEM((2, d, f), F32), pltpu.VMEM((2, f, d), F32),
                        pltpu.SemaphoreType.DMA((2,))],
    )
    return pl.pallas_call(
        kern,
        out_shape=jax.ShapeDtypeStruct(xs_rows.shape, U32),
        grid_spec=grid_spec,
        compiler_params=_params("arbitrary"),
        name="expert_mlp",
    )(tile_expert, tile_block, tile_state, tile_next, tile_slot, xs_rows, w1, w3, w2)


def _final_kernel(slot_ref, eo_hbm, rec_ref, x1_ref, gt_ref, g_ref, o_ref, e_scr, sems, *, tf):
    i = pl.program_id(0)
    par = i % 2
    n_tok = slot_ref.shape[0] // 2

    def start_all(step, buf):
        def body(r, c):
            tok = step * tf + r
            for k in range(2):
                pltpu.make_async_copy(_slab_rows(eo_hbm, slot_ref[k * n_tok + tok]),
                                      _slab_rows(e_scr.at[buf, k], r), sems.at[buf]).start()
            return c
        lax.fori_loop(0, tf, body, 0, unroll=CHUNK_UNROLL)

    def wait_all(buf):
        for k in range(2):
            pltpu.make_async_copy(eo_hbm.at[pl.ds(0, tf * ROW_SLABS), :], e_scr.at[buf, k], sems.at[buf]).wait()

    @pl.when(i == 0)
    def _():
        start_all(0, 0)

    @pl.when(i + 1 < pl.num_programs(0))
    def _():
        start_all(i + 1, 1 - par)

    wait_all(par)

    def chunk(c, carry):
        r0 = pl.multiple_of(c * ROW_CHUNK, ROW_CHUNK)
        rows = pl.ds(r0, ROW_CHUNK)
        rec = rec_ref[rows, :]
        w1 = rec[:, ROUTE_W1:ROUTE_W1 + 1]
        w2 = rec[:, ROUTE_W2:ROUTE_W2 + 1]
        lo1, hi1 = _load_row_slabs(e_scr.at[par, 0], r0, ROW_CHUNK)
        lo2, hi2 = _load_row_slabs(e_scr.at[par, 1], r0, ROW_CHUNK)
        y = jnp.concatenate([w1 * lo1 + w2 * lo2, w1 * hi1 + w2 * hi2], axis=1)
        o_ref[rows, :] = x1_ref[rows, :] + gt_ref[0] * (_rms(y) * g_ref[...])
        return carry

    lax.fori_loop(0, tf // ROW_CHUNK, chunk, 0, unroll=CHUNK_UNROLL)


def _final(slot, eo_rows, rec, x1, gt, g_post, seq, tf=COMBINE_TF):
    t, d = x1.shape
    tf = min(tf, seq)
    per_b = seq // tf
    kern = functools.partial(_final_kernel, tf=tf)
    grid_spec = pltpu.PrefetchScalarGridSpec(
        num_scalar_prefetch=1,
        grid=(t // tf,),
        in_specs=[pl.BlockSpec(memory_space=pl.ANY),
                  pl.BlockSpec((tf, LANES), lambda i, sl: (i, 0)),
                  pl.BlockSpec((tf, d), lambda i, sl: (i, 0)),
                  pl.BlockSpec((1, 1, d), lambda i, sl: (i // per_b, 0, 0)),
                  pl.BlockSpec((1, d), lambda i, sl: (0, 0))],
        out_specs=pl.BlockSpec((tf, d), lambda i, sl: (i, 0)),
        scratch_shapes=[pltpu.VMEM((2, 2, tf * ROW_SLABS, LANES), U32),
                        pltpu.SemaphoreType.DMA((2,))],
    )
    return pl.pallas_call(
        kern,
        out_shape=jax.ShapeDtypeStruct((t, d), F32),
        grid_spec=grid_spec,
        compiler_params=_params("arbitrary"),
        name="combine_final",
    )(slot, eo_rows, rec, x1, gt, g_post)


def _dispatch_tables(rows, counts, t, tm):
    e = rows[ROUTE_E1:ROUTE_E2 + 1].astype(jnp.int32)
    rank = rows[ROUTE_R1:ROUTE_R2 + 1].astype(jnp.int32)
    cnt = counts[ROUTER_EXPERT_LANE0:ROUTER_EXPERT_LANE0 + N_EXPERTS, 0].astype(jnp.int32)
    tiles_per = (cnt + tm - 1) // tm
    tile_end = jnp.cumsum(tiles_per)
    tile_start = tile_end - tiles_per
    n_tiles = (2 * t + N_EXPERTS * (tm - 1)) // tm
    experts = jnp.arange(N_EXPERTS, dtype=jnp.int32)
    start_of = jnp.sum(jnp.where(e[None] == experts[:, None, None], tile_start[:, None, None], 0),
                       axis=0)
    slot = (start_of * tm + rank).reshape(-1)
    pad_start = tile_start * tm + cnt
    pad_len = tiles_per * tm - cnt
    tile_id = jnp.arange(n_tiles, dtype=jnp.int32)
    used = tile_end[-1]
    tblk = jnp.minimum(tile_id, used - 1)
    texp = jnp.sum(tile_end[None, :] <= tblk[:, None], axis=-1).astype(jnp.int32)
    tstate = jnp.where(tile_id < used, jnp.where(tile_id == tile_start[texp], TILE_NEW_EXPERT, TILE_USED),
                       TILE_UNUSED).astype(jnp.int32)
    nonempty = cnt > 0
    ordinal = jnp.cumsum(nonempty.astype(jnp.int32)) - 1
    later = jnp.where(nonempty[None, :] & (experts[None, :] > experts[:, None]), experts[None, :], N_EXPERTS)
    next_expert = jnp.min(later, axis=-1)
    next_expert = jnp.where(next_expert == N_EXPERTS, -1, next_expert)
    tnext = next_expert[texp].astype(jnp.int32)
    tslot = (ordinal[texp] % 2).astype(jnp.int32)
    return slot, pad_start, pad_len, used.reshape(1), texp, tblk, tstate, tnext, tslot, n_tiles * tm


def kernel(x, c, rel_bias, w_ada, b_ada, g_pre_mix, g_post_mix, w_in, w_alpha, b_alpha, lam_q1, lam_k1, lam_q2,
           lam_k2, g_sub_a, g_norm_b, w_out, g_pre_ffn, g_post_ffn, w_router_g, b_router_g, w_router_e,
           b_router_e, w1, w3, w2):
    batch, seq, d = x.shape
    t = batch * seq
    depth = w_in.shape[0]
    tq = min(ATTN_TQ, seq)
    tm_e = EXPERT_TM
    xf = x.reshape(t, d)
    for i in range(depth):
        lam_init = 0.8 - 0.6 * math.exp(-0.3 * i)
        c_pad = jnp.pad(c, ((0, 8 - batch % 8 if batch % 8 else 0), (0, 0)))
        ada = _ada(c_pad, w_ada[i], b_ada[i][None, :])[:batch]
        sh_m, sc_m, gt_m, sh_f, sc_f, gt_f = [a[:, None, :] for a in jnp.split(ada, 6, axis=-1)]

        w_in_b = w_in[i].astype(BF16)
        w_z = jnp.pad(w_in_b[:, D_MAIN:], ((0, 0), (0, LANES - GATE_RANK)))
        proj, zb = _inproj(xf, g_pre_mix[i][None, :], sc_m, sh_m, w_in_b, w_z, seq)

        oa = _attention(proj, _bias_tiles(rel_bias, tq), lam_q1[i][None, :], lam_k1[i][None, :],
                        lam_q2[i][None, :], lam_k2[i][None, :], g_sub_a[i][:, None], batch, seq, lam_init, tq)
        w_alpha_pad = jnp.pad(w_alpha[i], ((0, LANES - GATE_RANK), (0, 0)))
        ob = _gla(proj, zb, w_alpha_pad, b_alpha[i][None, :], g_norm_b[i][None, :], batch, seq)

        w_router = jnp.pad(jnp.concatenate([w_router_g[i], w_router_e[i]], axis=1),
                           ((0, 0), (0, LANES - N_GROUPS - N_EXPERTS)))
        b_router = jnp.pad(jnp.concatenate([b_router_g[i], b_router_e[i]]),
                           (0, LANES - N_GROUPS - N_EXPERTS))[None, :]
        x1, h2_rows, logits = _outproj(oa, ob, w_out[i].astype(BF16), xf, gt_m, g_post_mix[i][None, :],
                                       g_pre_ffn[i][None, :], sc_f, sh_f, w_router, b_router, seq)

        rec, rec_rows, counts = _route(logits)
        (slot, pad_start, pad_len, used, texp, tblk, tstate, tnext, tslot,
         n_slots) = _dispatch_tables(rec_rows, counts, t, tm_e)
        xs = _dispatch(slot, pad_start, pad_len, used, h2_rows, n_slots, tm_e)
        eo = _experts(texp, tblk, tstate, tnext, tslot, xs, w1[i], w3[i], w2[i], tm_e)
        xf = _final(slot, eo, rec, x1, gt_f, g_post_ffn[i][None, :], seq)
    return xf.reshape(batch, seq, d)
```

```python
import functools
import math

import jax
import jax.numpy as jnp
from jax import lax
from jax.experimental import pallas as pl
from jax.experimental.pallas import tpu as pltpu

F32 = jnp.float32
BF16 = jnp.bfloat16

D_MODEL = 2048
CHUNK = 64
A_HEADS = 8
A_DK = 64
A_DV = 2 * A_DK
A_WIDTH = A_HEADS * A_DV
B_HEADS = 4
B_WIDTH = D_MODEL - A_WIDTH
B_DV = B_WIDTH // B_HEADS
B_DK = B_DV // 2
GATE_RANK = 16
GATE_TAU = 16.0
N_BUCKETS = 32
MAX_DISTANCE = 256
N_GROUPS = 4
EXPERTS_PER_GROUP = 8
N_EXPERTS = N_GROUPS * EXPERTS_PER_GROUP
EPS = 1e-6
NEG_INF = -1e30
LOG2E = math.log2(math.e)

LANES = 128
U32 = jnp.uint32
HALF_D = D_MODEL // 2
ROW_SLABS = HALF_D // LANES
ROW_CHUNK = 16
CHUNK_UNROLL = 8
ONES_ROWS = 16
FAR_BIAS_DISTANCE = 166

ADA_TN = 1024
INPROJ_TM, INPROJ_TN = 512, 1024
ATTN_TQ, ATTN_TK = 512, 256
GLA_LC = 512
GLA_HEADS_PER_STEP = 4
OUTPROJ_TM = 512
OUTPROJ_PIECE = 256
ROUTE_TR = 1024
EXPERT_TM = 512
DISPATCH_TD = 2048
COMBINE_TF = 256
D_MAIN = 3 * A_WIDTH + 2 * B_HEADS * B_DK + 2 * B_WIDTH
COL_QA, COL_KA, COL_VA = 0, A_HEADS, 2 * A_HEADS
COL_QB = 3 * A_HEADS
COL_KB = COL_QB + B_HEADS
COL_VB256 = (3 * A_WIDTH + 2 * B_HEADS * B_DK) // B_DV
COL_RB256 = COL_VB256 + B_HEADS
ROUTE_E1, ROUTE_E2, ROUTE_W1, ROUTE_W2, ROUTE_R1, ROUTE_R2 = 0, 1, 2, 3, 4, 5
ROUTE_FIELD_ROWS = 8
ROUTER_EXPERT_LANE0 = N_GROUPS
ROUTER_ROWS = 48

VMEM_LIMIT = 56 * 1024 * 1024
INPROJ_VMEM_LIMIT = 60 * 1024 * 1024


def _params(*sem):
    return pltpu.CompilerParams(dimension_semantics=sem, vmem_limit_bytes=VMEM_LIMIT)


def _rms(v):
    return v * lax.rsqrt(jnp.mean(v * v, axis=-1, keepdims=True) + EPS)


def _silu(v):
    return v * jax.nn.sigmoid(v)


_HIGH_HALF = 0xFFFF0000


def _pack_bf16_pair(lo, hi):
    lo_bits = lax.bitcast_convert_type(lo.astype(BF16).astype(F32), U32) >> 16
    hi_bits = lax.bitcast_convert_type(hi.astype(BF16).astype(F32), U32) & U32(_HIGH_HALF)
    return hi_bits | lo_bits


def _unpack_bf16_pair(w):
    return (lax.bitcast_convert_type(w << 16, F32), lax.bitcast_convert_type(w & U32(_HIGH_HALF), F32))


def _store_row_slabs(ref, r0, nrows, rows_f32):
    packed = _pack_bf16_pair(rows_f32[:, :HALF_D], rows_f32[:, HALF_D:])
    for s in range(ROW_SLABS):
        ref[pl.ds(r0 * ROW_SLABS + s, nrows, stride=ROW_SLABS), :] = packed[:, s * LANES:(s + 1) * LANES]


def _load_row_slabs(ref, r0, nrows):
    slabs = [_unpack_bf16_pair(ref[pl.ds(r0 * ROW_SLABS + s, nrows, stride=ROW_SLABS), :]) for s in range(ROW_SLABS)]
    return (jnp.concatenate([lo for lo, _ in slabs], axis=1), jnp.concatenate([hi for _, hi in slabs], axis=1))


def _ada_kernel(c_ref, w_ref, b_ref, o_ref):
    s = _silu(c_ref[...])
    o_ref[...] = jnp.dot(s.astype(BF16), w_ref[...].astype(BF16), preferred_element_type=F32) + b_ref[...]


def _ada(c_pad, w, b, tn=ADA_TN):
    m, d = c_pad.shape
    n = w.shape[1]
    return pl.pallas_call(
        _ada_kernel,
        out_shape=jax.ShapeDtypeStruct((m, n), F32),
        grid=(n // tn,),
        in_specs=[pl.BlockSpec((m, d), lambda j: (0, 0)),
                  pl.BlockSpec((d, tn), lambda j: (0, j)),
                  pl.BlockSpec((1, tn), lambda j: (0, j))],
        out_specs=pl.BlockSpec((m, tn), lambda j: (0, j)),
        compiler_params=_params("arbitrary"),
        name="ada_proj",
    )(c_pad, w, b)


def _inproj_kernel(x_ref, xn_ref, g_ref, sc_ref, sh_ref, scn_ref, shn_ref, w_ref, wz_ref, o_ref, z_ref,
                   h_scr, hn_scr, *, tm, tn):
    def normed(src_ref, scale_ref, shift_ref, rows):
        h = _rms(src_ref[rows, :]) * g_ref[...]
        return (h * (1.0 + scale_ref[0]) + shift_ref[0]).astype(BF16)

    @pl.when(pl.program_id(0) == 0)
    def _():
        def chunk(c, carry):
            rows = pl.ds(pl.multiple_of(c * ROW_CHUNK, ROW_CHUNK), ROW_CHUNK)
            h_scr[rows, :] = normed(x_ref, sc_ref, sh_ref, rows)
            return carry
        lax.fori_loop(0, tm // ROW_CHUNK, chunk, 0, unroll=CHUNK_UNROLL)

    h = h_scr[...]
    z_ref[...] = jnp.dot(h, wz_ref[...], preferred_element_type=F32)
    n_col = D_MAIN // tn
    rows_per_col = tm // n_col // ROW_CHUNK * ROW_CHUNK
    next_row = 0
    for c in range(n_col):
        cols = slice(c * tn, (c + 1) * tn)
        o_ref[:, cols] = jnp.dot(h, w_ref[:, cols], preferred_element_type=F32).astype(BF16)
        stop = tm if c == n_col - 1 else next_row + rows_per_col
        for r0 in range(next_row, stop, ROW_CHUNK):
            rows = slice(r0, r0 + ROW_CHUNK)
            hn_scr[rows, :] = normed(xn_ref, scn_ref, shn_ref, rows)
        next_row = stop
    h_scr[...] = hn_scr[...]


def _inproj(x2d, g, sc, sh, w_all, w_z, seq, tm=INPROJ_TM, tn=INPROJ_TN):
    t, d = x2d.shape
    tm = min(tm, seq)
    per_b = seq // tm
    n_steps = t // tm
    kern = functools.partial(_inproj_kernel, tm=tm, tn=tn)
    nxt = lambda i: jnp.minimum(i + 1, n_steps - 1)
    per_batch = lambda step: pl.BlockSpec((1, 1, d), lambda i: (step(i) // per_b, 0, 0))
    return pl.pallas_call(
        kern,
        out_shape=(jax.ShapeDtypeStruct((t, D_MAIN), BF16), jax.ShapeDtypeStruct((t, LANES), F32)),
        grid=(n_steps,),
        in_specs=[pl.BlockSpec((tm, d), lambda i: (i, 0)),
                  pl.BlockSpec((tm, d), lambda i: (nxt(i), 0)),
                  pl.BlockSpec((1, d), lambda i: (0, 0)),
                  per_batch(lambda i: i), per_batch(lambda i: i), per_batch(nxt), per_batch(nxt),
                  pl.BlockSpec(w_all.shape, lambda i: (0, 0), pipeline_mode=pl.Buffered(1)),
                  pl.BlockSpec((d, LANES), lambda i: (0, 0), pipeline_mode=pl.Buffered(1))],
        out_specs=(pl.BlockSpec((tm, D_MAIN), lambda i: (i, 0)),
                   pl.BlockSpec((tm, LANES), lambda i: (i, 0))),
        scratch_shapes=[pltpu.VMEM((tm, d), BF16), pltpu.VMEM((tm, d), BF16)],
        compiler_params=pltpu.CompilerParams(dimension_semantics=("arbitrary",),
                                             vmem_limit_bytes=INPROJ_VMEM_LIMIT),
        name="in_proj",
    )(x2d, x2d, g, sc, sh, sc, sh, w_all, w_z)


def _t5_bucket(rel):
    nb = N_BUCKETS // 2
    max_exact = nb // 2
    base = jnp.where(rel > 0, nb, 0)
    n = jnp.abs(rel)
    nf = jnp.maximum(n, 1).astype(F32)
    large = max_exact + (jnp.log(nf / max_exact) / math.log(MAX_DISTANCE / max_exact)
                         * (nb - max_exact)).astype(jnp.int32)
    large = jnp.minimum(large, nb - 1)
    return base + jnp.where(n < max_exact, n, large)


def _bias_buckets(tq):
    kj = jnp.arange(tq, dtype=jnp.int32)[:, None]
    qi = jnp.arange(tq, dtype=jnp.int32)[None, :]
    near = _t5_bucket(kj - qi - tq)
    diag = jnp.where((kj // CHUNK) <= (qi // CHUNK), _t5_bucket(kj - qi), N_BUCKETS)
    return jnp.stack([near, diag]).astype(jnp.int32)


def _bias_kernel(rb_ref, bk_ref, o_ref, *, tq):
    h = pl.program_id(0)
    nb = N_BUCKETS // 2
    far = rb_ref[nb - 1, h]

    def lookup(bucket, buckets):
        acc = jnp.full(bucket.shape, NEG_INF, F32)
        for n in buckets:
            acc = jnp.where(bucket == n, (rb_ref[n, h] - far) * LOG2E, acc)
        return acc

    @pl.when(pl.program_id(1) == 0)
    def _():
        beyond = (tq - FAR_BIAS_DISTANCE + 1) // 8 * 8
        o_ref[:beyond, :] = jnp.zeros((beyond, tq), F32)
        o_ref[beyond:, :] = lookup(bk_ref[beyond:, :], range(1, nb))

    @pl.when(pl.program_id(1) == 1)
    def _():
        o_ref[...] = lookup(bk_ref[...], range(N_BUCKETS))


def _bias_tiles(rel_bias, tq):
    return pl.pallas_call(
        functools.partial(_bias_kernel, tq=tq),
        out_shape=jax.ShapeDtypeStruct((A_HEADS, 2, tq, tq), F32),
        grid=(A_HEADS, 2),
        in_specs=[pl.BlockSpec(memory_space=pltpu.SMEM),
                  pl.BlockSpec((None, tq, tq), lambda h, d: (d, 0, 0))],
        out_specs=pl.BlockSpec((None, None, tq, tq), lambda h, d: (h, d, 0, 0)),
        compiler_params=_params("arbitrary", "arbitrary"),
        name="bias_tiles",
    )(rel_bias, _bias_buckets(tq))


def _attn_kernel(q_ref, k_ref, v_ref, bias_ref, lq1_ref, lk1_ref, lq2_ref, lk2_ref, g_ref, o_ref,
                 vt_scr, sa_scr, sb_scr, m_scr, acc_scr, *, tq, tk, lam_init):
    nsub = tq // tk
    nq = q_ref.shape[0] // tq
    bufs = (sa_scr, sb_scr)

    ones = jnp.ones((ONES_ROWS, tk), BF16)
    for c in range(vt_scr.shape[0]):
        vt = v_ref[c * tk:(c + 1) * tk, :].astype(F32).T.astype(BF16)
        vt_scr[c] = jnp.concatenate([vt, ones], axis=0)

    lane = lax.broadcasted_iota(jnp.int32, (1, A_DV), 1)

    def query_tile(i, carry):
        def tile_rows(tile):
            return pl.ds(pl.multiple_of(tile * tq, tq), tq)

        def two_map_queries(tile):
            q = q_ref[tile_rows(tile), :] * (A_DK ** -0.5 * LOG2E)
            zero = jnp.zeros_like(q)
            return jnp.concatenate([jnp.where(lane < A_DK, q, zero), jnp.where(lane >= A_DK, q, zero)], axis=0)

        q2 = two_map_queries(i)
        m_scr[...] = jnp.full(m_scr.shape, NEG_INF, F32)
        acc_scr[...] = jnp.zeros(acc_scr.shape, F32)

        def scores(j, queries=q2, q_lo=0):
            k = k_ref[pl.ds(pl.multiple_of(j * tk, tk), tk), :]
            if q_lo:
                queries = jnp.concatenate([queries[q_lo:tq], queries[tq + q_lo:]], axis=0)
            return lax.dot_general(k, queries, (((1,), (1,)), ((), ())), preferred_element_type=F32)

        def softmax_pv(s_ref, j, bias, q_lo=0):
            width = tq - q_lo
            halves = (slice(q_lo, tq), slice(tq + q_lo, 2 * tq))
            both = lambda ref: ref[...] if not q_lo else jnp.concatenate([ref[:, h] for h in halves], axis=1)
            s = s_ref[:, :2 * width]
            if bias is not None:
                s = jnp.concatenate([s[:, :width] + bias, s[:, width:] + bias], axis=1)
            m_old = both(m_scr)
            m_new = jnp.maximum(m_old, jnp.max(s, axis=0, keepdims=True))
            alpha = jnp.exp2(m_old - m_new)
            p = jnp.exp2(s - m_new).astype(BF16)
            acc = alpha * both(acc_scr) + jnp.dot(vt_scr[j], p, preferred_element_type=F32)
            if not q_lo:
                acc_scr[...] = acc
                m_scr[...] = m_new
            else:
                for n, h in enumerate(halves):
                    acc_scr[:, h] = acc[:, n * width:(n + 1) * width]
                    m_scr[:, h] = m_new[:, n * width:(n + 1) * width]

        n_far = jnp.maximum(i - 1, 0) * nsub

        @pl.when(i == 0)
        def _():
            sa_scr[...] = scores(0)

        def far_steps(j, count):
            for c in range(count):
                bufs[(c + 1) % 2][...] = scores(j + c + 1)
                softmax_pv(bufs[c % 2], j + c, None)

        def far_quad(jj, inner):
            far_steps(4 * jj, 4)
            return inner

        n_quads = n_far // 4
        lax.fori_loop(0, n_quads, far_quad, 0)

        @pl.when(n_far - 4 * n_quads >= 2)
        def _():
            far_steps(4 * n_quads, 2)

        def biased_steps(first_tile):
            j0 = (i - 1 + first_tile) * nsub
            count = (2 - first_tile) * nsub

            def geometry(c):
                d, r = first_tile + c // nsub, (c % nsub) * tk
                return d, r, (r if d == 1 else 0)

            for c in range(count):
                if c + 1 < count:
                    q_next = geometry(c + 1)[2]
                    bufs[(c + 1) % 2][:, :2 * (tq - q_next)] = scores(j0 + c + 1, q_lo=q_next)
                d, r, q_lo = geometry(c)
                no_bias = d == 0 and r + tk - 1 - tq <= -FAR_BIAS_DISTANCE
                bias = None if no_bias else bias_ref[d, r:r + tk, q_lo:]
                softmax_pv(bufs[c % 2], j0 + c, bias, q_lo)

        def finish():
            lam = (jnp.exp(jnp.sum(lq1_ref[...] * lk1_ref[...], axis=-1, keepdims=True))
                   - jnp.exp(jnp.sum(lq2_ref[...] * lk2_ref[...], axis=-1, keepdims=True)) + lam_init)
            on = acc_scr[:A_DV, :] / acc_scr[A_DV:A_DV + 1, :]
            o = on[:, :tq] - lam * on[:, tq:]
            y = o * lax.rsqrt(jnp.mean(o * o, axis=0, keepdims=True) + EPS) * g_ref[...] * (1.0 - lam_init)
            o_ref[tile_rows(i), :] = y.T.astype(BF16)

        last = i + 1 == nq
        for first_tile, applies in ((0, i >= 1), (1, i == 0)):
            @pl.when(applies & jnp.logical_not(last))
            def _():
                biased_steps(first_tile)
                sa_scr[...] = scores(0, two_map_queries(i + 1))
                finish()

            @pl.when(applies & last)
            def _():
                biased_steps(first_tile)
                finish()
        return carry

    lax.fori_loop(0, nq, query_tile, 0)


def _attention(proj, bias_tiles, lq1, lk1, lq2, lk2, g_sub_col, batch, seq, lam_init, tq, tk=ATTN_TK):
    t = proj.shape[0]
    assert (tq // tk) % 2 == 0 and tq % tk == 0, "the score pipeline alternates two buffers per query tile"
    assert tq + 1 >= FAR_BIAS_DISTANCE, "key tiles two or more before the query tile must be past the bias horizon"
    kern = functools.partial(_attn_kernel, tq=tq, tk=tk, lam_init=lam_init)
    vec = lambda n: pl.BlockSpec((1, n), lambda b, h: (0, 0))
    return pl.pallas_call(
        kern,
        out_shape=jax.ShapeDtypeStruct((t, A_WIDTH), BF16),
        grid=(batch, A_HEADS),
        in_specs=[pl.BlockSpec((seq, A_DV), lambda b, h: (b, COL_QA + h)),
                  pl.BlockSpec((seq, A_DV), lambda b, h: (b, COL_KA + h)),
                  pl.BlockSpec((seq, A_DV), lambda b, h: (b, COL_VA + h)),
                  pl.BlockSpec((None, 2, tq, tq), lambda b, h: (h, 0, 0, 0)),
                  vec(A_DK), vec(A_DK), vec(A_DK), vec(A_DK),
                  pl.BlockSpec((A_DV, 1), lambda b, h: (0, 0))],
        out_specs=pl.BlockSpec((seq, A_DV), lambda b, h: (b, h)),
        scratch_shapes=[pltpu.VMEM((seq // tk, A_DV + ONES_ROWS, tk), BF16),
                        pltpu.VMEM((tk, 2 * tq), F32), pltpu.VMEM((tk, 2 * tq), F32),
                        pltpu.VMEM((1, 2 * tq), F32),
                        pltpu.VMEM((A_DV + ONES_ROWS, 2 * tq), F32)],
        compiler_params=_params("arbitrary", "arbitrary"),
        name="diff_attention",
    )(proj, proj, proj, bias_tiles, lq1, lk1, lq2, lk2, g_sub_col)


def _split3(a):
    a1 = a.astype(BF16)
    r1 = a - a1.astype(F32)
    a2 = r1.astype(BF16)
    return a1, a2, (r1 - a2.astype(F32)).astype(BF16)


def _sum3(x):
    return x[:, :B_DK] + x[:, B_DK:2 * B_DK] + x[:, 2 * B_DK:]


def _gla_kernel(q_ref, k_ref, v_ref, r_ref, z_ref, wa_ref, ba_ref, g_ref, o_ref,
                state_scr, mask_scr, kv_scr, st_scr, *, n_chunks):
    lc = n_chunks * CHUNK

    @pl.when(pl.program_id(2) == 0)
    def _():
        state_scr[...] = jnp.zeros(state_scr.shape, F32)
        row = lax.broadcasted_iota(jnp.int32, (lc, lc), 0)
        col = lax.broadcasted_iota(jnp.int32, (lc, lc), 1)
        same = (row // CHUNK) == (col // CHUNK)
        mask_scr[...] = (same & (row >= col)).astype(BF16)

    z = z_ref[...]
    zh = z.astype(BF16)
    zl = (z - zh.astype(F32)).astype(BF16)
    z3 = jnp.concatenate([zh, zl, zh], axis=1)
    kcols = lambda hh: slice(hh * B_DK, (hh + 1) * B_DK)
    vcols = lambda hh: slice(hh * B_DV, (hh + 1) * B_DV)
    chunk_rows = lambda c: slice(c * CHUNK, (c + 1) * CHUNK)
    live = [dict() for _ in range(GLA_HEADS_PER_STEP)]

    def gate_stage(hh):
        pre = jnp.dot(z3, wa_ref[:, kcols(hh)], preferred_element_type=F32) + ba_ref[:, kcols(hh)]
        log_a = (jnp.minimum(pre, 0.0) - jnp.log1p(jnp.exp(-jnp.abs(pre)))) * (1.0 / GATE_TAU)
        live[hh]["parts"] = jnp.concatenate(_split3(log_a), axis=1)

    def decay_stage(hh):
        cum = _sum3(jnp.dot(mask_scr[...], live[hh].pop("parts"), preferred_element_type=F32))
        totals = [cum[(c + 1) * CHUNK - 1:(c + 1) * CHUNK, :] for c in range(n_chunks)]
        total = jnp.concatenate([jnp.broadcast_to(tc, (CHUNK, B_DK)) for tc in totals], axis=0)
        live[hh]["totals"] = totals
        live[hh]["k_dec"] = (k_ref[:, kcols(hh)].astype(F32) * jnp.exp(total - cum)).astype(BF16)

    def kv_stage(hh):
        k_dec = live[hh].pop("k_dec")
        for c in range(n_chunks):
            kv_scr[hh, c] = lax.dot_general(v_ref[chunk_rows(c), vcols(hh)], k_dec[chunk_rows(c)],
                                            (((0,), (0,)), ((), ())), preferred_element_type=F32)

    def state_stage(hh):
        totals = live[hh].pop("totals")
        state = state_scr[hh]
        for c in range(n_chunks):
            state = state * jnp.exp(totals[c]) + kv_scr[hh, c]
            st_scr[hh, c] = state.astype(BF16)
        state_scr[hh] = state

    def output_stage(hh):
        for c in range(n_chunks):
            rows = chunk_rows(c)
            o = lax.dot_general(q_ref[rows, kcols(hh)], st_scr[hh, c], (((1,), (1,)), ((), ())),
                                preferred_element_type=F32) * (B_DK ** -0.5)
            o_ref[rows, vcols(hh)] = (_rms(o) * g_ref[...]
                                      * _silu(r_ref[rows, vcols(hh)].astype(F32))).astype(BF16)

    stages = (gate_stage, decay_stage, kv_stage, state_stage, output_stage)
    for tick in range(GLA_HEADS_PER_STEP + len(stages) - 1):
        for k, stage in enumerate(stages):
            if 0 <= tick - k < GLA_HEADS_PER_STEP:
                stage(tick - k)


def _gla(proj, zb, w_alpha_pad, b_alpha, g_norm, batch, seq, lc=GLA_LC):
    t = proj.shape[0]
    lc = min(lc, seq)
    nl = seq // lc
    n_chunks = lc // CHUNK
    hps = GLA_HEADS_PER_STEP
    assert B_HEADS % hps == 0 and COL_QB % hps == 0 and COL_KB % hps == 0
    kern = functools.partial(_gla_kernel, n_chunks=n_chunks)
    wa_hi = w_alpha_pad.astype(BF16)
    wa_lo = (w_alpha_pad - wa_hi.astype(F32)).astype(BF16)
    wa3 = jnp.concatenate([wa_hi, wa_hi, wa_lo], axis=0)
    kblock = lambda col0: pl.BlockSpec((lc, hps * B_DK), lambda b, h, l: (b * nl + l, col0 // hps + h))
    vblock = lambda col0: pl.BlockSpec((lc, hps * B_DV), lambda b, h, l: (b * nl + l, col0 // hps + h))
    return pl.pallas_call(
        kern,
        out_shape=jax.ShapeDtypeStruct((t, B_WIDTH), BF16),
        grid=(batch, B_HEADS // hps, nl),
        in_specs=[kblock(COL_QB), kblock(COL_KB), vblock(COL_VB256), vblock(COL_RB256),
                  pl.BlockSpec((lc, LANES), lambda b, h, l: (b * nl + l, 0)),
                  pl.BlockSpec((3 * LANES, hps * B_DK), lambda b, h, l: (0, h)),
                  pl.BlockSpec((1, hps * B_DK), lambda b, h, l: (0, h)),
                  pl.BlockSpec((1, B_DV), lambda b, h, l: (0, 0))],
        out_specs=pl.BlockSpec((lc, hps * B_DV), lambda b, h, l: (b * nl + l, h)),
        scratch_shapes=[pltpu.VMEM((hps, B_DV, B_DK), F32),
                        pltpu.VMEM((lc, lc), BF16),
                        pltpu.VMEM((hps, n_chunks, B_DV, B_DK), F32),
                        pltpu.VMEM((hps, n_chunks, B_DV, B_DK), BF16)],
        compiler_params=_params("arbitrary", "arbitrary", "arbitrary"),
        name="gla",
    )(proj, proj, proj, proj, zb, wa3, b_alpha, g_norm)


def _outproj_kernel(oa_ref, ob_ref, wo_ref, x_ref, gt_ref, gpost_ref, gpre_ref, sc_ref, sh_ref, wr_ref,
                    br_ref, x1_ref, h2_ref, lg_ref, *, tm):
    n_pieces = tm // OUTPROJ_PIECE

    def project(p):
        prow = slice(p * OUTPROJ_PIECE, (p + 1) * OUTPROJ_PIECE)
        return (jnp.dot(oa_ref[prow, :], wo_ref[:A_WIDTH, :], preferred_element_type=F32)
                + jnp.dot(ob_ref[prow, :], wo_ref[A_WIDTH:, :], preferred_element_type=F32))

    y_next = project(0)
    for p in range(n_pieces):
        p0 = p * OUTPROJ_PIECE
        prow = slice(p0, p0 + OUTPROJ_PIECE)
        y = y_next
        if p + 1 < n_pieces:
            y_next = project(p + 1)
        his, los = [], []
        for c in range(OUTPROJ_PIECE // ROW_CHUNK):
            r0 = p0 + c * ROW_CHUNK
            rows = slice(r0, r0 + ROW_CHUNK)
            x1 = x_ref[rows, :] + gt_ref[0] * (_rms(y[c * ROW_CHUNK:(c + 1) * ROW_CHUNK]) * gpost_ref[...])
            x1_ref[rows, :] = x1
            h2 = (_rms(x1) * gpre_ref[...]) * (1.0 + sc_ref[0]) + sh_ref[0]
            hi = h2.astype(BF16)
            his.append(hi)
            los.append((h2 - hi.astype(F32)).astype(BF16))
            _store_row_slabs(h2_ref, r0, ROW_CHUNK, h2)
        hi, lo = jnp.concatenate(his, axis=0), jnp.concatenate(los, axis=0)
        hw = jnp.dot(hi, wr_ref[...], preferred_element_type=F32)
        lw = jnp.dot(lo, wr_ref[:, :LANES], preferred_element_type=F32)
        lg_ref[prow, :] = hw[:, :LANES] + hw[:, LANES:] + lw + br_ref[...]


def _outproj(oa, ob, w_out, x2d, gt, g_post, g_pre, sc, sh, w_router, b_router, seq, tm=OUTPROJ_TM):
    t, d = x2d.shape
    tm = min(tm, seq)
    per_b = seq // tm
    kern = functools.partial(_outproj_kernel, tm=tm)
    wr_hi = w_router.astype(BF16)
    wr_lo = (w_router - wr_hi.astype(F32)).astype(BF16)
    wr_cat = jnp.concatenate([wr_hi, wr_lo], axis=1)
    row = lambda: pl.BlockSpec((1, d), lambda i: (0, 0))
    per_batch = lambda: pl.BlockSpec((1, 1, d), lambda i: (i // per_b, 0, 0))
    return pl.pallas_call(
        kern,
        out_shape=(jax.ShapeDtypeStruct((t, d), F32),
                   jax.ShapeDtypeStruct((t * ROW_SLABS, LANES), U32),
                   jax.ShapeDtypeStruct((t, LANES), F32)),
        grid=(t // tm,),
        in_specs=[pl.BlockSpec((tm, A_WIDTH), lambda i: (i, 0)),
                  pl.BlockSpec((tm, B_WIDTH), lambda i: (i, 0)),
                  pl.BlockSpec((d, d), lambda i: (0, 0)),
                  pl.BlockSpec((tm, d), lambda i: (i, 0)),
                  per_batch(), row(), row(), per_batch(), per_batch(),
                  pl.BlockSpec((d, 2 * LANES), lambda i: (0, 0)),
                  pl.BlockSpec((1, LANES), lambda i: (0, 0))],
        out_specs=(pl.BlockSpec((tm, d), lambda i: (i, 0)),
                   pl.BlockSpec((tm * ROW_SLABS, LANES), lambda i: (i, 0)),
                   pl.BlockSpec((tm, LANES), lambda i: (i, 0))),
        compiler_params=_params("arbitrary"),
        name="out_proj",
    )(oa, ob, w_out, x2d, gt, g_post, g_pre, sc, sh, wr_cat, b_router)


def _route_kernel(lg_ref, rec_ref, rows_ref, cnt_ref, carry_scr, before_scr, *, tr):
    @pl.when(pl.program_id(0) == 0)
    def _():
        carry_scr[...] = jnp.zeros(carry_scr.shape, F32)
        earlier = lax.broadcasted_iota(jnp.int32, (tr, tr), 0)
        token = lax.broadcasted_iota(jnp.int32, (tr, tr), 1)
        before_scr[...] = (earlier < token).astype(BF16)

    lg = lg_ref[...].T[:ROUTER_ROWS]
    row = lax.broadcasted_iota(jnp.int32, lg.shape, 0)
    big = jnp.int32(ROUTER_ROWS)

    def first_row(mask):
        return jnp.min(jnp.where(mask, row, big), axis=0, keepdims=True)

    gmask = row < N_GROUPS
    gmax = jnp.max(jnp.where(gmask, lg, -jnp.inf), axis=0, keepdims=True)
    gexp = jnp.where(gmask, jnp.exp(lg - gmax), 0.0)
    gprob = gexp / jnp.sum(gexp, axis=0, keepdims=True)
    g_val = jnp.max(gprob, axis=0, keepdims=True)
    g_idx = first_row(gmask & (gprob == g_val))

    lo = ROUTER_EXPERT_LANE0 + g_idx * EXPERTS_PER_GROUP
    emask = (row >= lo) & (row < lo + EXPERTS_PER_GROUP)
    emax = jnp.max(jnp.where(emask, lg, -jnp.inf), axis=0, keepdims=True)
    eexp = jnp.where(emask, jnp.exp(lg - emax), 0.0)
    eprob = eexp / jnp.sum(eexp, axis=0, keepdims=True)
    v1 = jnp.max(eprob, axis=0, keepdims=True)
    i1 = first_row(emask & (eprob == v1))
    rest = emask & (row != i1)
    v2 = jnp.max(jnp.where(rest, eprob, -1.0), axis=0, keepdims=True)
    i2 = first_row(rest & (eprob == v2))
    w1 = g_val * (v1 / (v1 + v2))
    w2 = g_val * (v2 / (v1 + v2))

    hit1 = row == i1
    hit2 = row == i2
    onehot = (hit1 | hit2).astype(BF16)
    pos = carry_scr[...] + jnp.dot(onehot, before_scr[...], preferred_element_type=F32)
    rank1 = jnp.sum(jnp.where(hit1, pos, 0.0), axis=0, keepdims=True)
    rank2 = jnp.sum(jnp.where(hit2, pos, 0.0), axis=0, keepdims=True)
    carry_scr[...] = carry_scr[...] + jnp.sum(onehot.astype(F32), axis=1, keepdims=True)
    cnt_ref[...] = carry_scr[...]

    e1 = (i1 - ROUTER_EXPERT_LANE0).astype(F32)
    e2 = (i2 - ROUTER_EXPERT_LANE0).astype(F32)
    field = lax.broadcasted_iota(jnp.int32, (LANES, tr), 0)
    rec = jnp.zeros((LANES, tr), F32)
    for ln, val in ((ROUTE_E1, e1), (ROUTE_E2, e2), (ROUTE_W1, w1), (ROUTE_W2, w2),
                    (ROUTE_R1, rank1), (ROUTE_R2, rank2)):
        rec = jnp.where(field == ln, val, rec)
    rec_ref[...] = rec.T
    rows_ref[...] = rec[:ROUTE_FIELD_ROWS]


def _route(logits, tr=ROUTE_TR):
    t = logits.shape[0]
    tr = min(tr, t)
    kern = functools.partial(_route_kernel, tr=tr)
    return pl.pallas_call(
        kern,
        out_shape=(jax.ShapeDtypeStruct((t, LANES), F32), jax.ShapeDtypeStruct((ROUTE_FIELD_ROWS, t), F32),
                   jax.ShapeDtypeStruct((ROUTER_ROWS, 1), F32)),
        grid=(t // tr,),
        in_specs=[pl.BlockSpec((tr, LANES), lambda i: (i, 0))],
        out_specs=(pl.BlockSpec((tr, LANES), lambda i: (i, 0)),
                   pl.BlockSpec((ROUTE_FIELD_ROWS, tr), lambda i: (0, i)),
                   pl.BlockSpec((ROUTER_ROWS, 1), lambda i: (0, 0))),
        scratch_shapes=[pltpu.VMEM((ROUTER_ROWS, 1), F32), pltpu.VMEM((tr, tr), BF16)],
        compiler_params=_params("arbitrary"),
        name="route",
    )(logits)


def _slab_rows(ref, row):
    return ref.at[pl.ds(pl.multiple_of(row * ROW_SLABS, ROW_SLABS), ROW_SLABS), :]


def _dispatch_kernel(slot_ref, pad_start_ref, pad_len_ref, used_ref, h2_ref, xs_hbm, zero_scr, sems, pad_sems,
                     *, td, tm):
    g = pl.program_id(0)
    tile_rows = tm * ROW_SLABS
    n_tiles = xs_hbm.shape[0] // tile_rows

    def zero_copy(slot, nslots, sem):
        rows = pl.ds(pl.multiple_of(slot * ROW_SLABS, ROW_SLABS), nslots * ROW_SLABS)
        return pltpu.make_async_copy(zero_scr.at[pl.ds(0, nslots * ROW_SLABS), :], xs_hbm.at[rows, :], sem)

    pad_sizes = [1 << b for b in reversed(range((tm - 1).bit_length()))]

    @pl.when(g == 0)
    def _():
        zero_scr[...] = jnp.zeros(zero_scr.shape, U32)

        def start_unused(tile, carry):
            zero_copy(tile * tm, tm, sems.at[1]).start()
            return carry

        def drain_unused(tile, carry):
            zero_copy(0, tm, sems.at[1]).wait()
            return carry

        lax.fori_loop(used_ref[0], n_tiles, start_unused, 0)

        def per_expert(e, counts):
            off = pad_start_ref[e]
            n = pad_len_ref[e]
            new_counts = []
            for b, size in enumerate(pad_sizes):
                hit = (n & size) != 0

                @pl.when(hit)
                def _():
                    zero_copy(off, size, pad_sems.at[b]).start()

                off = off + jnp.where(hit, size, 0)
                new_counts.append(counts[b] + hit.astype(jnp.int32))
            return tuple(new_counts)

        counts = lax.fori_loop(0, N_EXPERTS, per_expert, tuple(jnp.int32(0) for _ in pad_sizes))
        for b, size in enumerate(pad_sizes):
            def drain(r, c):
                zero_copy(0, size, pad_sems.at[b]).wait()
                return c
            lax.fori_loop(0, counts[b], drain, 0)
        lax.fori_loop(used_ref[0], n_tiles, drain_unused, 0)

    def row_copy(r, slot):
        return pltpu.make_async_copy(_slab_rows(h2_ref, r), _slab_rows(xs_hbm, slot), sems.at[0])

    base = g * td
    n_tok = slot_ref.shape[0] // 2

    def issue(r, c):
        tok = base + r
        row_copy(r, slot_ref[tok]).start()
        row_copy(r, slot_ref[n_tok + tok]).start()
        return c

    lax.fori_loop(0, td, issue, 0, unroll=CHUNK_UNROLL)
    for _ in range(2):
        pltpu.make_async_copy(h2_ref, xs_hbm.at[pl.ds(0, td * ROW_SLABS), :], sems.at[0]).wait()


def _dispatch(slot, pad_start, pad_len, used, h2_rows, n_slots, tm, td=DISPATCH_TD):
    t = slot.shape[0] // 2
    td = min(td, t)
    kern = functools.partial(_dispatch_kernel, td=td, tm=tm)
    grid_spec = pltpu.PrefetchScalarGridSpec(
        num_scalar_prefetch=4,
        grid=(t // td,),
        in_specs=[pl.BlockSpec((td * ROW_SLABS, LANES), lambda g, sl, ps, pn, us: (g, 0))],
        out_specs=pl.BlockSpec(memory_space=pl.ANY),
        scratch_shapes=[pltpu.VMEM((tm * ROW_SLABS, LANES), U32), pltpu.SemaphoreType.DMA((2,)),
                        pltpu.SemaphoreType.DMA(((tm - 1).bit_length(),))],
    )
    return pl.pallas_call(
        kern,
        out_shape=jax.ShapeDtypeStruct((n_slots * ROW_SLABS, LANES), U32),
        grid_spec=grid_spec,
        compiler_params=_params("arbitrary"),
        name="dispatch",
    )(slot, pad_start, pad_len, used, h2_rows)


TILE_UNUSED, TILE_USED, TILE_NEW_EXPERT = 0, 1, 2


def _expert_kernel(texp_ref, tblk_ref, tstate_ref, tnext_ref, tpar_ref, xs_ref, w1_hbm, w3_hbm, w2_hbm, eo_ref,
                   x_scr, w1_scr, w3_scr, w2_scr, w1_stage, w3_stage, w2_stage, sems, *, tm):
    i = pl.program_id(0)
    state = tstate_ref[i]
    slot = tpar_ref[i]

    def weight_copies(expert, dst_slot):
        return [pltpu.make_async_copy(hbm.at[expert], stage.at[dst_slot], sems.at[dst_slot])
                for hbm, stage in ((w1_hbm, w1_stage), (w3_hbm, w3_stage), (w2_hbm, w2_stage))]

    @pl.when(state == TILE_UNUSED)
    def _():
        eo_ref[...] = jnp.zeros(eo_ref.shape, U32)

    @pl.when(i == 0)
    def _():
        for cp in weight_copies(texp_ref[0], slot):
            cp.start()

    @pl.when(state == TILE_NEW_EXPERT)
    def _():
        for cp in weight_copies(texp_ref[i], slot):
            cp.wait()

        @pl.when(tnext_ref[i] >= 0)
        def _():
            for cp in weight_copies(tnext_ref[i], 1 - slot):
                cp.start()

        w1_scr[...] = w1_stage[slot].astype(BF16)
        w3_scr[...] = w3_stage[slot].astype(BF16)
        w2_scr[...] = w2_stage[slot].astype(BF16)

    @pl.when(state != TILE_UNUSED)
    def _():
        for s in range(ROW_SLABS):
            lo, hi = _unpack_bf16_pair(xs_ref[pl.ds(s, tm, stride=ROW_SLABS), :])
            x_scr[:, s * LANES:(s + 1) * LANES] = lo.astype(BF16)
            x_scr[:, HALF_D + s * LANES:HALF_D + (s + 1) * LANES] = hi.astype(BF16)
        x = x_scr[...]
        a = jnp.dot(x, w1_scr[...], preferred_element_type=F32)
        b = jnp.dot(x, w3_scr[...], preferred_element_type=F32)
        hid = (_silu(a) * b).astype(BF16)
        y = jnp.dot(hid, w2_scr[...], preferred_element_type=F32)
        _store_row_slabs(eo_ref, 0, tm, y)


def _experts(tile_expert, tile_block, tile_state, tile_next, tile_slot, xs_rows, w1, w3, w2, tm):
    n_tiles = tile_expert.shape[0]
    d, f = w1.shape[1], w1.shape[2]
    kern = functools.partial(_expert_kernel, tm=tm)
    grid_spec = pltpu.PrefetchScalarGridSpec(
        num_scalar_prefetch=5,
        grid=(n_tiles,),
        in_specs=[pl.BlockSpec((tm * ROW_SLABS, LANES), lambda i, te, tb, ts, tn, tp: (tb[i], 0)),
                  pl.BlockSpec(memory_space=pl.ANY), pl.BlockSpec(memory_space=pl.ANY),
                  pl.BlockSpec(memory_space=pl.ANY)],
        out_specs=pl.BlockSpec((tm * ROW_SLABS, LANES), lambda i, te, tb, ts, tn, tp: (i, 0)),
        scratch_shapes=[pltpu.VMEM((tm, d), BF16), pltpu.VMEM((d, f), BF16), pltpu.VMEM((d, f), BF16),
                        pltpu.VMEM((f, d), BF16),
                        pltpu.VMEM((2, d, f), F32), pltpu.VMEM((2, d, f), F32), pltpu.VMEM((2, f, d), F32),
                        pltpu.SemaphoreType.DMA((2,))],
    )
    return pl.pallas_call(
        kern,
        out_shape=jax.ShapeDtypeStruct(xs_rows.shape, U32),
        grid_spec=grid_spec,
        compiler_params=_params("arbitrary"),
        name="expert_mlp",
    )(tile_expert, tile_block, tile_state, tile_next, tile_slot, xs_rows, w1, w3, w2)


def _final_kernel(slot_ref, eo_hbm, rec_ref, x1_ref, gt_ref, g_ref, o_ref, e_scr, sems, *, tf):
    i = pl.program_id(0)
    par = i % 2
    n_tok = slot_ref.shape[0] // 2

    def start_all(step, buf):
        def body(r, c):
            tok = step * tf + r
            for k in range(2):
                pltpu.make_async_copy(_slab_rows(eo_hbm, slot_ref[k * n_tok + tok]),
                                      _slab_rows(e_scr.at[buf, k], r), sems.at[buf]).start()
            return c
        lax.fori_loop(0, tf, body, 0, unroll=CHUNK_UNROLL)

    def wait_all(buf):
        for k in range(2):
            pltpu.make_async_copy(eo_hbm.at[pl.ds(0, tf * ROW_SLABS), :], e_scr.at[buf, k], sems.at[buf]).wait()

    @pl.when(i == 0)
    def _():
        start_all(0, 0)

    @pl.when(i + 1 < pl.num_programs(0))
    def _():
        start_all(i + 1, 1 - par)

    wait_all(par)

    def chunk(c, carry):
        r0 = pl.multiple_of(c * ROW_CHUNK, ROW_CHUNK)
        rows = pl.ds(r0, ROW_CHUNK)
        rec = rec_ref[rows, :]
        w1 = rec[:, ROUTE_W1:ROUTE_W1 + 1]
        w2 = rec[:, ROUTE_W2:ROUTE_W2 + 1]
        lo1, hi1 = _load_row_slabs(e_scr.at[par, 0], r0, ROW_CHUNK)
        lo2, hi2 = _load_row_slabs(e_scr.at[par, 1], r0, ROW_CHUNK)
        y = jnp.concatenate([w1 * lo1 + w2 * lo2, w1 * hi1 + w2 * hi2], axis=1)
        o_ref[rows, :] = x1_ref[rows, :] + gt_ref[0] * (_rms(y) * g_ref[...])
        return carry

    lax.fori_loop(0, tf // ROW_CHUNK, chunk, 0, unroll=CHUNK_UNROLL)


def _final(slot, eo_rows, rec, x1, gt, g_post, seq, tf=COMBINE_TF):
    t, d = x1.shape
    tf = min(tf, seq)
    per_b = seq // tf
    kern = functools.partial(_final_kernel, tf=tf)
    grid_spec = pltpu.PrefetchScalarGridSpec(
        num_scalar_prefetch=1,
        grid=(t // tf,),
        in_specs=[pl.BlockSpec(memory_space=pl.ANY),
                  pl.BlockSpec((tf, LANES), lambda i, sl: (i, 0)),
                  pl.BlockSpec((tf, d), lambda i, sl: (i, 0)),
                  pl.BlockSpec((1, 1, d), lambda i, sl: (i // per_b, 0, 0)),
                  pl.BlockSpec((1, d), lambda i, sl: (0, 0))],
        out_specs=pl.BlockSpec((tf, d), lambda i, sl: (i, 0)),
        scratch_shapes=[pltpu.VMEM((2, 2, tf * ROW_SLABS, LANES), U32),
                        pltpu.SemaphoreType.DMA((2,))],
    )
    return pl.pallas_call(
        kern,
        out_shape=jax.ShapeDtypeStruct((t, d), F32),
        grid_spec=grid_spec,
        compiler_params=_params("arbitrary"),
        name="combine_final",
    )(slot, eo_rows, rec, x1, gt, g_post)


def _dispatch_tables(rows, counts, t, tm):
    e = rows[ROUTE_E1:ROUTE_E2 + 1].astype(jnp.int32)
    rank = rows[ROUTE_R1:ROUTE_R2 + 1].astype(jnp.int32)
    cnt = counts[ROUTER_EXPERT_LANE0:ROUTER_EXPERT_LANE0 + N_EXPERTS, 0].astype(jnp.int32)
    tiles_per = (cnt + tm - 1) // tm
    tile_end = jnp.cumsum(tiles_per)
    tile_start = tile_end - tiles_per
    n_tiles = (2 * t + N_EXPERTS * (tm - 1)) // tm
    experts = jnp.arange(N_EXPERTS, dtype=jnp.int32)
    start_of = jnp.sum(jnp.where(e[None] == experts[:, None, None], tile_start[:, None, None], 0),
                       axis=0)
    slot = (start_of * tm + rank).reshape(-1)
    pad_start = tile_start * tm + cnt
    pad_len = tiles_per * tm - cnt
    tile_id = jnp.arange(n_tiles, dtype=jnp.int32)
    used = tile_end[-1]
    tblk = jnp.minimum(tile_id, used - 1)
    texp = jnp.sum(tile_end[None, :] <= tblk[:, None], axis=-1).astype(jnp.int32)
    tstate = jnp.where(tile_id < used, jnp.where(tile_id == tile_start[texp], TILE_NEW_EXPERT, TILE_USED),
                       TILE_UNUSED).astype(jnp.int32)
    nonempty = cnt > 0
    ordinal = jnp.cumsum(nonempty.astype(jnp.int32)) - 1
    later = jnp.where(nonempty[None, :] & (experts[None, :] > experts[:, None]), experts[None, :], N_EXPERTS)
    next_expert = jnp.min(later, axis=-1)
    next_expert = jnp.where(next_expert == N_EXPERTS, -1, next_expert)
    tnext = next_expert[texp].astype(jnp.int32)
    tslot = (ordinal[texp] % 2).astype(jnp.int32)
    return slot, pad_start, pad_len, used.reshape(1), texp, tblk, tstate, tnext, tslot, n_tiles * tm


def kernel(x, c, rel_bias, w_ada, b_ada, g_pre_mix, g_post_mix, w_in, w_alpha, b_alpha, lam_q1, lam_k1, lam_q2,
           lam_k2, g_sub_a, g_norm_b, w_out, g_pre_ffn, g_post_ffn, w_router_g, b_router_g, w_router_e,
           b_router_e, w1, w3, w2):
    batch, seq, d = x.shape
    t = batch * seq
    depth = w_in.shape[0]
    tq = min(ATTN_TQ, seq)
    tm_e = EXPERT_TM
    xf = x.reshape(t, d)
    for i in range(depth):
        lam_init = 0.8 - 0.6 * math.exp(-0.3 * i)
        c_pad = jnp.pad(c, ((0, 8 - batch % 8 if batch % 8 else 0), (0, 0)))
        ada = _ada(c_pad, w_ada[i], b_ada[i][None, :])[:batch]
        sh_m, sc_m, gt_m, sh_f, sc_f, gt_f = [a[:, None, :] for a in jnp.split(ada, 6, axis=-1)]

        w_in_b = w_in[i].astype(BF16)
        w_z = jnp.pad(w_in_b[:, D_MAIN:], ((0, 0), (0, LANES - GATE_RANK)))
        proj, zb = _inproj(xf, g_pre_mix[i][None, :], sc_m, sh_m, w_in_b, w_z, seq)

        oa = _attention(proj, _bias_tiles(rel_bias, tq), lam_q1[i][None, :], lam_k1[i][None, :],
                        lam_q2[i][None, :], lam_k2[i][None, :], g_sub_a[i][:, None], batch, seq, lam_init, tq)
        w_alpha_pad = jnp.pad(w_alpha[i], ((0, LANES - GATE_RANK), (0, 0)))
        ob = _gla(proj, zb, w_alpha_pad, b_alpha[i][None, :], g_norm_b[i][None, :], batch, seq)

        w_router = jnp.pad(jnp.concatenate([w_router_g[i], w_router_e[i]], axis=1),
                           ((0, 0), (0, LANES - N_GROUPS - N_EXPERTS)))
        b_router = jnp.pad(jnp.concatenate([b_router_g[i], b_router_e[i]]),
                           (0, LANES - N_GROUPS - N_EXPERTS))[None, :]
        x1, h2_rows, logits = _outproj(oa, ob, w_out[i].astype(BF16), xf, gt_m, g_post_mix[i][None, :],
                                       g_pre_ffn[i][None, :], sc_f, sh_f, w_router, b_router, seq)

        rec, rec_rows, counts = _route(logits)
        (slot, pad_start, pad_len, used, texp, tblk, tstate, tnext, tslot,
         n_slots) = _dispatch_tables(rec_rows, counts, t, tm_e)
        xs = _dispatch(slot, pad_start, pad_len, used, h2_rows, n_slots, tm_e)
        eo = _experts(texp, tblk, tstate, tnext, tslot, xs, w1[i], w3[i], w2[i], tm_e)
        xf = _final(slot, eo, rec, x1, gt_f, g_post_ffn[i][None, :], seq)
    return xf.reshape(batch, seq, d)
```

```python
import functools
import math

import jax
import jax.numpy as jnp
from jax import lax
from jax.experimental import pallas as pl
from jax.experimental.pallas import tpu as pltpu

F32 = jnp.float32
BF16 = jnp.bfloat16

D_MODEL = 2048
CHUNK = 64
A_HEADS = 8
A_DK = 64
A_DV = 2 * A_DK
A_WIDTH = A_HEADS * A_DV
B_HEADS = 4
B_WIDTH = D_MODEL - A_WIDTH
B_DV = B_WIDTH // B_HEADS
B_DK = B_DV // 2
GATE_RANK = 16
GATE_TAU = 16.0
N_BUCKETS = 32
MAX_DISTANCE = 256
N_GROUPS = 4
EXPERTS_PER_GROUP = 8
N_EXPERTS = N_GROUPS * EXPERTS_PER_GROUP
EPS = 1e-6
NEG_INF = -1e30
LOG2E = math.log2(math.e)

LANES = 128
U32 = jnp.uint32
HALF_D = D_MODEL // 2
ROW_SLABS = HALF_D // LANES
ROW_CHUNK = 16
CHUNK_UNROLL = 8
ONES_ROWS = 16
FAR_BIAS_DISTANCE = 166

ADA_TN = 1024
INPROJ_TM, INPROJ_TN = 512, 1024
ATTN_TQ, ATTN_TK = 512, 256
GLA_LC = 512
GLA_HEADS_PER_STEP = 4
OUTPROJ_TM = 512
OUTPROJ_PIECE = 256
ROUTE_TR = 1024
EXPERT_TM = 512
DISPATCH_TD = 2048
COMBINE_TF = 256
D_MAIN = 3 * A_WIDTH + 2 * B_HEADS * B_DK + 2 * B_WIDTH
COL_QA, COL_KA, COL_VA = 0, A_HEADS, 2 * A_HEADS
COL_QB = 3 * A_HEADS
COL_KB = COL_QB + B_HEADS
COL_VB256 = (3 * A_WIDTH + 2 * B_HEADS * B_DK) // B_DV
COL_RB256 = COL_VB256 + B_HEADS
ROUTE_E1, ROUTE_E2, ROUTE_W1, ROUTE_W2, ROUTE_R1, ROUTE_R2 = 0, 1, 2, 3, 4, 5
ROUTE_FIELD_ROWS = 8
ROUTER_EXPERT_LANE0 = N_GROUPS
ROUTER_ROWS = 48

VMEM_LIMIT = 56 * 1024 * 1024
INPROJ_VMEM_LIMIT = 60 * 1024 * 1024


def _params(*sem):
    return pltpu.CompilerParams(dimension_semantics=sem, vmem_limit_bytes=VMEM_LIMIT)


def _rms(v):
    return v * lax.rsqrt(jnp.mean(v * v, axis=-1, keepdims=True) + EPS)


def _silu(v):
    return v * jax.nn.sigmoid(v)


_HIGH_HALF = 0xFFFF0000


def _pack_bf16_pair(lo, hi):
    lo_bits = lax.bitcast_convert_type(lo.astype(BF16).astype(F32), U32) >> 16
    hi_bits = lax.bitcast_convert_type(hi.astype(BF16).astype(F32), U32) & U32(_HIGH_HALF)
    return hi_bits | lo_bits


def _unpack_bf16_pair(w):
    return (lax.bitcast_convert_type(w << 16, F32), lax.bitcast_convert_type(w & U32(_HIGH_HALF), F32))


def _store_row_slabs(ref, r0, nrows, rows_f32):
    packed = _pack_bf16_pair(rows_f32[:, :HALF_D], rows_f32[:, HALF_D:])
    for s in range(ROW_SLABS):
        ref[pl.ds(r0 * ROW_SLABS + s, nrows, stride=ROW_SLABS), :] = packed[:, s * LANES:(s + 1) * LANES]


def _load_row_slabs(ref, r0, nrows):
    slabs = [_unpack_bf16_pair(ref[pl.ds(r0 * ROW_SLABS + s, nrows, stride=ROW_SLABS), :]) for s in range(ROW_SLABS)]
    return (jnp.concatenate([lo for lo, _ in slabs], axis=1), jnp.concatenate([hi for _, hi in slabs], axis=1))


def _ada_kernel(c_ref, w_ref, b_ref, o_ref):
    s = _silu(c_ref[...])
    o_ref[...] = jnp.dot(s.astype(BF16), w_ref[...].astype(BF16), preferred_element_type=F32) + b_ref[...]


def _ada(c_pad, w, b, tn=ADA_TN):
    m, d = c_pad.shape
    n = w.shape[1]
    return pl.pallas_call(
        _ada_kernel,
        out_shape=jax.ShapeDtypeStruct((m, n), F32),
        grid=(n // tn,),
        in_specs=[pl.BlockSpec((m, d), lambda j: (0, 0)),
                  pl.BlockSpec((d, tn), lambda j: (0, j)),
                  pl.BlockSpec((1, tn), lambda j: (0, j))],
        out_specs=pl.BlockSpec((m, tn), lambda j: (0, j)),
        compiler_params=_params("arbitrary"),
        name="ada_proj",
    )(c_pad, w, b)


def _inproj_kernel(x_ref, xn_ref, g_ref, sc_ref, sh_ref, scn_ref, shn_ref, w_ref, wz_ref, o_ref, z_ref,
                   h_scr, hn_scr, *, tm, tn):
    def normed(src_ref, scale_ref, shift_ref, rows):
        h = _rms(src_ref[rows, :]) * g_ref[...]
        return (h * (1.0 + scale_ref[0]) + shift_ref[0]).astype(BF16)

    @pl.when(pl.program_id(0) == 0)
    def _():
        def chunk(c, carry):
            rows = pl.ds(pl.multiple_of(c * ROW_CHUNK, ROW_CHUNK), ROW_CHUNK)
            h_scr[rows, :] = normed(x_ref, sc_ref, sh_ref, rows)
            return carry
        lax.fori_loop(0, tm // ROW_CHUNK, chunk, 0, unroll=CHUNK_UNROLL)

    h = h_scr[...]
    z_ref[...] = jnp.dot(h, wz_ref[...], preferred_element_type=F32)
    n_col = D_MAIN // tn
    rows_per_col = tm // n_col // ROW_CHUNK * ROW_CHUNK
    next_row = 0
    for c in range(n_col):
        cols = slice(c * tn, (c + 1) * tn)
        o_ref[:, cols] = jnp.dot(h, w_ref[:, cols], preferred_element_type=F32).astype(BF16)
        stop = tm if c == n_col - 1 else next_row + rows_per_col
        for r0 in range(next_row, stop, ROW_CHUNK):
            rows = slice(r0, r0 + ROW_CHUNK)
            hn_scr[rows, :] = normed(xn_ref, scn_ref, shn_ref, rows)
        next_row = stop
    h_scr[...] = hn_scr[...]


def _inproj(x2d, g, sc, sh, w_all, w_z, seq, tm=INPROJ_TM, tn=INPROJ_TN):
    t, d = x2d.shape
    tm = min(tm, seq)
    per_b = seq // tm
    n_steps = t // tm
    kern = functools.partial(_inproj_kernel, tm=tm, tn=tn)
    nxt = lambda i: jnp.minimum(i + 1, n_steps - 1)
    per_batch = lambda step: pl.BlockSpec((1, 1, d), lambda i: (step(i) // per_b, 0, 0))
    return pl.pallas_call(
        kern,
        out_shape=(jax.ShapeDtypeStruct((t, D_MAIN), BF16), jax.ShapeDtypeStruct((t, LANES), F32)),
        grid=(n_steps,),
        in_specs=[pl.BlockSpec((tm, d), lambda i: (i, 0)),
                  pl.BlockSpec((tm, d), lambda i: (nxt(i), 0)),
                  pl.BlockSpec((1, d), lambda i: (0, 0)),
                  per_batch(lambda i: i), per_batch(lambda i: i), per_batch(nxt), per_batch(nxt),
                  pl.BlockSpec(w_all.shape, lambda i: (0, 0), pipeline_mode=pl.Buffered(1)),
                  pl.BlockSpec((d, LANES), lambda i: (0, 0), pipeline_mode=pl.Buffered(1))],
        out_specs=(pl.BlockSpec((tm, D_MAIN), lambda i: (i, 0)),
                   pl.BlockSpec((tm, LANES), lambda i: (i, 0))),
        scratch_shapes=[pltpu.VMEM((tm, d), BF16), pltpu.VMEM((tm, d), BF16)],
        compiler_params=pltpu.CompilerParams(dimension_semantics=("arbitrary",),
                                             vmem_limit_bytes=INPROJ_VMEM_LIMIT),
        name="in_proj",
    )(x2d, x2d, g, sc, sh, sc, sh, w_all, w_z)


def _t5_bucket(rel):
    nb = N_BUCKETS // 2
    max_exact = nb // 2
    base = jnp.where(rel > 0, nb, 0)
    n = jnp.abs(rel)
    nf = jnp.maximum(n, 1).astype(F32)
    large = max_exact + (jnp.log(nf / max_exact) / math.log(MAX_DISTANCE / max_exact)
                         * (nb - max_exact)).astype(jnp.int32)
    large = jnp.minimum(large, nb - 1)
    return base + jnp.where(n < max_exact, n, large)


def _bias_buckets(tq):
    kj = jnp.arange(tq, dtype=jnp.int32)[:, None]
    qi = jnp.arange(tq, dtype=jnp.int32)[None, :]
    near = _t5_bucket(kj - qi - tq)
    diag = jnp.where((kj // CHUNK) <= (qi // CHUNK), _t5_bucket(kj - qi), N_BUCKETS)
    return jnp.stack([near, diag]).astype(jnp.int32)


def _bias_kernel(rb_ref, bk_ref, o_ref, *, tq):
    h = pl.program_id(0)
    nb = N_BUCKETS // 2
    far = rb_ref[nb - 1, h]

    def lookup(bucket, buckets):
        acc = jnp.full(bucket.shape, NEG_INF, F32)
        for n in buckets:
            acc = jnp.where(bucket == n, (rb_ref[n, h] - far) * LOG2E, acc)
        return acc

    @pl.when(pl.program_id(1) == 0)
    def _():
        beyond = (tq - FAR_BIAS_DISTANCE + 1) // 8 * 8
        o_ref[:beyond, :] = jnp.zeros((beyond, tq), F32)
        o_ref[beyond:, :] = lookup(bk_ref[beyond:, :], range(1, nb))

    @pl.when(pl.program_id(1) == 1)
    def _():
        o_ref[...] = lookup(bk_ref[...], range(N_BUCKETS))


def _bias_tiles(rel_bias, tq):
    return pl.pallas_call(
        functools.partial(_bias_kernel, tq=tq),
        out_shape=jax.ShapeDtypeStruct((A_HEADS, 2, tq, tq), F32),
        grid=(A_HEADS, 2),
        in_specs=[pl.BlockSpec(memory_space=pltpu.SMEM),
                  pl.BlockSpec((None, tq, tq), lambda h, d: (d, 0, 0))],
        out_specs=pl.BlockSpec((None, None, tq, tq), lambda h, d: (h, d, 0, 0)),
        compiler_params=_params("arbitrary", "arbitrary"),
        name="bias_tiles",
    )(rel_bias, _bias_buckets(tq))


def _attn_kernel(q_ref, k_ref, v_ref, bias_ref, lq1_ref, lk1_ref, lq2_ref, lk2_ref, g_ref, o_ref,
                 vt_scr, sa_scr, sb_scr, m_scr, acc_scr, *, tq, tk, lam_init):
    nsub = tq // tk
    nq = q_ref.shape[0] // tq
    bufs = (sa_scr, sb_scr)

    ones = jnp.ones((ONES_ROWS, tk), BF16)
    for c in range(vt_scr.shape[0]):
        vt = v_ref[c * tk:(c + 1) * tk, :].astype(F32).T.astype(BF16)
        vt_scr[c] = jnp.concatenate([vt, ones], axis=0)

    lane = lax.broadcasted_iota(jnp.int32, (1, A_DV), 1)

    def query_tile(i, carry):
        def tile_rows(tile):
            return pl.ds(pl.multiple_of(tile * tq, tq), tq)

        def two_map_queries(tile):
            q = q_ref[tile_rows(tile), :] * (A_DK ** -0.5 * LOG2E)
            zero = jnp.zeros_like(q)
            return jnp.concatenate([jnp.where(lane < A_DK, q, zero), jnp.where(lane >= A_DK, q, zero)], axis=0)

        q2 = two_map_queries(i)
        m_scr[...] = jnp.full(m_scr.shape, NEG_INF, F32)
        acc_scr[...] = jnp.zeros(acc_scr.shape, F32)

        def scores(j, queries=q2, q_lo=0):
            k = k_ref[pl.ds(pl.multiple_of(j * tk, tk), tk), :]
            if q_lo:
                queries = jnp.concatenate([queries[q_lo:tq], queries[tq + q_lo:]], axis=0)
            return lax.dot_general(k, queries, (((1,), (1,)), ((), ())), preferred_element_type=F32)

        def softmax_pv(s_ref, j, bias, q_lo=0):
            width = tq - q_lo
            halves = (slice(q_lo, tq), slice(tq + q_lo, 2 * tq))
            both = lambda ref: ref[...] if not q_lo else jnp.concatenate([ref[:, h] for h in halves], axis=1)
            s = s_ref[:, :2 * width]
            if bias is not None:
                s = jnp.concatenate([s[:, :width] + bias, s[:, width:] + bias], axis=1)
            m_old = both(m_scr)
            m_new = jnp.maximum(m_old, jnp.max(s, axis=0, keepdims=True))
            alpha = jnp.exp2(m_old - m_new)
            p = jnp.exp2(s - m_new).astype(BF16)
            acc = alpha * both(acc_scr) + jnp.dot(vt_scr[j], p, preferred_element_type=F32)
            if not q_lo:
                acc_scr[...] = acc
                m_scr[...] = m_new
            else:
                for n, h in enumerate(halves):
                    acc_scr[:, h] = acc[:, n * width:(n + 1) * width]
                    m_scr[:, h] = m_new[:, n * width:(n + 1) * width]

        n_far = jnp.maximum(i - 1, 0) * nsub

        @pl.when(i == 0)
        def _():
            sa_scr[...] = scores(0)

        def far_steps(j, count):
            for c in range(count):
                bufs[(c + 1) % 2][...] = scores(j + c + 1)
                softmax_pv(bufs[c % 2], j + c, None)

        def far_quad(jj, inner):
            far_steps(4 * jj, 4)
            return inner

        n_quads = n_far // 4
        lax.fori_loop(0, n_quads, far_quad, 0)

        @pl.when(n_far - 4 * n_quads >= 2)
        def _():
            far_steps(4 * n_quads, 2)

        def biased_steps(first_tile):
            j0 = (i - 1 + first_tile) * nsub
            count = (2 - first_tile) * nsub

            def geometry(c):
                d, r = first_tile + c // nsub, (c % nsub) * tk
                return d, r, (r if d == 1 else 0)

            for c in range(count):
                if c + 1 < count:
                    q_next = geometry(c + 1)[2]
                    bufs[(c + 1) % 2][:, :2 * (tq - q_next)] = scores(j0 + c + 1, q_lo=q_next)
                d, r, q_lo = geometry(c)
                no_bias = d == 0 and r + tk - 1 - tq <= -FAR_BIAS_DISTANCE
                bias = None if no_bias else bias_ref[d, r:r + tk, q_lo:]
                softmax_pv(bufs[c % 2], j0 + c, bias, q_lo)

        def finish():
            lam = (jnp.exp(jnp.sum(lq1_ref[...] * lk1_ref[...], axis=-1, keepdims=True))
                   - jnp.exp(jnp.sum(lq2_ref[...] * lk2_ref[...], axis=-1, keepdims=True)) + lam_init)
            on = acc_scr[:A_DV, :] / acc_scr[A_DV:A_DV + 1, :]
            o = on[:, :tq] - lam * on[:, tq:]
            y = o * lax.rsqrt(jnp.mean(o * o, axis=0, keepdims=True) + EPS) * g_ref[...] * (1.0 - lam_init)
            o_ref[tile_rows(i), :] = y.T.astype(BF16)

        last = i + 1 == nq
        for first_tile, applies in ((0, i >= 1), (1, i == 0)):
            @pl.when(applies & jnp.logical_not(last))
            def _():
                biased_steps(first_tile)
                sa_scr[...] = scores(0, two_map_queries(i + 1))
                finish()

            @pl.when(applies & last)
            def _():
                biased_steps(first_tile)
                finish()
        return carry

    lax.fori_loop(0, nq, query_tile, 0)


def _attention(proj, bias_tiles, lq1, lk1, lq2, lk2, g_sub_col, batch, seq, lam_init, tq, tk=ATTN_TK):
    t = proj.shape[0]
    assert (tq // tk) % 2 == 0 and tq % tk == 0, "the score pipeline alternates two buffers per query tile"
    assert tq + 1 >= FAR_BIAS_DISTANCE, "key tiles two or more before the query tile must be past the bias horizon"
    kern = functools.partial(_attn_kernel, tq=tq, tk=tk, lam_init=lam_init)
    vec = lambda n: pl.BlockSpec((1, n), lambda b, h: (0, 0))
    return pl.pallas_call(
        kern,
        out_shape=jax.ShapeDtypeStruct((t, A_WIDTH), BF16),
        grid=(batch, A_HEADS),
        in_specs=[pl.BlockSpec((seq, A_DV), lambda b, h: (b, COL_QA + h)),
                  pl.BlockSpec((seq, A_DV), lambda b, h: (b, COL_KA + h)),
                  pl.BlockSpec((seq, A_DV), lambda b, h: (b, COL_VA + h)),
                  pl.BlockSpec((None, 2, tq, tq), lambda b, h: (h, 0, 0, 0)),
                  vec(A_DK), vec(A_DK), vec(A_DK), vec(A_DK),
                  pl.BlockSpec((A_DV, 1), lambda b, h: (0, 0))],
        out_specs=pl.BlockSpec((seq, A_DV), lambda b, h: (b, h)),
        scratch_shapes=[pltpu.VMEM((seq // tk, A_DV + ONES_ROWS, tk), BF16),
                        pltpu.VMEM((tk, 2 * tq), F32), pltpu.VMEM((tk, 2 * tq), F32),
                        pltpu.VMEM((1, 2 * tq), F32),
                        pltpu.VMEM((A_DV + ONES_ROWS, 2 * tq), F32)],
        compiler_params=_params("arbitrary", "arbitrary"),
        name="diff_attention",
    )(proj, proj, proj, bias_tiles, lq1, lk1, lq2, lk2, g_sub_col)


def _split3(a):
    a1 = a.astype(BF16)
    r1 = a - a1.astype(F32)
    a2 = r1.astype(BF16)
    return a1, a2, (r1 - a2.astype(F32)).astype(BF16)


def _sum3(x):
    return x[:, :B_DK] + x[:, B_DK:2 * B_DK] + x[:, 2 * B_DK:]


def _gla_kernel(q_ref, k_ref, v_ref, r_ref, z_ref, wa_ref, ba_ref, g_ref, o_ref,
                state_scr, mask_scr, kv_scr, st_scr, *, n_chunks):
    lc = n_chunks * CHUNK

    @pl.when(pl.program_id(2) == 0)
    def _():
        state_scr[...] = jnp.zeros(state_scr.shape, F32)
        row = lax.broadcasted_iota(jnp.int32, (lc, lc), 0)
        col = lax.broadcasted_iota(jnp.int32, (lc, lc), 1)
        same = (row // CHUNK) == (col // CHUNK)
        mask_scr[...] = (same & (row >= col)).astype(BF16)

    z = z_ref[...]
    zh = z.astype(BF16)
    zl = (z - zh.astype(F32)).astype(BF16)
    z3 = jnp.concatenate([zh, zl, zh], axis=1)
    kcols = lambda hh: slice(hh * B_DK, (hh + 1) * B_DK)
    vcols = lambda hh: slice(hh * B_DV, (hh + 1) * B_DV)
    chunk_rows = lambda c: slice(c * CHUNK, (c + 1) * CHUNK)
    live = [dict() for _ in range(GLA_HEADS_PER_STEP)]

    def gate_stage(hh):
        pre = jnp.dot(z3, wa_ref[:, kcols(hh)], preferred_element_type=F32) + ba_ref[:, kcols(hh)]
        log_a = (jnp.minimum(pre, 0.0) - jnp.log1p(jnp.exp(-jnp.abs(pre)))) * (1.0 / GATE_TAU)
        live[hh]["parts"] = jnp.concatenate(_split3(log_a), axis=1)

    def decay_stage(hh):
        cum = _sum3(jnp.dot(mask_scr[...], live[hh].pop("parts"), preferred_element_type=F32))
        totals = [cum[(c + 1) * CHUNK - 1:(c + 1) * CHUNK, :] for c in range(n_chunks)]
        total = jnp.concatenate([jnp.broadcast_to(tc, (CHUNK, B_DK)) for tc in totals], axis=0)
        live[hh]["totals"] = totals
        live[hh]["k_dec"] = (k_ref[:, kcols(hh)].astype(F32) * jnp.exp(total - cum)).astype(BF16)

    def kv_stage(hh):
        k_dec = live[hh].pop("k_dec")
        for c in range(n_chunks):
            kv_scr[hh, c] = lax.dot_general(v_ref[chunk_rows(c), vcols(hh)], k_dec[chunk_rows(c)],
                                            (((0,), (0,)), ((), ())), preferred_element_type=F32)

    def state_stage(hh):
        totals = live[hh].pop("totals")
        state = state_scr[hh]
        for c in range(n_chunks):
            state = state * jnp.exp(totals[c]) + kv_scr[hh, c]
            st_scr[hh, c] = state.astype(BF16)
        state_scr[hh] = state

    def output_stage(hh):
        for c in range(n_chunks):
            rows = chunk_rows(c)
            o = lax.dot_general(q_ref[rows, kcols(hh)], st_scr[hh, c], (((1,), (1,)), ((), ())),
                                preferred_element_type=F32) * (B_DK ** -0.5)
            o_ref[rows, vcols(hh)] = (_rms(o) * g_ref[...]
                                      * _silu(r_ref[rows, vcols(hh)].astype(F32))).astype(BF16)

    stages = (gate_stage, decay_stage, kv_stage, state_stage, output_stage)
    for tick in range(GLA_HEADS_PER_STEP + len(stages) - 1):
        for k, stage in enumerate(stages):
            if 0 <= tick - k < GLA_HEADS_PER_STEP:
                stage(tick - k)


def _gla(proj, zb, w_alpha_pad, b_alpha, g_norm, batch, seq, lc=GLA_LC):
    t = proj.shape[0]
    lc = min(lc, seq)
    nl = seq // lc
    n_chunks = lc // CHUNK
    hps = GLA_HEADS_PER_STEP
    assert B_HEADS % hps == 0 and COL_QB % hps == 0 and COL_KB % hps == 0
    kern = functools.partial(_gla_kernel, n_chunks=n_chunks)
    wa_hi = w_alpha_pad.astype(BF16)
    wa_lo = (w_alpha_pad - wa_hi.astype(F32)).astype(BF16)
    wa3 = jnp.concatenate([wa_hi, wa_hi, wa_lo], axis=0)
    kblock = lambda col0: pl.BlockSpec((lc, hps * B_DK), lambda b, h, l: (b * nl + l, col0 // hps + h))
    vblock = lambda col0: pl.BlockSpec((lc, hps * B_DV), lambda b, h, l: (b * nl + l, col0 // hps + h))
    return pl.pallas_call(
        kern,
        out_shape=jax.ShapeDtypeStruct((t, B_WIDTH), BF16),
        grid=(batch, B_HEADS // hps, nl),
        in_specs=[kblock(COL_QB), kblock(COL_KB), vblock(COL_VB256), vblock(COL_RB256),
                  pl.BlockSpec((lc, LANES), lambda b, h, l: (b * nl + l, 0)),
                  pl.BlockSpec((3 * LANES, hps * B_DK), lambda b, h, l: (0, h)),
                  pl.BlockSpec((1, hps * B_DK), lambda b, h, l: (0, h)),
                  pl.BlockSpec((1, B_DV), lambda b, h, l: (0, 0))],
        out_specs=pl.BlockSpec((lc, hps * B_DV), lambda b, h, l: (b * nl + l, h)),
        scratch_shapes=[pltpu.VMEM((hps, B_DV, B_DK), F32),
                        pltpu.VMEM((lc, lc), BF16),
                        pltpu.VMEM((hps, n_chunks, B_DV, B_DK), F32),
                        pltpu.VMEM((hps, n_chunks, B_DV, B_DK), BF16)],
        compiler_params=_params("arbitrary", "arbitrary", "arbitrary"),
        name="gla",
    )(proj, proj, proj, proj, zb, wa3, b_alpha, g_norm)


def _outproj_kernel(oa_ref, ob_ref, wo_ref, x_ref, gt_ref, gpost_ref, gpre_ref, sc_ref, sh_ref, wr_ref,
                    br_ref, x1_ref, h2_ref, lg_ref, *, tm):
    n_pieces = tm // OUTPROJ_PIECE

    def project(p):
        prow = slice(p * OUTPROJ_PIECE, (p + 1) * OUTPROJ_PIECE)
        return (jnp.dot(oa_ref[prow, :], wo_ref[:A_WIDTH, :], preferred_element_type=F32)
                + jnp.dot(ob_ref[prow, :], wo_ref[A_WIDTH:, :], preferred_element_type=F32))

    y_next = project(0)
    for p in range(n_pieces):
        p0 = p * OUTPROJ_PIECE
        prow = slice(p0, p0 + OUTPROJ_PIECE)
        y = y_next
        if p + 1 < n_pieces:
            y_next = project(p + 1)
        his, los = [], []
        for c in range(OUTPROJ_PIECE // ROW_CHUNK):
            r0 = p0 + c * ROW_CHUNK
            rows = slice(r0, r0 + ROW_CHUNK)
            x1 = x_ref[rows, :] + gt_ref[0] * (_rms(y[c * ROW_CHUNK:(c + 1) * ROW_CHUNK]) * gpost_ref[...])
            x1_ref[rows, :] = x1
            h2 = (_rms(x1) * gpre_ref[...]) * (1.0 + sc_ref[0]) + sh_ref[0]
            hi = h2.astype(BF16)
            his.append(hi)
            los.append((h2 - hi.astype(F32)).astype(BF16))
            _store_row_slabs(h2_ref, r0, ROW_CHUNK, h2)
        hi, lo = jnp.concatenate(his, axis=0), jnp.concatenate(los, axis=0)
        hw = jnp.dot(hi, wr_ref[...], preferred_element_type=F32)
        lw = jnp.dot(lo, wr_ref[:, :LANES], preferred_element_type=F32)
        lg_ref[prow, :] = hw[:, :LANES] + hw[:, LANES:] + lw + br_ref[...]


def _outproj(oa, ob, w_out, x2d, gt, g_post, g_pre, sc, sh, w_router, b_router, seq, tm=OUTPROJ_TM):
    t, d = x2d.shape
    tm = min(tm, seq)
    per_b = seq // tm
    kern = functools.partial(_outproj_kernel, tm=tm)
    wr_hi = w_router.astype(BF16)
    wr_lo = (w_router - wr_hi.astype(F32)).astype(BF16)
    wr_cat = jnp.concatenate([wr_hi, wr_lo], axis=1)
    row = lambda: pl.BlockSpec((1, d), lambda i: (0, 0))
    per_batch = lambda: pl.BlockSpec((1, 1, d), lambda i: (i // per_b, 0, 0))
    return pl.pallas_call(
        kern,
        out_shape=(jax.ShapeDtypeStruct((t, d), F32),
                   jax.ShapeDtypeStruct((t * ROW_SLABS, LANES), U32),
                   jax.ShapeDtypeStruct((t, LANES), F32)),
        grid=(t // tm,),
        in_specs=[pl.BlockSpec((tm, A_WIDTH), lambda i: (i, 0)),
                  pl.BlockSpec((tm, B_WIDTH), lambda i: (i, 0)),
                  pl.BlockSpec((d, d), lambda i: (0, 0)),
                  pl.BlockSpec((tm, d), lambda i: (i, 0)),
                  per_batch(), row(), row(), per_batch(), per_batch(),
                  pl.BlockSpec((d, 2 * LANES), lambda i: (0, 0)),
                  pl.BlockSpec((1, LANES), lambda i: (0, 0))],
        out_specs=(pl.BlockSpec((tm, d), lambda i: (i, 0)),
                   pl.BlockSpec((tm * ROW_SLABS, LANES), lambda i: (i, 0)),
                   pl.BlockSpec((tm, LANES), lambda i: (i, 0))),
        compiler_params=_params("arbitrary"),
        name="out_proj",
    )(oa, ob, w_out, x2d, gt, g_post, g_pre, sc, sh, wr_cat, b_router)


def _route_kernel(lg_ref, rec_ref, rows_ref, cnt_ref, carry_scr, before_scr, *, tr):
    @pl.when(pl.program_id(0) == 0)
    def _():
        carry_scr[...] = jnp.zeros(carry_scr.shape, F32)
        earlier = lax.broadcasted_iota(jnp.int32, (tr, tr), 0)
        token = lax.broadcasted_iota(jnp.int32, (tr, tr), 1)
        before_scr[...] = (earlier < token).astype(BF16)

    lg = lg_ref[...].T[:ROUTER_ROWS]
    row = lax.broadcasted_iota(jnp.int32, lg.shape, 0)
    big = jnp.int32(ROUTER_ROWS)

    def first_row(mask):
        return jnp.min(jnp.where(mask, row, big), axis=0, keepdims=True)

    gmask = row < N_GROUPS
    gmax = jnp.max(jnp.where(gmask, lg, -jnp.inf), axis=0, keepdims=True)
    gexp = jnp.where(gmask, jnp.exp(lg - gmax), 0.0)
    gprob = gexp / jnp.sum(gexp, axis=0, keepdims=True)
    g_val = jnp.max(gprob, axis=0, keepdims=True)
    g_idx = first_row(gmask & (gprob == g_val))

    lo = ROUTER_EXPERT_LANE0 + g_idx * EXPERTS_PER_GROUP
    emask = (row >= lo) & (row < lo + EXPERTS_PER_GROUP)
    emax = jnp.max(jnp.where(emask, lg, -jnp.inf), axis=0, keepdims=True)
    eexp = jnp.where(emask, jnp.exp(lg - emax), 0.0)
    eprob = eexp / jnp.sum(eexp, axis=0, keepdims=True)
    v1 = jnp.max(eprob, axis=0, keepdims=True)
    i1 = first_row(emask & (eprob == v1))
    rest = emask & (row != i1)
    v2 = jnp.max(jnp.where(rest, eprob, -1.0), axis=0, keepdims=True)
    i2 = first_row(rest & (eprob == v2))
    w1 = g_val * (v1 / (v1 + v2))
    w2 = g_val * (v2 / (v1 + v2))

    hit1 = row == i1
    hit2 = row == i2
    onehot = (hit1 | hit2).astype(BF16)
    pos = carry_scr[...] + jnp.dot(onehot, before_scr[...], preferred_element_type=F32)
    rank1 = jnp.sum(jnp.where(hit1, pos, 0.0), axis=0, keepdims=True)
    rank2 = jnp.sum(jnp.where(hit2, pos, 0.0), axis=0, keepdims=True)
    carry_scr[...] = carry_scr[...] + jnp.sum(onehot.astype(F32), axis=1, keepdims=True)
    cnt_ref[...] = carry_scr[...]

    e1 = (i1 - ROUTER_EXPERT_LANE0).astype(F32)
    e2 = (i2 - ROUTER_EXPERT_LANE0).astype(F32)
    field = lax.broadcasted_iota(jnp.int32, (LANES, tr), 0)
    rec = jnp.zeros((LANES, tr), F32)
    for ln, val in ((ROUTE_E1, e1), (ROUTE_E2, e2), (ROUTE_W1, w1), (ROUTE_W2, w2),
                    (ROUTE_R1, rank1), (ROUTE_R2, rank2)):
        rec = jnp.where(field == ln, val, rec)
    rec_ref[...] = rec.T
    rows_ref[...] = rec[:ROUTE_FIELD_ROWS]


def _route(logits, tr=ROUTE_TR):
    t = logits.shape[0]
    tr = min(tr, t)
    kern = functools.partial(_route_kernel, tr=tr)
    return pl.pallas_call(
        kern,
        out_shape=(jax.ShapeDtypeStruct((t, LANES), F32), jax.ShapeDtypeStruct((ROUTE_FIELD_ROWS, t), F32),
                   jax.ShapeDtypeStruct((ROUTER_ROWS, 1), F32)),
        grid=(t // tr,),
        in_specs=[pl.BlockSpec((tr, LANES), lambda i: (i, 0))],
        out_specs=(pl.BlockSpec((tr, LANES), lambda i: (i, 0)),
                   pl.BlockSpec((ROUTE_FIELD_ROWS, tr), lambda i: (0, i)),
                   pl.BlockSpec((ROUTER_ROWS, 1), lambda i: (0, 0))),
        scratch_shapes=[pltpu.VMEM((ROUTER_ROWS, 1), F32), pltpu.VMEM((tr, tr), BF16)],
        compiler_params=_params("arbitrary"),
        name="route",
    )(logits)


def _slab_rows(ref, row):
    return ref.at[pl.ds(pl.multiple_of(row * ROW_SLABS, ROW_SLABS), ROW_SLABS), :]


def _dispatch_kernel(slot_ref, pad_start_ref, pad_len_ref, used_ref, h2_ref, xs_hbm, zero_scr, sems, pad_sems,
                     *, td, tm):
    g = pl.program_id(0)
    tile_rows = tm * ROW_SLABS
    n_tiles = xs_hbm.shape[0] // tile_rows

    def zero_copy(slot, nslots, sem):
        rows = pl.ds(pl.multiple_of(slot * ROW_SLABS, ROW_SLABS), nslots * ROW_SLABS)
        return pltpu.make_async_copy(zero_scr.at[pl.ds(0, nslots * ROW_SLABS), :], xs_hbm.at[rows, :], sem)

    pad_sizes = [1 << b for b in reversed(range((tm - 1).bit_length()))]

    @pl.when(g == 0)
    def _():
        zero_scr[...] = jnp.zeros(zero_scr.shape, U32)

        def start_unused(tile, carry):
            zero_copy(tile * tm, tm, sems.at[1]).start()
            return carry

        def drain_unused(tile, carry):
            zero_copy(0, tm, sems.at[1]).wait()
            return carry

        lax.fori_loop(used_ref[0], n_tiles, start_unused, 0)

        def per_expert(e, counts):
            off = pad_start_ref[e]
            n = pad_len_ref[e]
            new_counts = []
            for b, size in enumerate(pad_sizes):
                hit = (n & size) != 0

                @pl.when(hit)
                def _():
                    zero_copy(off, size, pad_sems.at[b]).start()

                off = off + jnp.where(hit, size, 0)
                new_counts.append(counts[b] + hit.astype(jnp.int32))
            return tuple(new_counts)

        counts = lax.fori_loop(0, N_EXPERTS, per_expert, tuple(jnp.int32(0) for _ in pad_sizes))
        for b, size in enumerate(pad_sizes):
            def drain(r, c):
                zero_copy(0, size, pad_sems.at[b]).wait()
                return c
            lax.fori_loop(0, counts[b], drain, 0)
        lax.fori_loop(used_ref[0], n_tiles, drain_unused, 0)

    def row_copy(r, slot):
        return pltpu.make_async_copy(_slab_rows(h2_ref, r), _slab_rows(xs_hbm, slot), sems.at[0])

    base = g * td
    n_tok = slot_ref.shape[0] // 2

    def issue(r, c):
        tok = base + r
        row_copy(r, slot_ref[tok]).start(priority=0)
        row_copy(r, slot_ref[n_tok + tok]).start(priority=1)
        return c

    lax.fori_loop(0, td, issue, 0, unroll=CHUNK_UNROLL)
    for _ in range(2):
        pltpu.make_async_copy(h2_ref, xs_hbm.at[pl.ds(0, td * ROW_SLABS), :], sems.at[0]).wait()


def _dispatch(slot, pad_start, pad_len, used, h2_rows, n_slots, tm, td=DISPATCH_TD):
    t = slot.shape[0] // 2
    td = min(td, t)
    kern = functools.partial(_dispatch_kernel, td=td, tm=tm)
    grid_spec = pltpu.PrefetchScalarGridSpec(
        num_scalar_prefetch=4,
        grid=(t // td,),
        in_specs=[pl.BlockSpec((td * ROW_SLABS, LANES), lambda g, sl, ps, pn, us: (g, 0))],
        out_specs=pl.BlockSpec(memory_space=pl.ANY),
        scratch_shapes=[pltpu.VMEM((tm * ROW_SLABS, LANES), U32), pltpu.SemaphoreType.DMA((2,)),
                        pltpu.SemaphoreType.DMA(((tm - 1).bit_length(),))],
    )
    return pl.pallas_call(
        kern,
        out_shape=jax.ShapeDtypeStruct((n_slots * ROW_SLABS, LANES), U32),
        grid_spec=grid_spec,
        compiler_params=_params("arbitrary"),
        name="dispatch",
    )(slot, pad_start, pad_len, used, h2_rows)


TILE_UNUSED, TILE_USED, TILE_NEW_EXPERT = 0, 1, 2


def _expert_kernel(texp_ref, tblk_ref, tstate_ref, tnext_ref, tpar_ref, xs_ref, w1_hbm, w3_hbm, w2_hbm, eo_ref,
                   x_scr, w1_scr, w3_scr, w2_scr, w1_stage, w3_stage, w2_stage, sems, *, tm):
    i = pl.program_id(0)
    state = tstate_ref[i]
    slot = tpar_ref[i]

    def weight_copies(expert, dst_slot):
        return [pltpu.make_async_copy(hbm.at[expert], stage.at[dst_slot], sems.at[dst_slot])
                for hbm, stage in ((w1_hbm, w1_stage), (w3_hbm, w3_stage), (w2_hbm, w2_stage))]

    @pl.when(state == TILE_UNUSED)
    def _():
        eo_ref[...] = jnp.zeros(eo_ref.shape, U32)

    @pl.when(i == 0)
    def _():
        for cp in weight_copies(texp_ref[0], slot):
            cp.start()

    @pl.when(state == TILE_NEW_EXPERT)
    def _():
        for cp in weight_copies(texp_ref[i], slot):
            cp.wait()

        @pl.when(tnext_ref[i] >= 0)
        def _():
            for cp in weight_copies(tnext_ref[i], 1 - slot):
                cp.start()

        w1_scr[...] = w1_stage[slot].astype(BF16)
        w3_scr[...] = w3_stage[slot].astype(BF16)
        w2_scr[...] = w2_stage[slot].astype(BF16)

    @pl.when(state != TILE_UNUSED)
    def _():
        for s in range(ROW_SLABS):
            lo, hi = _unpack_bf16_pair(xs_ref[pl.ds(s, tm, stride=ROW_SLABS), :])
            x_scr[:, s * LANES:(s + 1) * LANES] = lo.astype(BF16)
            x_scr[:, HALF_D + s * LANES:HALF_D + (s + 1) * LANES] = hi.astype(BF16)
        x = x_scr[...]
        a = jnp.dot(x, w1_scr[...], preferred_element_type=F32)
        b = jnp.dot(x, w3_scr[...], preferred_element_type=F32)
        hid = (_silu(a) * b).astype(BF16)
        y = jnp.dot(hid, w2_scr[...], preferred_element_type=F32)
        _store_row_slabs(eo_ref, 0, tm, y)


def _experts(tile_expert, tile_block, tile_state, tile_next, tile_slot, xs_rows, w1, w3, w2, tm):
    n_tiles = tile_expert.shape[0]
    d, f = w1.shape[1], w1.shape[2]
    kern = functools.partial(_expert_kernel, tm=tm)
    grid_spec = pltpu.PrefetchScalarGridSpec(
        num_scalar_prefetch=5,
        grid=(n_tiles,),
        in_specs=[pl.BlockSpec((tm * ROW_SLABS, LANES), lambda i, te, tb, ts, tn, tp: (tb[i], 0)),
                  pl.BlockSpec(memory_space=pl.ANY), pl.BlockSpec(memory_space=pl.ANY),
                  pl.BlockSpec(memory_space=pl.ANY)],
        out_specs=pl.BlockSpec((tm * ROW_SLABS, LANES), lambda i, te, tb, ts, tn, tp: (i, 0)),
        scratch_shapes=[pltpu.VMEM((tm, d), BF16), pltpu.VMEM((d, f), BF16), pltpu.VMEM((d, f), BF16),
                        pltpu.VMEM((f, d), BF16),
                        pltpu.VMEM((2, d, f), F32), pltpu.VMEM((2, d, f), F32), pltpu.VMEM((2, f, d), F32),
                        pltpu.SemaphoreType.DMA((2,))],
    )
    return pl.pallas_call(
        kern,
        out_shape=jax.ShapeDtypeStruct(xs_rows.shape, U32),
        grid_spec=grid_spec,
        compiler_params=_params("arbitrary"),
        name="expert_mlp",
    )(tile_expert, tile_block, tile_state, tile_next, tile_slot, xs_rows, w1, w3, w2)


def _final_kernel(slot_ref, eo_hbm, rec_ref, x1_ref, gt_ref, g_ref, o_ref, e_scr, sems, *, tf):
    i = pl.program_id(0)
    par = i % 2
    n_tok = slot_ref.shape[0] // 2

    def start_all(step, buf):
        def body(r, c):
            tok = step * tf + r
            for k in range(2):
                pltpu.make_async_copy(_slab_rows(eo_hbm, slot_ref[k * n_tok + tok]),
                                      _slab_rows(e_scr.at[buf, k], r), sems.at[buf]).start(priority=k)
            return c
        lax.fori_loop(0, tf, body, 0, unroll=CHUNK_UNROLL)

    def wait_all(buf):
        for k in range(2):
            pltpu.make_async_copy(eo_hbm.at[pl.ds(0, tf * ROW_SLABS), :], e_scr.at[buf, k], sems.at[buf]).wait()

    @pl.when(i == 0)
    def _():
        start_all(0, 0)

    @pl.when(i + 1 < pl.num_programs(0))
    def _():
        start_all(i + 1, 1 - par)

    wait_all(par)

    def chunk(c, carry):
        r0 = pl.multiple_of(c * ROW_CHUNK, ROW_CHUNK)
        rows = pl.ds(r0, ROW_CHUNK)
        rec = rec_ref[rows, :]
        w1 = rec[:, ROUTE_W1:ROUTE_W1 + 1]
        w2 = rec[:, ROUTE_W2:ROUTE_W2 + 1]
        lo1, hi1 = _load_row_slabs(e_scr.at[par, 0], r0, ROW_CHUNK)
        lo2, hi2 = _load_row_slabs(e_scr.at[par, 1], r0, ROW_CHUNK)
        y = jnp.concatenate([w1 * lo1 + w2 * lo2, w1 * hi1 + w2 * hi2], axis=1)
        o_ref[rows, :] = x1_ref[rows, :] + gt_ref[0] * (_rms(y) * g_ref[...])
        return carry

    lax.fori_loop(0, tf // ROW_CHUNK, chunk, 0, unroll=CHUNK_UNROLL)


def _final(slot, eo_rows, rec, x1, gt, g_post, seq, tf=COMBINE_TF):
    t, d = x1.shape
    tf = min(tf, seq)
    per_b = seq // tf
    kern = functools.partial(_final_kernel, tf=tf)
    grid_spec = pltpu.PrefetchScalarGridSpec(
        num_scalar_prefetch=1,
        grid=(t // tf,),
        in_specs=[pl.BlockSpec(memory_space=pl.ANY),
                  pl.BlockSpec((tf, LANES), lambda i, sl: (i, 0)),
                  pl.BlockSpec((tf, d), lambda i, sl: (i, 0)),
                  pl.BlockSpec((1, 1, d), lambda i, sl: (i // per_b, 0, 0)),
                  pl.BlockSpec((1, d), lambda i, sl: (0, 0))],
        out_specs=pl.BlockSpec((tf, d), lambda i, sl: (i, 0)),
        scratch_shapes=[pltpu.VMEM((2, 2, tf * ROW_SLABS, LANES), U32),
                        pltpu.SemaphoreType.DMA((2,))],
    )
    return pl.pallas_call(
        kern,
        out_shape=jax.ShapeDtypeStruct((t, d), F32),
        grid_spec=grid_spec,
        compiler_params=_params("arbitrary"),
        name="combine_final",
    )(slot, eo_rows, rec, x1, gt, g_post)


def _dispatch_tables(rows, counts, t, tm):
    e = rows[ROUTE_E1:ROUTE_E2 + 1].astype(jnp.int32)
    rank = rows[ROUTE_R1:ROUTE_R2 + 1].astype(jnp.int32)
    cnt = counts[ROUTER_EXPERT_LANE0:ROUTER_EXPERT_LANE0 + N_EXPERTS, 0].astype(jnp.int32)
    tiles_per = (cnt + tm - 1) // tm
    tile_end = jnp.cumsum(tiles_per)
    tile_start = tile_end - tiles_per
    n_tiles = (2 * t + N_EXPERTS * (tm - 1)) // tm
    experts = jnp.arange(N_EXPERTS, dtype=jnp.int32)
    start_of = jnp.sum(jnp.where(e[None] == experts[:, None, None], tile_start[:, None, None], 0),
                       axis=0)
    slot = (start_of * tm + rank).reshape(-1)
    pad_start = tile_start * tm + cnt
    pad_len = tiles_per * tm - cnt
    tile_id = jnp.arange(n_tiles, dtype=jnp.int32)
    used = tile_end[-1]
    tblk = jnp.minimum(tile_id, used - 1)
    texp = jnp.sum(tile_end[None, :] <= tblk[:, None], axis=-1).astype(jnp.int32)
    tstate = jnp.where(tile_id < used, jnp.where(tile_id == tile_start[texp], TILE_NEW_EXPERT, TILE_USED),
                       TILE_UNUSED).astype(jnp.int32)
    nonempty = cnt > 0
    ordinal = jnp.cumsum(nonempty.astype(jnp.int32)) - 1
    later = jnp.where(nonempty[None, :] & (experts[None, :] > experts[:, None]), experts[None, :], N_EXPERTS)
    next_expert = jnp.min(later, axis=-1)
    next_expert = jnp.where(next_expert == N_EXPERTS, -1, next_expert)
    tnext = next_expert[texp].astype(jnp.int32)
    tslot = (ordinal[texp] % 2).astype(jnp.int32)
    return slot, pad_start, pad_len, used.reshape(1), texp, tblk, tstate, tnext, tslot, n_tiles * tm


def kernel(x, c, rel_bias, w_ada, b_ada, g_pre_mix, g_post_mix, w_in, w_alpha, b_alpha, lam_q1, lam_k1, lam_q2,
           lam_k2, g_sub_a, g_norm_b, w_out, g_pre_ffn, g_post_ffn, w_router_g, b_router_g, w_router_e,
           b_router_e, w1, w3, w2):
    batch, seq, d = x.shape
    t = batch * seq
    depth = w_in.shape[0]
    tq = min(ATTN_TQ, seq)
    tm_e = EXPERT_TM
    xf = x.reshape(t, d)
    for i in range(depth):
        lam_init = 0.8 - 0.6 * math.exp(-0.3 * i)
        c_pad = jnp.pad(c, ((0, 8 - batch % 8 if batch % 8 else 0), (0, 0)))
        ada = _ada(c_pad, w_ada[i], b_ada[i][None, :])[:batch]
        sh_m, sc_m, gt_m, sh_f, sc_f, gt_f = [a[:, None, :] for a in jnp.split(ada, 6, axis=-1)]

        w_in_b = w_in[i].astype(BF16)
        w_z = jnp.pad(w_in_b[:, D_MAIN:], ((0, 0), (0, LANES - GATE_RANK)))
        proj, zb = _inproj(xf, g_pre_mix[i][None, :], sc_m, sh_m, w_in_b, w_z, seq)

        oa = _attention(proj, _bias_tiles(rel_bias, tq), lam_q1[i][None, :], lam_k1[i][None, :],
                        lam_q2[i][None, :], lam_k2[i][None, :], g_sub_a[i][:, None], batch, seq, lam_init, tq)
        w_alpha_pad = jnp.pad(w_alpha[i], ((0, LANES - GATE_RANK), (0, 0)))
        ob = _gla(proj, zb, w_alpha_pad, b_alpha[i][None, :], g_norm_b[i][None, :], batch, seq)

        w_router = jnp.pad(jnp.concatenate([w_router_g[i], w_router_e[i]], axis=1),
                           ((0, 0), (0, LANES - N_GROUPS - N_EXPERTS)))
        b_router = jnp.pad(jnp.concatenate([b_router_g[i], b_router_e[i]]),
                           (0, LANES - N_GROUPS - N_EXPERTS))[None, :]
        x1, h2_rows, logits = _outproj(oa, ob, w_out[i].astype(BF16), xf, gt_m, g_post_mix[i][None, :],
                                       g_pre_ffn[i][None, :], sc_f, sh_f, w_router, b_router, seq)

        rec, rec_rows, counts = _route(logits)
        (slot, pad_start, pad_len, used, texp, tblk, tstate, tnext, tslot,
         n_slots) = _dispatch_tables(rec_rows, counts, t, tm_e)
        xs = _dispatch(slot, pad_start, pad_len, used, h2_rows, n_slots, tm_e)
        eo = _experts(texp, tblk, tstate, tnext, tslot, xs, w1[i], w3[i], w2[i], tm_e)
        xf = _final(slot, eo, rec, x1, gt_f, g_post_ffn[i][None, :], seq)
    return xf.reshape(batch, seq, d)
```
